```python
import jax, jax.numpy as jnp
from jax import lax
import numpy as np

D_MODEL = 1024
BATCH = 1
SEQ = 16384
DEPTH = 2

GMLP_GROUPS = 8
GMLP_GROUP_DIM = 64
GMLP_WIDTH = GMLP_GROUPS * GMLP_GROUP_DIM
CHUNK = 128
MLA_HEADS = 8
QK_NOPE_DIM = 64
QK_ROPE_DIM = 32
V_HEAD_DIM = 64
Q_LORA_RANK = 384
KV_LORA_RANK = 256
QK_HEAD_DIM = QK_NOPE_DIM + QK_ROPE_DIM
MLA_WIDTH = MLA_HEADS * V_HEAD_DIM
ROPE_THETA = 10000.0
Q_BLOCK = 128
IN_SPLITS = (
    GMLP_WIDTH,
    2 * GMLP_WIDTH,
    2 * GMLP_WIDTH + Q_LORA_RANK,
    2 * GMLP_WIDTH + Q_LORA_RANK + KV_LORA_RANK,
    2 * GMLP_WIDTH + Q_LORA_RANK + KV_LORA_RANK + QK_ROPE_DIM,
    2 * GMLP_WIDTH + Q_LORA_RANK + KV_LORA_RANK + QK_ROPE_DIM + D_MODEL,
)
IN_TOTAL = 2 * GMLP_WIDTH + Q_LORA_RANK + KV_LORA_RANK + QK_ROPE_DIM + 2 * D_MODEL
DENSE_FF = 2752
N_EXPERTS = 8
TOP_K = 2
EXPERT_FF = 3584
MOE_BLOCK = 128
N_DENSE = (DEPTH + 1) // 2
N_MOE = DEPTH // 2
EPS = 1e-6

kernel_name = "hybrid_gmlp_mla_moe_block"


def rms_norm(x, g):
    xf = x.astype(jnp.float32)
    y = xf * lax.rsqrt(jnp.mean(xf * xf, axis=-1, keepdims=True) + EPS)
    return (y * g.astype(jnp.float32)).astype(x.dtype)


def layer_norm(x, g, b):
    xf = x.astype(jnp.float32)
    mu = jnp.mean(xf, axis=-1, keepdims=True)
    xc = xf - mu
    y = xc * lax.rsqrt(jnp.mean(xc * xc, axis=-1, keepdims=True) + EPS)
    return (y * g.astype(jnp.float32) + b.astype(jnp.float32)).astype(x.dtype)


def rope_tables(positions, dtype):
    inv_freq = 1.0 / (ROPE_THETA ** (jnp.arange(0, QK_ROPE_DIM, 2, dtype=jnp.float32) / QK_ROPE_DIM))
    ang = positions.astype(jnp.float32)[..., None] * inv_freq
    return jnp.cos(ang).astype(dtype), jnp.sin(ang).astype(dtype)


def rope(x, cos, sin):
    x1, x2 = jnp.split(x, 2, axis=-1)
    return jnp.concatenate([x1 * cos - x2 * sin, x2 * cos + x1 * sin], axis=-1)


def gmlp_spatial_gating(u, v, ln_g, ln_b, w_s, b_s):
    B, S, _ = u.shape
    u = jax.nn.gelu(u, approximate=False)
    v = layer_norm(jax.nn.gelu(v, approximate=False), ln_g, ln_b)
    vc = v.reshape(B, S // CHUNK, CHUNK, GMLP_GROUPS, GMLP_GROUP_DIM)
    causal = jnp.tril(jnp.ones((CHUNK, CHUNK), dtype=bool))
    ws = jnp.where(causal[None], w_s, jnp.zeros((), w_s.dtype))
    z = jnp.einsum('gts,bcsgd->bctgd', ws, vc) + b_s.T[None, None, :, :, None]
    return u * z.reshape(B, S, GMLP_WIDTH)


def mla_attention(c_q, c_kv, k_rope, cos, sin, q_norm, w_uq, kv_norm, w_ukv):
    B, S, _ = c_q.shape
    q = (rms_norm(c_q, q_norm) @ w_uq).reshape(B, S, MLA_HEADS, QK_HEAD_DIM)
    q_nope, q_pe = jnp.split(q, [QK_NOPE_DIM], axis=-1)
    q_pe = rope(q_pe, cos[:, :, None, :], sin[:, :, None, :])
    kv = (rms_norm(c_kv, kv_norm) @ w_ukv).reshape(B, S, MLA_HEADS, QK_NOPE_DIM + V_HEAD_DIM)
    k_nope, v = jnp.split(kv, [QK_NOPE_DIM], axis=-1)
    k_pe = rope(k_rope, cos, sin)[:, :, None, :]
    q = jnp.concatenate([q_nope, q_pe], axis=-1)
    k = jnp.concatenate([k_nope, jnp.broadcast_to(k_pe, (B, S, MLA_HEADS, QK_ROPE_DIM))], axis=-1)
    scale = QK_HEAD_DIM ** -0.5
    kpos = jnp.arange(S)

    def attend(i):
        q0 = i * Q_BLOCK
        qb = lax.dynamic_slice_in_dim(q, q0, Q_BLOCK, axis=1)
        s = jnp.einsum('bqhd,bkhd->bhqk', qb, k).astype(jnp.float32) * scale
        qpos = q0 + jnp.arange(Q_BLOCK)
        s = jnp.where(kpos[None, :] <= qpos[:, None], s, -jnp.inf)
        p = jax.nn.softmax(s, axis=-1).astype(v.dtype)
        return jnp.einsum('bhqk,bkhd->bqhd', p, v)

    o = lax.map(attend, jnp.arange(S // Q_BLOCK))
    return jnp.moveaxis(o, 0, 1).reshape(B, S, MLA_WIDTH)


def hybrid_mixer(h, cos, sin, w_in, gln_g, gln_b, g_ws, g_bs, q_norm, w_uq, kv_norm, w_ukv,
                 w_branch_a, w_branch_b, w_out):
    proj = h @ w_in
    u, v, c_q, c_kv, k_rope, gate_a, gate_b = jnp.split(proj, IN_SPLITS, axis=-1)
    y_a = gmlp_spatial_gating(u, v, gln_g, gln_b, g_ws, g_bs) @ w_branch_a
    y_b = mla_attention(c_q, c_kv, k_rope, cos, sin, q_norm, w_uq, kv_norm, w_ukv) @ w_branch_b
    merged = jax.nn.sigmoid(gate_a) * y_a + jax.nn.sigmoid(gate_b) * y_b
    return merged @ w_out


def swiglu(h, w_gate, w_up, w_down):
    return (jax.nn.silu(h @ w_gate) * (h @ w_up)) @ w_down


def moe_swiglu(h, w_router, b_router, w1, w3, w2):
    B, S, D = h.shape
    N = B * S
    xt = h.reshape(N, D)
    logits = (xt @ w_router + b_router).astype(jnp.float32)
    top_val, top_idx = lax.top_k(logits, TOP_K)
    gates = jax.nn.softmax(top_val, axis=-1).astype(h.dtype)
    A = N * TOP_K
    flat_e = top_idx.reshape(A).astype(jnp.int32)
    flat_tok = jnp.arange(A, dtype=jnp.int32) // TOP_K
    flat_g = gates.reshape(A)
    order = jnp.argsort(flat_e)
    se, stok, sg = flat_e[order], flat_tok[order], flat_g[order]
    counts = jnp.bincount(flat_e, length=N_EXPERTS).astype(jnp.int32)
    padded = ((counts + MOE_BLOCK - 1) // MOE_BLOCK) * MOE_BLOCK
    start = jnp.cumsum(counts) - counts
    pend = jnp.cumsum(padded)
    pstart = pend - padded
    dest = pstart[se] + (jnp.arange(A, dtype=jnp.int32) - start[se])
    n_blocks = (A + MOE_BLOCK - 1) // MOE_BLOCK + N_EXPERTS
    P = n_blocks * MOE_BLOCK
    buf_tok = jnp.full((P,), N, dtype=jnp.int32).at[dest].set(stok)
    buf_g = jnp.zeros((P,), h.dtype).at[dest].set(sg)
    block_e = jnp.minimum(
        jnp.searchsorted(pend, jnp.arange(n_blocks, dtype=jnp.int32) * MOE_BLOCK, side='right'),
        N_EXPERTS - 1)
    x_pad = jnp.concatenate([xt, jnp.zeros((1, D), h.dtype)], axis=0)

    def run_block(args):
        tok, e = args
        xb = x_pad[tok]
        hb = jax.nn.silu(xb @ w1[e]) * (xb @ w3[e])
        return hb @ w2[e]

    yb = lax.map(run_block, (buf_tok.reshape(n_blocks, MOE_BLOCK), block_e))
    yb = yb.reshape(P, D) * buf_g[:, None]
    out = jnp.zeros((N + 1, D), h.dtype).at[buf_tok].add(yb)[:N]
    return out.reshape(B, S, D)


def setup_inputs(seed: int = 0) -> dict:
    key = jax.random.key(seed)
    ks = iter(jax.random.split(key, 32))
    L, D = DEPTH, D_MODEL

    def nrm(shape, scale):
        return jax.random.normal(next(ks), shape, jnp.float32) * scale

    return {
        "x": nrm((BATCH, SEQ, D), 1.0),
        "c": nrm((BATCH, D), 1.0),
        "positions": jnp.broadcast_to(jnp.arange(SEQ, dtype=jnp.int32)[None, :], (BATCH, SEQ)),
        "ada_w": nrm((L, D, 6 * D), D ** -0.5),
        "ada_b": nrm((L, 6 * D), 0.02),
        "norm_mix_pre": 1.0 + nrm((L, D), 0.05),
        "norm_mix_post": 1.0 + nrm((L, D), 0.05),
        "norm_ffn_pre": 1.0 + nrm((L, D), 0.05),
        "norm_ffn_post": 1.0 + nrm((L, D), 0.05),
        "w_in": nrm((L, D, IN_TOTAL), D ** -0.5),
        "gmlp_ln_g": 1.0 + nrm((L, GMLP_WIDTH), 0.05),
        "gmlp_ln_b": nrm((L, GMLP_WIDTH), 0.02),
        "gmlp_ws": nrm((L, GMLP_GROUPS, CHUNK, CHUNK), CHUNK ** -0.5),
        "gmlp_bs": 1.0 + nrm((L, GMLP_GROUPS, CHUNK), 0.02),
        "mla_q_norm": 1.0 + nrm((L, Q_LORA_RANK), 0.05),
        "mla_w_uq": nrm((L, Q_LORA_RANK, MLA_HEADS * QK_HEAD_DIM), Q_LORA_RANK ** -0.5),
        "mla_kv_norm": 1.0 + nrm((L, KV_LORA_RANK), 0.05),
        "mla_w_ukv": nrm((L, KV_LORA_RANK, MLA_HEADS * (QK_NOPE_DIM + V_HEAD_DIM)), KV_LORA_RANK ** -0.5),
        "w_branch_a": nrm((L, GMLP_WIDTH, D), GMLP_WIDTH ** -0.5),
        "w_branch_b": nrm((L, MLA_WIDTH, D), MLA_WIDTH ** -0.5),
        "w_out": nrm((L, D, D), D ** -0.5),
        "ffn_w_gate": nrm((N_DENSE, D, DENSE_FF), D ** -0.5),
        "ffn_w_up": nrm((N_DENSE, D, DENSE_FF), D ** -0.5),
        "ffn_w_down": nrm((N_DENSE, DENSE_FF, D), DENSE_FF ** -0.5),
        "moe_router": nrm((N_MOE, D, N_EXPERTS), D ** -0.5),
        "moe_router_bias": nrm((N_MOE, N_EXPERTS), 0.01),
        "moe_w1": nrm((N_MOE, N_EXPERTS, D, EXPERT_FF), D ** -0.5),
        "moe_w3": nrm((N_MOE, N_EXPERTS, D, EXPERT_FF), D ** -0.5),
        "moe_w2": nrm((N_MOE, N_EXPERTS, EXPERT_FF, D), EXPERT_FF ** -0.5),
    }


def reference(x, c, positions, ada_w, ada_b, norm_mix_pre, norm_mix_post, norm_ffn_pre, norm_ffn_post,
              w_in, gmlp_ln_g, gmlp_ln_b, gmlp_ws, gmlp_bs, mla_q_norm, mla_w_uq, mla_kv_norm, mla_w_ukv,
              w_branch_a, w_branch_b, w_out, ffn_w_gate, ffn_w_up, ffn_w_down,
              moe_router, moe_router_bias, moe_w1, moe_w3, moe_w2):
    cos, sin = rope_tables(positions, x.dtype)
    c_act = jax.nn.silu(c)
    for l in range(DEPTH):
        mod = (c_act @ ada_w[l] + ada_b[l])[:, None, :]
        sh_m, sc_m, g_m, sh_f, sc_f, g_f = jnp.split(mod, 6, axis=-1)
        h = rms_norm(x, norm_mix_pre[l]) * (1 + sc_m) + sh_m
        y = hybrid_mixer(h, cos, sin, w_in[l], gmlp_ln_g[l], gmlp_ln_b[l], gmlp_ws[l], gmlp_bs[l],
                         mla_q_norm[l], mla_w_uq[l], mla_kv_norm[l], mla_w_ukv[l],
                         w_branch_a[l], w_branch_b[l], w_out[l])
        x = x + g_m * rms_norm(y, norm_mix_post[l])
        h = rms_norm(x, norm_ffn_pre[l]) * (1 + sc_f) + sh_f
        if l % 2 == 0:
            j = l // 2
            y = swiglu(h, ffn_w_gate[j], ffn_w_up[j], ffn_w_down[j])
        else:
            j = l // 2
            y = moe_swiglu(h, moe_router[j], moe_router_bias[j], moe_w1[j], moe_w3[j], moe_w2[j])
        x = x + g_f * rms_norm(y, norm_ffn_post[l])
    return x
```

```python
import functools

import jax
import jax.numpy as jnp
import numpy as np
from jax import lax
from jax.experimental import pallas as pl
from jax.experimental.pallas import tpu as pltpu

F32 = jnp.float32
BF16 = jnp.bfloat16

EPS = 1e-6
LANES = 128
BF16_ROWS = 16
GMLP_GROUPS = 8
GMLP_GROUP_DIM = 64
GMLP_WIDTH = GMLP_GROUPS * GMLP_GROUP_DIM
CHUNK = 128
HEADS = 8
NOPE = 64
ROPE = 32
VDIM = 64
HEAD_PAD = 128
Q_RANK = 384
KV_RANK = 256
ROPE_THETA = 10000.0
N_EXPERTS = 8
NEG = -1e30

ROW_TILE = 512
ATTN_TILE = 512
EXPERT_ROWS = 512
VMEM_LIMIT = 52 * 1024 * 1024


def _params(*sem):
    return pltpu.CompilerParams(dimension_semantics=sem, vmem_limit_bytes=VMEM_LIMIT)


def _dot(a, b):
    return jnp.dot(a, b, preferred_element_type=F32)


def _rms(x, g):
    return x * lax.rsqrt(jnp.mean(x * x, axis=-1, keepdims=True) + EPS) * g


def _gelu(x):
    return 0.5 * x * (1.0 + lax.erf(x * np.float32(0.7071067811865476)))


def _full(shape):
    return pl.BlockSpec(shape, lambda *_: (0,) * len(shape))


def _mod_body(c_ref, w_ref, b_ref, o_ref):
    c = c_ref[...]
    ca = c * jax.nn.sigmoid(c)
    o_ref[0] = jnp.sum(ca * w_ref[0], axis=0, keepdims=True) + b_ref[0]


def _ada_mod(c, ada_w, ada_b):
    n_layers, d, n = ada_w.shape
    tn = n // 4
    return pl.pallas_call(
        _mod_body,
        grid=(n_layers, n // tn),
        in_specs=[pl.BlockSpec((d, 1), lambda l, j: (0, 0)),
                  pl.BlockSpec((1, d, tn), lambda l, j: (l, 0, j)),
                  pl.BlockSpec((1, 1, tn), lambda l, j: (l, 0, j))],
        out_specs=pl.BlockSpec((1, 1, tn), lambda l, j: (l, 0, j)),
        out_shape=jax.ShapeDtypeStruct((n_layers, 1, n), F32),
        compiler_params=_params("arbitrary", "arbitrary"),
        name="ada_mod",
    )(c.reshape(d, 1), ada_w, ada_b.reshape(n_layers, 1, n))


def _rope_body(pos_ref, invf_ref, cos_ref, sin_ref):
    ang = pos_ref[...].astype(F32) * invf_ref[...]
    cos_ref[...] = jnp.cos(ang)
    sin_ref[...] = jnp.sin(ang)


def _rope_tables(positions):
    s = positions.shape[-1]
    half = ROPE // 2
    per_row = LANES // half
    inv_freq = 1.0 / (ROPE_THETA ** (jnp.arange(0, ROPE, 2, dtype=F32) / ROPE))
    pos_dense = jnp.repeat(positions.reshape(s), half).reshape(s // per_row, LANES)
    invf = jnp.tile(inv_freq, per_row).reshape(1, LANES)
    rows = s // per_row
    tr = rows // 4
    cos_d, sin_d = pl.pallas_call(
        _rope_body,
        grid=(rows // tr,),
        in_specs=[pl.BlockSpec((tr, LANES), lambda i: (i, 0)), _full((1, LANES))],
        out_specs=[pl.BlockSpec((tr, LANES), lambda i: (i, 0))] * 2,
        out_shape=[jax.ShapeDtypeStruct((rows, LANES), F32)] * 2,
        compiler_params=_params("arbitrary"),
        name="rope_tables",
    )(pos_dense, invf)
    cos = cos_d.reshape(s, half)
    sin = sin_d.reshape(s, half)
    ones = jnp.ones((s, NOPE), F32)
    zeros = jnp.zeros((s, NOPE), F32)
    pad = jnp.zeros((s, HEAD_PAD - NOPE - ROPE), F32)
    cos_t = jnp.concatenate([ones, cos, cos, pad], axis=1)
    sin_t = jnp.concatenate([zeros, sin, sin, pad], axis=1)
    return cos_t, sin_t


def _inproj_body(x_ref, g_ref, sc_ref, sh_ref, w_ref, uv_ref, cq_ref, ckv_ref, kr_ref, ga_ref, gb_ref):
    h = _rms(x_ref[...], g_ref[...]) * (1.0 + sc_ref[...]) + sh_ref[...]
    hb = h.astype(BF16)
    col = 0
    for ref in (uv_ref, cq_ref, ckv_ref, kr_ref, ga_ref, gb_ref):
        n = ref.shape[1]
        ref[...] = _dot(hb, w_ref[:, col:col + n]).astype(ref.dtype)
        col += n


def _inproj(x, g, sc, sh, w):
    s, d = x.shape
    tm = ROW_TILE
    widths = (2 * GMLP_WIDTH, Q_RANK, KV_RANK, 2 * HEAD_PAD, d, d)
    row = lambda n: pl.BlockSpec((tm, n), lambda i: (i, 0))
    return pl.pallas_call(
        _inproj_body,
        grid=(s // tm,),
        in_specs=[row(d), _full((1, d)), _full((1, d)), _full((1, d)), _full(w.shape)],
        out_specs=[row(n) for n in widths],
        out_shape=[jax.ShapeDtypeStruct((s, n), BF16) for n in widths],
        compiler_params=_params("arbitrary"),
        name="mixer_inproj",
    )(x, g, sc, sh, w)


def _gmlp_body(uv_ref, ga_ref, lng_ref, lnb_ref, ws_ref, bias_ref, wa_ref, o_ref):
    tm = uv_ref.shape[0]
    u = uv_ref[:, :GMLP_WIDTH].astype(F32)
    v = uv_ref[:, GMLP_WIDTH:].astype(F32)
    gu = _gelu(u)
    gv = _gelu(v)
    mu = jnp.mean(gv, axis=-1, keepdims=True)
    xc = gv - mu
    vn = xc * lax.rsqrt(jnp.mean(xc * xc, axis=-1, keepdims=True) + EPS) * lng_ref[...] + lnb_ref[...]
    vb = vn.astype(BF16)
    t_idx = lax.broadcasted_iota(jnp.int32, (CHUNK, CHUNK), 0)
    s_idx = lax.broadcasted_iota(jnp.int32, (CHUNK, CHUNK), 1)
    causal = s_idx <= t_idx
    ws = [jnp.where(causal, ws_ref[g], 0.0).astype(BF16) for g in range(GMLP_GROUPS)]
    left = lax.broadcasted_iota(jnp.int32, (CHUNK, LANES), 1) < GMLP_GROUP_DIM
    bias = bias_ref[...]
    z_rows = []
    for c in range(tm // CHUNK):
        vc = vb[c * CHUNK:(c + 1) * CHUNK]
        z_cols = []
        for j in range(GMLP_WIDTH // LANES):
            vp = vc[:, j * LANES:(j + 1) * LANES]
            z_cols.append(jnp.where(left, _dot(ws[2 * j], vp), _dot(ws[2 * j + 1], vp)))
        z_rows.append(jnp.concatenate(z_cols, axis=1) + bias)
    z = jnp.concatenate(z_rows, axis=0)
    gated = (gu * z).astype(BF16)
    ya = _dot(gated, wa_ref[...])
    o_ref[...] = (jax.nn.sigmoid(ga_ref[...].astype(F32)) * ya).astype(o_ref.dtype)


def _gmlp(uv, ga, lng, lnb, ws, bias, wa):
    s, d = ga.shape
    tm = ROW_TILE
    row = lambda n: pl.BlockSpec((tm, n), lambda i: (i, 0))
    return pl.pallas_call(
        _gmlp_body,
        grid=(s // tm,),
        in_specs=[row(2 * GMLP_WIDTH), row(d), _full((1, GMLP_WIDTH)), _full((1, GMLP_WIDTH)),
                  _full(ws.shape), _full(bias.shape), _full(wa.shape)],
        out_specs=row(d),
        out_shape=jax.ShapeDtypeStruct((s, d), BF16),
        compiler_params=_params("arbitrary"),
        name="mixer_gmlp",
    )(uv, ga, lng, lnb, ws, bias, wa)


def _mla_prep_body(cq_ref, ckv_ref, kr_ref, cos_ref, sin_ref, qn_ref, kvn_ref, wqm_ref, wqs_ref,
                   wk_ref, wv_ref, vone_ref, q_ref, k_ref, v_ref, *, scale):
    cos = cos_ref[...]
    sin = sin_ref[...]
    cqn = _rms(cq_ref[...].astype(F32), qn_ref[...]).astype(BF16)
    qm = _dot(cqn, wqm_ref[...])
    qs = _dot(cqn, wqs_ref[...])
    ckn = _rms(ckv_ref[...].astype(F32), kvn_ref[...]).astype(BF16)
    km = _dot(ckn, wk_ref[...])
    kr = kr_ref[...].astype(F32)
    kpe = kr[:, :HEAD_PAD] * cos + kr[:, HEAD_PAD:] * sin
    for h in range(HEADS):
        sl = slice(h * HEAD_PAD, (h + 1) * HEAD_PAD)
        q_ref[:, sl] = ((qm[:, sl] * cos + qs[:, sl] * sin) * scale).astype(q_ref.dtype)
        k_ref[:, sl] = (km[:, sl] + kpe).astype(k_ref.dtype)
    v_ref[...] = (_dot(ckn, wv_ref[...]) + vone_ref[...]).astype(v_ref.dtype)


def _mla_prep(cq, ckv, kr, cos_t, sin_t, qn, kvn, wqm, wqs, wk, wv, vone):
    s = cq.shape[0]
    tm = ROW_TILE
    width = HEADS * HEAD_PAD
    row = lambda n: pl.BlockSpec((tm, n), lambda i: (i, 0))
    scale = float((NOPE + ROPE) ** -0.5)
    return pl.pallas_call(
        functools.partial(_mla_prep_body, scale=scale),
        grid=(s // tm,),
        in_specs=[row(Q_RANK), row(KV_RANK), row(2 * HEAD_PAD), row(HEAD_PAD), row(HEAD_PAD),
                  _full((1, Q_RANK)), _full((1, KV_RANK)), _full(wqm.shape), _full(wqs.shape),
                  _full(wk.shape), _full(wv.shape), _full((1, width))],
        out_specs=[row(width)] * 3,
        out_shape=[jax.ShapeDtypeStruct((s, width), BF16)] * 3,
        compiler_params=_params("arbitrary"),
        name="mla_prep",
    )(cq, ckv, kr, cos_t, sin_t, qn, kvn, wqm, wqs, wk, wv, vone)


def _attn_body(q_ref, k_ref, v_ref, o_ref, m_ref, acc_ref):
    t = q_ref.shape[0]
    qi = pl.program_id(1)
    q = q_ref[...]
    m_ref[...] = jnp.full(m_ref.shape, NEG, F32)
    acc_ref[...] = jnp.zeros(acc_ref.shape, F32)

    def step(kb, vb, mask):
        s = lax.dot_general(q, kb, (((1,), (1,)), ((), ())), preferred_element_type=F32)
        if mask is not None:
            s = jnp.where(mask, s, NEG)
        m_prev = m_ref[...]
        m_new = jnp.maximum(m_prev, jnp.max(s, axis=-1, keepdims=True))
        alpha = jnp.exp(m_prev - m_new)
        p = jnp.exp(s - m_new[:, :1]).astype(BF16)
        acc_ref[...] = alpha * acc_ref[...] + _dot(p, vb)
        m_ref[...] = m_new

    def body(ki, carry):
        off = pl.multiple_of(ki * t, t)
        step(k_ref[pl.ds(off, t), :], v_ref[pl.ds(off, t), :], None)
        return carry

    lax.fori_loop(0, qi, body, 0)
    off = pl.multiple_of(qi * t, t)
    row = lax.broadcasted_iota(jnp.int32, (t, t), 0)
    col = lax.broadcasted_iota(jnp.int32, (t, t), 1)
    step(k_ref[pl.ds(off, t), :], v_ref[pl.ds(off, t), :], col <= row)
    acc = acc_ref[...]
    o_ref[...] = (acc / acc[:, VDIM:VDIM + 1]).astype(o_ref.dtype)


def _attention(q, k, v):
    s, width = q.shape
    t = ATTN_TILE
    return pl.pallas_call(
        _attn_body,
        grid=(HEADS, s // t),
        in_specs=[pl.BlockSpec((t, HEAD_PAD), lambda h, i: (i, h)),
                  pl.BlockSpec((s, HEAD_PAD), lambda h, i: (0, h)),
                  pl.BlockSpec((s, HEAD_PAD), lambda h, i: (0, h))],
        out_specs=pl.BlockSpec((t, HEAD_PAD), lambda h, i: (i, h)),
        out_shape=jax.ShapeDtypeStruct((s, width), BF16),
        scratch_shapes=[pltpu.VMEM((t, HEAD_PAD), F32), pltpu.VMEM((t, HEAD_PAD), F32)],
        compiler_params=_params("arbitrary", "arbitrary"),
        name="mla_attention",
    )(q, k, v)


def _merge_body(a_ref, gb_ref, o_ref, x_ref, wb_ref, wo_ref, gp_ref, gm_ref, out_ref):
    yb = _dot(o_ref[...], wb_ref[...])
    merged = a_ref[...].astype(F32) + jax.nn.sigmoid(gb_ref[...].astype(F32)) * yb
    y = _dot(merged.astype(BF16), wo_ref[...])
    out_ref[...] = x_ref[...] + gm_ref[...] * _rms(y, gp_ref[...])


def _merge(a, gb, o, x, wb, wo, gp, gm):
    s, d = x.shape
    tm = ROW_TILE
    row = pl.BlockSpec((tm, d), lambda i: (i, 0))
    return pl.pallas_call(
        _merge_body,
        grid=(s // tm,),
        in_specs=[row, row, row, row, _full(wb.shape), _full(wo.shape), _full((1, d)), _full((1, d))],
        out_specs=row,
        out_shape=jax.ShapeDtypeStruct((s, d), F32),
        compiler_params=_params("arbitrary"),
        name="mixer_merge",
    )(a, gb, o, x, wb, wo, gp, gm)


def _ffn_body(x_ref, g_ref, sc_ref, sh_ref, wg_ref, wu_ref, wd_ref, gp_ref, gf_ref, o_ref, h_ref, acc_ref):
    f = pl.program_id(1)

    @pl.when(f == 0)
    def _():
        h = _rms(x_ref[...], g_ref[...]) * (1.0 + sc_ref[...]) + sh_ref[...]
        h_ref[...] = h.astype(BF16)

    hb = h_ref[...]
    a = _dot(hb, wg_ref[...])
    b = _dot(hb, wu_ref[...])
    part = _dot((a * jax.nn.sigmoid(a) * b).astype(BF16), wd_ref[...])

    @pl.when(f == 0)
    def _():
        acc_ref[...] = part

    @pl.when(f > 0)
    def _():
        acc_ref[...] += part

    @pl.when(f == pl.num_programs(1) - 1)
    def _():
        o_ref[...] = x_ref[...] + gf_ref[...] * _rms(acc_ref[...], gp_ref[...])


def _dense_ffn(x, g, sc, sh, wg, wu, wd, gp, gf, tf):
    s, d = x.shape
    ff = wg.shape[1]
    tm = ROW_TILE
    row = pl.BlockSpec((tm, d), lambda i, f: (i, 0))
    vec = pl.BlockSpec((1, d), lambda i, f: (0, 0))
    return pl.pallas_call(
        _ffn_body,
        grid=(s // tm, ff // tf),
        in_specs=[row, vec, vec, vec,
                  pl.BlockSpec((d, tf), lambda i, f: (0, f)),
                  pl.BlockSpec((d, tf), lambda i, f: (0, f)),
                  pl.BlockSpec((tf, d), lambda i, f: (f, 0)),
                  vec, vec],
        out_specs=row,
        out_shape=jax.ShapeDtypeStruct((s, d), F32),
        scratch_shapes=[pltpu.VMEM((tm, d), BF16), pltpu.VMEM((tm, d), F32)],
        compiler_params=_params("arbitrary", "arbitrary"),
        name="dense_ffn",
    )(x, g, sc, sh, wg, wu, wd, gp, gf)


def _route_body(x_ref, g_ref, sc_ref, sh_ref, wr_ref, br_ref, info_ref, offs_ref, xs_ref,
                xbuf, zbuf, carry, sem, *, region, tail):
    i = pl.program_id(0)
    nt = pl.num_programs(0)
    tm = x_ref.shape[0]

    @pl.when(i == 0)
    def _():
        for e in range(N_EXPERTS):
            carry[e] = 0

    h = _rms(x_ref[...], g_ref[...]) * (1.0 + sc_ref[...]) + sh_ref[...]
    hb = h.astype(BF16)
    h_lo = (h - hb.astype(F32)).astype(BF16)
    w = wr_ref[...]
    w_hi = w.astype(BF16)
    w_lo = (w - w_hi.astype(F32)).astype(BF16)
    logits = _dot(hb, w_hi) + (_dot(h_lo, w_hi) + _dot(hb, w_lo)) + br_ref[...]

    lane = lax.broadcasted_iota(jnp.int32, (tm, LANES), 1)
    m1 = jnp.max(logits, axis=-1, keepdims=True)
    i1 = jnp.min(jnp.where(logits == m1, lane, LANES), axis=-1, keepdims=True)
    oh1 = lane == i1
    rest = jnp.where(oh1, -3e38, logits)
    m2 = jnp.max(rest, axis=-1, keepdims=True)
    i2 = jnp.min(jnp.where(rest == m2, lane, LANES), axis=-1, keepdims=True)
    oh2 = lane == i2
    ex = jnp.exp(m2 - m1)
    g1 = 1.0 / (1.0 + ex)
    g2 = ex / (1.0 + ex)

    ohf = jnp.where(oh1 | oh2, 1.0, 0.0)
    r_idx = lax.broadcasted_iota(jnp.int32, (tm, tm), 0)
    c_idx = lax.broadcasted_iota(jnp.int32, (tm, tm), 1)
    earlier = jnp.where(c_idx < r_idx, 1.0, 0.0).astype(BF16)
    rank = _dot(earlier, ohf.astype(BF16))
    cnt = jnp.sum(ohf, axis=0, keepdims=True).astype(jnp.int32)
    cnt_al = ((cnt + (BF16_ROWS - 1)) // BF16_ROWS) * BF16_ROWS

    lane1 = lax.broadcasted_iota(jnp.int32, (1, LANES), 1)
    base = jnp.zeros((1, LANES), jnp.int32)
    for e in range(N_EXPERTS):
        base = jnp.where(lane1 == e, carry[e], base)
    pos = rank + base.astype(F32)
    pos1 = jnp.sum(jnp.where(oh1, pos, 0.0), axis=-1, keepdims=True)
    pos2 = jnp.sum(jnp.where(oh2, pos, 0.0), axis=-1, keepdims=True)
    info = jnp.where(lane == 0, i1.astype(F32),
           jnp.where(lane == 1, i2.astype(F32),
           jnp.where(lane == 2, g1,
           jnp.where(lane == 3, g2,
           jnp.where(lane == 4, pos1,
           jnp.where(lane == 5, pos2, 0.0))))))
    info_ref[...] = info

    def append(e):
        return pltpu.make_async_copy(xbuf.at[e], xs_ref.at[pl.ds(0, tm)], sem.at[e])

    slot = c_idx.astype(F32)
    for e in range(N_EXPERTS):
        hit = (rank[:, e:e + 1] == slot) & (ohf[:, e:e + 1] > 0.0)
        sel_t = jnp.where(hit, 1.0, 0.0).astype(BF16)
        xe = lax.dot_general(sel_t, hb, (((0,), (0,)), ((), ())), preferred_element_type=F32)

        @pl.when(i > 0)
        def _():
            append(e).wait()

        xbuf[e] = xe.astype(BF16)
        off = carry[e]
        offs_ref[i * N_EXPERTS + e] = off
        dst = pl.multiple_of(e * region + off, BF16_ROWS)
        pltpu.make_async_copy(xbuf.at[e], xs_ref.at[pl.ds(dst, tm)], sem.at[e]).start()
        carry[e] = off + cnt_al[0, e]

    @pl.when(i == nt - 1)
    def _():
        zbuf[...] = jnp.zeros(zbuf.shape, zbuf.dtype)
        for e in range(N_EXPERTS):
            append(e).wait()
            end = carry[e]
            offs_ref[nt * N_EXPERTS + e] = end
            dst = pl.multiple_of(e * region + end, BF16_ROWS)
            fill = pltpu.make_async_copy(zbuf, xs_ref.at[pl.ds(dst, tail)], sem.at[e])
            fill.start()
            fill.wait()


def _route(x, g, sc, sh, wr, br, region):
    s, d = x.shape
    tm = ROW_TILE
    nt = s // tm
    tail = EXPERT_ROWS
    row = pl.BlockSpec((tm, d), lambda i: (i, 0))
    return pl.pallas_call(
        functools.partial(_route_body, region=region, tail=tail),
        grid=(nt,),
        in_specs=[row, _full((1, d)), _full((1, d)), _full((1, d)), _full(wr.shape), _full(br.shape)],
        out_specs=[pl.BlockSpec((tm, LANES), lambda i: (i, 0)),
                   pl.BlockSpec(memory_space=pltpu.SMEM),
                   pl.BlockSpec(memory_space=pl.ANY)],
        out_shape=[jax.ShapeDtypeStruct((s, LANES), F32),
                   jax.ShapeDtypeStruct(((nt + 1) * N_EXPERTS,), jnp.int32),
                   jax.ShapeDtypeStruct((N_EXPERTS * region, d), BF16)],
        scratch_shapes=[pltpu.VMEM((N_EXPERTS, tm, d), BF16), pltpu.VMEM((tail, d), BF16),
                        pltpu.SMEM((N_EXPERTS,), jnp.int32), pltpu.SemaphoreType.DMA((N_EXPERTS,))],
        compiler_params=_params("arbitrary"),
        name="moe_route",
    )(x, g, sc, sh, wr, br)


def _expert_body(brow_ref, bexp_ref, nval_ref, x_ref, w1_ref, w3_ref, w2_ref, o_ref, acc_ref):
    i = pl.program_id(0)
    f = pl.program_id(1)

    @pl.when(i < nval_ref[0])
    def _():
        xb = x_ref[...]
        a = _dot(xb, w1_ref[0])
        b = _dot(xb, w3_ref[0])
        part = _dot((a * jax.nn.sigmoid(a) * b).astype(BF16), w2_ref[0])

        @pl.when(f == 0)
        def _():
            acc_ref[...] = part

        @pl.when(f > 0)
        def _():
            acc_ref[...] += part

        @pl.when(f == pl.num_programs(1) - 1)
        def _():
            o_ref[...] = acc_ref[...].astype(o_ref.dtype)


def _experts(xs, w1, w3, w2, brow, bexp, nval, tf):
    rows, d = xs.shape
    ff = w1.shape[2]
    tb = EXPERT_ROWS
    nff = ff // tf
    nb = brow.shape[0]

    def fcol(i, f, nval):
        return jnp.where(i < nval[0], f, nff - 1)

    grid_spec = pltpu.PrefetchScalarGridSpec(
        num_scalar_prefetch=3,
        grid=(nb, nff),
        in_specs=[pl.BlockSpec((tb, d), lambda i, f, br, be, nv: (br[i], 0)),
                  pl.BlockSpec((1, d, tf), lambda i, f, br, be, nv: (be[i], 0, fcol(i, f, nv))),
                  pl.BlockSpec((1, d, tf), lambda i, f, br, be, nv: (be[i], 0, fcol(i, f, nv))),
                  pl.BlockSpec((1, tf, d), lambda i, f, br, be, nv: (be[i], fcol(i, f, nv), 0))],
        out_specs=pl.BlockSpec((tb, d), lambda i, f, br, be, nv: (br[i], 0)),
        scratch_shapes=[pltpu.VMEM((tb, d), F32)],
    )
    return pl.pallas_call(
        _expert_body,
        grid_spec=grid_spec,
        out_shape=jax.ShapeDtypeStruct((rows, d), BF16),
        compiler_params=_params("arbitrary", "arbitrary"),
        name="moe_experts",
    )(brow, bexp, nval, xs, w1, w3, w2)


def _combine_body(offs_ref, info_ref, x_ref, gp_ref, gf_ref, ys_ref, o_ref, ybuf, sem, *, region):
    i = pl.program_id(0)
    tm = x_ref.shape[0]
    starts = []
    for e in range(N_EXPERTS):
        end = offs_ref[(i + 1) * N_EXPERTS + e]
        start = pl.multiple_of(jnp.maximum(end - tm, 0), BF16_ROWS)
        starts.append(start)
        src = pl.multiple_of(e * region + start, BF16_ROWS)
        pltpu.make_async_copy(ys_ref.at[pl.ds(src, tm)], ybuf.at[e], sem.at[e]).start()

    info = info_ref[...]
    i1 = info[:, 0:1]
    i2 = info[:, 1:2]
    g1 = info[:, 2:3]
    g2 = info[:, 3:4]
    pos1 = info[:, 4:5]
    pos2 = info[:, 5:6]
    slot = lax.broadcasted_iota(jnp.int32, (tm, tm), 1).astype(F32)
    acc = jnp.zeros(x_ref.shape, F32)
    for e in range(N_EXPERTS):
        r1 = i1 == float(e)
        r2 = i2 == float(e)
        local = jnp.where(r1, pos1, pos2) - starts[e].astype(F32)
        gate = jnp.where(r1, g1, jnp.where(r2, g2, 0.0))
        sel = jnp.where((local == slot) & (r1 | r2), 1.0, 0.0).astype(BF16)
        pltpu.make_async_copy(ys_ref.at[pl.ds(0, tm)], ybuf.at[e], sem.at[e]).wait()
        acc = acc + gate * _dot(sel, ybuf[e])
    o_ref[...] = x_ref[...] + gf_ref[...] * _rms(acc, gp_ref[...])


def _combine(offs, info, x, gp, gf, ys, region):
    s, d = x.shape
    tm = ROW_TILE
    grid_spec = pltpu.PrefetchScalarGridSpec(
        num_scalar_prefetch=1,
        grid=(s // tm,),
        in_specs=[pl.BlockSpec((tm, LANES), lambda i, o: (i, 0)),
                  pl.BlockSpec((tm, d), lambda i, o: (i, 0)),
                  pl.BlockSpec((1, d), lambda i, o: (0, 0)),
                  pl.BlockSpec((1, d), lambda i, o: (0, 0)),
                  pl.BlockSpec(memory_space=pl.ANY)],
        out_specs=pl.BlockSpec((tm, d), lambda i, o: (i, 0)),
        scratch_shapes=[pltpu.VMEM((N_EXPERTS, tm, d), BF16), pltpu.SemaphoreType.DMA((N_EXPERTS,))],
    )
    return pl.pallas_call(
        functools.partial(_combine_body, region=region),
        grid_spec=grid_spec,
        out_shape=jax.ShapeDtypeStruct((s, d), F32),
        compiler_params=_params("arbitrary"),
        name="moe_combine",
    )(offs, info, x, gp, gf, ys)


def _rot_half_cols(w):
    half = ROPE // 2
    return jnp.concatenate([-w[..., half:], w[..., :half]], axis=-1)


def _pad_cols(w, before, total):
    return jnp.pad(w, ((0, 0), (before, total - before - w.shape[1])))


def _mixer_weights(w_in, w_uq, w_ukv, w_branch_b):
    d = w_in.shape[0]
    o = 0
    parts = {}
    for name, n in (("u", GMLP_WIDTH), ("v", GMLP_WIDTH), ("cq", Q_RANK), ("ckv", KV_RANK),
                    ("kr", ROPE), ("ga", d), ("gb", d)):
        parts[name] = w_in[:, o:o + n]
        o += n
    kr_main = _pad_cols(parts["kr"], NOPE, HEAD_PAD)
    kr_swap = _pad_cols(_rot_half_cols(parts["kr"]), NOPE, HEAD_PAD)
    w_in_p = jnp.concatenate([parts["u"], parts["v"], parts["cq"], parts["ckv"], kr_main, kr_swap,
                              parts["ga"], parts["gb"]], axis=1).astype(BF16)

    wq = w_uq.reshape(Q_RANK, HEADS, NOPE + ROPE)
    zq = jnp.zeros((Q_RANK, HEADS, HEAD_PAD - NOPE - ROPE), w_uq.dtype)
    wq_main = jnp.concatenate([wq, zq], axis=-1).reshape(Q_RANK, HEADS * HEAD_PAD).astype(BF16)
    wq_swap = jnp.concatenate([jnp.zeros((Q_RANK, HEADS, NOPE), w_uq.dtype),
                               _rot_half_cols(wq[..., NOPE:]), zq], axis=-1)
    wq_swap = wq_swap.reshape(Q_RANK, HEADS * HEAD_PAD).astype(BF16)

    wkv = w_ukv.reshape(KV_RANK, HEADS, NOPE + VDIM)
    zk = jnp.zeros((KV_RANK, HEADS, HEAD_PAD - NOPE), w_ukv.dtype)
    wk = jnp.concatenate([wkv[..., :NOPE], zk], axis=-1).reshape(KV_RANK, HEADS * HEAD_PAD).astype(BF16)
    zv = jnp.zeros((KV_RANK, HEADS, HEAD_PAD - VDIM), w_ukv.dtype)
    wv = jnp.concatenate([wkv[..., NOPE:], zv], axis=-1).reshape(KV_RANK, HEADS * HEAD_PAD).astype(BF16)

    wb = w_branch_b.reshape(HEADS, VDIM, d)
    wb = jnp.concatenate([wb, jnp.zeros((HEADS, HEAD_PAD - VDIM, d), wb.dtype)], axis=1)
    wb = wb.reshape(HEADS * HEAD_PAD, d).astype(BF16)
    return w_in_p, wq_main, wq_swap, wk, wv, wb


def _expert_blocks(ends, region, nb):
    tb = EXPERT_ROWS
    per = jnp.maximum((ends + tb - 1) // tb, 1)
    stop = jnp.cumsum(per)
    total = stop[-1]
    step = jnp.minimum(jnp.arange(nb, dtype=jnp.int32), total - 1)
    e = jnp.minimum(jnp.searchsorted(stop, step, side="right"), N_EXPERTS - 1).astype(jnp.int32)
    j = step - (stop - per)[e]
    brow = e * (region // tb) + j
    return brow.astype(jnp.int32), e, total.astype(jnp.int32).reshape(1)


def kernel(x, c, positions, ada_w, ada_b, norm_mix_pre, norm_mix_post, norm_ffn_pre, norm_ffn_post, w_in, gmlp_ln_g, gmlp_ln_b, gmlp_ws, gmlp_bs, mla_q_norm, mla_w_uq, mla_kv_norm, mla_w_ukv, w_branch_a, w_branch_b, w_out, ffn_w_gate, ffn_w_up, ffn_w_down, moe_router, moe_router_bias, moe_w1, moe_w3, moe_w2):
    batch, s, d = x.shape
    assert batch == 1 and s % ROW_TILE == 0 and s % ATTN_TILE == 0
    depth = ada_w.shape[0]
    xs = x.reshape(s, d)
    mod = _ada_mod(c, ada_w, ada_b)
    cos_t, sin_t = _rope_tables(positions)
    vone = jnp.zeros((HEADS, HEAD_PAD), F32).at[:, VDIM].set(1.0).reshape(1, HEADS * HEAD_PAD)
    row = lambda v: v.reshape(1, -1)

    for l in range(depth):
        sh_m, sc_m, g_m, sh_f, sc_f, g_f = [mod[l, :, k * d:(k + 1) * d] for k in range(6)]
        w_in_p, wq_main, wq_swap, wk, wv, wb = _mixer_weights(w_in[l], mla_w_uq[l], mla_w_ukv[l], w_branch_b[l])
        uv, cq, ckv, kr, ga, gb = _inproj(xs, row(norm_mix_pre[l]), sc_m, sh_m, w_in_p)
        bias = jnp.repeat(gmlp_bs[l].T, GMLP_GROUP_DIM, axis=1)
        a = _gmlp(uv, ga, row(gmlp_ln_g[l]), row(gmlp_ln_b[l]), gmlp_ws[l], bias, w_branch_a[l].astype(BF16))
        q, k, v = _mla_prep(cq, ckv, kr, cos_t, sin_t, row(mla_q_norm[l]), row(mla_kv_norm[l]),
                            wq_main, wq_swap, wk, wv, vone)
        o = _attention(q, k, v)
        xs = _merge(a, gb, o, xs, wb, w_out[l].astype(BF16), row(norm_mix_post[l]), g_m)

        j = l // 2
        if l % 2 == 0:
            ff = ffn_w_gate.shape[2]
            ff_pad = -(-ff // (2 * LANES)) * (2 * LANES)
            wg = jnp.pad(ffn_w_gate[j], ((0, 0), (0, ff_pad - ff))).astype(BF16)
            wu = jnp.pad(ffn_w_up[j], ((0, 0), (0, ff_pad - ff))).astype(BF16)
            wd = jnp.pad(ffn_w_down[j], ((0, ff_pad - ff), (0, 0))).astype(BF16)
            xs = _dense_ffn(xs, row(norm_ffn_pre[l]), sc_f, sh_f, wg, wu, wd, row(norm_ffn_post[l]), g_f,
                            tf=ff_pad // 2)
        else:
            nt = s // ROW_TILE
            region = s + nt * BF16_ROWS + max(ROW_TILE, EXPERT_ROWS)
            region = -(-region // EXPERT_ROWS) * EXPERT_ROWS
            wr = jnp.pad(moe_router[j], ((0, 0), (0, LANES - N_EXPERTS)))
            br = jnp.pad(moe_router_bias[j], (0, LANES - N_EXPERTS), constant_values=NEG).reshape(1, LANES)
            info, offs, xsort = _route(xs, row(norm_ffn_pre[l]), sc_f, sh_f, wr, br, region)
            nb = (2 * s + nt * N_EXPERTS * (BF16_ROWS - 1)) // EXPERT_ROWS + N_EXPERTS + 1
            brow, bexp, nval = _expert_blocks(offs[nt * N_EXPERTS:], region, nb)
            ysort = _experts(xsort, moe_w1[j].astype(BF16), moe_w3[j].astype(BF16), moe_w2[j].astype(BF16),
                             brow, bexp, nval, tf=moe_w1.shape[3] // 4)
            xs = _combine(offs, info, xs, row(norm_ffn_post[l]), g_f, ysort, region)
    return xs.reshape(batch, s, d)
```

```python
import functools

import jax
import jax.numpy as jnp
import numpy as np
from jax import lax
from jax.experimental import pallas as pl
from jax.experimental.pallas import tpu as pltpu

F32 = jnp.float32
BF16 = jnp.bfloat16

EPS = 1e-6
LANES = 128
BF16_ROWS = 16
GMLP_GROUPS = 8
GMLP_GROUP_DIM = 64
GMLP_WIDTH = GMLP_GROUPS * GMLP_GROUP_DIM
CHUNK = 128
HEADS = 8
NOPE = 64
ROPE = 32
VDIM = 64
HEAD_PAD = 128
Q_RANK = 384
KV_RANK = 256
ROPE_THETA = 10000.0
N_EXPERTS = 8
NEG = -1e30

ROW_TILE = 512
ATTN_TILE = 512
ATTN_HEADS = 2
EXPERT_ROWS = 512
VMEM_LIMIT = 52 * 1024 * 1024


def _params(*sem):
    return pltpu.CompilerParams(dimension_semantics=sem, vmem_limit_bytes=VMEM_LIMIT)


def _dot(a, b):
    return jnp.dot(a, b, preferred_element_type=F32)


def _rms(x, g):
    return x * lax.rsqrt(jnp.mean(x * x, axis=-1, keepdims=True) + EPS) * g


def _gelu(x):
    return 0.5 * x * (1.0 + lax.erf(x * np.float32(0.7071067811865476)))


def _full(shape):
    return pl.BlockSpec(shape, lambda *_: (0,) * len(shape))


def _mod_body(c_ref, w_ref, b_ref, o_ref):
    c = c_ref[...]
    ca = c * jax.nn.sigmoid(c)
    o_ref[0] = jnp.sum(ca * w_ref[0], axis=0, keepdims=True) + b_ref[0]


def _ada_mod(c, ada_w, ada_b):
    n_layers, d, n = ada_w.shape
    tn = n // 4
    return pl.pallas_call(
        _mod_body,
        grid=(n_layers, n // tn),
        in_specs=[pl.BlockSpec((d, 1), lambda l, j: (0, 0)),
                  pl.BlockSpec((1, d, tn), lambda l, j: (l, 0, j)),
                  pl.BlockSpec((1, 1, tn), lambda l, j: (l, 0, j))],
        out_specs=pl.BlockSpec((1, 1, tn), lambda l, j: (l, 0, j)),
        out_shape=jax.ShapeDtypeStruct((n_layers, 1, n), F32),
        compiler_params=_params("arbitrary", "arbitrary"),
        name="ada_mod",
    )(c.reshape(d, 1), ada_w, ada_b.reshape(n_layers, 1, n))


def _rope_body(pos_ref, invf_ref, cos_ref, sin_ref):
    ang = pos_ref[...].astype(F32) * invf_ref[...]
    cos_ref[...] = jnp.cos(ang)
    sin_ref[...] = jnp.sin(ang)


def _rope_tables(positions):
    s = positions.shape[-1]
    half = ROPE // 2
    per_row = LANES // half
    inv_freq = 1.0 / (ROPE_THETA ** (jnp.arange(0, ROPE, 2, dtype=F32) / ROPE))
    pos_dense = jnp.repeat(positions.reshape(s), half).reshape(s // per_row, LANES)
    invf = jnp.tile(inv_freq, per_row).reshape(1, LANES)
    rows = s // per_row
    tr = rows // 4
    cos_d, sin_d = pl.pallas_call(
        _rope_body,
        grid=(rows // tr,),
        in_specs=[pl.BlockSpec((tr, LANES), lambda i: (i, 0)), _full((1, LANES))],
        out_specs=[pl.BlockSpec((tr, LANES), lambda i: (i, 0))] * 2,
        out_shape=[jax.ShapeDtypeStruct((rows, LANES), F32)] * 2,
        compiler_params=_params("arbitrary"),
        name="rope_tables",
    )(pos_dense, invf)
    cos = cos_d.reshape(s, half)
    sin = sin_d.reshape(s, half)
    ones = jnp.ones((s, NOPE), F32)
    zeros = jnp.zeros((s, NOPE), F32)
    pad = jnp.zeros((s, HEAD_PAD - NOPE - ROPE), F32)
    cos_t = jnp.concatenate([ones, cos, cos, pad], axis=1)
    sin_t = jnp.concatenate([zeros, sin, sin, pad], axis=1)
    return cos_t, sin_t


def _inproj_body(x_ref, g_ref, sc_ref, sh_ref, w_ref, uv_ref, cq_ref, ckv_ref, kr_ref, ga_ref, gb_ref):
    h = _rms(x_ref[...], g_ref[...]) * (1.0 + sc_ref[...]) + sh_ref[...]
    hb = h.astype(BF16)
    col = 0
    for ref in (uv_ref, cq_ref, ckv_ref, kr_ref, ga_ref, gb_ref):
        n = ref.shape[1]
        ref[...] = _dot(hb, w_ref[:, col:col + n]).astype(ref.dtype)
        col += n


def _inproj(x, g, sc, sh, w):
    s, d = x.shape
    tm = ROW_TILE
    widths = (2 * GMLP_WIDTH, Q_RANK, KV_RANK, 2 * HEAD_PAD, d, d)
    row = lambda n: pl.BlockSpec((tm, n), lambda i: (i, 0))
    return pl.pallas_call(
        _inproj_body,
        grid=(s // tm,),
        in_specs=[row(d), _full((1, d)), _full((1, d)), _full((1, d)), _full(w.shape)],
        out_specs=[row(n) for n in widths],
        out_shape=[jax.ShapeDtypeStruct((s, n), BF16) for n in widths],
        compiler_params=_params("arbitrary"),
        name="mixer_inproj",
    )(x, g, sc, sh, w)


def _gmlp_body(uv_ref, ga_ref, lng_ref, lnb_ref, ws_ref, bias_ref, wa_ref, o_ref):
    tm = uv_ref.shape[0]
    u = uv_ref[:, :GMLP_WIDTH].astype(F32)
    v = uv_ref[:, GMLP_WIDTH:].astype(F32)
    gu = _gelu(u)
    gv = _gelu(v)
    mu = jnp.mean(gv, axis=-1, keepdims=True)
    xc = gv - mu
    vn = xc * lax.rsqrt(jnp.mean(xc * xc, axis=-1, keepdims=True) + EPS) * lng_ref[...] + lnb_ref[...]
    vb = vn.astype(BF16)
    t_idx = lax.broadcasted_iota(jnp.int32, (CHUNK, CHUNK), 0)
    s_idx = lax.broadcasted_iota(jnp.int32, (CHUNK, CHUNK), 1)
    causal = s_idx <= t_idx
    ws = [jnp.where(causal, ws_ref[g], 0.0).astype(BF16) for g in range(GMLP_GROUPS)]
    left = lax.broadcasted_iota(jnp.int32, (CHUNK, LANES), 1) < GMLP_GROUP_DIM
    bias = bias_ref[...]
    z_rows = []
    for c in range(tm // CHUNK):
        vc = vb[c * CHUNK:(c + 1) * CHUNK]
        z_cols = []
        for j in range(GMLP_WIDTH // LANES):
            vp = vc[:, j * LANES:(j + 1) * LANES]
            z_cols.append(jnp.where(left, _dot(ws[2 * j], vp), _dot(ws[2 * j + 1], vp)))
        z_rows.append(jnp.concatenate(z_cols, axis=1) + bias)
    z = jnp.concatenate(z_rows, axis=0)
    gated = (gu * z).astype(BF16)
    ya = _dot(gated, wa_ref[...])
    o_ref[...] = (jax.nn.sigmoid(ga_ref[...].astype(F32)) * ya).astype(o_ref.dtype)


def _gmlp(uv, ga, lng, lnb, ws, bias, wa):
    s, d = ga.shape
    tm = ROW_TILE
    row = lambda n: pl.BlockSpec((tm, n), lambda i: (i, 0))
    return pl.pallas_call(
        _gmlp_body,
        grid=(s // tm,),
        in_specs=[row(2 * GMLP_WIDTH), row(d), _full((1, GMLP_WIDTH)), _full((1, GMLP_WIDTH)),
                  _full(ws.shape), _full(bias.shape), _full(wa.shape)],
        out_specs=row(d),
        out_shape=jax.ShapeDtypeStruct((s, d), BF16),
        compiler_params=_params("arbitrary"),
        name="mixer_gmlp",
    )(uv, ga, lng, lnb, ws, bias, wa)


def _mla_prep_body(cq_ref, ckv_ref, kr_ref, cos_ref, sin_ref, qn_ref, kvn_ref, wqm_ref, wqs_ref,
                   wk_ref, wv_ref, vone_ref, q_ref, k_ref, v_ref, *, scale):
    cos = cos_ref[...]
    sin = sin_ref[...]
    cqn = _rms(cq_ref[...].astype(F32), qn_ref[...]).astype(BF16)
    qm = _dot(cqn, wqm_ref[...])
    qs = _dot(cqn, wqs_ref[...])
    ckn = _rms(ckv_ref[...].astype(F32), kvn_ref[...]).astype(BF16)
    km = _dot(ckn, wk_ref[...])
    kr = kr_ref[...].astype(F32)
    kpe = kr[:, :HEAD_PAD] * cos + kr[:, HEAD_PAD:] * sin
    for h in range(HEADS):
        sl = slice(h * HEAD_PAD, (h + 1) * HEAD_PAD)
        q_ref[:, sl] = ((qm[:, sl] * cos + qs[:, sl] * sin) * scale).astype(q_ref.dtype)
        k_ref[:, sl] = (km[:, sl] + kpe).astype(k_ref.dtype)
    v_ref[...] = (_dot(ckn, wv_ref[...]) + vone_ref[...]).astype(v_ref.dtype)


def _mla_prep(cq, ckv, kr, cos_t, sin_t, qn, kvn, wqm, wqs, wk, wv, vone):
    s = cq.shape[0]
    tm = ROW_TILE
    width = HEADS * HEAD_PAD
    row = lambda n: pl.BlockSpec((tm, n), lambda i: (i, 0))
    scale = float((NOPE + ROPE) ** -0.5 * np.log2(np.e))
    return pl.pallas_call(
        functools.partial(_mla_prep_body, scale=scale),
        grid=(s // tm,),
        in_specs=[row(Q_RANK), row(KV_RANK), row(2 * HEAD_PAD), row(HEAD_PAD), row(HEAD_PAD),
                  _full((1, Q_RANK)), _full((1, KV_RANK)), _full(wqm.shape), _full(wqs.shape),
                  _full(wk.shape), _full(wv.shape), _full((1, width))],
        out_specs=[row(width)] * 3,
        out_shape=[jax.ShapeDtypeStruct((s, width), BF16)] * 3,
        compiler_params=_params("arbitrary"),
        name="mla_prep",
    )(cq, ckv, kr, cos_t, sin_t, qn, kvn, wqm, wqs, wk, wv, vone)


def _attn_body(q_ref, k_ref, v_ref, o_ref, s_ref, m_ref, acc_ref):
    t = q_ref.shape[0]
    heads = q_ref.shape[1] // HEAD_PAD
    qi = pl.program_id(1)
    m_ref[...] = jnp.full(m_ref.shape, NEG, F32)
    acc_ref[...] = jnp.zeros(acc_ref.shape, F32)

    def scores(blk, slot):
        off = pl.multiple_of(blk * t, t)
        for h in range(heads):
            sl = slice(h * HEAD_PAD, (h + 1) * HEAD_PAD)
            s_ref[slot, h] = lax.dot_general(q_ref[:, sl], k_ref[pl.ds(off, t), sl],
                                             (((1,), (1,)), ((), ())), preferred_element_type=F32)

    def consume(blk, slot, mask):
        off = pl.multiple_of(blk * t, t)
        for h in range(heads):
            sl = slice(h * HEAD_PAD, (h + 1) * HEAD_PAD)
            s = s_ref[slot, h]
            if mask is not None:
                s = jnp.where(mask, s, NEG)
            m_prev = m_ref[h]
            m_new = jnp.maximum(m_prev, jnp.max(s, axis=-1, keepdims=True))
            alpha = jnp.exp2(m_prev - m_new)
            p = jnp.concatenate([jnp.exp2(s[:, j * LANES:(j + 1) * LANES] - m_new) for j in range(t // LANES)],
                                axis=1).astype(BF16)
            acc_ref[h] = alpha * acc_ref[h] + _dot(p, v_ref[pl.ds(off, t), sl])
            m_ref[h] = m_new

    scores(0, 0)

    def pair(j, carry):
        blk = 2 * j
        scores(blk + 1, 1)
        consume(blk, 0, None)
        scores(blk + 2, 0)
        consume(blk + 1, 1, None)
        return carry

    lax.fori_loop(0, qi // 2, pair, 0)
    row = lax.broadcasted_iota(jnp.int32, (t, t), 0)
    col = lax.broadcasted_iota(jnp.int32, (t, t), 1)
    causal = col <= row
    odd = lax.rem(qi, 2) == 1

    @pl.when(odd)
    def _():
        scores(qi, 1)
        consume(qi - 1, 0, None)
        consume(qi, 1, causal)

    @pl.when(jnp.logical_not(odd))
    def _():
        consume(qi, 0, causal)

    for h in range(heads):
        acc = acc_ref[h]
        o_ref[:, h * HEAD_PAD:(h + 1) * HEAD_PAD] = (acc / acc[:, VDIM:VDIM + 1]).astype(o_ref.dtype)


def _attention(q, k, v):
    s, width = q.shape
    t = ATTN_TILE
    gw = ATTN_HEADS * HEAD_PAD
    return pl.pallas_call(
        _attn_body,
        grid=(width // gw, s // t),
        in_specs=[pl.BlockSpec((t, gw), lambda h, i: (i, h)),
                  pl.BlockSpec((s, gw), lambda h, i: (0, h)),
                  pl.BlockSpec((s, gw), lambda h, i: (0, h))],
        out_specs=pl.BlockSpec((t, gw), lambda h, i: (i, h)),
        out_shape=jax.ShapeDtypeStruct((s, width), BF16),
        scratch_shapes=[pltpu.VMEM((2, ATTN_HEADS, t, t), F32),
                        pltpu.VMEM((ATTN_HEADS, t, HEAD_PAD), F32),
                        pltpu.VMEM((ATTN_HEADS, t, HEAD_PAD), F32)],
        compiler_params=_params("arbitrary", "arbitrary"),
        name="mla_attention",
    )(q, k, v)


def _merge_body(a_ref, gb_ref, o_ref, x_ref, wb_ref, wo_ref, gp_ref, gm_ref, out_ref):
    yb = _dot(o_ref[...], wb_ref[...])
    merged = a_ref[...].astype(F32) + jax.nn.sigmoid(gb_ref[...].astype(F32)) * yb
    y = _dot(merged.astype(BF16), wo_ref[...])
    out_ref[...] = x_ref[...] + gm_ref[...] * _rms(y, gp_ref[...])


def _merge(a, gb, o, x, wb, wo, gp, gm):
    s, d = x.shape
    tm = ROW_TILE
    row = pl.BlockSpec((tm, d), lambda i: (i, 0))
    return pl.pallas_call(
        _merge_body,
        grid=(s // tm,),
        in_specs=[row, row, row, row, _full(wb.shape), _full(wo.shape), _full((1, d)), _full((1, d))],
        out_specs=row,
        out_shape=jax.ShapeDtypeStruct((s, d), F32),
        compiler_params=_params("arbitrary"),
        name="mixer_merge",
    )(a, gb, o, x, wb, wo, gp, gm)


def _ffn_body(x_ref, g_ref, sc_ref, sh_ref, wg_ref, wu_ref, wd_ref, gp_ref, gf_ref, o_ref, h_ref, acc_ref):
    f = pl.program_id(1)

    @pl.when(f == 0)
    def _():
        h = _rms(x_ref[...], g_ref[...]) * (1.0 + sc_ref[...]) + sh_ref[...]
        h_ref[...] = h.astype(BF16)

    hb = h_ref[...]
    a = _dot(hb, wg_ref[...])
    b = _dot(hb, wu_ref[...])
    part = _dot((a * jax.nn.sigmoid(a) * b).astype(BF16), wd_ref[...])

    @pl.when(f == 0)
    def _():
        acc_ref[...] = part

    @pl.when(f > 0)
    def _():
        acc_ref[...] += part

    @pl.when(f == pl.num_programs(1) - 1)
    def _():
        o_ref[...] = x_ref[...] + gf_ref[...] * _rms(acc_ref[...], gp_ref[...])


def _dense_ffn(x, g, sc, sh, wg, wu, wd, gp, gf, tf):
    s, d = x.shape
    ff = wg.shape[1]
    tm = ROW_TILE
    row = pl.BlockSpec((tm, d), lambda i, f: (i, 0))
    vec = pl.BlockSpec((1, d), lambda i, f: (0, 0))
    return pl.pallas_call(
        _ffn_body,
        grid=(s // tm, ff // tf),
        in_specs=[row, vec, vec, vec,
                  pl.BlockSpec((d, tf), lambda i, f: (0, f)),
                  pl.BlockSpec((d, tf), lambda i, f: (0, f)),
                  pl.BlockSpec((tf, d), lambda i, f: (f, 0)),
                  vec, vec],
        out_specs=row,
        out_shape=jax.ShapeDtypeStruct((s, d), F32),
        scratch_shapes=[pltpu.VMEM((tm, d), BF16), pltpu.VMEM((tm, d), F32)],
        compiler_params=_params("arbitrary", "arbitrary"),
        name="dense_ffn",
    )(x, g, sc, sh, wg, wu, wd, gp, gf)


def _route_body(x_ref, g_ref, sc_ref, sh_ref, wr_ref, br_ref, info_ref, offs_ref, xs_ref,
                xbuf, zbuf, carry, sem, *, region, tail):
    i = pl.program_id(0)
    nt = pl.num_programs(0)
    tm = x_ref.shape[0]

    @pl.when(i == 0)
    def _():
        for e in range(N_EXPERTS):
            carry[e] = 0

    h = _rms(x_ref[...], g_ref[...]) * (1.0 + sc_ref[...]) + sh_ref[...]
    hb = h.astype(BF16)
    h_lo = (h - hb.astype(F32)).astype(BF16)
    w = wr_ref[...]
    w_hi = w.astype(BF16)
    w_lo = (w - w_hi.astype(F32)).astype(BF16)
    logits = _dot(hb, w_hi) + (_dot(h_lo, w_hi) + _dot(hb, w_lo)) + br_ref[...]

    lane = lax.broadcasted_iota(jnp.int32, (tm, LANES), 1)
    m1 = jnp.max(logits, axis=-1, keepdims=True)
    i1 = jnp.min(jnp.where(logits == m1, lane, LANES), axis=-1, keepdims=True)
    oh1 = lane == i1
    rest = jnp.where(oh1, -3e38, logits)
    m2 = jnp.max(rest, axis=-1, keepdims=True)
    i2 = jnp.min(jnp.where(rest == m2, lane, LANES), axis=-1, keepdims=True)
    oh2 = lane == i2
    ex = jnp.exp(m2 - m1)
    g1 = 1.0 / (1.0 + ex)
    g2 = ex / (1.0 + ex)

    ohf = jnp.where(oh1 | oh2, 1.0, 0.0)
    r_idx = lax.broadcasted_iota(jnp.int32, (tm, tm), 0)
    c_idx = lax.broadcasted_iota(jnp.int32, (tm, tm), 1)
    earlier = jnp.where(c_idx < r_idx, 1.0, 0.0).astype(BF16)
    rank = _dot(earlier, ohf.astype(BF16))
    cnt = jnp.sum(ohf, axis=0, keepdims=True).astype(jnp.int32)
    cnt_al = ((cnt + (BF16_ROWS - 1)) // BF16_ROWS) * BF16_ROWS

    lane1 = lax.broadcasted_iota(jnp.int32, (1, LANES), 1)
    base = jnp.zeros((1, LANES), jnp.int32)
    for e in range(N_EXPERTS):
        base = jnp.where(lane1 == e, carry[e], base)
    pos = rank + base.astype(F32)
    pos1 = jnp.sum(jnp.where(oh1, pos, 0.0), axis=-1, keepdims=True)
    pos2 = jnp.sum(jnp.where(oh2, pos, 0.0), axis=-1, keepdims=True)
    info = jnp.where(lane == 0, i1.astype(F32),
           jnp.where(lane == 1, i2.astype(F32),
           jnp.where(lane == 2, g1,
           jnp.where(lane == 3, g2,
           jnp.where(lane == 4, pos1,
           jnp.where(lane == 5, pos2, 0.0))))))
    info_ref[...] = info

    def append(e):
        return pltpu.make_async_copy(xbuf.at[e], xs_ref.at[pl.ds(0, tm)], sem.at[e])

    slot = c_idx.astype(F32)
    for e in range(N_EXPERTS):
        hit = (rank[:, e:e + 1] == slot) & (ohf[:, e:e + 1] > 0.0)
        sel_t = jnp.where(hit, 1.0, 0.0).astype(BF16)
        xe = lax.dot_general(sel_t, hb, (((0,), (0,)), ((), ())), preferred_element_type=F32)

        @pl.when(i > 0)
        def _():
            append(e).wait()

        xbuf[e] = xe.astype(BF16)
        off = carry[e]
        offs_ref[i * N_EXPERTS + e] = off
        dst = pl.multiple_of(e * region + off, BF16_ROWS)
        pltpu.make_async_copy(xbuf.at[e], xs_ref.at[pl.ds(dst, tm)], sem.at[e]).start()
        carry[e] = off + cnt_al[0, e]

    @pl.when(i == nt - 1)
    def _():
        zbuf[...] = jnp.zeros(zbuf.shape, zbuf.dtype)
        for e in range(N_EXPERTS):
            append(e).wait()
            end = carry[e]
            offs_ref[nt * N_EXPERTS + e] = end
            dst = pl.multiple_of(e * region + end, BF16_ROWS)
            fill = pltpu.make_async_copy(zbuf, xs_ref.at[pl.ds(dst, tail)], sem.at[e])
            fill.start()
            fill.wait()


def _route(x, g, sc, sh, wr, br, region):
    s, d = x.shape
    tm = ROW_TILE
    nt = s // tm
    tail = EXPERT_ROWS
    row = pl.BlockSpec((tm, d), lambda i: (i, 0))
    return pl.pallas_call(
        functools.partial(_route_body, region=region, tail=tail),
        grid=(nt,),
        in_specs=[row, _full((1, d)), _full((1, d)), _full((1, d)), _full(wr.shape), _full(br.shape)],
        out_specs=[pl.BlockSpec((tm, LANES), lambda i: (i, 0)),
                   pl.BlockSpec(memory_space=pltpu.SMEM),
                   pl.BlockSpec(memory_space=pl.ANY)],
        out_shape=[jax.ShapeDtypeStruct((s, LANES), F32),
                   jax.ShapeDtypeStruct(((nt + 1) * N_EXPERTS,), jnp.int32),
                   jax.ShapeDtypeStruct((N_EXPERTS * region, d), BF16)],
        scratch_shapes=[pltpu.VMEM((N_EXPERTS, tm, d), BF16), pltpu.VMEM((tail, d), BF16),
                        pltpu.SMEM((N_EXPERTS,), jnp.int32), pltpu.SemaphoreType.DMA((N_EXPERTS,))],
        compiler_params=_params("arbitrary"),
        name="moe_route",
    )(x, g, sc, sh, wr, br)


def _expert_body(brow_ref, bexp_ref, nval_ref, x_ref, w1_ref, w3_ref, w2_ref, o_ref, acc_ref):
    i = pl.program_id(0)
    f = pl.program_id(1)

    @pl.when(i < nval_ref[0])
    def _():
        xb = x_ref[...]
        a = _dot(xb, w1_ref[0])
        b = _dot(xb, w3_ref[0])
        part = _dot((a * jax.nn.sigmoid(a) * b).astype(BF16), w2_ref[0])

        @pl.when(f == 0)
        def _():
            acc_ref[...] = part

        @pl.when(f > 0)
        def _():
            acc_ref[...] += part

        @pl.when(f == pl.num_programs(1) - 1)
        def _():
            o_ref[...] = acc_ref[...].astype(o_ref.dtype)


def _experts(xs, w1, w3, w2, brow, bexp, nval, tf):
    rows, d = xs.shape
    ff = w1.shape[2]
    tb = EXPERT_ROWS
    nff = ff // tf
    nb = brow.shape[0]

    def fcol(i, f, nval):
        return jnp.where(i < nval[0], f, nff - 1)

    grid_spec = pltpu.PrefetchScalarGridSpec(
        num_scalar_prefetch=3,
        grid=(nb, nff),
        in_specs=[pl.BlockSpec((tb, d), lambda i, f, br, be, nv: (br[i], 0)),
                  pl.BlockSpec((1, d, tf), lambda i, f, br, be, nv: (be[i], 0, fcol(i, f, nv))),
                  pl.BlockSpec((1, d, tf), lambda i, f, br, be, nv: (be[i], 0, fcol(i, f, nv))),
                  pl.BlockSpec((1, tf, d), lambda i, f, br, be, nv: (be[i], fcol(i, f, nv), 0))],
        out_specs=pl.BlockSpec((tb, d), lambda i, f, br, be, nv: (br[i], 0)),
        scratch_shapes=[pltpu.VMEM((tb, d), F32)],
    )
    return pl.pallas_call(
        _expert_body,
        grid_spec=grid_spec,
        out_shape=jax.ShapeDtypeStruct((rows, d), BF16),
        compiler_params=_params("arbitrary", "arbitrary"),
        name="moe_experts",
    )(brow, bexp, nval, xs, w1, w3, w2)


def _combine_body(offs_ref, info_ref, x_ref, gp_ref, gf_ref, ys_ref, o_ref, ybuf, sem, *, region):
    i = pl.program_id(0)
    tm = x_ref.shape[0]
    starts = []
    for e in range(N_EXPERTS):
        end = offs_ref[(i + 1) * N_EXPERTS + e]
        start = pl.multiple_of(jnp.maximum(end - tm, 0), BF16_ROWS)
        starts.append(start)
        src = pl.multiple_of(e * region + start, BF16_ROWS)
        pltpu.make_async_copy(ys_ref.at[pl.ds(src, tm)], ybuf.at[e], sem.at[e]).start()

    info = info_ref[...]
    i1 = info[:, 0:1]
    i2 = info[:, 1:2]
    g1 = info[:, 2:3]
    g2 = info[:, 3:4]
    pos1 = info[:, 4:5]
    pos2 = info[:, 5:6]
    slot = lax.broadcasted_iota(jnp.int32, (tm, tm), 1).astype(F32)
    acc = jnp.zeros(x_ref.shape, F32)
    for e in range(N_EXPERTS):
        r1 = i1 == float(e)
        r2 = i2 == float(e)
        local = jnp.where(r1, pos1, pos2) - starts[e].astype(F32)
        gate = jnp.where(r1, g1, jnp.where(r2, g2, 0.0))
        sel = jnp.where((local == slot) & (r1 | r2), 1.0, 0.0).astype(BF16)
        pltpu.make_async_copy(ys_ref.at[pl.ds(0, tm)], ybuf.at[e], sem.at[e]).wait()
        acc = acc + gate * _dot(sel, ybuf[e])
    o_ref[...] = x_ref[...] + gf_ref[...] * _rms(acc, gp_ref[...])


def _combine(offs, info, x, gp, gf, ys, region):
    s, d = x.shape
    tm = ROW_TILE
    grid_spec = pltpu.PrefetchScalarGridSpec(
        num_scalar_prefetch=1,
        grid=(s // tm,),
        in_specs=[pl.BlockSpec((tm, LANES), lambda i, o: (i, 0)),
                  pl.BlockSpec((tm, d), lambda i, o: (i, 0)),
                  pl.BlockSpec((1, d), lambda i, o: (0, 0)),
                  pl.BlockSpec((1, d), lambda i, o: (0, 0)),
                  pl.BlockSpec(memory_space=pl.ANY)],
        out_specs=pl.BlockSpec((tm, d), lambda i, o: (i, 0)),
        scratch_shapes=[pltpu.VMEM((N_EXPERTS, tm, d), BF16), pltpu.SemaphoreType.DMA((N_EXPERTS,))],
    )
    return pl.pallas_call(
        functools.partial(_combine_body, region=region),
        grid_spec=grid_spec,
        out_shape=jax.ShapeDtypeStruct((s, d), F32),
        compiler_params=_params("arbitrary"),
        name="moe_combine",
    )(offs, info, x, gp, gf, ys)


def _rot_half_cols(w):
    half = ROPE // 2
    return jnp.concatenate([-w[..., half:], w[..., :half]], axis=-1)


def _pad_cols(w, before, total):
    return jnp.pad(w, ((0, 0), (before, total - before - w.shape[1])))


def _mixer_weights(w_in, w_uq, w_ukv, w_branch_b):
    d = w_in.shape[0]
    o = 0
    parts = {}
    for name, n in (("u", GMLP_WIDTH), ("v", GMLP_WIDTH), ("cq", Q_RANK), ("ckv", KV_RANK),
                    ("kr", ROPE), ("ga", d), ("gb", d)):
        parts[name] = w_in[:, o:o + n]
        o += n
    kr_main = _pad_cols(parts["kr"], NOPE, HEAD_PAD)
    kr_swap = _pad_cols(_rot_half_cols(parts["kr"]), NOPE, HEAD_PAD)
    w_in_p = jnp.concatenate([parts["u"], parts["v"], parts["cq"], parts["ckv"], kr_main, kr_swap,
                              parts["ga"], parts["gb"]], axis=1).astype(BF16)

    wq = w_uq.reshape(Q_RANK, HEADS, NOPE + ROPE)
    zq = jnp.zeros((Q_RANK, HEADS, HEAD_PAD - NOPE - ROPE), w_uq.dtype)
    wq_main = jnp.concatenate([wq, zq], axis=-1).reshape(Q_RANK, HEADS * HEAD_PAD).astype(BF16)
    wq_swap = jnp.concatenate([jnp.zeros((Q_RANK, HEADS, NOPE), w_uq.dtype),
                               _rot_half_cols(wq[..., NOPE:]), zq], axis=-1)
    wq_swap = wq_swap.reshape(Q_RANK, HEADS * HEAD_PAD).astype(BF16)

    wkv = w_ukv.reshape(KV_RANK, HEADS, NOPE + VDIM)
    zk = jnp.zeros((KV_RANK, HEADS, HEAD_PAD - NOPE), w_ukv.dtype)
    wk = jnp.concatenate([wkv[..., :NOPE], zk], axis=-1).reshape(KV_RANK, HEADS * HEAD_PAD).astype(BF16)
    zv = jnp.zeros((KV_RANK, HEADS, HEAD_PAD - VDIM), w_ukv.dtype)
    wv = jnp.concatenate([wkv[..., NOPE:], zv], axis=-1).reshape(KV_RANK, HEADS * HEAD_PAD).astype(BF16)

    wb = w_branch_b.reshape(HEADS, VDIM, d)
    wb = jnp.concatenate([wb, jnp.zeros((HEADS, HEAD_PAD - VDIM, d), wb.dtype)], axis=1)
    wb = wb.reshape(HEADS * HEAD_PAD, d).astype(BF16)
    return w_in_p, wq_main, wq_swap, wk, wv, wb


def _expert_blocks(ends, region, nb):
    tb = EXPERT_ROWS
    per = jnp.maximum((ends + tb - 1) // tb, 1)
    stop = jnp.cumsum(per)
    total = stop[-1]
    step = jnp.minimum(jnp.arange(nb, dtype=jnp.int32), total - 1)
    e = jnp.minimum(jnp.searchsorted(stop, step, side="right"), N_EXPERTS - 1).astype(jnp.int32)
    j = step - (stop - per)[e]
    brow = e * (region // tb) + j
    return brow.astype(jnp.int32), e, total.astype(jnp.int32).reshape(1)


def kernel(x, c, positions, ada_w, ada_b, norm_mix_pre, norm_mix_post, norm_ffn_pre, norm_ffn_post, w_in, gmlp_ln_g, gmlp_ln_b, gmlp_ws, gmlp_bs, mla_q_norm, mla_w_uq, mla_kv_norm, mla_w_ukv, w_branch_a, w_branch_b, w_out, ffn_w_gate, ffn_w_up, ffn_w_down, moe_router, moe_router_bias, moe_w1, moe_w3, moe_w2):
    batch, s, d = x.shape
    assert batch == 1 and s % ROW_TILE == 0 and s % ATTN_TILE == 0
    depth = ada_w.shape[0]
    xs = x.reshape(s, d)
    mod = _ada_mod(c, ada_w, ada_b)
    cos_t, sin_t = _rope_tables(positions)
    vone = jnp.zeros((HEADS, HEAD_PAD), F32).at[:, VDIM].set(1.0).reshape(1, HEADS * HEAD_PAD)
    row = lambda v: v.reshape(1, -1)

    for l in range(depth):
        sh_m, sc_m, g_m, sh_f, sc_f, g_f = [mod[l, :, k * d:(k + 1) * d] for k in range(6)]
        w_in_p, wq_main, wq_swap, wk, wv, wb = _mixer_weights(w_in[l], mla_w_uq[l], mla_w_ukv[l], w_branch_b[l])
        uv, cq, ckv, kr, ga, gb = _inproj(xs, row(norm_mix_pre[l]), sc_m, sh_m, w_in_p)
        bias = jnp.repeat(gmlp_bs[l].T, GMLP_GROUP_DIM, axis=1)
        a = _gmlp(uv, ga, row(gmlp_ln_g[l]), row(gmlp_ln_b[l]), gmlp_ws[l], bias, w_branch_a[l].astype(BF16))
        q, k, v = _mla_prep(cq, ckv, kr, cos_t, sin_t, row(mla_q_norm[l]), row(mla_kv_norm[l]),
                            wq_main, wq_swap, wk, wv, vone)
        o = _attention(q, k, v)
        xs = _merge(a, gb, o, xs, wb, w_out[l].astype(BF16), row(norm_mix_post[l]), g_m)

        j = l // 2
        if l % 2 == 0:
            ff = ffn_w_gate.shape[2]
            ff_pad = -(-ff // (2 * LANES)) * (2 * LANES)
            wg = jnp.pad(ffn_w_gate[j], ((0, 0), (0, ff_pad - ff))).astype(BF16)
            wu = jnp.pad(ffn_w_up[j], ((0, 0), (0, ff_pad - ff))).astype(BF16)
            wd = jnp.pad(ffn_w_down[j], ((0, ff_pad - ff), (0, 0))).astype(BF16)
            xs = _dense_ffn(xs, row(norm_ffn_pre[l]), sc_f, sh_f, wg, wu, wd, row(norm_ffn_post[l]), g_f,
                            tf=ff_pad // 2)
        else:
            nt = s // ROW_TILE
            region = s + nt * BF16_ROWS + max(ROW_TILE, EXPERT_ROWS)
            region = -(-region // EXPERT_ROWS) * EXPERT_ROWS
            wr = jnp.pad(moe_router[j], ((0, 0), (0, LANES - N_EXPERTS)))
            br = jnp.pad(moe_router_bias[j], (0, LANES - N_EXPERTS), constant_values=NEG).reshape(1, LANES)
            info, offs, xsort = _route(xs, row(norm_ffn_pre[l]), sc_f, sh_f, wr, br, region)
            nb = (2 * s + nt * N_EXPERTS * (BF16_ROWS - 1)) // EXPERT_ROWS + N_EXPERTS + 1
            brow, bexp, nval = _expert_blocks(offs[nt * N_EXPERTS:], region, nb)
            ysort = _experts(xsort, moe_w1[j].astype(BF16), moe_w3[j].astype(BF16), moe_w2[j].astype(BF16),
                             brow, bexp, nval, tf=moe_w1.shape[3] // 4)
            xs = _combine(offs, info, xs, row(norm_ffn_post[l]), g_f, ysort, region)
    return xs.reshape(batch, s, d)
```

```python
import functools

import jax
import jax.numpy as jnp
import numpy as np
from jax import lax
from jax.experimental import pallas as pl
from jax.experimental.pallas import tpu as pltpu

F32 = jnp.float32
BF16 = jnp.bfloat16

EPS = 1e-6
LANES = 128
BF16_ROWS = 16
GMLP_GROUPS = 8
GMLP_GROUP_DIM = 64
GMLP_WIDTH = GMLP_GROUPS * GMLP_GROUP_DIM
CHUNK = 128
HEADS = 8
NOPE = 64
ROPE = 32
VDIM = 64
HEAD_PAD = 128
Q_RANK = 384
KV_RANK = 256
ROPE_THETA = 10000.0
N_EXPERTS = 8
NEG = -1e30

ROW_TILE = 512
ATTN_TILE = 512
ATTN_HEADS = 2
EXPERT_ROWS = 512
DISPATCH_ROWS = 128
COMBINE_ROWS = 256
REGION_SLACK = DISPATCH_ROWS - BF16_ROWS
VMEM_LIMIT = 52 * 1024 * 1024


def _params(*sem):
    return pltpu.CompilerParams(dimension_semantics=sem, vmem_limit_bytes=VMEM_LIMIT)


def _dot(a, b):
    return jnp.dot(a, b, preferred_element_type=F32)


def _rms(x, g):
    return x * lax.rsqrt(jnp.mean(x * x, axis=-1, keepdims=True) + EPS) * g


def _gelu(x):
    return 0.5 * x * (1.0 + lax.erf(x * np.float32(0.7071067811865476)))


def _full(shape):
    return pl.BlockSpec(shape, lambda *_: (0,) * len(shape), pipeline_mode=pl.Buffered(1))


def _mod_body(c_ref, w_ref, b_ref, o_ref):
    c = c_ref[...]
    ca = c * jax.nn.sigmoid(c)
    o_ref[0] = jnp.sum(ca * w_ref[0], axis=0, keepdims=True) + b_ref[0]


def _ada_mod(c, ada_w, ada_b):
    n_layers, d, n = ada_w.shape
    tn = n // 4
    return pl.pallas_call(
        _mod_body,
        grid=(n_layers, n // tn),
        in_specs=[pl.BlockSpec((d, 1), lambda l, j: (0, 0)),
                  pl.BlockSpec((1, d, tn), lambda l, j: (l, 0, j)),
                  pl.BlockSpec((1, 1, tn), lambda l, j: (l, 0, j))],
        out_specs=pl.BlockSpec((1, 1, tn), lambda l, j: (l, 0, j)),
        out_shape=jax.ShapeDtypeStruct((n_layers, 1, n), F32),
        compiler_params=_params("arbitrary", "arbitrary"),
        name="ada_mod",
    )(c.reshape(d, 1), ada_w, ada_b.reshape(n_layers, 1, n))


def _rope_body(pos_ref, invf_ref, cos_ref, sin_ref):
    ang = pos_ref[...].astype(F32) * invf_ref[...]
    cos_ref[...] = jnp.cos(ang)
    sin_ref[...] = jnp.sin(ang)


def _rope_tables(positions):
    s = positions.shape[-1]
    half = ROPE // 2
    per_row = LANES // half
    inv_freq = 1.0 / (ROPE_THETA ** (jnp.arange(0, ROPE, 2, dtype=F32) / ROPE))
    pos_dense = jnp.repeat(positions.reshape(s), half).reshape(s // per_row, LANES)
    invf = jnp.tile(inv_freq, per_row).reshape(1, LANES)
    rows = s // per_row
    tr = rows // 4
    cos_d, sin_d = pl.pallas_call(
        _rope_body,
        grid=(rows // tr,),
        in_specs=[pl.BlockSpec((tr, LANES), lambda i: (i, 0)), _full((1, LANES))],
        out_specs=[pl.BlockSpec((tr, LANES), lambda i: (i, 0))] * 2,
        out_shape=[jax.ShapeDtypeStruct((rows, LANES), F32)] * 2,
        compiler_params=_params("arbitrary"),
        name="rope_tables",
    )(pos_dense, invf)
    cos = cos_d.reshape(s, half)
    sin = sin_d.reshape(s, half)
    ones = jnp.ones((s, NOPE), F32)
    zeros = jnp.zeros((s, NOPE), F32)
    pad = jnp.zeros((s, HEAD_PAD - NOPE - ROPE), F32)
    cos_t = jnp.concatenate([ones, cos, cos, pad], axis=1)
    sin_t = jnp.concatenate([zeros, sin, sin, pad], axis=1)
    return cos_t, sin_t


def _inproj_body(x_ref, g_ref, sc_ref, sh_ref, w_ref, uv_ref, cq_ref, ckv_ref, kr_ref, ga_ref, gb_ref):
    h = _rms(x_ref[...], g_ref[...]) * (1.0 + sc_ref[...]) + sh_ref[...]
    proj = _dot(h.astype(BF16), w_ref[...])
    col = 0
    for ref in (uv_ref, cq_ref, ckv_ref, kr_ref, ga_ref, gb_ref):
        n = ref.shape[1]
        ref[...] = proj[:, col:col + n].astype(ref.dtype)
        col += n


def _inproj(x, g, sc, sh, w):
    s, d = x.shape
    tm = ROW_TILE
    widths = (2 * GMLP_WIDTH, Q_RANK, KV_RANK, 2 * HEAD_PAD, d, d)
    row = lambda n: pl.BlockSpec((tm, n), lambda i: (i, 0))
    return pl.pallas_call(
        _inproj_body,
        grid=(s // tm,),
        in_specs=[row(d), _full((1, d)), _full((1, d)), _full((1, d)), _full(w.shape)],
        out_specs=[row(n) for n in widths],
        out_shape=[jax.ShapeDtypeStruct((s, n), BF16) for n in widths],
        compiler_params=_params("arbitrary"),
        name="mixer_inproj",
    )(x, g, sc, sh, w)


def _gmlp_body(uv_ref, ga_ref, lng_ref, lnb_ref, ws_ref, bias_ref, wa_ref, o_ref):
    tm = uv_ref.shape[0]
    u = uv_ref[:, :GMLP_WIDTH].astype(F32)
    v = uv_ref[:, GMLP_WIDTH:].astype(F32)
    gu = _gelu(u)
    gv = _gelu(v)
    mu = jnp.mean(gv, axis=-1, keepdims=True)
    xc = gv - mu
    vn = xc * lax.rsqrt(jnp.mean(xc * xc, axis=-1, keepdims=True) + EPS) * lng_ref[...] + lnb_ref[...]
    vb = vn.astype(BF16)
    t_idx = lax.broadcasted_iota(jnp.int32, (CHUNK, CHUNK), 0)
    s_idx = lax.broadcasted_iota(jnp.int32, (CHUNK, CHUNK), 1)
    causal = s_idx <= t_idx
    ws = [jnp.where(causal, ws_ref[g], 0.0).astype(BF16) for g in range(GMLP_GROUPS)]
    left = lax.broadcasted_iota(jnp.int32, (CHUNK, LANES), 1) < GMLP_GROUP_DIM
    bias = bias_ref[...]
    z_rows = []
    for c in range(tm // CHUNK):
        vc = vb[c * CHUNK:(c + 1) * CHUNK]
        z_cols = []
        for j in range(GMLP_WIDTH // LANES):
            vp = vc[:, j * LANES:(j + 1) * LANES]
            z_cols.append(jnp.where(left, _dot(ws[2 * j], vp), _dot(ws[2 * j + 1], vp)))
        z_rows.append(jnp.concatenate(z_cols, axis=1) + bias)
    z = jnp.concatenate(z_rows, axis=0)
    gated = (gu * z).astype(BF16)
    ya = _dot(gated, wa_ref[...])
    o_ref[...] = (jax.nn.sigmoid(ga_ref[...].astype(F32)) * ya).astype(o_ref.dtype)


def _gmlp(uv, ga, lng, lnb, ws, bias, wa):
    s, d = ga.shape
    tm = ROW_TILE
    row = lambda n: pl.BlockSpec((tm, n), lambda i: (i, 0))
    return pl.pallas_call(
        _gmlp_body,
        grid=(s // tm,),
        in_specs=[row(2 * GMLP_WIDTH), row(d), _full((1, GMLP_WIDTH)), _full((1, GMLP_WIDTH)),
                  _full(ws.shape), _full(bias.shape), _full(wa.shape)],
        out_specs=row(d),
        out_shape=jax.ShapeDtypeStruct((s, d), BF16),
        compiler_params=_params("arbitrary"),
        name="mixer_gmlp",
    )(uv, ga, lng, lnb, ws, bias, wa)


def _mla_prep_body(cq_ref, ckv_ref, kr_ref, cos_ref, sin_ref, qn_ref, kvn_ref, wqm_ref, wqs_ref,
                   wk_ref, wv_ref, vone_ref, q_ref, k_ref, v_ref, *, scale):
    cos = cos_ref[...]
    sin = sin_ref[...]
    cqn = _rms(cq_ref[...].astype(F32), qn_ref[...]).astype(BF16)
    qm = _dot(cqn, wqm_ref[...])
    qs = _dot(cqn, wqs_ref[...])
    ckn = _rms(ckv_ref[...].astype(F32), kvn_ref[...]).astype(BF16)
    km = _dot(ckn, wk_ref[...])
    kr = kr_ref[...].astype(F32)
    kpe = kr[:, :HEAD_PAD] * cos + kr[:, HEAD_PAD:] * sin
    for h in range(HEADS):
        sl = slice(h * HEAD_PAD, (h + 1) * HEAD_PAD)
        q_ref[:, sl] = ((qm[:, sl] * cos + qs[:, sl] * sin) * scale).astype(q_ref.dtype)
        k_ref[:, sl] = (km[:, sl] + kpe).astype(k_ref.dtype)
    v_ref[...] = (_dot(ckn, wv_ref[...]) + vone_ref[...]).astype(v_ref.dtype)


def _mla_prep(cq, ckv, kr, cos_t, sin_t, qn, kvn, wqm, wqs, wk, wv, vone):
    s = cq.shape[0]
    tm = ROW_TILE
    width = HEADS * HEAD_PAD
    row = lambda n: pl.BlockSpec((tm, n), lambda i: (i, 0))
    scale = float((NOPE + ROPE) ** -0.5 * np.log2(np.e))
    return pl.pallas_call(
        functools.partial(_mla_prep_body, scale=scale),
        grid=(s // tm,),
        in_specs=[row(Q_RANK), row(KV_RANK), row(2 * HEAD_PAD), row(HEAD_PAD), row(HEAD_PAD),
                  _full((1, Q_RANK)), _full((1, KV_RANK)), _full(wqm.shape), _full(wqs.shape),
                  _full(wk.shape), _full(wv.shape), _full((1, width))],
        out_specs=[row(width)] * 3,
        out_shape=[jax.ShapeDtypeStruct((s, width), BF16)] * 3,
        compiler_params=_params("arbitrary"),
        name="mla_prep",
    )(cq, ckv, kr, cos_t, sin_t, qn, kvn, wqm, wqs, wk, wv, vone)


def _attn_body(q_ref, k_ref, v_ref, o_ref, s_ref, m_ref, acc_ref):
    t = q_ref.shape[0]
    heads = q_ref.shape[1] // HEAD_PAD
    qi = pl.program_id(1)
    m_ref[...] = jnp.full(m_ref.shape, NEG, F32)
    acc_ref[...] = jnp.zeros(acc_ref.shape, F32)

    def scores(blk, slot):
        off = pl.multiple_of(blk * t, t)
        for h in range(heads):
            sl = slice(h * HEAD_PAD, (h + 1) * HEAD_PAD)
            s_ref[slot, h] = lax.dot_general(q_ref[:, sl], k_ref[pl.ds(off, t), sl],
                                             (((1,), (1,)), ((), ())), preferred_element_type=F32)

    def consume(blk, slot, mask):
        off = pl.multiple_of(blk * t, t)
        for h in range(heads):
            sl = slice(h * HEAD_PAD, (h + 1) * HEAD_PAD)
            s = s_ref[slot, h]
            if mask is not None:
                s = jnp.where(mask, s, NEG)
            m_prev = m_ref[h]
            m_new = jnp.maximum(m_prev, jnp.max(s, axis=-1, keepdims=True))
            alpha = jnp.exp2(m_prev - m_new)
            p = jnp.concatenate([jnp.exp2(s[:, j * LANES:(j + 1) * LANES] - m_new) for j in range(t // LANES)],
                                axis=1).astype(BF16)
            acc_ref[h] = alpha * acc_ref[h] + _dot(p, v_ref[pl.ds(off, t), sl])
            m_ref[h] = m_new

    scores(0, 0)

    def pair(j, carry):
        blk = 2 * j
        scores(blk + 1, 1)
        consume(blk, 0, None)
        scores(blk + 2, 0)
        consume(blk + 1, 1, None)
        return carry

    lax.fori_loop(0, qi // 2, pair, 0)
    row = lax.broadcasted_iota(jnp.int32, (t, t), 0)
    col = lax.broadcasted_iota(jnp.int32, (t, t), 1)
    causal = col <= row
    odd = lax.rem(qi, 2) == 1

    @pl.when(odd)
    def _():
        scores(qi, 1)
        consume(qi - 1, 0, None)
        consume(qi, 1, causal)

    @pl.when(jnp.logical_not(odd))
    def _():
        consume(qi, 0, causal)

    for h in range(heads):
        acc = acc_ref[h]
        o_ref[:, h * HEAD_PAD:(h + 1) * HEAD_PAD] = (acc / acc[:, VDIM:VDIM + 1]).astype(o_ref.dtype)


def _attention(q, k, v):
    s, width = q.shape
    t = ATTN_TILE
    gw = ATTN_HEADS * HEAD_PAD
    return pl.pallas_call(
        _attn_body,
        grid=(width // gw, s // t),
        in_specs=[pl.BlockSpec((t, gw), lambda h, i: (i, h)),
                  pl.BlockSpec((s, gw), lambda h, i: (0, h)),
                  pl.BlockSpec((s, gw), lambda h, i: (0, h))],
        out_specs=pl.BlockSpec((t, gw), lambda h, i: (i, h)),
        out_shape=jax.ShapeDtypeStruct((s, width), BF16),
        scratch_shapes=[pltpu.VMEM((2, ATTN_HEADS, t, t), F32),
                        pltpu.VMEM((ATTN_HEADS, t, HEAD_PAD), F32),
                        pltpu.VMEM((ATTN_HEADS, t, HEAD_PAD), F32)],
        compiler_params=_params("arbitrary", "arbitrary"),
        name="mla_attention",
    )(q, k, v)


def _merge_body(a_ref, gb_ref, o_ref, x_ref, wb_ref, wo_ref, gp_ref, gm_ref, out_ref):
    yb = _dot(o_ref[...], wb_ref[...])
    merged = a_ref[...].astype(F32) + jax.nn.sigmoid(gb_ref[...].astype(F32)) * yb
    y = _dot(merged.astype(BF16), wo_ref[...])
    out_ref[...] = x_ref[...] + gm_ref[...] * _rms(y, gp_ref[...])


def _merge(a, gb, o, x, wb, wo, gp, gm):
    s, d = x.shape
    tm = ROW_TILE
    row = pl.BlockSpec((tm, d), lambda i: (i, 0))
    return pl.pallas_call(
        _merge_body,
        grid=(s // tm,),
        in_specs=[row, row, row, row, _full(wb.shape), _full(wo.shape), _full((1, d)), _full((1, d))],
        out_specs=row,
        out_shape=jax.ShapeDtypeStruct((s, d), F32),
        compiler_params=_params("arbitrary"),
        name="mixer_merge",
    )(a, gb, o, x, wb, wo, gp, gm)


def _ffn_body(x_ref, g_ref, sc_ref, sh_ref, wg_ref, wu_ref, wd_ref, gp_ref, gf_ref, o_ref):
    x = x_ref[...]
    hb = (_rms(x, g_ref[...]) * (1.0 + sc_ref[...]) + sh_ref[...]).astype(BF16)
    a = _dot(hb, wg_ref[...])
    b = _dot(hb, wu_ref[...])
    y = _dot((a * jax.nn.sigmoid(a) * b).astype(BF16), wd_ref[...])
    o_ref[...] = x + gf_ref[...] * _rms(y, gp_ref[...])


def _dense_ffn(x, g, sc, sh, wg, wu, wd, gp, gf):
    s, d = x.shape
    tm = ROW_TILE
    row = pl.BlockSpec((tm, d), lambda i: (i, 0))
    vec = _full((1, d))
    return pl.pallas_call(
        _ffn_body,
        grid=(s // tm,),
        in_specs=[row, vec, vec, vec, _full(wg.shape), _full(wu.shape), _full(wd.shape), vec, vec],
        out_specs=row,
        out_shape=jax.ShapeDtypeStruct((s, d), F32),
        compiler_params=_params("arbitrary"),
        name="dense_ffn",
    )(x, g, sc, sh, wg, wu, wd, gp, gf)


def _route_body(x_ref, g_ref, sc_ref, sh_ref, wr_ref, br_ref, hb_ref, info_ref, cnt_ref):
    i = pl.program_id(0)
    tm = x_ref.shape[0]
    h = _rms(x_ref[...], g_ref[...]) * (1.0 + sc_ref[...]) + sh_ref[...]
    hb = h.astype(BF16)
    hb_ref[...] = hb
    h_lo = (h - hb.astype(F32)).astype(BF16)
    w = wr_ref[...]
    w_hi = w.astype(BF16)
    w_lo = (w - w_hi.astype(F32)).astype(BF16)
    logits = _dot(hb, w_hi) + (_dot(h_lo, w_hi) + _dot(hb, w_lo)) + br_ref[...]

    lane = lax.broadcasted_iota(jnp.int32, (tm, LANES), 1)
    m1 = jnp.max(logits, axis=-1, keepdims=True)
    i1 = jnp.min(jnp.where(logits == m1, lane, LANES), axis=-1, keepdims=True)
    oh1 = lane == i1
    rest = jnp.where(oh1, -3e38, logits)
    m2 = jnp.max(rest, axis=-1, keepdims=True)
    i2 = jnp.min(jnp.where(rest == m2, lane, LANES), axis=-1, keepdims=True)
    oh2 = lane == i2
    ex = jnp.exp(m2 - m1)
    g1 = 1.0 / (1.0 + ex)
    g2 = ex / (1.0 + ex)

    ohf = jnp.where(oh1 | oh2, 1.0, 0.0)
    r_idx = lax.broadcasted_iota(jnp.int32, (tm, tm), 0)
    c_idx = lax.broadcasted_iota(jnp.int32, (tm, tm), 1)
    earlier = jnp.where(c_idx < r_idx, 1.0, 0.0).astype(BF16)
    rank = _dot(earlier, ohf.astype(BF16))
    rank1 = jnp.sum(jnp.where(oh1, rank, 0.0), axis=-1, keepdims=True)
    rank2 = jnp.sum(jnp.where(oh2, rank, 0.0), axis=-1, keepdims=True)
    info_ref[...] = jnp.where(lane == 0, i1.astype(F32),
                    jnp.where(lane == 1, i2.astype(F32),
                    jnp.where(lane == 2, g1,
                    jnp.where(lane == 3, g2,
                    jnp.where(lane == 4, rank1,
                    jnp.where(lane == 5, rank2, 0.0))))))
    cnt = jnp.sum(ohf, axis=0, keepdims=True).astype(jnp.int32)
    cnt_al = ((cnt + (BF16_ROWS - 1)) // BF16_ROWS) * BF16_ROWS
    for e in range(N_EXPERTS):
        cnt_ref[i * N_EXPERTS + e] = cnt_al[0, e]


def _route(x, g, sc, sh, wr, br):
    s, d = x.shape
    tm = ROW_TILE
    nt = s // tm
    row = pl.BlockSpec((tm, d), lambda i: (i, 0))
    return pl.pallas_call(
        _route_body,
        grid=(nt,),
        in_specs=[row, _full((1, d)), _full((1, d)), _full((1, d)), _full(wr.shape), _full(br.shape)],
        out_specs=[row,
                   pl.BlockSpec((tm, LANES), lambda i: (i, 0)),
                   pl.BlockSpec(memory_space=pltpu.SMEM)],
        out_shape=[jax.ShapeDtypeStruct((s, d), BF16),
                   jax.ShapeDtypeStruct((s, LANES), F32),
                   jax.ShapeDtypeStruct((nt * N_EXPERTS,), jnp.int32)],
        compiler_params=_params("arbitrary"),
        name="moe_route",
    )(x, g, sc, sh, wr, br)


def _routing(info):
    return [(info[:, k:k + 1], info[:, 2 + k:3 + k], info[:, 4 + k:5 + k]) for k in range(2)]


def _dispatch_body(offs_ref, cnts_ref, fill_ref, nval_ref, hb_ref, info_ref, xs_ref, xbuf, zbuf, sem, zsem, *, nblk):
    i = pl.program_id(0)
    tm = hb_ref.shape[0]
    tb = zbuf.shape[0]

    def zero_block(blk):
        cp = pltpu.make_async_copy(zbuf, xs_ref.at[pl.ds(pl.multiple_of(blk * tb, tb), tb)], zsem)
        cp.start()
        cp.wait()

    @pl.when(i == 0)
    def _():
        zbuf[...] = jnp.zeros(zbuf.shape, zbuf.dtype)
        for n in range(fill_ref.shape[0]):
            @pl.when(fill_ref[n] >= 0)
            def _():
                zero_block(fill_ref[n])

        def unused(blk, carry):
            zero_block(blk)
            return carry

        lax.fori_loop(nval_ref[0], nblk, unused, 0)

    hb = hb_ref[...]
    (e1, _, r1), (e2, _, r2) = _routing(info_ref[...])
    slot = lax.broadcasted_iota(jnp.int32, (tm, DISPATCH_ROWS), 1).astype(F32)
    chunks = tm // DISPATCH_ROWS

    def copy(e, c, dst):
        return pltpu.make_async_copy(xbuf.at[e, c], xs_ref.at[pl.ds(dst, DISPATCH_ROWS)], sem.at[e, c])

    for e in range(N_EXPERTS):
        cnt = cnts_ref[i * N_EXPERTS + e]
        off = offs_ref[i * N_EXPERTS + e]
        first = e1 == float(e)
        routed = first | (e2 == float(e))
        rank = jnp.where(first, r1, r2)
        for c in range(chunks):
            @pl.when(cnt > c * DISPATCH_ROWS)
            def _():
                hit = (rank == slot + float(c * DISPATCH_ROWS)) & routed
                sel_t = jnp.where(hit, 1.0, 0.0).astype(BF16)
                rows = lax.dot_general(sel_t, hb, (((0,), (0,)), ((), ())), preferred_element_type=F32)
                xbuf[e, c] = rows.astype(BF16)
                copy(e, c, pl.multiple_of(off + c * DISPATCH_ROWS, BF16_ROWS)).start()

    for e in range(N_EXPERTS):
        cnt = cnts_ref[i * N_EXPERTS + e]
        for c in range(chunks):
            @pl.when(cnt > c * DISPATCH_ROWS)
            def _():
                copy(e, c, 0).wait()


def _dispatch(offs, cnts, fill, nval, hb, info, nblk):
    s, d = hb.shape
    tm = ROW_TILE
    tb = EXPERT_ROWS
    grid_spec = pltpu.PrefetchScalarGridSpec(
        num_scalar_prefetch=4,
        grid=(s // tm,),
        in_specs=[pl.BlockSpec((tm, d), lambda i, *_: (i, 0)),
                  pl.BlockSpec((tm, LANES), lambda i, *_: (i, 0))],
        out_specs=pl.BlockSpec(memory_space=pl.ANY),
        scratch_shapes=[pltpu.VMEM((N_EXPERTS, tm // DISPATCH_ROWS, DISPATCH_ROWS, d), BF16),
                        pltpu.VMEM((tb, d), BF16),
                        pltpu.SemaphoreType.DMA((N_EXPERTS, tm // DISPATCH_ROWS)),
                        pltpu.SemaphoreType.DMA(())],
    )
    return pl.pallas_call(
        functools.partial(_dispatch_body, nblk=nblk),
        grid_spec=grid_spec,
        out_shape=jax.ShapeDtypeStruct((nblk * tb, d), BF16),
        compiler_params=_params("arbitrary"),
        name="moe_dispatch",
    )(offs, cnts, fill, nval, hb, info)


def _expert_body(bexp_ref, nval_ref, x_ref, w1_ref, w3_ref, w2_ref, o_ref, acc_ref):
    i = pl.program_id(0)
    f = pl.program_id(1)
    last = pl.num_programs(1) - 1
    used = i < nval_ref[0]

    @pl.when(used)
    def _():
        xb = x_ref[...]
        a = _dot(xb, w1_ref[0])
        b = _dot(xb, w3_ref[0])
        part = _dot((a * jax.nn.sigmoid(a) * b).astype(BF16), w2_ref[0])

        @pl.when(f == 0)
        def _():
            acc_ref[...] = part

        @pl.when(f > 0)
        def _():
            acc_ref[...] += part

        @pl.when(f == last)
        def _():
            o_ref[...] = acc_ref[...].astype(o_ref.dtype)

    @pl.when(jnp.logical_not(used) & (f == last))
    def _():
        o_ref[...] = jnp.zeros(o_ref.shape, o_ref.dtype)


def _experts(xs, w1, w3, w2, bexp, nval, tf):
    rows, d = xs.shape
    ff = w1.shape[2]
    tb = EXPERT_ROWS
    nff = ff // tf

    def xrow(i, f, be, nv):
        return (jnp.minimum(i, nv[0] - 1), 0)

    def fcol(i, f, nv):
        return jnp.where(i < nv[0], f, nff - 1)

    grid_spec = pltpu.PrefetchScalarGridSpec(
        num_scalar_prefetch=2,
        grid=(rows // tb, nff),
        in_specs=[pl.BlockSpec((tb, d), xrow),
                  pl.BlockSpec((1, d, tf), lambda i, f, be, nv: (be[i], 0, fcol(i, f, nv))),
                  pl.BlockSpec((1, d, tf), lambda i, f, be, nv: (be[i], 0, fcol(i, f, nv))),
                  pl.BlockSpec((1, tf, d), lambda i, f, be, nv: (be[i], fcol(i, f, nv), 0))],
        out_specs=pl.BlockSpec((tb, d), lambda i, f, be, nv: (i, 0)),
        scratch_shapes=[pltpu.VMEM((tb, d), F32)],
    )
    return pl.pallas_call(
        _expert_body,
        grid_spec=grid_spec,
        out_shape=jax.ShapeDtypeStruct((rows, d), BF16),
        compiler_params=_params("arbitrary", "arbitrary"),
        name="moe_experts",
    )(bexp, nval, xs, w1, w3, w2)


def _combine_body(offs_ref, cnts_ref, info_ref, x_ref, gp_ref, gf_ref, ys_ref, o_ref, ybuf, acc_ref, sem):
    i = pl.program_id(0)
    tm = x_ref.shape[0]
    chunks = tm // COMBINE_ROWS

    def copy(e, c, src):
        return pltpu.make_async_copy(ys_ref.at[pl.ds(src, COMBINE_ROWS)], ybuf.at[e, c], sem.at[e, c])

    for e in range(N_EXPERTS):
        cnt = cnts_ref[i * N_EXPERTS + e]
        off = offs_ref[i * N_EXPERTS + e]
        for c in range(chunks):
            @pl.when(cnt > c * COMBINE_ROWS)
            def _():
                copy(e, c, pl.multiple_of(off + c * COMBINE_ROWS, BF16_ROWS)).start()

    acc_ref[...] = jnp.zeros(acc_ref.shape, F32)
    (e1, g1, r1), (e2, g2, r2) = _routing(info_ref[...])
    slot = lax.broadcasted_iota(jnp.int32, (tm, COMBINE_ROWS), 1).astype(F32)
    for e in range(N_EXPERTS):
        cnt = cnts_ref[i * N_EXPERTS + e]
        first = e1 == float(e)
        second = e2 == float(e)
        rank = jnp.where(first, r1, r2)
        gate = jnp.where(first, g1, jnp.where(second, g2, 0.0))
        for c in range(chunks):
            @pl.when(cnt > c * COMBINE_ROWS)
            def _():
                copy(e, c, 0).wait()
                hit = (rank == slot + float(c * COMBINE_ROWS)) & (first | second)
                sel = jnp.where(hit, 1.0, 0.0).astype(BF16)
                acc_ref[...] += gate * _dot(sel, ybuf[e, c])

    o_ref[...] = x_ref[...] + gf_ref[...] * _rms(acc_ref[...], gp_ref[...])


def _combine(offs, cnts, info, x, gp, gf, ys):
    s, d = x.shape
    tm = ROW_TILE
    chunks = tm // COMBINE_ROWS
    grid_spec = pltpu.PrefetchScalarGridSpec(
        num_scalar_prefetch=2,
        grid=(s // tm,),
        in_specs=[pl.BlockSpec((tm, LANES), lambda i, *_: (i, 0)),
                  pl.BlockSpec((tm, d), lambda i, *_: (i, 0)),
                  pl.BlockSpec((1, d), lambda i, *_: (0, 0)),
                  pl.BlockSpec((1, d), lambda i, *_: (0, 0)),
                  pl.BlockSpec(memory_space=pl.ANY)],
        out_specs=pl.BlockSpec((tm, d), lambda i, *_: (i, 0)),
        scratch_shapes=[pltpu.VMEM((N_EXPERTS, chunks, COMBINE_ROWS, d), BF16),
                        pltpu.VMEM((tm, d), F32),
                        pltpu.SemaphoreType.DMA((N_EXPERTS, chunks))],
    )
    return pl.pallas_call(
        _combine_body,
        grid_spec=grid_spec,
        out_shape=jax.ShapeDtypeStruct((s, d), F32),
        compiler_params=_params("arbitrary"),
        name="moe_combine",
    )(offs, cnts, info, x, gp, gf, ys)


def _rot_half_cols(w):
    half = ROPE // 2
    return jnp.concatenate([-w[..., half:], w[..., :half]], axis=-1)


def _pad_cols(w, before, total):
    return jnp.pad(w, ((0, 0), (before, total - before - w.shape[1])))


def _mixer_weights(w_in, w_uq, w_ukv, w_branch_b):
    d = w_in.shape[0]
    o = 0
    parts = {}
    for name, n in (("u", GMLP_WIDTH), ("v", GMLP_WIDTH), ("cq", Q_RANK), ("ckv", KV_RANK),
                    ("kr", ROPE), ("ga", d), ("gb", d)):
        parts[name] = w_in[:, o:o + n]
        o += n
    kr_main = _pad_cols(parts["kr"], NOPE, HEAD_PAD)
    kr_swap = _pad_cols(_rot_half_cols(parts["kr"]), NOPE, HEAD_PAD)
    w_in_p = jnp.concatenate([parts["u"], parts["v"], parts["cq"], parts["ckv"], kr_main, kr_swap,
                              parts["ga"], parts["gb"]], axis=1).astype(BF16)

    wq = w_uq.reshape(Q_RANK, HEADS, NOPE + ROPE)
    zq = jnp.zeros((Q_RANK, HEADS, HEAD_PAD - NOPE - ROPE), w_uq.dtype)
    wq_main = jnp.concatenate([wq, zq], axis=-1).reshape(Q_RANK, HEADS * HEAD_PAD).astype(BF16)
    wq_swap = jnp.concatenate([jnp.zeros((Q_RANK, HEADS, NOPE), w_uq.dtype),
                               _rot_half_cols(wq[..., NOPE:]), zq], axis=-1)
    wq_swap = wq_swap.reshape(Q_RANK, HEADS * HEAD_PAD).astype(BF16)

    wkv = w_ukv.reshape(KV_RANK, HEADS, NOPE + VDIM)
    zk = jnp.zeros((KV_RANK, HEADS, HEAD_PAD - NOPE), w_ukv.dtype)
    wk = jnp.concatenate([wkv[..., :NOPE], zk], axis=-1).reshape(KV_RANK, HEADS * HEAD_PAD).astype(BF16)
    zv = jnp.zeros((KV_RANK, HEADS, HEAD_PAD - VDIM), w_ukv.dtype)
    wv = jnp.concatenate([wkv[..., NOPE:], zv], axis=-1).reshape(KV_RANK, HEADS * HEAD_PAD).astype(BF16)

    wb = w_branch_b.reshape(HEADS, VDIM, d)
    wb = jnp.concatenate([wb, jnp.zeros((HEADS, HEAD_PAD - VDIM, d), wb.dtype)], axis=1)
    wb = wb.reshape(HEADS * HEAD_PAD, d).astype(BF16)
    return w_in_p, wq_main, wq_swap, wk, wv, wb


def _expert_layout(cnts, nt, nblk):
    tb = EXPERT_ROWS
    c = cnts.reshape(nt, N_EXPERTS)
    total = jnp.sum(c, axis=0)
    per = jnp.where(total > 0, (total + REGION_SLACK + tb - 1) // tb, 0)
    stop = jnp.cumsum(per)
    start = stop - per
    offs = start[None, :] * tb + (jnp.cumsum(c, axis=0) - c)
    nval = stop[-1]
    blk = jnp.minimum(jnp.arange(nblk, dtype=jnp.int32), nval - 1)
    bexp = jnp.minimum(jnp.searchsorted(stop, blk, side="right"), N_EXPERTS - 1)
    fill = jnp.concatenate([jnp.where(per > 0, stop - 1, -1), jnp.where(per > 1, stop - 2, -1)])
    i32 = lambda v: v.astype(jnp.int32)
    return i32(offs.reshape(-1)), i32(bexp), i32(nval.reshape(1)), i32(fill)


def kernel(x, c, positions, ada_w, ada_b, norm_mix_pre, norm_mix_post, norm_ffn_pre, norm_ffn_post, w_in, gmlp_ln_g, gmlp_ln_b, gmlp_ws, gmlp_bs, mla_q_norm, mla_w_uq, mla_kv_norm, mla_w_ukv, w_branch_a, w_branch_b, w_out, ffn_w_gate, ffn_w_up, ffn_w_down, moe_router, moe_router_bias, moe_w1, moe_w3, moe_w2):
    batch, s, d = x.shape
    assert batch == 1 and s % ROW_TILE == 0 and s % ATTN_TILE == 0
    depth = ada_w.shape[0]
    xs = x.reshape(s, d)
    mod = _ada_mod(c, ada_w, ada_b)
    cos_t, sin_t = _rope_tables(positions)
    vone = jnp.zeros((HEADS, HEAD_PAD), F32).at[:, VDIM].set(1.0).reshape(1, HEADS * HEAD_PAD)
    row = lambda v: v.reshape(1, -1)

    for l in range(depth):
        sh_m, sc_m, g_m, sh_f, sc_f, g_f = [mod[l, :, k * d:(k + 1) * d] for k in range(6)]
        w_in_p, wq_main, wq_swap, wk, wv, wb = _mixer_weights(w_in[l], mla_w_uq[l], mla_w_ukv[l], w_branch_b[l])
        uv, cq, ckv, kr, ga, gb = _inproj(xs, row(norm_mix_pre[l]), sc_m, sh_m, w_in_p)
        bias = jnp.repeat(gmlp_bs[l].T, GMLP_GROUP_DIM, axis=1)
        a = _gmlp(uv, ga, row(gmlp_ln_g[l]), row(gmlp_ln_b[l]), gmlp_ws[l], bias, w_branch_a[l].astype(BF16))
        q, k, v = _mla_prep(cq, ckv, kr, cos_t, sin_t, row(mla_q_norm[l]), row(mla_kv_norm[l]),
                            wq_main, wq_swap, wk, wv, vone)
        o = _attention(q, k, v)
        xs = _merge(a, gb, o, xs, wb, w_out[l].astype(BF16), row(norm_mix_post[l]), g_m)

        j = l // 2
        if l % 2 == 0:
            ff = ffn_w_gate.shape[2]
            ff_pad = -(-ff // (2 * LANES)) * (2 * LANES)
            wg = jnp.pad(ffn_w_gate[j], ((0, 0), (0, ff_pad - ff))).astype(BF16)
            wu = jnp.pad(ffn_w_up[j], ((0, 0), (0, ff_pad - ff))).astype(BF16)
            wd = jnp.pad(ffn_w_down[j], ((0, ff_pad - ff), (0, 0))).astype(BF16)
            xs = _dense_ffn(xs, row(norm_ffn_pre[l]), sc_f, sh_f, wg, wu, wd, row(norm_ffn_post[l]), g_f)
        else:
            nt = s // ROW_TILE
            tb = EXPERT_ROWS
            max_rows = 2 * s + nt * N_EXPERTS * (BF16_ROWS - 1) + N_EXPERTS * (REGION_SLACK + tb - 1)
            nblk = -(-max_rows // tb) + 1
            wr = jnp.pad(moe_router[j], ((0, 0), (0, LANES - N_EXPERTS)))
            br = jnp.pad(moe_router_bias[j], (0, LANES - N_EXPERTS), constant_values=NEG).reshape(1, LANES)
            hb, info, cnts = _route(xs, row(norm_ffn_pre[l]), sc_f, sh_f, wr, br)
            offs, bexp, nval, fill = _expert_layout(cnts, nt, nblk)
            xsort = _dispatch(offs, cnts, fill, nval, hb, info, nblk)
            ysort = _experts(xsort, moe_w1[j].astype(BF16), moe_w3[j].astype(BF16), moe_w2[j].astype(BF16),
                             bexp, nval, tf=moe_w1.shape[3] // 2)
            xs = _combine(offs, cnts, info, xs, row(norm_ffn_post[l]), g_f, ysort)
    return xs.reshape(batch, s, d)
```

```python
import functools

import jax
import jax.numpy as jnp
import numpy as np
from jax import lax
from jax.experimental import pallas as pl
from jax.experimental.pallas import tpu as pltpu

F32 = jnp.float32
BF16 = jnp.bfloat16

EPS = 1e-6
LANES = 128
BF16_ROWS = 16
GMLP_GROUPS = 8
GMLP_GROUP_DIM = 64
GMLP_WIDTH = GMLP_GROUPS * GMLP_GROUP_DIM
CHUNK = 128
HEADS = 8
NOPE = 64
ROPE = 32
VDIM = 64
HEAD_PAD = 128
Q_RANK = 384
KV_RANK = 256
ROPE_THETA = 10000.0
N_EXPERTS = 8
NEG = -1e30

ROW_TILE = 512
ATTN_TILE = 512
ATTN_HEADS = 2
EXPERT_ROWS = 512
DISPATCH_ROWS = 128
COMBINE_ROWS = 256
FIRST_ROWS = 256
ROUTE_FIELDS = 8
REGION_SLACK = DISPATCH_ROWS - BF16_ROWS
VMEM_LIMIT = 52 * 1024 * 1024


def _params(*sem):
    return pltpu.CompilerParams(dimension_semantics=sem, vmem_limit_bytes=VMEM_LIMIT)


def _dot(a, b):
    return jnp.dot(a, b, preferred_element_type=F32)


def _rms(x, g):
    return x * lax.rsqrt(jnp.mean(x * x, axis=-1, keepdims=True) + EPS) * g


def _gelu(x):
    return 0.5 * x * (1.0 + lax.erf(x * np.float32(0.7071067811865476)))


def _full(shape):
    return pl.BlockSpec(shape, lambda *_: (0,) * len(shape), pipeline_mode=pl.Buffered(1))


def _mod_body(c_ref, w_ref, b_ref, o_ref):
    c = c_ref[...]
    ca = c * jax.nn.sigmoid(c)
    o_ref[0] = jnp.sum(ca * w_ref[0], axis=0, keepdims=True) + b_ref[0]


def _ada_mod(c, ada_w, ada_b):
    n_layers, d, n = ada_w.shape
    tn = n // 4
    return pl.pallas_call(
        _mod_body,
        grid=(n_layers, n // tn),
        in_specs=[pl.BlockSpec((d, 1), lambda l, j: (0, 0)),
                  pl.BlockSpec((1, d, tn), lambda l, j: (l, 0, j)),
                  pl.BlockSpec((1, 1, tn), lambda l, j: (l, 0, j))],
        out_specs=pl.BlockSpec((1, 1, tn), lambda l, j: (l, 0, j)),
        out_shape=jax.ShapeDtypeStruct((n_layers, 1, n), F32),
        compiler_params=_params("arbitrary", "arbitrary"),
        name="ada_mod",
    )(c.reshape(d, 1), ada_w, ada_b.reshape(n_layers, 1, n))


def _rope_body(pos_ref, invf_ref, cos_ref, sin_ref):
    ang = pos_ref[...].astype(F32) * invf_ref[...]
    cos_ref[...] = jnp.cos(ang)
    sin_ref[...] = jnp.sin(ang)


def _rope_tables(positions):
    s = positions.shape[-1]
    half = ROPE // 2
    per_row = LANES // half
    inv_freq = 1.0 / (ROPE_THETA ** (jnp.arange(0, ROPE, 2, dtype=F32) / ROPE))
    pos_dense = jnp.repeat(positions.reshape(s), half).reshape(s // per_row, LANES)
    invf = jnp.tile(inv_freq, per_row).reshape(1, LANES)
    rows = s // per_row
    tr = rows // 4
    cos_d, sin_d = pl.pallas_call(
        _rope_body,
        grid=(rows // tr,),
        in_specs=[pl.BlockSpec((tr, LANES), lambda i: (i, 0)), _full((1, LANES))],
        out_specs=[pl.BlockSpec((tr, LANES), lambda i: (i, 0))] * 2,
        out_shape=[jax.ShapeDtypeStruct((rows, LANES), F32)] * 2,
        compiler_params=_params("arbitrary"),
        name="rope_tables",
    )(pos_dense, invf)
    cos = cos_d.reshape(s, half)
    sin = sin_d.reshape(s, half)
    ones = jnp.ones((s, NOPE), F32)
    zeros = jnp.zeros((s, NOPE), F32)
    pad = jnp.zeros((s, HEAD_PAD - NOPE - ROPE), F32)
    cos_t = jnp.concatenate([ones, cos, cos, pad], axis=1)
    sin_t = jnp.concatenate([zeros, sin, sin, pad], axis=1)
    return cos_t, sin_t


def _inproj_body(x_ref, g_ref, sc_ref, sh_ref, w_ref, uv_ref, cq_ref, ckv_ref, kr_ref, ga_ref, gb_ref):
    h = _rms(x_ref[...], g_ref[...]) * (1.0 + sc_ref[...]) + sh_ref[...]
    proj = _dot(h.astype(BF16), w_ref[...])
    col = 0
    for ref in (uv_ref, cq_ref, ckv_ref, kr_ref, ga_ref, gb_ref):
        n = ref.shape[1]
        ref[...] = proj[:, col:col + n].astype(ref.dtype)
        col += n


def _inproj(x, g, sc, sh, w):
    s, d = x.shape
    tm = ROW_TILE
    widths = (2 * GMLP_WIDTH, Q_RANK, KV_RANK, 2 * HEAD_PAD, d, d)
    row = lambda n: pl.BlockSpec((tm, n), lambda i: (i, 0))
    return pl.pallas_call(
        _inproj_body,
        grid=(s // tm,),
        in_specs=[row(d), _full((1, d)), _full((1, d)), _full((1, d)), _full(w.shape)],
        out_specs=[row(n) for n in widths],
        out_shape=[jax.ShapeDtypeStruct((s, n), BF16) for n in widths],
        compiler_params=_params("arbitrary"),
        name="mixer_inproj",
    )(x, g, sc, sh, w)


def _gmlp_body(uv_ref, ga_ref, lng_ref, lnb_ref, ws_ref, bias_ref, wa_ref, o_ref):
    tm = uv_ref.shape[0]
    u = uv_ref[:, :GMLP_WIDTH].astype(F32)
    v = uv_ref[:, GMLP_WIDTH:].astype(F32)
    gu = _gelu(u)
    gv = _gelu(v)
    mu = jnp.mean(gv, axis=-1, keepdims=True)
    xc = gv - mu
    vn = xc * lax.rsqrt(jnp.mean(xc * xc, axis=-1, keepdims=True) + EPS) * lng_ref[...] + lnb_ref[...]
    vb = vn.astype(BF16)
    t_idx = lax.broadcasted_iota(jnp.int32, (CHUNK, CHUNK), 0)
    s_idx = lax.broadcasted_iota(jnp.int32, (CHUNK, CHUNK), 1)
    causal = s_idx <= t_idx
    ws = [jnp.where(causal, ws_ref[g], 0.0).astype(BF16) for g in range(GMLP_GROUPS)]
    left = lax.broadcasted_iota(jnp.int32, (CHUNK, LANES), 1) < GMLP_GROUP_DIM
    bias = bias_ref[...]
    z_rows = []
    for c in range(tm // CHUNK):
        vc = vb[c * CHUNK:(c + 1) * CHUNK]
        z_cols = []
        for j in range(GMLP_WIDTH // LANES):
            vp = vc[:, j * LANES:(j + 1) * LANES]
            z_cols.append(jnp.where(left, _dot(ws[2 * j], vp), _dot(ws[2 * j + 1], vp)))
        z_rows.append(jnp.concatenate(z_cols, axis=1) + bias)
    z = jnp.concatenate(z_rows, axis=0)
    gated = (gu * z).astype(BF16)
    ya = _dot(gated, wa_ref[...])
    o_ref[...] = (jax.nn.sigmoid(ga_ref[...].astype(F32)) * ya).astype(o_ref.dtype)


def _gmlp(uv, ga, lng, lnb, ws, bias, wa):
    s, d = ga.shape
    tm = ROW_TILE
    row = lambda n: pl.BlockSpec((tm, n), lambda i: (i, 0))
    return pl.pallas_call(
        _gmlp_body,
        grid=(s // tm,),
        in_specs=[row(2 * GMLP_WIDTH), row(d), _full((1, GMLP_WIDTH)), _full((1, GMLP_WIDTH)),
                  _full(ws.shape), _full(bias.shape), _full(wa.shape)],
        out_specs=row(d),
        out_shape=jax.ShapeDtypeStruct((s, d), BF16),
        compiler_params=_params("arbitrary"),
        name="mixer_gmlp",
    )(uv, ga, lng, lnb, ws, bias, wa)


def _mla_prep_body(cq_ref, ckv_ref, kr_ref, cos_ref, sin_ref, qn_ref, kvn_ref, wqm_ref, wqs_ref,
                   wk_ref, wv_ref, vone_ref, q_ref, k_ref, v_ref, *, scale):
    cos = cos_ref[...]
    sin = sin_ref[...]
    cqn = _rms(cq_ref[...].astype(F32), qn_ref[...]).astype(BF16)
    qm = _dot(cqn, wqm_ref[...])
    qs = _dot(cqn, wqs_ref[...])
    ckn = _rms(ckv_ref[...].astype(F32), kvn_ref[...]).astype(BF16)
    km = _dot(ckn, wk_ref[...])
    kr = kr_ref[...].astype(F32)
    kpe = kr[:, :HEAD_PAD] * cos + kr[:, HEAD_PAD:] * sin
    for h in range(HEADS):
        sl = slice(h * HEAD_PAD, (h + 1) * HEAD_PAD)
        q_ref[:, sl] = ((qm[:, sl] * cos + qs[:, sl] * sin) * scale).astype(q_ref.dtype)
        k_ref[:, sl] = (km[:, sl] + kpe).astype(k_ref.dtype)
    v_ref[...] = (_dot(ckn, wv_ref[...]) + vone_ref[...]).astype(v_ref.dtype)


def _mla_prep(cq, ckv, kr, cos_t, sin_t, qn, kvn, wqm, wqs, wk, wv, vone):
    s = cq.shape[0]
    tm = ROW_TILE
    width = HEADS * HEAD_PAD
    row = lambda n: pl.BlockSpec((tm, n), lambda i: (i, 0))
    scale = float((NOPE + ROPE) ** -0.5 * np.log2(np.e))
    return pl.pallas_call(
        functools.partial(_mla_prep_body, scale=scale),
        grid=(s // tm,),
        in_specs=[row(Q_RANK), row(KV_RANK), row(2 * HEAD_PAD), row(HEAD_PAD), row(HEAD_PAD),
                  _full((1, Q_RANK)), _full((1, KV_RANK)), _full(wqm.shape), _full(wqs.shape),
                  _full(wk.shape), _full(wv.shape), _full((1, width))],
        out_specs=[row(width)] * 3,
        out_shape=[jax.ShapeDtypeStruct((s, width), BF16)] * 3,
        compiler_params=_params("arbitrary"),
        name="mla_prep",
    )(cq, ckv, kr, cos_t, sin_t, qn, kvn, wqm, wqs, wk, wv, vone)


def _attn_body(q_ref, k_ref, v_ref, o_ref, s_ref, m_ref, acc_ref):
    t = q_ref.shape[0]
    heads = q_ref.shape[1] // HEAD_PAD
    qi = pl.program_id(1)
    m_ref[...] = jnp.full(m_ref.shape, NEG, F32)
    acc_ref[...] = jnp.zeros(acc_ref.shape, F32)

    def scores(blk, slot):
        off = pl.multiple_of(blk * t, t)
        for h in range(heads):
            sl = slice(h * HEAD_PAD, (h + 1) * HEAD_PAD)
            s_ref[slot, h] = lax.dot_general(q_ref[:, sl], k_ref[pl.ds(off, t), sl],
                                             (((1,), (1,)), ((), ())), preferred_element_type=F32)

    def consume(blk, slot, mask):
        off = pl.multiple_of(blk * t, t)
        for h in range(heads):
            sl = slice(h * HEAD_PAD, (h + 1) * HEAD_PAD)
            s = s_ref[slot, h]
            if mask is not None:
                s = jnp.where(mask, s, NEG)
            m_prev = m_ref[h]
            m_new = jnp.maximum(m_prev, jnp.max(s, axis=-1, keepdims=True))
            alpha = jnp.exp2(m_prev - m_new)
            p = jnp.concatenate([jnp.exp2(s[:, j * LANES:(j + 1) * LANES] - m_new) for j in range(t // LANES)],
                                axis=1).astype(BF16)
            acc_ref[h] = alpha * acc_ref[h] + _dot(p, v_ref[pl.ds(off, t), sl])
            m_ref[h] = m_new

    scores(0, 0)

    def run(blk, n):
        for u in range(n):
            scores(blk + u + 1, (u + 1) % 2)
            consume(blk + u, u % 2, None)

    def quad(j, carry):
        run(4 * j, 4)
        return carry

    lax.fori_loop(0, qi // 4, quad, 0)
    done = (qi // 4) * 4

    @pl.when(qi - done >= 2)
    def _():
        run(done, 2)

    row = lax.broadcasted_iota(jnp.int32, (t, t), 0)
    col = lax.broadcasted_iota(jnp.int32, (t, t), 1)
    causal = col <= row
    odd = lax.rem(qi, 2) == 1

    @pl.when(odd)
    def _():
        scores(qi, 1)
        consume(qi - 1, 0, None)
        consume(qi, 1, causal)

    @pl.when(jnp.logical_not(odd))
    def _():
        consume(qi, 0, causal)

    for h in range(heads):
        acc = acc_ref[h]
        o_ref[:, h * HEAD_PAD:(h + 1) * HEAD_PAD] = (acc / acc[:, VDIM:VDIM + 1]).astype(o_ref.dtype)


def _attention(q, k, v):
    s, width = q.shape
    t = ATTN_TILE
    gw = ATTN_HEADS * HEAD_PAD
    return pl.pallas_call(
        _attn_body,
        grid=(width // gw, s // t),
        in_specs=[pl.BlockSpec((t, gw), lambda h, i: (i, h)),
                  pl.BlockSpec((s, gw), lambda h, i: (0, h)),
                  pl.BlockSpec((s, gw), lambda h, i: (0, h))],
        out_specs=pl.BlockSpec((t, gw), lambda h, i: (i, h)),
        out_shape=jax.ShapeDtypeStruct((s, width), BF16),
        scratch_shapes=[pltpu.VMEM((2, ATTN_HEADS, t, t), F32),
                        pltpu.VMEM((ATTN_HEADS, t, HEAD_PAD), F32),
                        pltpu.VMEM((ATTN_HEADS, t, HEAD_PAD), F32)],
        compiler_params=_params("arbitrary", "arbitrary"),
        name="mla_attention",
    )(q, k, v)


def _merge_body(a_ref, gb_ref, o_ref, x_ref, wb_ref, wo_ref, gp_ref, gm_ref, out_ref):
    yb = _dot(o_ref[...], wb_ref[...])
    merged = a_ref[...].astype(F32) + jax.nn.sigmoid(gb_ref[...].astype(F32)) * yb
    y = _dot(merged.astype(BF16), wo_ref[...])
    out_ref[...] = x_ref[...] + gm_ref[...] * _rms(y, gp_ref[...])


def _merge(a, gb, o, x, wb, wo, gp, gm):
    s, d = x.shape
    tm = ROW_TILE
    row = pl.BlockSpec((tm, d), lambda i: (i, 0))
    return pl.pallas_call(
        _merge_body,
        grid=(s // tm,),
        in_specs=[row, row, row, row, _full(wb.shape), _full(wo.shape), _full((1, d)), _full((1, d))],
        out_specs=row,
        out_shape=jax.ShapeDtypeStruct((s, d), F32),
        compiler_params=_params("arbitrary"),
        name="mixer_merge",
    )(a, gb, o, x, wb, wo, gp, gm)


def _ffn_body(x_ref, g_ref, sc_ref, sh_ref, wg_ref, wu_ref, wd_ref, gp_ref, gf_ref, o_ref):
    x = x_ref[...]
    hb = (_rms(x, g_ref[...]) * (1.0 + sc_ref[...]) + sh_ref[...]).astype(BF16)
    a = _dot(hb, wg_ref[...])
    b = _dot(hb, wu_ref[...])
    y = _dot((a * jax.nn.sigmoid(a) * b).astype(BF16), wd_ref[...])
    o_ref[...] = x + gf_ref[...] * _rms(y, gp_ref[...])


def _dense_ffn(x, g, sc, sh, wg, wu, wd, gp, gf):
    s, d = x.shape
    tm = ROW_TILE
    row = pl.BlockSpec((tm, d), lambda i: (i, 0))
    vec = _full((1, d))
    return pl.pallas_call(
        _ffn_body,
        grid=(s // tm,),
        in_specs=[row, vec, vec, vec, _full(wg.shape), _full(wu.shape), _full(wd.shape), vec, vec],
        out_specs=row,
        out_shape=jax.ShapeDtypeStruct((s, d), F32),
        compiler_params=_params("arbitrary"),
        name="dense_ffn",
    )(x, g, sc, sh, wg, wu, wd, gp, gf)


def _route_body(x_ref, g_ref, sc_ref, sh_ref, wr_ref, br_ref, hb_ref, info_ref, infot_ref, cnt_ref):
    i = pl.program_id(0)
    tm = x_ref.shape[0]
    h = _rms(x_ref[...], g_ref[...]) * (1.0 + sc_ref[...]) + sh_ref[...]
    hb = h.astype(BF16)
    hb_ref[...] = hb
    h_lo = (h - hb.astype(F32)).astype(BF16)
    w = wr_ref[...]
    w_hi = w.astype(BF16)
    w_lo = (w - w_hi.astype(F32)).astype(BF16)
    logits = _dot(hb, w_hi) + (_dot(h_lo, w_hi) + _dot(hb, w_lo)) + br_ref[...]

    lane = lax.broadcasted_iota(jnp.int32, (tm, LANES), 1)
    m1 = jnp.max(logits, axis=-1, keepdims=True)
    i1 = jnp.min(jnp.where(logits == m1, lane, LANES), axis=-1, keepdims=True)
    oh1 = lane == i1
    rest = jnp.where(oh1, -3e38, logits)
    m2 = jnp.max(rest, axis=-1, keepdims=True)
    i2 = jnp.min(jnp.where(rest == m2, lane, LANES), axis=-1, keepdims=True)
    oh2 = lane == i2
    ex = jnp.exp(m2 - m1)
    g1 = 1.0 / (1.0 + ex)
    g2 = ex / (1.0 + ex)

    ohf = jnp.where(oh1 | oh2, 1.0, 0.0)
    r_idx = lax.broadcasted_iota(jnp.int32, (tm, tm), 0)
    c_idx = lax.broadcasted_iota(jnp.int32, (tm, tm), 1)
    earlier = jnp.where(c_idx < r_idx, 1.0, 0.0).astype(BF16)
    rank = _dot(earlier, ohf.astype(BF16))
    rank1 = jnp.sum(jnp.where(oh1, rank, 0.0), axis=-1, keepdims=True)
    rank2 = jnp.sum(jnp.where(oh2, rank, 0.0), axis=-1, keepdims=True)
    info = jnp.where(lane == 0, i1.astype(F32),
           jnp.where(lane == 1, i2.astype(F32),
           jnp.where(lane == 2, g1,
           jnp.where(lane == 3, g2,
           jnp.where(lane == 4, rank1,
           jnp.where(lane == 5, rank2, 0.0))))))
    info_ref[...] = info
    infot_ref[...] = info.T[:ROUTE_FIELDS]
    cnt = jnp.sum(ohf, axis=0, keepdims=True).astype(jnp.int32)
    cnt_al = ((cnt + (BF16_ROWS - 1)) // BF16_ROWS) * BF16_ROWS
    for e in range(N_EXPERTS):
        cnt_ref[i * N_EXPERTS + e] = cnt_al[0, e]


def _route(x, g, sc, sh, wr, br):
    s, d = x.shape
    tm = ROW_TILE
    nt = s // tm
    row = pl.BlockSpec((tm, d), lambda i: (i, 0))
    return pl.pallas_call(
        _route_body,
        grid=(nt,),
        in_specs=[row, _full((1, d)), _full((1, d)), _full((1, d)), _full(wr.shape), _full(br.shape)],
        out_specs=[row,
                   pl.BlockSpec((tm, LANES), lambda i: (i, 0)),
                   pl.BlockSpec((ROUTE_FIELDS, tm), lambda i: (0, i)),
                   pl.BlockSpec(memory_space=pltpu.SMEM)],
        out_shape=[jax.ShapeDtypeStruct((s, d), BF16),
                   jax.ShapeDtypeStruct((s, LANES), F32),
                   jax.ShapeDtypeStruct((ROUTE_FIELDS, s), F32),
                   jax.ShapeDtypeStruct((nt * N_EXPERTS,), jnp.int32)],
        compiler_params=_params("arbitrary"),
        name="moe_route",
    )(x, g, sc, sh, wr, br)


def _routing(info):
    return [(info[:, k:k + 1], info[:, 2 + k:3 + k], info[:, 4 + k:5 + k]) for k in range(2)]


def _dispatch_body(offs_ref, cnts_ref, fill_ref, nval_ref, hb_ref, infot_ref, xs_ref, xbuf, zbuf, sem, zsem, *, nblk):
    i = pl.program_id(0)
    tm = hb_ref.shape[0]
    tb = zbuf.shape[0]

    def zero_block(blk):
        cp = pltpu.make_async_copy(zbuf, xs_ref.at[pl.ds(pl.multiple_of(blk * tb, tb), tb)], zsem)
        cp.start()
        cp.wait()

    @pl.when(i == 0)
    def _():
        zbuf[...] = jnp.zeros(zbuf.shape, zbuf.dtype)
        for n in range(fill_ref.shape[0]):
            @pl.when(fill_ref[n] >= 0)
            def _():
                zero_block(fill_ref[n])

        def unused(blk, carry):
            zero_block(blk)
            return carry

        lax.fori_loop(nval_ref[0], nblk, unused, 0)

    hb = hb_ref[...]
    fields = infot_ref[...]
    e1, e2, r1, r2 = fields[0:1], fields[1:2], fields[4:5], fields[5:6]
    chunks = tm // DISPATCH_ROWS
    buf = lax.rem(i, 2)

    def first_slot(e, r):
        return jnp.where(r < float(FIRST_ROWS), e * float(FIRST_ROWS) + r, -1.0)

    p1 = first_slot(e1, r1)
    p2 = first_slot(e2, r2)
    slot = lax.broadcasted_iota(jnp.int32, (N_EXPERTS * FIRST_ROWS, tm), 0).astype(F32)
    sel = jnp.where((p1 == slot) | (p2 == slot), 1.0, 0.0).astype(BF16)
    xbuf[buf, 0] = _dot(sel, hb).astype(BF16)

    first_chunks = FIRST_ROWS // DISPATCH_ROWS

    def buf_rows(e, c):
        return c // first_chunks, e * FIRST_ROWS + (c % first_chunks) * DISPATCH_ROWS

    for e in range(N_EXPERTS):
        for c in range(first_chunks, chunks):
            @pl.when(cnts_ref[i * N_EXPERTS + e] > c * DISPATCH_ROWS)
            def _():
                first = e1 == float(e)
                rank = jnp.where(first, r1, r2)
                late = (lax.broadcasted_iota(jnp.int32, (DISPATCH_ROWS, tm), 0) + c * DISPATCH_ROWS).astype(F32)
                hit = (rank == late) & (first | (e2 == float(e)))
                part, row0 = buf_rows(e, c)
                xbuf[buf, part, row0:row0 + DISPATCH_ROWS] = _dot(jnp.where(hit, 1.0, 0.0).astype(BF16),
                                                                  hb).astype(BF16)

    def copy(b, e, c, dst):
        part, row0 = buf_rows(e, c)
        return pltpu.make_async_copy(xbuf.at[b, part, pl.ds(row0, DISPATCH_ROWS)],
                                     xs_ref.at[pl.ds(dst, DISPATCH_ROWS)], sem.at[b, e, c])

    def for_each_copy(step, fn):
        for e in range(N_EXPERTS):
            for c in range(chunks):
                @pl.when(cnts_ref[step * N_EXPERTS + e] > c * DISPATCH_ROWS)
                def _():
                    fn(e, c)

    @pl.when(i > 0)
    def _():
        for_each_copy(i - 1, lambda e, c: copy(1 - buf, e, c, 0).wait())

    def start(e, c):
        off = offs_ref[i * N_EXPERTS + e]
        copy(buf, e, c, pl.multiple_of(off + c * DISPATCH_ROWS, BF16_ROWS)).start()

    for_each_copy(i, start)

    @pl.when(i == pl.num_programs(0) - 1)
    def _():
        for_each_copy(i, lambda e, c: copy(buf, e, c, 0).wait())


def _dispatch(offs, cnts, fill, nval, hb, infot, nblk):
    s, d = hb.shape
    tm = ROW_TILE
    tb = EXPERT_ROWS
    grid_spec = pltpu.PrefetchScalarGridSpec(
        num_scalar_prefetch=4,
        grid=(s // tm,),
        in_specs=[pl.BlockSpec((tm, d), lambda i, *_: (i, 0)),
                  pl.BlockSpec((ROUTE_FIELDS, tm), lambda i, *_: (0, i))],
        out_specs=pl.BlockSpec(memory_space=pl.ANY),
        scratch_shapes=[pltpu.VMEM((2, tm // FIRST_ROWS, N_EXPERTS * FIRST_ROWS, d), BF16),
                        pltpu.VMEM((tb, d), BF16),
                        pltpu.SemaphoreType.DMA((2, N_EXPERTS, tm // DISPATCH_ROWS)),
                        pltpu.SemaphoreType.DMA(())],
    )
    return pl.pallas_call(
        functools.partial(_dispatch_body, nblk=nblk),
        grid_spec=grid_spec,
        out_shape=jax.ShapeDtypeStruct((nblk * tb, d), BF16),
        compiler_params=_params("arbitrary"),
        name="moe_dispatch",
    )(offs, cnts, fill, nval, hb, infot)


def _expert_body(bexp_ref, nval_ref, x_ref, w1_ref, w3_ref, w2_ref, o_ref, acc_ref):
    i = pl.program_id(0)
    f = pl.program_id(1)
    last = pl.num_programs(1) - 1
    used = i < nval_ref[0]

    @pl.when(used)
    def _():
        xb = x_ref[...]
        a = _dot(xb, w1_ref[0])
        b = _dot(xb, w3_ref[0])
        part = _dot((a * jax.nn.sigmoid(a) * b).astype(BF16), w2_ref[0])

        @pl.when(f == 0)
        def _():
            acc_ref[...] = part

        @pl.when(f > 0)
        def _():
            acc_ref[...] += part

        @pl.when(f == last)
        def _():
            o_ref[...] = acc_ref[...].astype(o_ref.dtype)

    @pl.when(jnp.logical_not(used) & (f == last))
    def _():
        o_ref[...] = jnp.zeros(o_ref.shape, o_ref.dtype)


def _experts(xs, w1, w3, w2, bexp, nval, tf):
    rows, d = xs.shape
    ff = w1.shape[2]
    tb = EXPERT_ROWS
    nff = ff // tf

    def xrow(i, f, be, nv):
        return (jnp.minimum(i, nv[0] - 1), 0)

    def fcol(i, f, nv):
        return jnp.where(i < nv[0], f, nff - 1)

    grid_spec = pltpu.PrefetchScalarGridSpec(
        num_scalar_prefetch=2,
        grid=(rows // tb, nff),
        in_specs=[pl.BlockSpec((tb, d), xrow),
                  pl.BlockSpec((1, d, tf), lambda i, f, be, nv: (be[i], 0, fcol(i, f, nv))),
                  pl.BlockSpec((1, d, tf), lambda i, f, be, nv: (be[i], 0, fcol(i, f, nv))),
                  pl.BlockSpec((1, tf, d), lambda i, f, be, nv: (be[i], fcol(i, f, nv), 0))],
        out_specs=pl.BlockSpec((tb, d), lambda i, f, be, nv: (i, 0)),
        scratch_shapes=[pltpu.VMEM((tb, d), F32)],
    )
    return pl.pallas_call(
        _expert_body,
        grid_spec=grid_spec,
        out_shape=jax.ShapeDtypeStruct((rows, d), BF16),
        compiler_params=_params("arbitrary", "arbitrary"),
        name="moe_experts",
    )(bexp, nval, xs, w1, w3, w2)


def _combine_body(offs_ref, cnts_ref, info_ref, x_ref, gp_ref, gf_ref, ys_ref, o_ref, ybuf, acc_ref, sem):
    i = pl.program_id(0)
    nt = pl.num_programs(0)
    tm = x_ref.shape[0]
    chunks = tm // COMBINE_ROWS
    buf = lax.rem(i, 2)

    def copy(b, e, c, src):
        dst = ybuf.at[b, pl.ds((c * N_EXPERTS + e) * COMBINE_ROWS, COMBINE_ROWS)]
        return pltpu.make_async_copy(ys_ref.at[pl.ds(src, COMBINE_ROWS)], dst, sem.at[b, e, c])

    def fetch(step, b):
        for e in range(N_EXPERTS):
            off = offs_ref[step * N_EXPERTS + e]
            copy(b, e, 0, pl.multiple_of(off, BF16_ROWS)).start()
            for c in range(1, chunks):
                @pl.when(cnts_ref[step * N_EXPERTS + e] > c * COMBINE_ROWS)
                def _():
                    copy(b, e, c, pl.multiple_of(off + c * COMBINE_ROWS, BF16_ROWS)).start()

    @pl.when(i == 0)
    def _():
        fetch(0, 0)

    @pl.when(i + 1 < nt)
    def _():
        fetch(i + 1, 1 - buf)

    (e1, g1, r1), (e2, g2, r2) = _routing(info_ref[...])
    for e in range(N_EXPERTS):
        copy(buf, e, 0, 0).wait()
    stack = N_EXPERTS * COMBINE_ROWS
    slot = lax.broadcasted_iota(jnp.int32, (tm, stack), 1).astype(F32)
    y_first = ybuf[buf, 0:stack]
    acc = None
    for ek, gk, rk in ((e1, g1, r1), (e2, g2, r2)):
        pos = jnp.where(rk < float(COMBINE_ROWS), ek * float(COMBINE_ROWS) + rk, -1.0)
        term = gk * _dot(jnp.where(pos == slot, 1.0, 0.0).astype(BF16), y_first)
        acc = term if acc is None else acc + term
    acc_ref[...] = acc

    late = lax.broadcasted_iota(jnp.int32, (tm, COMBINE_ROWS), 1).astype(F32)
    for e in range(N_EXPERTS):
        for c in range(1, chunks):
            @pl.when(cnts_ref[i * N_EXPERTS + e] > c * COMBINE_ROWS)
            def _():
                first = e1 == float(e)
                second = e2 == float(e)
                rank = jnp.where(first, r1, r2)
                gate = jnp.where(first, g1, jnp.where(second, g2, 0.0))
                copy(buf, e, c, 0).wait()
                hit = (rank == late + float(c * COMBINE_ROWS)) & (first | second)
                rows = ybuf[buf, pl.ds((c * N_EXPERTS + e) * COMBINE_ROWS, COMBINE_ROWS)]
                acc_ref[...] += gate * _dot(jnp.where(hit, 1.0, 0.0).astype(BF16), rows)

    o_ref[...] = x_ref[...] + gf_ref[...] * _rms(acc_ref[...], gp_ref[...])


def _combine(offs, cnts, info, x, gp, gf, ys):
    s, d = x.shape
    tm = ROW_TILE
    chunks = tm // COMBINE_ROWS
    grid_spec = pltpu.PrefetchScalarGridSpec(
        num_scalar_prefetch=2,
        grid=(s // tm,),
        in_specs=[pl.BlockSpec((tm, LANES), lambda i, *_: (i, 0)),
                  pl.BlockSpec((tm, d), lambda i, *_: (i, 0)),
                  pl.BlockSpec((1, d), lambda i, *_: (0, 0)),
                  pl.BlockSpec((1, d), lambda i, *_: (0, 0)),
                  pl.BlockSpec(memory_space=pl.ANY)],
        out_specs=pl.BlockSpec((tm, d), lambda i, *_: (i, 0)),
        scratch_shapes=[pltpu.VMEM((2, chunks * N_EXPERTS * COMBINE_ROWS, d), BF16),
                        pltpu.VMEM((tm, d), F32),
                        pltpu.SemaphoreType.DMA((2, N_EXPERTS, chunks))],
    )
    return pl.pallas_call(
        _combine_body,
        grid_spec=grid_spec,
        out_shape=jax.ShapeDtypeStruct((s, d), F32),
        compiler_params=_params("arbitrary"),
        name="moe_combine",
    )(offs, cnts, info, x, gp, gf, ys)


def _rot_half_cols(w):
    half = ROPE // 2
    return jnp.concatenate([-w[..., half:], w[..., :half]], axis=-1)


def _pad_cols(w, before, total):
    return jnp.pad(w, ((0, 0), (before, total - before - w.shape[1])))


def _mixer_weights(w_in, w_uq, w_ukv, w_branch_b):
    d = w_in.shape[0]
    o = 0
    parts = {}
    for name, n in (("u", GMLP_WIDTH), ("v", GMLP_WIDTH), ("cq", Q_RANK), ("ckv", KV_RANK),
                    ("kr", ROPE), ("ga", d), ("gb", d)):
        parts[name] = w_in[:, o:o + n]
        o += n
    kr_main = _pad_cols(parts["kr"], NOPE, HEAD_PAD)
    kr_swap = _pad_cols(_rot_half_cols(parts["kr"]), NOPE, HEAD_PAD)
    w_in_p = jnp.concatenate([parts["u"], parts["v"], parts["cq"], parts["ckv"], kr_main, kr_swap,
                              parts["ga"], parts["gb"]], axis=1).astype(BF16)

    wq = w_uq.reshape(Q_RANK, HEADS, NOPE + ROPE)
    zq = jnp.zeros((Q_RANK, HEADS, HEAD_PAD - NOPE - ROPE), w_uq.dtype)
    wq_main = jnp.concatenate([wq, zq], axis=-1).reshape(Q_RANK, HEADS * HEAD_PAD).astype(BF16)
    wq_swap = jnp.concatenate([jnp.zeros((Q_RANK, HEADS, NOPE), w_uq.dtype),
                               _rot_half_cols(wq[..., NOPE:]), zq], axis=-1)
    wq_swap = wq_swap.reshape(Q_RANK, HEADS * HEAD_PAD).astype(BF16)

    wkv = w_ukv.reshape(KV_RANK, HEADS, NOPE + VDIM)
    zk = jnp.zeros((KV_RANK, HEADS, HEAD_PAD - NOPE), w_ukv.dtype)
    wk = jnp.concatenate([wkv[..., :NOPE], zk], axis=-1).reshape(KV_RANK, HEADS * HEAD_PAD).astype(BF16)
    zv = jnp.zeros((KV_RANK, HEADS, HEAD_PAD - VDIM), w_ukv.dtype)
    wv = jnp.concatenate([wkv[..., NOPE:], zv], axis=-1).reshape(KV_RANK, HEADS * HEAD_PAD).astype(BF16)

    wb = w_branch_b.reshape(HEADS, VDIM, d)
    wb = jnp.concatenate([wb, jnp.zeros((HEADS, HEAD_PAD - VDIM, d), wb.dtype)], axis=1)
    wb = wb.reshape(HEADS * HEAD_PAD, d).astype(BF16)
    return w_in_p, wq_main, wq_swap, wk, wv, wb


def _expert_layout(cnts, nt, nblk):
    tb = EXPERT_ROWS
    c = cnts.reshape(nt, N_EXPERTS)
    total = jnp.sum(c, axis=0)
    per = jnp.where(total > 0, (total + REGION_SLACK + tb - 1) // tb, 0)
    stop = jnp.cumsum(per)
    start = stop - per
    offs = start[None, :] * tb + (jnp.cumsum(c, axis=0) - c)
    nval = stop[-1]
    blk = jnp.minimum(jnp.arange(nblk, dtype=jnp.int32), nval - 1)
    bexp = jnp.minimum(jnp.sum(blk[:, None] >= stop[None, :], axis=1), N_EXPERTS - 1)
    fill = jnp.concatenate([jnp.where(per > 0, stop - 1, -1), jnp.where(per > 1, stop - 2, -1)])
    i32 = lambda v: v.astype(jnp.int32)
    return i32(offs.reshape(-1)), i32(bexp), i32(nval.reshape(1)), i32(fill)


def kernel(x, c, positions, ada_w, ada_b, norm_mix_pre, norm_mix_post, norm_ffn_pre, norm_ffn_post, w_in, gmlp_ln_g, gmlp_ln_b, gmlp_ws, gmlp_bs, mla_q_norm, mla_w_uq, mla_kv_norm, mla_w_ukv, w_branch_a, w_branch_b, w_out, ffn_w_gate, ffn_w_up, ffn_w_down, moe_router, moe_router_bias, moe_w1, moe_w3, moe_w2):
    batch, s, d = x.shape
    assert batch == 1 and s % ROW_TILE == 0 and s % ATTN_TILE == 0
    depth = ada_w.shape[0]
    xs = x.reshape(s, d)
    mod = _ada_mod(c, ada_w, ada_b)
    cos_t, sin_t = _rope_tables(positions)
    vone = jnp.zeros((HEADS, HEAD_PAD), F32).at[:, VDIM].set(1.0).reshape(1, HEADS * HEAD_PAD)
    row = lambda v: v.reshape(1, -1)

    for l in range(depth):
        sh_m, sc_m, g_m, sh_f, sc_f, g_f = [mod[l, :, k * d:(k + 1) * d] for k in range(6)]
        w_in_p, wq_main, wq_swap, wk, wv, wb = _mixer_weights(w_in[l], mla_w_uq[l], mla_w_ukv[l], w_branch_b[l])
        uv, cq, ckv, kr, ga, gb = _inproj(xs, row(norm_mix_pre[l]), sc_m, sh_m, w_in_p)
        bias = jnp.repeat(gmlp_bs[l].T, GMLP_GROUP_DIM, axis=1)
        a = _gmlp(uv, ga, row(gmlp_ln_g[l]), row(gmlp_ln_b[l]), gmlp_ws[l], bias, w_branch_a[l].astype(BF16))
        q, k, v = _mla_prep(cq, ckv, kr, cos_t, sin_t, row(mla_q_norm[l]), row(mla_kv_norm[l]),
                            wq_main, wq_swap, wk, wv, vone)
        o = _attention(q, k, v)
        xs = _merge(a, gb, o, xs, wb, w_out[l].astype(BF16), row(norm_mix_post[l]), g_m)

        j = l // 2
        if l % 2 == 0:
            ff = ffn_w_gate.shape[2]
            ff_pad = -(-ff // (2 * LANES)) * (2 * LANES)
            wg = jnp.pad(ffn_w_gate[j], ((0, 0), (0, ff_pad - ff))).astype(BF16)
            wu = jnp.pad(ffn_w_up[j], ((0, 0), (0, ff_pad - ff))).astype(BF16)
            wd = jnp.pad(ffn_w_down[j], ((0, ff_pad - ff), (0, 0))).astype(BF16)
            xs = _dense_ffn(xs, row(norm_ffn_pre[l]), sc_f, sh_f, wg, wu, wd, row(norm_ffn_post[l]), g_f)
        else:
            nt = s // ROW_TILE
            tb = EXPERT_ROWS
            max_rows = 2 * s + nt * N_EXPERTS * (BF16_ROWS - 1) + N_EXPERTS * (REGION_SLACK + tb - 1)
            nblk = -(-max_rows // tb) + 1
            wr = jnp.pad(moe_router[j], ((0, 0), (0, LANES - N_EXPERTS)))
            br = jnp.pad(moe_router_bias[j], (0, LANES - N_EXPERTS), constant_values=NEG).reshape(1, LANES)
            hb, info, infot, cnts = _route(xs, row(norm_ffn_pre[l]), sc_f, sh_f, wr, br)
            offs, bexp, nval, fill = _expert_layout(cnts, nt, nblk)
            xsort = _dispatch(offs, cnts, fill, nval, hb, infot, nblk)
            ysort = _experts(xsort, moe_w1[j].astype(BF16), moe_w3[j].astype(BF16), moe_w2[j].astype(BF16),
                             bexp, nval, tf=moe_w1.shape[3] // 2)
            xs = _combine(offs, cnts, info, xs, row(norm_ffn_post[l]), g_f, ysort)
    return xs.reshape(batch, s, d)
```

```python
import functools

import jax
import jax.numpy as jnp
import numpy as np
from jax import lax
from jax.experimental import pallas as pl
from jax.experimental.pallas import tpu as pltpu

F32 = jnp.float32
BF16 = jnp.bfloat16

EPS = 1e-6
LANES = 128
BF16_ROWS = 16
GMLP_GROUPS = 8
GMLP_GROUP_DIM = 64
GMLP_WIDTH = GMLP_GROUPS * GMLP_GROUP_DIM
CHUNK = 128
HEADS = 8
NOPE = 64
ROPE = 32
VDIM = 64
HEAD_PAD = 128
Q_RANK = 384
KV_RANK = 256
ROPE_THETA = 10000.0
N_EXPERTS = 8
NEG = -1e30

ROW_TILE = 512
ATTN_TILE = 512
ATTN_HEADS = 2
EXPERT_ROWS = 512
DISPATCH_ROWS = 128
COMBINE_ROWS = 256
FIRST_ROWS = 256
ROUTE_FIELDS = 8
REGION_SLACK = DISPATCH_ROWS - BF16_ROWS
VMEM_LIMIT = 52 * 1024 * 1024


def _params(*sem):
    return pltpu.CompilerParams(dimension_semantics=sem, vmem_limit_bytes=VMEM_LIMIT)


def _dot(a, b):
    return jnp.dot(a, b, preferred_element_type=F32)


def _rms(x, g):
    return x * lax.rsqrt(jnp.mean(x * x, axis=-1, keepdims=True) + EPS) * g


def _gelu(x):
    return 0.5 * x * (1.0 + lax.erf(x * np.float32(0.7071067811865476)))


def _full(shape):
    return pl.BlockSpec(shape, lambda *_: (0,) * len(shape), pipeline_mode=pl.Buffered(1))


def _mod_body(c_ref, w_ref, b_ref, o_ref):
    c = c_ref[...]
    ca = c * jax.nn.sigmoid(c)
    o_ref[0] = jnp.sum(ca * w_ref[0], axis=0, keepdims=True) + b_ref[0]


def _ada_mod(c, ada_w, ada_b):
    n_layers, d, n = ada_w.shape
    tn = n // 4
    return pl.pallas_call(
        _mod_body,
        grid=(n_layers, n // tn),
        in_specs=[pl.BlockSpec((d, 1), lambda l, j: (0, 0)),
                  pl.BlockSpec((1, d, tn), lambda l, j: (l, 0, j)),
                  pl.BlockSpec((1, 1, tn), lambda l, j: (l, 0, j))],
        out_specs=pl.BlockSpec((1, 1, tn), lambda l, j: (l, 0, j)),
        out_shape=jax.ShapeDtypeStruct((n_layers, 1, n), F32),
        compiler_params=_params("arbitrary", "arbitrary"),
        name="ada_mod",
    )(c.reshape(d, 1), ada_w, ada_b.reshape(n_layers, 1, n))


def _rope_body(pos_ref, invf_ref, cos_ref, sin_ref):
    ang = pos_ref[...].astype(F32) * invf_ref[...]
    cos_ref[...] = jnp.cos(ang)
    sin_ref[...] = jnp.sin(ang)


def _rope_tables(positions):
    s = positions.shape[-1]
    half = ROPE // 2
    per_row = LANES // half
    inv_freq = 1.0 / (ROPE_THETA ** (jnp.arange(0, ROPE, 2, dtype=F32) / ROPE))
    pos_dense = jnp.broadcast_to(positions.reshape(s, 1), (s, half)).reshape(s // per_row, LANES)
    invf = jnp.tile(inv_freq, per_row).reshape(1, LANES)
    rows = s // per_row
    tr = rows // 4
    cos_d, sin_d = pl.pallas_call(
        _rope_body,
        grid=(rows // tr,),
        in_specs=[pl.BlockSpec((tr, LANES), lambda i: (i, 0)), _full((1, LANES))],
        out_specs=[pl.BlockSpec((tr, LANES), lambda i: (i, 0))] * 2,
        out_shape=[jax.ShapeDtypeStruct((rows, LANES), F32)] * 2,
        compiler_params=_params("arbitrary"),
        name="rope_tables",
    )(pos_dense, invf)
    cos = cos_d.reshape(s, half)
    sin = sin_d.reshape(s, half)
    ones = jnp.ones((s, NOPE), F32)
    zeros = jnp.zeros((s, NOPE), F32)
    pad = jnp.zeros((s, HEAD_PAD - NOPE - ROPE), F32)
    cos_t = jnp.concatenate([ones, cos, cos, pad], axis=1)
    sin_t = jnp.concatenate([zeros, sin, sin, pad], axis=1)
    return cos_t, sin_t


def _inproj_body(x_ref, g_ref, sc_ref, sh_ref, w_ref, uv_ref, cq_ref, ckv_ref, kr_ref, ga_ref, gb_ref):
    h = _rms(x_ref[...], g_ref[...]) * (1.0 + sc_ref[...]) + sh_ref[...]
    proj = _dot(h.astype(BF16), w_ref[...])
    col = 0
    for ref in (uv_ref, cq_ref, ckv_ref, kr_ref, ga_ref, gb_ref):
        n = ref.shape[1]
        ref[...] = proj[:, col:col + n].astype(ref.dtype)
        col += n


def _inproj(x, g, sc, sh, w):
    s, d = x.shape
    tm = ROW_TILE
    widths = (2 * GMLP_WIDTH, Q_RANK, KV_RANK, 2 * HEAD_PAD, d, d)
    row = lambda n: pl.BlockSpec((tm, n), lambda i: (i, 0))
    return pl.pallas_call(
        _inproj_body,
        grid=(s // tm,),
        in_specs=[row(d), _full((1, d)), _full((1, d)), _full((1, d)), _full(w.shape)],
        out_specs=[row(n) for n in widths],
        out_shape=[jax.ShapeDtypeStruct((s, n), BF16) for n in widths],
        compiler_params=_params("arbitrary"),
        name="mixer_inproj",
    )(x, g, sc, sh, w)


def _gmlp_body(uv_ref, ga_ref, lng_ref, lnb_ref, ws_ref, bias_ref, wa_ref, o_ref):
    tm = uv_ref.shape[0]
    u = uv_ref[:, :GMLP_WIDTH].astype(F32)
    v = uv_ref[:, GMLP_WIDTH:].astype(F32)
    gu = _gelu(u)
    gv = _gelu(v)
    mu = jnp.mean(gv, axis=-1, keepdims=True)
    xc = gv - mu
    vn = xc * lax.rsqrt(jnp.mean(xc * xc, axis=-1, keepdims=True) + EPS) * lng_ref[...] + lnb_ref[...]
    vb = vn.astype(BF16)
    t_idx = lax.broadcasted_iota(jnp.int32, (CHUNK, CHUNK), 0)
    s_idx = lax.broadcasted_iota(jnp.int32, (CHUNK, CHUNK), 1)
    causal = s_idx <= t_idx
    ws = [jnp.where(causal, ws_ref[g], 0.0).astype(BF16) for g in range(GMLP_GROUPS)]
    left = lax.broadcasted_iota(jnp.int32, (CHUNK, LANES), 1) < GMLP_GROUP_DIM
    bias = bias_ref[...]
    z_rows = []
    for c in range(tm // CHUNK):
        vc = vb[c * CHUNK:(c + 1) * CHUNK]
        z_cols = []
        for j in range(GMLP_WIDTH // LANES):
            vp = vc[:, j * LANES:(j + 1) * LANES]
            z_cols.append(jnp.where(left, _dot(ws[2 * j], vp), _dot(ws[2 * j + 1], vp)))
        z_rows.append(jnp.concatenate(z_cols, axis=1) + bias)
    z = jnp.concatenate(z_rows, axis=0)
    gated = (gu * z).astype(BF16)
    ya = _dot(gated, wa_ref[...])
    o_ref[...] = (jax.nn.sigmoid(ga_ref[...].astype(F32)) * ya).astype(o_ref.dtype)


def _gmlp(uv, ga, lng, lnb, ws, bias, wa):
    s, d = ga.shape
    tm = ROW_TILE
    row = lambda n: pl.BlockSpec((tm, n), lambda i: (i, 0))
    return pl.pallas_call(
        _gmlp_body,
        grid=(s // tm,),
        in_specs=[row(2 * GMLP_WIDTH), row(d), _full((1, GMLP_WIDTH)), _full((1, GMLP_WIDTH)),
                  _full(ws.shape), _full(bias.shape), _full(wa.shape)],
        out_specs=row(d),
        out_shape=jax.ShapeDtypeStruct((s, d), BF16),
        compiler_params=_params("arbitrary"),
        name="mixer_gmlp",
    )(uv, ga, lng, lnb, ws, bias, wa)


def _mla_prep_body(cq_ref, ckv_ref, kr_ref, cos_ref, sin_ref, qn_ref, kvn_ref, wqm_ref, wqs_ref,
                   wk_ref, wv_ref, vone_ref, q_ref, k_ref, v_ref, *, scale):
    cos = cos_ref[...]
    sin = sin_ref[...]
    cqn = _rms(cq_ref[...].astype(F32), qn_ref[...]).astype(BF16)
    qm = _dot(cqn, wqm_ref[...])
    qs = _dot(cqn, wqs_ref[...])
    ckn = _rms(ckv_ref[...].astype(F32), kvn_ref[...]).astype(BF16)
    km = _dot(ckn, wk_ref[...])
    kr = kr_ref[...].astype(F32)
    kpe = kr[:, :HEAD_PAD] * cos + kr[:, HEAD_PAD:] * sin
    for h in range(HEADS):
        sl = slice(h * HEAD_PAD, (h + 1) * HEAD_PAD)
        q_ref[:, sl] = ((qm[:, sl] * cos + qs[:, sl] * sin) * scale).astype(q_ref.dtype)
        k_ref[:, sl] = (km[:, sl] + kpe).astype(k_ref.dtype)
    v_ref[...] = (_dot(ckn, wv_ref[...]) + vone_ref[...]).astype(v_ref.dtype)


def _mla_prep(cq, ckv, kr, cos_t, sin_t, qn, kvn, wqm, wqs, wk, wv, vone):
    s = cq.shape[0]
    tm = ROW_TILE
    width = HEADS * HEAD_PAD
    row = lambda n: pl.BlockSpec((tm, n), lambda i: (i, 0))
    scale = float((NOPE + ROPE) ** -0.5 * np.log2(np.e))
    return pl.pallas_call(
        functools.partial(_mla_prep_body, scale=scale),
        grid=(s // tm,),
        in_specs=[row(Q_RANK), row(KV_RANK), row(2 * HEAD_PAD), row(HEAD_PAD), row(HEAD_PAD),
                  _full((1, Q_RANK)), _full((1, KV_RANK)), _full(wqm.shape), _full(wqs.shape),
                  _full(wk.shape), _full(wv.shape), _full((1, width))],
        out_specs=[row(width)] * 3,
        out_shape=[jax.ShapeDtypeStruct((s, width), BF16)] * 3,
        compiler_params=_params("arbitrary"),
        name="mla_prep",
    )(cq, ckv, kr, cos_t, sin_t, qn, kvn, wqm, wqs, wk, wv, vone)


def _attn_body(q_ref, k_ref, v_ref, o_ref, s_ref, mx_ref, m_ref, acc_ref):
    t = q_ref.shape[0]
    heads = q_ref.shape[1] // HEAD_PAD
    qi = pl.program_id(1)
    m_ref[...] = jnp.full(m_ref.shape, NEG, F32)
    acc_ref[...] = jnp.zeros(acc_ref.shape, F32)

    def scores(blk, slot):
        off = pl.multiple_of(blk * t, t)
        for h in range(heads):
            sl = slice(h * HEAD_PAD, (h + 1) * HEAD_PAD)
            s = lax.dot_general(q_ref[:, sl], k_ref[pl.ds(off, t), sl],
                                (((1,), (1,)), ((), ())), preferred_element_type=F32)
            s_ref[slot, h] = s
            mx_ref[slot, h] = jnp.broadcast_to(jnp.max(s, axis=-1, keepdims=True), (t, HEAD_PAD))

    def consume(blk, slot, mask):
        off = pl.multiple_of(blk * t, t)
        for h in range(heads):
            sl = slice(h * HEAD_PAD, (h + 1) * HEAD_PAD)
            s = s_ref[slot, h]
            if mask is None:
                mx = mx_ref[slot, h]
            else:
                s = jnp.where(mask, s, NEG)
                mx = jnp.max(s, axis=-1, keepdims=True)
            m_prev = m_ref[h]
            m_new = jnp.maximum(m_prev, mx)
            alpha = jnp.exp2(m_prev - m_new)
            pv = alpha * acc_ref[h]
            half = t // 2
            for part in range(2):
                p = jnp.concatenate([jnp.exp2(s[:, j * LANES:(j + 1) * LANES] - m_new)
                                     for j in range(part * half // LANES, (part + 1) * half // LANES)],
                                    axis=1).astype(BF16)
                pv = pv + _dot(p, v_ref[pl.ds(off + part * half, half), sl])
            acc_ref[h] = pv
            m_ref[h] = m_new

    scores(0, 0)

    def run(blk, n):
        for u in range(n):
            scores(blk + u + 1, (u + 1) % 2)
            consume(blk + u, u % 2, None)

    def quad(j, carry):
        run(4 * j, 4)
        return carry

    lax.fori_loop(0, qi // 4, quad, 0)
    done = (qi // 4) * 4

    @pl.when(qi - done >= 2)
    def _():
        run(done, 2)

    row = lax.broadcasted_iota(jnp.int32, (t, t), 0)
    col = lax.broadcasted_iota(jnp.int32, (t, t), 1)
    causal = col <= row
    odd = lax.rem(qi, 2) == 1

    @pl.when(odd)
    def _():
        scores(qi, 1)
        consume(qi - 1, 0, None)
        consume(qi, 1, causal)

    @pl.when(jnp.logical_not(odd))
    def _():
        consume(qi, 0, causal)

    lower = lax.broadcasted_iota(jnp.int32, (t, HEAD_PAD), 1) < VDIM
    for j in range(heads // 2):
        outs = []
        for h in (2 * j, 2 * j + 1):
            acc = acc_ref[h]
            outs.append(acc / acc[:, VDIM:VDIM + 1])
        packed = jnp.where(lower, outs[0], pltpu.roll(outs[1], VDIM, 1))
        o_ref[:, j * HEAD_PAD:(j + 1) * HEAD_PAD] = packed.astype(o_ref.dtype)


def _attention(q, k, v):
    s, width = q.shape
    t = ATTN_TILE
    gw = ATTN_HEADS * HEAD_PAD
    return pl.pallas_call(
        _attn_body,
        grid=(width // gw, s // t),
        in_specs=[pl.BlockSpec((t, gw), lambda h, i: (i, h)),
                  pl.BlockSpec((s, gw), lambda h, i: (0, h)),
                  pl.BlockSpec((s, gw), lambda h, i: (0, h))],
        out_specs=pl.BlockSpec((t, ATTN_HEADS * VDIM), lambda h, i: (i, h)),
        out_shape=jax.ShapeDtypeStruct((s, HEADS * VDIM), BF16),
        scratch_shapes=[pltpu.VMEM((2, ATTN_HEADS, t, t), F32),
                        pltpu.VMEM((2, ATTN_HEADS, t, HEAD_PAD), F32),
                        pltpu.VMEM((ATTN_HEADS, t, HEAD_PAD), F32),
                        pltpu.VMEM((ATTN_HEADS, t, HEAD_PAD), F32)],
        compiler_params=_params("arbitrary", "arbitrary"),
        name="mla_attention",
    )(q, k, v)


def _merge_body(a_ref, gb_ref, o_ref, x_ref, wb_ref, wo_ref, gp_ref, gm_ref, out_ref):
    yb = _dot(o_ref[...], wb_ref[...])
    merged = a_ref[...].astype(F32) + jax.nn.sigmoid(gb_ref[...].astype(F32)) * yb
    y = _dot(merged.astype(BF16), wo_ref[...])
    out_ref[...] = x_ref[...] + gm_ref[...] * _rms(y, gp_ref[...])


def _merge(a, gb, o, x, wb, wo, gp, gm):
    s, d = x.shape
    tm = ROW_TILE
    row = pl.BlockSpec((tm, d), lambda i: (i, 0))
    return pl.pallas_call(
        _merge_body,
        grid=(s // tm,),
        in_specs=[row, row, pl.BlockSpec((tm, o.shape[1]), lambda i: (i, 0)), row,
                  _full(wb.shape), _full(wo.shape), _full((1, d)), _full((1, d))],
        out_specs=row,
        out_shape=jax.ShapeDtypeStruct((s, d), F32),
        compiler_params=_params("arbitrary"),
        name="mixer_merge",
    )(a, gb, o, x, wb, wo, gp, gm)


def _ffn_body(x_ref, g_ref, sc_ref, sh_ref, wg_ref, wu_ref, wd_ref, gp_ref, gf_ref, o_ref):
    x = x_ref[...]
    hb = (_rms(x, g_ref[...]) * (1.0 + sc_ref[...]) + sh_ref[...]).astype(BF16)
    a = _dot(hb, wg_ref[...])
    b = _dot(hb, wu_ref[...])
    y = _dot((a * jax.nn.sigmoid(a) * b).astype(BF16), wd_ref[...])
    o_ref[...] = x + gf_ref[...] * _rms(y, gp_ref[...])


def _dense_ffn(x, g, sc, sh, wg, wu, wd, gp, gf):
    s, d = x.shape
    tm = ROW_TILE
    row = pl.BlockSpec((tm, d), lambda i: (i, 0))
    vec = _full((1, d))
    return pl.pallas_call(
        _ffn_body,
        grid=(s // tm,),
        in_specs=[row, vec, vec, vec, _full(wg.shape), _full(wu.shape), _full(wd.shape), vec, vec],
        out_specs=row,
        out_shape=jax.ShapeDtypeStruct((s, d), F32),
        compiler_params=_params("arbitrary"),
        name="dense_ffn",
    )(x, g, sc, sh, wg, wu, wd, gp, gf)


def _route_body(x_ref, g_ref, sc_ref, sh_ref, wr_ref, br_ref, hb_ref, info_ref, infot_ref, cnt_ref):
    i = pl.program_id(0)
    tm = x_ref.shape[0]
    h = _rms(x_ref[...], g_ref[...]) * (1.0 + sc_ref[...]) + sh_ref[...]
    hb = h.astype(BF16)
    hb_ref[...] = hb
    h_lo = (h - hb.astype(F32)).astype(BF16)
    w = wr_ref[...]
    w_hi = w.astype(BF16)
    w_lo = (w - w_hi.astype(F32)).astype(BF16)
    logits = _dot(hb, w_hi) + (_dot(h_lo, w_hi) + _dot(hb, w_lo)) + br_ref[...]

    lane = lax.broadcasted_iota(jnp.int32, (tm, LANES), 1)
    m1 = jnp.max(logits, axis=-1, keepdims=True)
    i1 = jnp.min(jnp.where(logits == m1, lane, LANES), axis=-1, keepdims=True)
    oh1 = lane == i1
    rest = jnp.where(oh1, -3e38, logits)
    m2 = jnp.max(rest, axis=-1, keepdims=True)
    i2 = jnp.min(jnp.where(rest == m2, lane, LANES), axis=-1, keepdims=True)
    oh2 = lane == i2
    ex = jnp.exp(m2 - m1)
    g1 = 1.0 / (1.0 + ex)
    g2 = ex / (1.0 + ex)

    ohf = jnp.where(oh1 | oh2, 1.0, 0.0)
    r_idx = lax.broadcasted_iota(jnp.int32, (tm, tm), 0)
    c_idx = lax.broadcasted_iota(jnp.int32, (tm, tm), 1)
    earlier = jnp.where(c_idx < r_idx, 1.0, 0.0).astype(BF16)
    rank = _dot(earlier, ohf.astype(BF16))
    rank1 = jnp.sum(jnp.where(oh1, rank, 0.0), axis=-1, keepdims=True)
    rank2 = jnp.sum(jnp.where(oh2, rank, 0.0), axis=-1, keepdims=True)
    info = jnp.where(lane == 0, i1.astype(F32),
           jnp.where(lane == 1, i2.astype(F32),
           jnp.where(lane == 2, g1,
           jnp.where(lane == 3, g2,
           jnp.where(lane == 4, rank1,
           jnp.where(lane == 5, rank2, 0.0))))))
    info_ref[...] = info
    infot_ref[...] = info.T[:ROUTE_FIELDS]
    cnt = jnp.sum(ohf, axis=0, keepdims=True).astype(jnp.int32)
    cnt_al = ((cnt + (BF16_ROWS - 1)) // BF16_ROWS) * BF16_ROWS
    for e in range(N_EXPERTS):
        cnt_ref[i * N_EXPERTS + e] = cnt_al[0, e]


def _route(x, g, sc, sh, wr, br):
    s, d = x.shape
    tm = ROW_TILE
    nt = s // tm
    row = pl.BlockSpec((tm, d), lambda i: (i, 0))
    return pl.pallas_call(
        _route_body,
        grid=(nt,),
        in_specs=[row, _full((1, d)), _full((1, d)), _full((1, d)), _full(wr.shape), _full(br.shape)],
        out_specs=[row,
                   pl.BlockSpec((tm, LANES), lambda i: (i, 0)),
                   pl.BlockSpec((ROUTE_FIELDS, tm), lambda i: (0, i)),
                   pl.BlockSpec(memory_space=pltpu.SMEM)],
        out_shape=[jax.ShapeDtypeStruct((s, d), BF16),
                   jax.ShapeDtypeStruct((s, LANES), F32),
                   jax.ShapeDtypeStruct((ROUTE_FIELDS, s), F32),
                   jax.ShapeDtypeStruct((nt * N_EXPERTS,), jnp.int32)],
        compiler_params=_params("arbitrary"),
        name="moe_route",
    )(x, g, sc, sh, wr, br)


def _routing(info):
    return [(info[:, k:k + 1], info[:, 2 + k:3 + k], info[:, 4 + k:5 + k]) for k in range(2)]


def _dispatch_body(offs_ref, cnts_ref, fill_ref, nval_ref, hb_ref, infot_ref, xs_ref, xbuf, zbuf, sem, zsem, *, nblk):
    i = pl.program_id(0)
    tm = hb_ref.shape[0]
    tb = zbuf.shape[0]

    def zero_block(blk):
        cp = pltpu.make_async_copy(zbuf, xs_ref.at[pl.ds(pl.multiple_of(blk * tb, tb), tb)], zsem)
        cp.start()
        cp.wait()

    @pl.when(i == 0)
    def _():
        zbuf[...] = jnp.zeros(zbuf.shape, zbuf.dtype)
        for n in range(fill_ref.shape[0]):
            @pl.when(fill_ref[n] >= 0)
            def _():
                zero_block(fill_ref[n])

        def unused(blk, carry):
            zero_block(blk)
            return carry

        lax.fori_loop(nval_ref[0], nblk, unused, 0)

    hb = hb_ref[...]
    fields = infot_ref[...]
    e1, e2, r1, r2 = fields[0:1], fields[1:2], fields[4:5], fields[5:6]
    chunks = tm // DISPATCH_ROWS
    buf = lax.rem(i, 2)

    def first_slot(e, r):
        return jnp.where(r < float(FIRST_ROWS), e * float(FIRST_ROWS) + r, -1.0)

    p1 = first_slot(e1, r1)
    p2 = first_slot(e2, r2)
    slot = lax.broadcasted_iota(jnp.int32, (N_EXPERTS * FIRST_ROWS, tm), 0).astype(F32)
    sel = jnp.where((p1 == slot) | (p2 == slot), 1.0, 0.0).astype(BF16)
    xbuf[buf, 0] = _dot(sel, hb).astype(BF16)

    first_chunks = FIRST_ROWS // DISPATCH_ROWS

    def buf_rows(e, c):
        return c // first_chunks, e * FIRST_ROWS + (c % first_chunks) * DISPATCH_ROWS

    for e in range(N_EXPERTS):
        for c in range(first_chunks, chunks):
            @pl.when(cnts_ref[i * N_EXPERTS + e] > c * DISPATCH_ROWS)
            def _():
                first = e1 == float(e)
                rank = jnp.where(first, r1, r2)
                late = (lax.broadcasted_iota(jnp.int32, (DISPATCH_ROWS, tm), 0) + c * DISPATCH_ROWS).astype(F32)
                hit = (rank == late) & (first | (e2 == float(e)))
                part, row0 = buf_rows(e, c)
                xbuf[buf, part, row0:row0 + DISPATCH_ROWS] = _dot(jnp.where(hit, 1.0, 0.0).astype(BF16),
                                                                  hb).astype(BF16)

    def copy(b, e, c, dst):
        part, row0 = buf_rows(e, c)
        return pltpu.make_async_copy(xbuf.at[b, part, pl.ds(row0, DISPATCH_ROWS)],
                                     xs_ref.at[pl.ds(dst, DISPATCH_ROWS)], sem.at[b, e, c])

    def for_each_copy(step, fn):
        for e in range(N_EXPERTS):
            for c in range(chunks):
                @pl.when(cnts_ref[step * N_EXPERTS + e] > c * DISPATCH_ROWS)
                def _():
                    fn(e, c)

    @pl.when(i > 0)
    def _():
        for_each_copy(i - 1, lambda e, c: copy(1 - buf, e, c, 0).wait())

    def start(e, c):
        off = offs_ref[i * N_EXPERTS + e]
        copy(buf, e, c, pl.multiple_of(off + c * DISPATCH_ROWS, BF16_ROWS)).start()

    for_each_copy(i, start)

    @pl.when(i == pl.num_programs(0) - 1)
    def _():
        for_each_copy(i, lambda e, c: copy(buf, e, c, 0).wait())


def _dispatch(offs, cnts, fill, nval, hb, infot, nblk):
    s, d = hb.shape
    tm = ROW_TILE
    tb = EXPERT_ROWS
    grid_spec = pltpu.PrefetchScalarGridSpec(
        num_scalar_prefetch=4,
        grid=(s // tm,),
        in_specs=[pl.BlockSpec((tm, d), lambda i, *_: (i, 0)),
                  pl.BlockSpec((ROUTE_FIELDS, tm), lambda i, *_: (0, i))],
        out_specs=pl.BlockSpec(memory_space=pl.ANY),
        scratch_shapes=[pltpu.VMEM((2, tm // FIRST_ROWS, N_EXPERTS * FIRST_ROWS, d), BF16),
                        pltpu.VMEM((tb, d), BF16),
                        pltpu.SemaphoreType.DMA((2, N_EXPERTS, tm // DISPATCH_ROWS)),
                        pltpu.SemaphoreType.DMA(())],
    )
    return pl.pallas_call(
        functools.partial(_dispatch_body, nblk=nblk),
        grid_spec=grid_spec,
        out_shape=jax.ShapeDtypeStruct((nblk * tb, d), BF16),
        compiler_params=_params("arbitrary"),
        name="moe_dispatch",
    )(offs, cnts, fill, nval, hb, infot)


def _expert_body(bexp_ref, nval_ref, x_ref, w1_ref, w3_ref, w2_ref, o_ref, acc_ref):
    i = pl.program_id(0)
    f = pl.program_id(1)
    last = pl.num_programs(1) - 1
    used = i < nval_ref[0]

    @pl.when(used)
    def _():
        xb = x_ref[...]
        a = _dot(xb, w1_ref[0])
        b = _dot(xb, w3_ref[0])
        part = _dot((a * jax.nn.sigmoid(a) * b).astype(BF16), w2_ref[0])

        @pl.when(f == 0)
        def _():
            acc_ref[...] = part

        @pl.when(f > 0)
        def _():
            acc_ref[...] += part

        @pl.when(f == last)
        def _():
            o_ref[...] = acc_ref[...].astype(o_ref.dtype)

    @pl.when(jnp.logical_not(used) & (f == last))
    def _():
        o_ref[...] = jnp.zeros(o_ref.shape, o_ref.dtype)


def _experts(xs, w1, w3, w2, bexp, nval, tf):
    rows, d = xs.shape
    ff = w1.shape[2]
    tb = EXPERT_ROWS
    nff = ff // tf

    def xrow(i, f, be, nv):
        return (jnp.minimum(i, nv[0] - 1), 0)

    def fcol(i, f, nv):
        return jnp.where(i < nv[0], f, nff - 1)

    grid_spec = pltpu.PrefetchScalarGridSpec(
        num_scalar_prefetch=2,
        grid=(rows // tb, nff),
        in_specs=[pl.BlockSpec((tb, d), xrow),
                  pl.BlockSpec((1, d, tf), lambda i, f, be, nv: (be[i], 0, fcol(i, f, nv))),
                  pl.BlockSpec((1, d, tf), lambda i, f, be, nv: (be[i], 0, fcol(i, f, nv))),
                  pl.BlockSpec((1, tf, d), lambda i, f, be, nv: (be[i], fcol(i, f, nv), 0))],
        out_specs=pl.BlockSpec((tb, d), lambda i, f, be, nv: (i, 0)),
        scratch_shapes=[pltpu.VMEM((tb, d), F32)],
    )
    return pl.pallas_call(
        _expert_body,
        grid_spec=grid_spec,
        out_shape=jax.ShapeDtypeStruct((rows, d), BF16),
        compiler_params=_params("arbitrary", "arbitrary"),
        name="moe_experts",
    )(bexp, nval, xs, w1, w3, w2)


def _combine_body(offs_ref, cnts_ref, info_ref, x_ref, gp_ref, gf_ref, ys_ref, o_ref, ybuf, acc_ref, sem):
    i = pl.program_id(0)
    nt = pl.num_programs(0)
    tm = x_ref.shape[0]
    chunks = tm // COMBINE_ROWS
    buf = lax.rem(i, 2)

    def copy(b, e, c, src):
        dst = ybuf.at[b, pl.ds((c * N_EXPERTS + e) * COMBINE_ROWS, COMBINE_ROWS)]
        return pltpu.make_async_copy(ys_ref.at[pl.ds(src, COMBINE_ROWS)], dst, sem.at[b, e, c])

    def fetch(step, b):
        for e in range(N_EXPERTS):
            off = offs_ref[step * N_EXPERTS + e]
            copy(b, e, 0, pl.multiple_of(off, BF16_ROWS)).start()
            for c in range(1, chunks):
                @pl.when(cnts_ref[step * N_EXPERTS + e] > c * COMBINE_ROWS)
                def _():
                    copy(b, e, c, pl.multiple_of(off + c * COMBINE_ROWS, BF16_ROWS)).start()

    @pl.when(i == 0)
    def _():
        fetch(0, 0)

    @pl.when(i + 1 < nt)
    def _():
        fetch(i + 1, 1 - buf)

    (e1, g1, r1), (e2, g2, r2) = _routing(info_ref[...])
    for e in range(N_EXPERTS):
        copy(buf, e, 0, 0).wait()
    stack = N_EXPERTS * COMBINE_ROWS
    slot = lax.broadcasted_iota(jnp.int32, (tm, stack), 1).astype(F32)
    y_first = ybuf[buf, 0:stack]
    acc = None
    for ek, gk, rk in ((e1, g1, r1), (e2, g2, r2)):
        pos = jnp.where(rk < float(COMBINE_ROWS), ek * float(COMBINE_ROWS) + rk, -1.0)
        term = gk * _dot(jnp.where(pos == slot, 1.0, 0.0).astype(BF16), y_first)
        acc = term if acc is None else acc + term
    acc_ref[...] = acc

    late = lax.broadcasted_iota(jnp.int32, (tm, COMBINE_ROWS), 1).astype(F32)
    for e in range(N_EXPERTS):
        for c in range(1, chunks):
            @pl.when(cnts_ref[i * N_EXPERTS + e] > c * COMBINE_ROWS)
            def _():
                first = e1 == float(e)
                second = e2 == float(e)
                rank = jnp.where(first, r1, r2)
                gate = jnp.where(first, g1, jnp.where(second, g2, 0.0))
                copy(buf, e, c, 0).wait()
                hit = (rank == late + float(c * COMBINE_ROWS)) & (first | second)
                rows = ybuf[buf, pl.ds((c * N_EXPERTS + e) * COMBINE_ROWS, COMBINE_ROWS)]
                acc_ref[...] += gate * _dot(jnp.where(hit, 1.0, 0.0).astype(BF16), rows)

    o_ref[...] = x_ref[...] + gf_ref[...] * _rms(acc_ref[...], gp_ref[...])


def _combine(offs, cnts, info, x, gp, gf, ys):
    s, d = x.shape
    tm = ROW_TILE
    chunks = tm // COMBINE_ROWS
    grid_spec = pltpu.PrefetchScalarGridSpec(
        num_scalar_prefetch=2,
        grid=(s // tm,),
        in_specs=[pl.BlockSpec((tm, LANES), lambda i, *_: (i, 0)),
                  pl.BlockSpec((tm, d), lambda i, *_: (i, 0)),
                  pl.BlockSpec((1, d), lambda i, *_: (0, 0)),
                  pl.BlockSpec((1, d), lambda i, *_: (0, 0)),
                  pl.BlockSpec(memory_space=pl.ANY)],
        out_specs=pl.BlockSpec((tm, d), lambda i, *_: (i, 0)),
        scratch_shapes=[pltpu.VMEM((2, chunks * N_EXPERTS * COMBINE_ROWS, d), BF16),
                        pltpu.VMEM((tm, d), F32),
                        pltpu.SemaphoreType.DMA((2, N_EXPERTS, chunks))],
    )
    return pl.pallas_call(
        _combine_body,
        grid_spec=grid_spec,
        out_shape=jax.ShapeDtypeStruct((s, d), F32),
        compiler_params=_params("arbitrary"),
        name="moe_combine",
    )(offs, cnts, info, x, gp, gf, ys)


def _rot_half_cols(w):
    half = ROPE // 2
    return jnp.concatenate([-w[..., half:], w[..., :half]], axis=-1)


def _pad_cols(w, before, total):
    return jnp.pad(w, ((0, 0), (before, total - before - w.shape[1])))


def _mixer_weights(w_in, w_uq, w_ukv, w_branch_b):
    d = w_in.shape[0]
    o = 0
    parts = {}
    for name, n in (("u", GMLP_WIDTH), ("v", GMLP_WIDTH), ("cq", Q_RANK), ("ckv", KV_RANK),
                    ("kr", ROPE), ("ga", d), ("gb", d)):
        parts[name] = w_in[:, o:o + n]
        o += n
    kr_main = _pad_cols(parts["kr"], NOPE, HEAD_PAD)
    kr_swap = _pad_cols(_rot_half_cols(parts["kr"]), NOPE, HEAD_PAD)
    w_in_p = jnp.concatenate([parts["u"], parts["v"], parts["cq"], parts["ckv"], kr_main, kr_swap,
                              parts["ga"], parts["gb"]], axis=1).astype(BF16)

    wq = w_uq.reshape(Q_RANK, HEADS, NOPE + ROPE)
    zq = jnp.zeros((Q_RANK, HEADS, HEAD_PAD - NOPE - ROPE), w_uq.dtype)
    wq_main = jnp.concatenate([wq, zq], axis=-1).reshape(Q_RANK, HEADS * HEAD_PAD).astype(BF16)
    wq_swap = jnp.concatenate([jnp.zeros((Q_RANK, HEADS, NOPE), w_uq.dtype),
                               _rot_half_cols(wq[..., NOPE:]), zq], axis=-1)
    wq_swap = wq_swap.reshape(Q_RANK, HEADS * HEAD_PAD).astype(BF16)

    wkv = w_ukv.reshape(KV_RANK, HEADS, NOPE + VDIM)
    zk = jnp.zeros((KV_RANK, HEADS, HEAD_PAD - NOPE), w_ukv.dtype)
    wk = jnp.concatenate([wkv[..., :NOPE], zk], axis=-1).reshape(KV_RANK, HEADS * HEAD_PAD).astype(BF16)
    zv = jnp.zeros((KV_RANK, HEADS, HEAD_PAD - VDIM), w_ukv.dtype)
    wv = jnp.concatenate([wkv[..., NOPE:], zv], axis=-1).reshape(KV_RANK, HEADS * HEAD_PAD).astype(BF16)

    return w_in_p, wq_main, wq_swap, wk, wv, w_branch_b.astype(BF16)


def _expert_layout(cnts, nt, nblk):
    tb = EXPERT_ROWS
    c = cnts.reshape(nt, N_EXPERTS)
    total = jnp.sum(c, axis=0)
    per = jnp.where(total > 0, (total + REGION_SLACK + tb - 1) // tb, 0)
    stop = jnp.cumsum(per)
    start = stop - per
    offs = start[None, :] * tb + (jnp.cumsum(c, axis=0) - c)
    nval = stop[-1]
    blk = jnp.minimum(jnp.arange(nblk, dtype=jnp.int32), nval - 1)
    bexp = jnp.minimum(jnp.sum(blk[:, None] >= stop[None, :], axis=1), N_EXPERTS - 1)
    fill = jnp.concatenate([jnp.where(per > 0, stop - 1, -1), jnp.where(per > 1, stop - 2, -1)])
    i32 = lambda v: v.astype(jnp.int32)
    return i32(offs.reshape(-1)), i32(bexp), i32(nval.reshape(1)), i32(fill)


def kernel(x, c, positions, ada_w, ada_b, norm_mix_pre, norm_mix_post, norm_ffn_pre, norm_ffn_post, w_in, gmlp_ln_g, gmlp_ln_b, gmlp_ws, gmlp_bs, mla_q_norm, mla_w_uq, mla_kv_norm, mla_w_ukv, w_branch_a, w_branch_b, w_out, ffn_w_gate, ffn_w_up, ffn_w_down, moe_router, moe_router_bias, moe_w1, moe_w3, moe_w2):
    batch, s, d = x.shape
    assert batch == 1 and s % ROW_TILE == 0 and s % ATTN_TILE == 0
    depth = ada_w.shape[0]
    xs = x.reshape(s, d)
    mod = _ada_mod(c, ada_w, ada_b)
    cos_t, sin_t = _rope_tables(positions)
    vone = jnp.zeros((HEADS, HEAD_PAD), F32).at[:, VDIM].set(1.0).reshape(1, HEADS * HEAD_PAD)
    row = lambda v: v.reshape(1, -1)

    for l in range(depth):
        sh_m, sc_m, g_m, sh_f, sc_f, g_f = [mod[l, :, k * d:(k + 1) * d] for k in range(6)]
        w_in_p, wq_main, wq_swap, wk, wv, wb = _mixer_weights(w_in[l], mla_w_uq[l], mla_w_ukv[l], w_branch_b[l])
        uv, cq, ckv, kr, ga, gb = _inproj(xs, row(norm_mix_pre[l]), sc_m, sh_m, w_in_p)
        bias = jnp.broadcast_to(gmlp_bs[l].T[:, :, None], (CHUNK, GMLP_GROUPS, GMLP_GROUP_DIM))
        bias = bias.reshape(CHUNK, GMLP_WIDTH)
        a = _gmlp(uv, ga, row(gmlp_ln_g[l]), row(gmlp_ln_b[l]), gmlp_ws[l], bias, w_branch_a[l].astype(BF16))
        q, k, v = _mla_prep(cq, ckv, kr, cos_t, sin_t, row(mla_q_norm[l]), row(mla_kv_norm[l]),
                            wq_main, wq_swap, wk, wv, vone)
        o = _attention(q, k, v)
        xs = _merge(a, gb, o, xs, wb, w_out[l].astype(BF16), row(norm_mix_post[l]), g_m)

        j = l // 2
        if l % 2 == 0:
            ff = ffn_w_gate.shape[2]
            ff_pad = -(-ff // (2 * LANES)) * (2 * LANES)
            wg = jnp.pad(ffn_w_gate[j], ((0, 0), (0, ff_pad - ff))).astype(BF16)
            wu = jnp.pad(ffn_w_up[j], ((0, 0), (0, ff_pad - ff))).astype(BF16)
            wd = jnp.pad(ffn_w_down[j], ((0, ff_pad - ff), (0, 0))).astype(BF16)
            xs = _dense_ffn(xs, row(norm_ffn_pre[l]), sc_f, sh_f, wg, wu, wd, row(norm_ffn_post[l]), g_f)
        else:
            nt = s // ROW_TILE
            tb = EXPERT_ROWS
            max_rows = 2 * s + nt * N_EXPERTS * (BF16_ROWS - 1) + N_EXPERTS * (REGION_SLACK + tb - 1)
            nblk = -(-max_rows // tb) + 1
            wr = jnp.pad(moe_router[j], ((0, 0), (0, LANES - N_EXPERTS)))
            br = jnp.pad(moe_router_bias[j], (0, LANES - N_EXPERTS), constant_values=NEG).reshape(1, LANES)
            hb, info, infot, cnts = _route(xs, row(norm_ffn_pre[l]), sc_f, sh_f, wr, br)
            offs, bexp, nval, fill = _expert_layout(cnts, nt, nblk)
            xsort = _dispatch(offs, cnts, fill, nval, hb, infot, nblk)
            ysort = _experts(xsort, moe_w1[j].astype(BF16), moe_w3[j].astype(BF16), moe_w2[j].astype(BF16),
                             bexp, nval, tf=moe_w1.shape[3] // 2)
            xs = _combine(offs, cnts, info, xs, row(norm_ffn_post[l]), g_f, ysort)
    return xs.reshape(batch, s, d)
```

```python
import functools

import jax
import jax.numpy as jnp
import numpy as np
from jax import lax
from jax.experimental import pallas as pl
from jax.experimental.pallas import tpu as pltpu

F32 = jnp.float32
BF16 = jnp.bfloat16

EPS = 1e-6
LANES = 128
BF16_ROWS = 16
GMLP_GROUPS = 8
GMLP_GROUP_DIM = 64
GMLP_WIDTH = GMLP_GROUPS * GMLP_GROUP_DIM
CHUNK = 128
HEADS = 8
NOPE = 64
ROPE = 32
VDIM = 64
HEAD_PAD = 128
VT_ROWS = 80
Q_RANK = 384
KV_RANK = 256
ROPE_THETA = 10000.0
N_EXPERTS = 8
NEG = -1e30

ROW_TILE = 512
ATTN_TILE = 512
ATTN_HEADS = 2
STAT_ROWS = 8
EXPERT_ROWS = 512
DISPATCH_ROWS = 128
COMBINE_ROWS = 256
FIRST_ROWS = 256
ROUTE_FIELDS = 8
REGION_SLACK = DISPATCH_ROWS - BF16_ROWS
VMEM_LIMIT = 52 * 1024 * 1024


def _params(*sem):
    return pltpu.CompilerParams(dimension_semantics=sem, vmem_limit_bytes=VMEM_LIMIT)


def _dot(a, b):
    return jnp.dot(a, b, preferred_element_type=F32)


def _rms(x, g):
    return x * lax.rsqrt(jnp.mean(x * x, axis=-1, keepdims=True) + EPS) * g


def _gelu(x):
    return 0.5 * x * (1.0 + lax.erf(x * np.float32(0.7071067811865476)))


def _full(shape):
    return pl.BlockSpec(shape, lambda *_: (0,) * len(shape), pipeline_mode=pl.Buffered(1))


def _mod_body(c_ref, w_ref, b_ref, o_ref):
    c = c_ref[...]
    ca = c * jax.nn.sigmoid(c)
    o_ref[0] = jnp.sum(ca * w_ref[0], axis=0, keepdims=True) + b_ref[0]


def _ada_mod(c, ada_w, ada_b):
    n_layers, d, n = ada_w.shape
    tn = n // 4
    return pl.pallas_call(
        _mod_body,
        grid=(n_layers, n // tn),
        in_specs=[pl.BlockSpec((d, 1), lambda l, j: (0, 0)),
                  pl.BlockSpec((1, d, tn), lambda l, j: (l, 0, j)),
                  pl.BlockSpec((1, 1, tn), lambda l, j: (l, 0, j))],
        out_specs=pl.BlockSpec((1, 1, tn), lambda l, j: (l, 0, j)),
        out_shape=jax.ShapeDtypeStruct((n_layers, 1, n), F32),
        compiler_params=_params("arbitrary", "arbitrary"),
        name="ada_mod",
    )(c.reshape(d, 1), ada_w, ada_b.reshape(n_layers, 1, n))


def _rope_body(pos_ref, invf_ref, cos_ref, sin_ref):
    ang = pos_ref[...].astype(F32) * invf_ref[...]
    cos_ref[...] = jnp.cos(ang)
    sin_ref[...] = jnp.sin(ang)


def _rope_tables(positions):
    s = positions.shape[-1]
    half = ROPE // 2
    per_row = LANES // half
    inv_freq = 1.0 / (ROPE_THETA ** (jnp.arange(0, ROPE, 2, dtype=F32) / ROPE))
    pos_dense = jnp.broadcast_to(positions.reshape(s, 1), (s, half)).reshape(s // per_row, LANES)
    invf = jnp.tile(inv_freq, per_row).reshape(1, LANES)
    rows = s // per_row
    tr = rows // 4
    cos_d, sin_d = pl.pallas_call(
        _rope_body,
        grid=(rows // tr,),
        in_specs=[pl.BlockSpec((tr, LANES), lambda i: (i, 0)), _full((1, LANES))],
        out_specs=[pl.BlockSpec((tr, LANES), lambda i: (i, 0))] * 2,
        out_shape=[jax.ShapeDtypeStruct((rows, LANES), F32)] * 2,
        compiler_params=_params("arbitrary"),
        name="rope_tables",
    )(pos_dense, invf)
    cos = cos_d.reshape(s, half)
    sin = sin_d.reshape(s, half)
    ones = jnp.ones((s, NOPE), F32)
    zeros = jnp.zeros((s, NOPE), F32)
    pad = jnp.zeros((s, HEAD_PAD - NOPE - ROPE), F32)
    cos_t = jnp.concatenate([ones, cos, cos, pad], axis=1)
    sin_t = jnp.concatenate([zeros, sin, sin, pad], axis=1)
    return cos_t, sin_t


def _inproj_body(x_ref, g_ref, sc_ref, sh_ref, w_ref, uv_ref, cq_ref, ckv_ref, kr_ref, ga_ref, gb_ref):
    h = _rms(x_ref[...], g_ref[...]) * (1.0 + sc_ref[...]) + sh_ref[...]
    proj = _dot(h.astype(BF16), w_ref[...])
    col = 0
    for ref in (uv_ref, cq_ref, ckv_ref, kr_ref, ga_ref, gb_ref):
        n = ref.shape[1]
        ref[...] = proj[:, col:col + n].astype(ref.dtype)
        col += n


def _inproj(x, g, sc, sh, w):
    s, d = x.shape
    tm = ROW_TILE
    widths = (2 * GMLP_WIDTH, Q_RANK, KV_RANK, 2 * HEAD_PAD, d, d)
    row = lambda n: pl.BlockSpec((tm, n), lambda i: (i, 0))
    return pl.pallas_call(
        _inproj_body,
        grid=(s // tm,),
        in_specs=[row(d), _full((1, d)), _full((1, d)), _full((1, d)), _full(w.shape)],
        out_specs=[row(n) for n in widths],
        out_shape=[jax.ShapeDtypeStruct((s, n), BF16) for n in widths],
        compiler_params=_params("arbitrary"),
        name="mixer_inproj",
    )(x, g, sc, sh, w)


def _gmlp_body(uv_ref, ga_ref, lng_ref, lnb_ref, ws_ref, bias_ref, wa_ref, o_ref):
    tm = uv_ref.shape[0]
    u = uv_ref[:, :GMLP_WIDTH].astype(F32)
    v = uv_ref[:, GMLP_WIDTH:].astype(F32)
    gu = _gelu(u)
    gv = _gelu(v)
    mu = jnp.mean(gv, axis=-1, keepdims=True)
    xc = gv - mu
    vn = xc * lax.rsqrt(jnp.mean(xc * xc, axis=-1, keepdims=True) + EPS) * lng_ref[...] + lnb_ref[...]
    vb = vn.astype(BF16)
    t_idx = lax.broadcasted_iota(jnp.int32, (CHUNK, CHUNK), 0)
    s_idx = lax.broadcasted_iota(jnp.int32, (CHUNK, CHUNK), 1)
    causal = s_idx <= t_idx
    ws = [jnp.where(causal, ws_ref[g], 0.0).astype(BF16) for g in range(GMLP_GROUPS)]
    left = lax.broadcasted_iota(jnp.int32, (CHUNK, LANES), 1) < GMLP_GROUP_DIM
    bias = bias_ref[...]
    z_rows = []
    for c in range(tm // CHUNK):
        vc = vb[c * CHUNK:(c + 1) * CHUNK]
        z_cols = []
        for j in range(GMLP_WIDTH // LANES):
            vp = vc[:, j * LANES:(j + 1) * LANES]
            z_cols.append(jnp.where(left, _dot(ws[2 * j], vp), _dot(ws[2 * j + 1], vp)))
        z_rows.append(jnp.concatenate(z_cols, axis=1) + bias)
    z = jnp.concatenate(z_rows, axis=0)
    gated = (gu * z).astype(BF16)
    ya = _dot(gated, wa_ref[...])
    o_ref[...] = (jax.nn.sigmoid(ga_ref[...].astype(F32)) * ya).astype(o_ref.dtype)


def _gmlp(uv, ga, lng, lnb, ws, bias, wa):
    s, d = ga.shape
    tm = ROW_TILE
    row = lambda n: pl.BlockSpec((tm, n), lambda i: (i, 0))
    return pl.pallas_call(
        _gmlp_body,
        grid=(s // tm,),
        in_specs=[row(2 * GMLP_WIDTH), row(d), _full((1, GMLP_WIDTH)), _full((1, GMLP_WIDTH)),
                  _full(ws.shape), _full(bias.shape), _full(wa.shape)],
        out_specs=row(d),
        out_shape=jax.ShapeDtypeStruct((s, d), BF16),
        compiler_params=_params("arbitrary"),
        name="mixer_gmlp",
    )(uv, ga, lng, lnb, ws, bias, wa)


def _mla_prep_body(cq_ref, ckv_ref, kr_ref, cos_ref, sin_ref, qn_ref, kvn_ref, wqm_ref, wqs_ref,
                   wk_ref, wvt_ref, vone_ref, q_ref, k_ref, vt_ref, *, scale):
    cos = cos_ref[...]
    sin = sin_ref[...]
    cqn = _rms(cq_ref[...].astype(F32), qn_ref[...]).astype(BF16)
    qm = _dot(cqn, wqm_ref[...])
    qs = _dot(cqn, wqs_ref[...])
    ckn = _rms(ckv_ref[...].astype(F32), kvn_ref[...]).astype(BF16)
    km = _dot(ckn, wk_ref[...])
    kr = kr_ref[...].astype(F32)
    kpe = kr[:, :HEAD_PAD] * cos + kr[:, HEAD_PAD:] * sin
    for h in range(HEADS):
        sl = slice(h * HEAD_PAD, (h + 1) * HEAD_PAD)
        q_ref[:, sl] = ((qm[:, sl] * cos + qs[:, sl] * sin) * scale).astype(q_ref.dtype)
        k_ref[:, sl] = (km[:, sl] + kpe).astype(k_ref.dtype)
    vt = lax.dot_general(wvt_ref[...], ckn, (((1,), (1,)), ((), ())), preferred_element_type=F32)
    vt_ref[...] = (vt + vone_ref[...]).astype(vt_ref.dtype)


def _mla_prep(cq, ckv, kr, cos_t, sin_t, qn, kvn, wqm, wqs, wk, wvt, vone):
    s = cq.shape[0]
    tm = ROW_TILE
    width = HEADS * HEAD_PAD
    row = lambda n: pl.BlockSpec((tm, n), lambda i: (i, 0))
    scale = float((NOPE + ROPE) ** -0.5 * np.log2(np.e))
    return pl.pallas_call(
        functools.partial(_mla_prep_body, scale=scale),
        grid=(s // tm,),
        in_specs=[row(Q_RANK), row(KV_RANK), row(2 * HEAD_PAD), row(HEAD_PAD), row(HEAD_PAD),
                  _full((1, Q_RANK)), _full((1, KV_RANK)), _full(wqm.shape), _full(wqs.shape),
                  _full(wk.shape), _full(wvt.shape), _full((wvt.shape[0], 1))],
        out_specs=[row(width), row(width), pl.BlockSpec((wvt.shape[0], tm), lambda i: (0, i))],
        out_shape=[jax.ShapeDtypeStruct((s, width), BF16), jax.ShapeDtypeStruct((s, width), BF16),
                   jax.ShapeDtypeStruct((wvt.shape[0], s), BF16)],
        compiler_params=_params("arbitrary"),
        name="mla_prep",
    )(cq, ckv, kr, cos_t, sin_t, qn, kvn, wqm, wqs, wk, wvt, vone)


def _attn_body(q_ref, k_ref, vt_ref, o_ref, s_ref, mx_ref, m_ref, acc_ref):
    t = q_ref.shape[0]
    heads = q_ref.shape[1] // HEAD_PAD
    qi = pl.program_id(1)
    m_ref[...] = jnp.full(m_ref.shape, NEG, F32)
    acc_ref[...] = jnp.zeros(acc_ref.shape, F32)

    def scores(blk, slot):
        off = pl.multiple_of(blk * t, t)
        for h in range(heads):
            sl = slice(h * HEAD_PAD, (h + 1) * HEAD_PAD)
            s = lax.dot_general(k_ref[pl.ds(off, t), sl], q_ref[:, sl],
                                (((1,), (1,)), ((), ())), preferred_element_type=F32)
            s_ref[slot, h] = s
            mx_ref[slot, h] = jnp.broadcast_to(jnp.max(s, axis=0, keepdims=True), (STAT_ROWS, t))

    def consume(blk, slot, mask):
        off = pl.multiple_of(blk * t, t)
        for h in range(heads):
            sl = slice(h * HEAD_PAD, (h + 1) * HEAD_PAD)
            s = s_ref[slot, h]
            if mask is None:
                mx = mx_ref[slot, h]
            else:
                s = jnp.where(mask, s, NEG)
                mx = jnp.max(s, axis=0, keepdims=True)
            m_prev = m_ref[h]
            m_new = jnp.maximum(m_prev, mx)
            alpha = jnp.exp2(m_prev - m_new)
            pv = alpha[0:1] * acc_ref[h]
            half = t // 2
            for part in range(2):
                p = jnp.exp2(s[part * half:(part + 1) * half] - m_new[0:1]).astype(BF16)
                keys = pl.ds(pl.multiple_of(off + part * half, half), half)
                pv = pv + _dot(vt_ref[h * VT_ROWS:(h + 1) * VT_ROWS, keys], p)
            acc_ref[h] = pv
            m_ref[h] = m_new

    scores(0, 0)

    def run(blk, n):
        for u in range(n):
            scores(blk + u + 1, (u + 1) % 2)
            consume(blk + u, u % 2, None)

    def quad(j, carry):
        run(4 * j, 4)
        return carry

    lax.fori_loop(0, qi // 4, quad, 0)
    done = (qi // 4) * 4

    @pl.when(qi - done >= 2)
    def _():
        run(done, 2)

    row = lax.broadcasted_iota(jnp.int32, (t, t), 0)
    col = lax.broadcasted_iota(jnp.int32, (t, t), 1)
    causal = row <= col
    odd = lax.rem(qi, 2) == 1

    @pl.when(odd)
    def _():
        scores(qi, 1)
        consume(qi - 1, 0, None)
        consume(qi, 1, causal)

    @pl.when(jnp.logical_not(odd))
    def _():
        consume(qi, 0, causal)

    lower = lax.broadcasted_iota(jnp.int32, (t, HEAD_PAD), 1) < VDIM
    for j in range(heads // 2):
        outs = []
        for h in (2 * j, 2 * j + 1):
            acc = acc_ref[h]
            out_t = jnp.concatenate([acc / acc[VDIM:VDIM + 1], jnp.zeros((HEAD_PAD - VT_ROWS, t), F32)], axis=0)
            outs.append(out_t.T)
        packed = jnp.where(lower, outs[0], pltpu.roll(outs[1], VDIM, 1))
        o_ref[:, j * HEAD_PAD:(j + 1) * HEAD_PAD] = packed.astype(o_ref.dtype)


def _attention(q, k, vt):
    s, width = q.shape
    t = ATTN_TILE
    gw = ATTN_HEADS * HEAD_PAD
    return pl.pallas_call(
        _attn_body,
        grid=(width // gw, s // t),
        in_specs=[pl.BlockSpec((t, gw), lambda h, i: (i, h)),
                  pl.BlockSpec((s, gw), lambda h, i: (0, h)),
                  pl.BlockSpec((ATTN_HEADS * VT_ROWS, s), lambda h, i: (h, 0))],
        out_specs=pl.BlockSpec((t, ATTN_HEADS * VDIM), lambda h, i: (i, h)),
        out_shape=jax.ShapeDtypeStruct((s, HEADS * VDIM), BF16),
        scratch_shapes=[pltpu.VMEM((2, ATTN_HEADS, t, t), F32),
                        pltpu.VMEM((2, ATTN_HEADS, STAT_ROWS, t), F32),
                        pltpu.VMEM((ATTN_HEADS, STAT_ROWS, t), F32),
                        pltpu.VMEM((ATTN_HEADS, VT_ROWS, t), F32)],
        compiler_params=_params("arbitrary", "arbitrary"),
        name="mla_attention",
    )(q, k, vt)


def _merge_body(a_ref, gb_ref, o_ref, x_ref, wb_ref, wo_ref, gp_ref, gm_ref, out_ref):
    yb = _dot(o_ref[...], wb_ref[...])
    merged = a_ref[...].astype(F32) + jax.nn.sigmoid(gb_ref[...].astype(F32)) * yb
    y = _dot(merged.astype(BF16), wo_ref[...])
    out_ref[...] = x_ref[...] + gm_ref[...] * _rms(y, gp_ref[...])


def _merge(a, gb, o, x, wb, wo, gp, gm):
    s, d = x.shape
    tm = ROW_TILE
    row = pl.BlockSpec((tm, d), lambda i: (i, 0))
    return pl.pallas_call(
        _merge_body,
        grid=(s // tm,),
        in_specs=[row, row, pl.BlockSpec((tm, o.shape[1]), lambda i: (i, 0)), row,
                  _full(wb.shape), _full(wo.shape), _full((1, d)), _full((1, d))],
        out_specs=row,
        out_shape=jax.ShapeDtypeStruct((s, d), F32),
        compiler_params=_params("arbitrary"),
        name="mixer_merge",
    )(a, gb, o, x, wb, wo, gp, gm)


def _ffn_body(x_ref, g_ref, sc_ref, sh_ref, wg_ref, wu_ref, wd_ref, gp_ref, gf_ref, o_ref):
    x = x_ref[...]
    hb = (_rms(x, g_ref[...]) * (1.0 + sc_ref[...]) + sh_ref[...]).astype(BF16)
    a = _dot(hb, wg_ref[...])
    b = _dot(hb, wu_ref[...])
    y = _dot((a * jax.nn.sigmoid(a) * b).astype(BF16), wd_ref[...])
    o_ref[...] = x + gf_ref[...] * _rms(y, gp_ref[...])


def _dense_ffn(x, g, sc, sh, wg, wu, wd, gp, gf):
    s, d = x.shape
    tm = ROW_TILE
    row = pl.BlockSpec((tm, d), lambda i: (i, 0))
    vec = _full((1, d))
    return pl.pallas_call(
        _ffn_body,
        grid=(s // tm,),
        in_specs=[row, vec, vec, vec, _full(wg.shape), _full(wu.shape), _full(wd.shape), vec, vec],
        out_specs=row,
        out_shape=jax.ShapeDtypeStruct((s, d), F32),
        compiler_params=_params("arbitrary"),
        name="dense_ffn",
    )(x, g, sc, sh, wg, wu, wd, gp, gf)


def _route_body(x_ref, g_ref, sc_ref, sh_ref, wr_ref, br_ref, hb_ref, info_ref, infot_ref, cnt_ref):
    i = pl.program_id(0)
    tm = x_ref.shape[0]
    h = _rms(x_ref[...], g_ref[...]) * (1.0 + sc_ref[...]) + sh_ref[...]
    hb = h.astype(BF16)
    hb_ref[...] = hb
    h_lo = (h - hb.astype(F32)).astype(BF16)
    w = wr_ref[...]
    w_hi = w.astype(BF16)
    w_lo = (w - w_hi.astype(F32)).astype(BF16)
    logits = _dot(hb, w_hi) + (_dot(h_lo, w_hi) + _dot(hb, w_lo)) + br_ref[...]

    lane = lax.broadcasted_iota(jnp.int32, (tm, LANES), 1)
    m1 = jnp.max(logits, axis=-1, keepdims=True)
    i1 = jnp.min(jnp.where(logits == m1, lane, LANES), axis=-1, keepdims=True)
    oh1 = lane == i1
    rest = jnp.where(oh1, -3e38, logits)
    m2 = jnp.max(rest, axis=-1, keepdims=True)
    i2 = jnp.min(jnp.where(rest == m2, lane, LANES), axis=-1, keepdims=True)
    oh2 = lane == i2
    ex = jnp.exp(m2 - m1)
    g1 = 1.0 / (1.0 + ex)
    g2 = ex / (1.0 + ex)

    ohf = jnp.where(oh1 | oh2, 1.0, 0.0)
    r_idx = lax.broadcasted_iota(jnp.int32, (tm, tm), 0)
    c_idx = lax.broadcasted_iota(jnp.int32, (tm, tm), 1)
    earlier = jnp.where(c_idx < r_idx, 1.0, 0.0).astype(BF16)
    rank = _dot(earlier, ohf.astype(BF16))
    rank1 = jnp.sum(jnp.where(oh1, rank, 0.0), axis=-1, keepdims=True)
    rank2 = jnp.sum(jnp.where(oh2, rank, 0.0), axis=-1, keepdims=True)
    info = jnp.where(lane == 0, i1.astype(F32),
           jnp.where(lane == 1, i2.astype(F32),
           jnp.where(lane == 2, g1,
           jnp.where(lane == 3, g2,
           jnp.where(lane == 4, rank1,
           jnp.where(lane == 5, rank2, 0.0))))))
    info_ref[...] = info
    infot_ref[...] = info.T[:ROUTE_FIELDS]
    cnt = jnp.sum(ohf, axis=0, keepdims=True).astype(jnp.int32)
    cnt_al = ((cnt + (BF16_ROWS - 1)) // BF16_ROWS) * BF16_ROWS
    for e in range(N_EXPERTS):
        cnt_ref[i * N_EXPERTS + e] = cnt_al[0, e]


def _route(x, g, sc, sh, wr, br):
    s, d = x.shape
    tm = ROW_TILE
    nt = s // tm
    row = pl.BlockSpec((tm, d), lambda i: (i, 0))
    return pl.pallas_call(
        _route_body,
        grid=(nt,),
        in_specs=[row, _full((1, d)), _full((1, d)), _full((1, d)), _full(wr.shape), _full(br.shape)],
        out_specs=[row,
                   pl.BlockSpec((tm, LANES), lambda i: (i, 0)),
                   pl.BlockSpec((ROUTE_FIELDS, tm), lambda i: (0, i)),
                   pl.BlockSpec(memory_space=pltpu.SMEM)],
        out_shape=[jax.ShapeDtypeStruct((s, d), BF16),
                   jax.ShapeDtypeStruct((s, LANES), F32),
                   jax.ShapeDtypeStruct((ROUTE_FIELDS, s), F32),
                   jax.ShapeDtypeStruct((nt * N_EXPERTS,), jnp.int32)],
        compiler_params=_params("arbitrary"),
        name="moe_route",
    )(x, g, sc, sh, wr, br)


def _routing(info):
    return [(info[:, k:k + 1], info[:, 2 + k:3 + k], info[:, 4 + k:5 + k]) for k in range(2)]


def _dispatch_body(offs_ref, cnts_ref, fill_ref, nval_ref, hb_ref, infot_ref, xs_ref, xbuf, zbuf, sem, zsem, *, nblk):
    i = pl.program_id(0)
    tm = hb_ref.shape[0]
    tb = zbuf.shape[0]

    def zero_block(blk):
        cp = pltpu.make_async_copy(zbuf, xs_ref.at[pl.ds(pl.multiple_of(blk * tb, tb), tb)], zsem)
        cp.start()
        cp.wait()

    @pl.when(i == 0)
    def _():
        zbuf[...] = jnp.zeros(zbuf.shape, zbuf.dtype)
        for n in range(fill_ref.shape[0]):
            @pl.when(fill_ref[n] >= 0)
            def _():
                zero_block(fill_ref[n])

        def unused(blk, carry):
            zero_block(blk)
            return carry

        lax.fori_loop(nval_ref[0], nblk, unused, 0)

    hb = hb_ref[...]
    fields = infot_ref[...]
    e1, e2, r1, r2 = fields[0:1], fields[1:2], fields[4:5], fields[5:6]
    chunks = tm // DISPATCH_ROWS
    buf = lax.rem(i, 2)

    def first_slot(e, r):
        return jnp.where(r < float(FIRST_ROWS), e * float(FIRST_ROWS) + r, -1.0)

    p1 = first_slot(e1, r1)
    p2 = first_slot(e2, r2)
    slot = lax.broadcasted_iota(jnp.int32, (N_EXPERTS * FIRST_ROWS, tm), 0).astype(F32)
    sel = jnp.where((p1 == slot) | (p2 == slot), 1.0, 0.0).astype(BF16)
    xbuf[buf, 0] = _dot(sel, hb).astype(BF16)

    first_chunks = FIRST_ROWS // DISPATCH_ROWS

    def buf_rows(e, c):
        return c // first_chunks, e * FIRST_ROWS + (c % first_chunks) * DISPATCH_ROWS

    for e in range(N_EXPERTS):
        for c in range(first_chunks, chunks):
            @pl.when(cnts_ref[i * N_EXPERTS + e] > c * DISPATCH_ROWS)
            def _():
                first = e1 == float(e)
                rank = jnp.where(first, r1, r2)
                late = (lax.broadcasted_iota(jnp.int32, (DISPATCH_ROWS, tm), 0) + c * DISPATCH_ROWS).astype(F32)
                hit = (rank == late) & (first | (e2 == float(e)))
                part, row0 = buf_rows(e, c)
                xbuf[buf, part, row0:row0 + DISPATCH_ROWS] = _dot(jnp.where(hit, 1.0, 0.0).astype(BF16),
                                                                  hb).astype(BF16)

    def copy(b, e, c, dst):
        part, row0 = buf_rows(e, c)
        return pltpu.make_async_copy(xbuf.at[b, part, pl.ds(row0, DISPATCH_ROWS)],
                                     xs_ref.at[pl.ds(dst, DISPATCH_ROWS)], sem.at[b, e, c])

    def for_each_copy(step, fn):
        for e in range(N_EXPERTS):
            for c in range(chunks):
                @pl.when(cnts_ref[step * N_EXPERTS + e] > c * DISPATCH_ROWS)
                def _():
                    fn(e, c)

    @pl.when(i > 0)
    def _():
        for_each_copy(i - 1, lambda e, c: copy(1 - buf, e, c, 0).wait())

    def start(e, c):
        off = offs_ref[i * N_EXPERTS + e]
        copy(buf, e, c, pl.multiple_of(off + c * DISPATCH_ROWS, BF16_ROWS)).start()

    for_each_copy(i, start)

    @pl.when(i == pl.num_programs(0) - 1)
    def _():
        for_each_copy(i, lambda e, c: copy(buf, e, c, 0).wait())


def _dispatch(offs, cnts, fill, nval, hb, infot, nblk):
    s, d = hb.shape
    tm = ROW_TILE
    tb = EXPERT_ROWS
    grid_spec = pltpu.PrefetchScalarGridSpec(
        num_scalar_prefetch=4,
        grid=(s // tm,),
        in_specs=[pl.BlockSpec((tm, d), lambda i, *_: (i, 0)),
                  pl.BlockSpec((ROUTE_FIELDS, tm), lambda i, *_: (0, i))],
        out_specs=pl.BlockSpec(memory_space=pl.ANY),
        scratch_shapes=[pltpu.VMEM((2, tm // FIRST_ROWS, N_EXPERTS * FIRST_ROWS, d), BF16),
                        pltpu.VMEM((tb, d), BF16),
                        pltpu.SemaphoreType.DMA((2, N_EXPERTS, tm // DISPATCH_ROWS)),
                        pltpu.SemaphoreType.DMA(())],
    )
    return pl.pallas_call(
        functools.partial(_dispatch_body, nblk=nblk),
        grid_spec=grid_spec,
        out_shape=jax.ShapeDtypeStruct((nblk * tb, d), BF16),
        compiler_params=_params("arbitrary"),
        name="moe_dispatch",
    )(offs, cnts, fill, nval, hb, infot)


def _expert_body(bexp_ref, nval_ref, x_ref, w1_ref, w3_ref, w2_ref, o_ref, acc_ref):
    i = pl.program_id(0)
    f = pl.program_id(1)
    last = pl.num_programs(1) - 1
    used = i < nval_ref[0]

    @pl.when(used)
    def _():
        xb = x_ref[...]
        a = _dot(xb, w1_ref[0])
        b = _dot(xb, w3_ref[0])
        part = _dot((a * jax.nn.sigmoid(a) * b).astype(BF16), w2_ref[0])

        @pl.when(f == 0)
        def _():
            acc_ref[...] = part

        @pl.when(f > 0)
        def _():
            acc_ref[...] += part

        @pl.when(f == last)
        def _():
            o_ref[...] = acc_ref[...].astype(o_ref.dtype)

    @pl.when(jnp.logical_not(used) & (f == last))
    def _():
        o_ref[...] = jnp.zeros(o_ref.shape, o_ref.dtype)


def _experts(xs, w1, w3, w2, bexp, nval, tf):
    rows, d = xs.shape
    ff = w1.shape[2]
    tb = EXPERT_ROWS
    nff = ff // tf

    def xrow(i, f, be, nv):
        return (jnp.minimum(i, nv[0] - 1), 0)

    def fcol(i, f, nv):
        return jnp.where(i < nv[0], f, nff - 1)

    grid_spec = pltpu.PrefetchScalarGridSpec(
        num_scalar_prefetch=2,
        grid=(rows // tb, nff),
        in_specs=[pl.BlockSpec((tb, d), xrow),
                  pl.BlockSpec((1, d, tf), lambda i, f, be, nv: (be[i], 0, fcol(i, f, nv))),
                  pl.BlockSpec((1, d, tf), lambda i, f, be, nv: (be[i], 0, fcol(i, f, nv))),
                  pl.BlockSpec((1, tf, d), lambda i, f, be, nv: (be[i], fcol(i, f, nv), 0))],
        out_specs=pl.BlockSpec((tb, d), lambda i, f, be, nv: (i, 0)),
        scratch_shapes=[pltpu.VMEM((tb, d), F32)],
    )
    return pl.pallas_call(
        _expert_body,
        grid_spec=grid_spec,
        out_shape=jax.ShapeDtypeStruct((rows, d), BF16),
        compiler_params=_params("arbitrary", "arbitrary"),
        name="moe_experts",
    )(bexp, nval, xs, w1, w3, w2)


def _combine_body(offs_ref, cnts_ref, info_ref, x_ref, gp_ref, gf_ref, ys_ref, o_ref, ybuf, acc_ref, sem):
    i = pl.program_id(0)
    nt = pl.num_programs(0)
    tm = x_ref.shape[0]
    chunks = tm // COMBINE_ROWS
    buf = lax.rem(i, 2)

    def copy(b, e, c, src):
        dst = ybuf.at[b, pl.ds((c * N_EXPERTS + e) * COMBINE_ROWS, COMBINE_ROWS)]
        return pltpu.make_async_copy(ys_ref.at[pl.ds(src, COMBINE_ROWS)], dst, sem.at[b, e, c])

    def fetch(step, b):
        for e in range(N_EXPERTS):
            off = offs_ref[step * N_EXPERTS + e]
            copy(b, e, 0, pl.multiple_of(off, BF16_ROWS)).start()
            for c in range(1, chunks):
                @pl.when(cnts_ref[step * N_EXPERTS + e] > c * COMBINE_ROWS)
                def _():
                    copy(b, e, c, pl.multiple_of(off + c * COMBINE_ROWS, BF16_ROWS)).start()

    @pl.when(i == 0)
    def _():
        fetch(0, 0)

    @pl.when(i + 1 < nt)
    def _():
        fetch(i + 1, 1 - buf)

    (e1, g1, r1), (e2, g2, r2) = _routing(info_ref[...])
    for e in range(N_EXPERTS):
        copy(buf, e, 0, 0).wait()
    stack = N_EXPERTS * COMBINE_ROWS
    slot = lax.broadcasted_iota(jnp.int32, (tm, stack), 1).astype(F32)
    y_first = ybuf[buf, 0:stack]
    acc = None
    for ek, gk, rk in ((e1, g1, r1), (e2, g2, r2)):
        pos = jnp.where(rk < float(COMBINE_ROWS), ek * float(COMBINE_ROWS) + rk, -1.0)
        term = gk * _dot(jnp.where(pos == slot, 1.0, 0.0).astype(BF16), y_first)
        acc = term if acc is None else acc + term
    acc_ref[...] = acc

    late = lax.broadcasted_iota(jnp.int32, (tm, COMBINE_ROWS), 1).astype(F32)
    for e in range(N_EXPERTS):
        for c in range(1, chunks):
            @pl.when(cnts_ref[i * N_EXPERTS + e] > c * COMBINE_ROWS)
            def _():
                first = e1 == float(e)
                second = e2 == float(e)
                rank = jnp.where(first, r1, r2)
                gate = jnp.where(first, g1, jnp.where(second, g2, 0.0))
                copy(buf, e, c, 0).wait()
                hit = (rank == late + float(c * COMBINE_ROWS)) & (first | second)
                rows = ybuf[buf, pl.ds((c * N_EXPERTS + e) * COMBINE_ROWS, COMBINE_ROWS)]
                acc_ref[...] += gate * _dot(jnp.where(hit, 1.0, 0.0).astype(BF16), rows)

    o_ref[...] = x_ref[...] + gf_ref[...] * _rms(acc_ref[...], gp_ref[...])


def _combine(offs, cnts, info, x, gp, gf, ys):
    s, d = x.shape
    tm = ROW_TILE
    chunks = tm // COMBINE_ROWS
    grid_spec = pltpu.PrefetchScalarGridSpec(
        num_scalar_prefetch=2,
        grid=(s // tm,),
        in_specs=[pl.BlockSpec((tm, LANES), lambda i, *_: (i, 0)),
                  pl.BlockSpec((tm, d), lambda i, *_: (i, 0)),
                  pl.BlockSpec((1, d), lambda i, *_: (0, 0)),
                  pl.BlockSpec((1, d), lambda i, *_: (0, 0)),
                  pl.BlockSpec(memory_space=pl.ANY)],
        out_specs=pl.BlockSpec((tm, d), lambda i, *_: (i, 0)),
        scratch_shapes=[pltpu.VMEM((2, chunks * N_EXPERTS * COMBINE_ROWS, d), BF16),
                        pltpu.VMEM((tm, d), F32),
                        pltpu.SemaphoreType.DMA((2, N_EXPERTS, chunks))],
    )
    return pl.pallas_call(
        _combine_body,
        grid_spec=grid_spec,
        out_shape=jax.ShapeDtypeStruct((s, d), F32),
        compiler_params=_params("arbitrary"),
        name="moe_combine",
    )(offs, cnts, info, x, gp, gf, ys)


def _rot_half_cols(w):
    half = ROPE // 2
    return jnp.concatenate([-w[..., half:], w[..., :half]], axis=-1)


def _pad_cols(w, before, total):
    return jnp.pad(w, ((0, 0), (before, total - before - w.shape[1])))


def _mixer_weights(w_in, w_uq, w_ukv, w_branch_b):
    d = w_in.shape[0]
    o = 0
    parts = {}
    for name, n in (("u", GMLP_WIDTH), ("v", GMLP_WIDTH), ("cq", Q_RANK), ("ckv", KV_RANK),
                    ("kr", ROPE), ("ga", d), ("gb", d)):
        parts[name] = w_in[:, o:o + n]
        o += n
    kr_main = _pad_cols(parts["kr"], NOPE, HEAD_PAD)
    kr_swap = _pad_cols(_rot_half_cols(parts["kr"]), NOPE, HEAD_PAD)
    w_in_p = jnp.concatenate([parts["u"], parts["v"], parts["cq"], parts["ckv"], kr_main, kr_swap,
                              parts["ga"], parts["gb"]], axis=1).astype(BF16)

    wq = w_uq.reshape(Q_RANK, HEADS, NOPE + ROPE)
    zq = jnp.zeros((Q_RANK, HEADS, HEAD_PAD - NOPE - ROPE), w_uq.dtype)
    wq_main = jnp.concatenate([wq, zq], axis=-1).reshape(Q_RANK, HEADS * HEAD_PAD).astype(BF16)
    wq_swap = jnp.concatenate([jnp.zeros((Q_RANK, HEADS, NOPE), w_uq.dtype),
                               _rot_half_cols(wq[..., NOPE:]), zq], axis=-1)
    wq_swap = wq_swap.reshape(Q_RANK, HEADS * HEAD_PAD).astype(BF16)

    wkv = w_ukv.reshape(KV_RANK, HEADS, NOPE + VDIM)
    zk = jnp.zeros((KV_RANK, HEADS, HEAD_PAD - NOPE), w_ukv.dtype)
    wk = jnp.concatenate([wkv[..., :NOPE], zk], axis=-1).reshape(KV_RANK, HEADS * HEAD_PAD).astype(BF16)
    zv = jnp.zeros((KV_RANK, HEADS, VT_ROWS - VDIM), w_ukv.dtype)
    wv = jnp.concatenate([wkv[..., NOPE:], zv], axis=-1).reshape(KV_RANK, HEADS * VT_ROWS).T.astype(BF16)

    return w_in_p, wq_main, wq_swap, wk, wv, w_branch_b.astype(BF16)


def _expert_layout(cnts, nt, nblk):
    tb = EXPERT_ROWS
    c = cnts.reshape(nt, N_EXPERTS)
    total = jnp.sum(c, axis=0)
    per = jnp.where(total > 0, (total + REGION_SLACK + tb - 1) // tb, 0)
    stop = jnp.cumsum(per)
    start = stop - per
    offs = start[None, :] * tb + (jnp.cumsum(c, axis=0) - c)
    nval = stop[-1]
    blk = jnp.minimum(jnp.arange(nblk, dtype=jnp.int32), nval - 1)
    bexp = jnp.minimum(jnp.sum(blk[:, None] >= stop[None, :], axis=1), N_EXPERTS - 1)
    fill = jnp.concatenate([jnp.where(per > 0, stop - 1, -1), jnp.where(per > 1, stop - 2, -1)])
    i32 = lambda v: v.astype(jnp.int32)
    return i32(offs.reshape(-1)), i32(bexp), i32(nval.reshape(1)), i32(fill)


def kernel(x, c, positions, ada_w, ada_b, norm_mix_pre, norm_mix_post, norm_ffn_pre, norm_ffn_post, w_in, gmlp_ln_g, gmlp_ln_b, gmlp_ws, gmlp_bs, mla_q_norm, mla_w_uq, mla_kv_norm, mla_w_ukv, w_branch_a, w_branch_b, w_out, ffn_w_gate, ffn_w_up, ffn_w_down, moe_router, moe_router_bias, moe_w1, moe_w3, moe_w2):
    batch, s, d = x.shape
    assert batch == 1 and s % ROW_TILE == 0 and s % ATTN_TILE == 0
    depth = ada_w.shape[0]
    xs = x.reshape(s, d)
    mod = _ada_mod(c, ada_w, ada_b)
    cos_t, sin_t = _rope_tables(positions)
    vone = jnp.zeros((HEADS, VT_ROWS), F32).at[:, VDIM].set(1.0).reshape(HEADS * VT_ROWS, 1)
    row = lambda v: v.reshape(1, -1)

    for l in range(depth):
        sh_m, sc_m, g_m, sh_f, sc_f, g_f = [mod[l, :, k * d:(k + 1) * d] for k in range(6)]
        w_in_p, wq_main, wq_swap, wk, wv, wb = _mixer_weights(w_in[l], mla_w_uq[l], mla_w_ukv[l], w_branch_b[l])
        uv, cq, ckv, kr, ga, gb = _inproj(xs, row(norm_mix_pre[l]), sc_m, sh_m, w_in_p)
        bias = jnp.broadcast_to(gmlp_bs[l].T[:, :, None], (CHUNK, GMLP_GROUPS, GMLP_GROUP_DIM))
        bias = bias.reshape(CHUNK, GMLP_WIDTH)
        a = _gmlp(uv, ga, row(gmlp_ln_g[l]), row(gmlp_ln_b[l]), gmlp_ws[l], bias, w_branch_a[l].astype(BF16))
        q, k, v = _mla_prep(cq, ckv, kr, cos_t, sin_t, row(mla_q_norm[l]), row(mla_kv_norm[l]),
                            wq_main, wq_swap, wk, wv, vone)
        o = _attention(q, k, v)
        xs = _merge(a, gb, o, xs, wb, w_out[l].astype(BF16), row(norm_mix_post[l]), g_m)

        j = l // 2
        if l % 2 == 0:
            ff = ffn_w_gate.shape[2]
            ff_pad = -(-ff // (2 * LANES)) * (2 * LANES)
            wg = jnp.pad(ffn_w_gate[j], ((0, 0), (0, ff_pad - ff))).astype(BF16)
            wu = jnp.pad(ffn_w_up[j], ((0, 0), (0, ff_pad - ff))).astype(BF16)
            wd = jnp.pad(ffn_w_down[j], ((0, ff_pad - ff), (0, 0))).astype(BF16)
            xs = _dense_ffn(xs, row(norm_ffn_pre[l]), sc_f, sh_f, wg, wu, wd, row(norm_ffn_post[l]), g_f)
        else:
            nt = s // ROW_TILE
            tb = EXPERT_ROWS
            max_rows = 2 * s + nt * N_EXPERTS * (BF16_ROWS - 1) + N_EXPERTS * (REGION_SLACK + tb - 1)
            nblk = -(-max_rows // tb) + 1
            wr = jnp.pad(moe_router[j], ((0, 0), (0, LANES - N_EXPERTS)))
            br = jnp.pad(moe_router_bias[j], (0, LANES - N_EXPERTS), constant_values=NEG).reshape(1, LANES)
            hb, info, infot, cnts = _route(xs, row(norm_ffn_pre[l]), sc_f, sh_f, wr, br)
            offs, bexp, nval, fill = _expert_layout(cnts, nt, nblk)
            xsort = _dispatch(offs, cnts, fill, nval, hb, infot, nblk)
            ysort = _experts(xsort, moe_w1[j].astype(BF16), moe_w3[j].astype(BF16), moe_w2[j].astype(BF16),
                             bexp, nval, tf=moe_w1.shape[3] // 2)
            xs = _combine(offs, cnts, info, xs, row(norm_ffn_post[l]), g_f, ysort)
    return xs.reshape(batch, s, d)
```

```python
import functools

import jax
import jax.numpy as jnp
import numpy as np
from jax import lax
from jax.experimental import pallas as pl
from jax.experimental.pallas import tpu as pltpu

F32 = jnp.float32
BF16 = jnp.bfloat16

EPS = 1e-6
LANES = 128
BF16_ROWS = 16
GMLP_GROUPS = 8
GMLP_GROUP_DIM = 64
GMLP_WIDTH = GMLP_GROUPS * GMLP_GROUP_DIM
CHUNK = 128
HEADS = 8
NOPE = 64
ROPE = 32
VDIM = 64
HEAD_PAD = 128
VT_ROWS = 80
Q_RANK = 384
KV_RANK = 256
ROPE_THETA = 10000.0
N_EXPERTS = 8
NEG = -1e30

ROW_TILE = 512
ATTN_TILE = 512
ATTN_HEADS = 2
STAT_ROWS = 8
EXPERT_ROWS = 512
DISPATCH_ROWS = 128
COMBINE_ROWS = 256
FIRST_ROWS = 256
ROUTE_FIELDS = 8
REGION_SLACK = DISPATCH_ROWS - BF16_ROWS
VMEM_LIMIT = 52 * 1024 * 1024


def _params(*sem):
    return pltpu.CompilerParams(dimension_semantics=sem, vmem_limit_bytes=VMEM_LIMIT)


def _dot(a, b):
    return jnp.dot(a, b, preferred_element_type=F32)


def _rms(x, g):
    return x * lax.rsqrt(jnp.mean(x * x, axis=-1, keepdims=True) + EPS) * g


def _gelu(x):
    return 0.5 * x * (1.0 + lax.erf(x * np.float32(0.7071067811865476)))


def _full(shape):
    return pl.BlockSpec(shape, lambda *_: (0,) * len(shape), pipeline_mode=pl.Buffered(1))


def _mod_body(c_ref, w_ref, b_ref, o_ref):
    c = c_ref[...]
    ca = c * jax.nn.sigmoid(c)
    o_ref[0] = jnp.sum(ca * w_ref[0], axis=0, keepdims=True) + b_ref[0]


def _ada_mod(c, ada_w, ada_b):
    n_layers, d, n = ada_w.shape
    tn = n // 4
    return pl.pallas_call(
        _mod_body,
        grid=(n_layers, n // tn),
        in_specs=[pl.BlockSpec((d, 1), lambda l, j: (0, 0)),
                  pl.BlockSpec((1, d, tn), lambda l, j: (l, 0, j)),
                  pl.BlockSpec((1, 1, tn), lambda l, j: (l, 0, j))],
        out_specs=pl.BlockSpec((1, 1, tn), lambda l, j: (l, 0, j)),
        out_shape=jax.ShapeDtypeStruct((n_layers, 1, n), F32),
        compiler_params=_params("arbitrary", "arbitrary"),
        name="ada_mod",
    )(c.reshape(d, 1), ada_w, ada_b.reshape(n_layers, 1, n))


def _rope_body(pos_ref, invf_ref, cos_ref, sin_ref):
    ang = pos_ref[...].astype(F32) * invf_ref[...]
    cos_ref[...] = jnp.cos(ang)
    sin_ref[...] = jnp.sin(ang)


def _rope_tables(positions):
    s = positions.shape[-1]
    half = ROPE // 2
    per_row = LANES // half
    inv_freq = 1.0 / (ROPE_THETA ** (jnp.arange(0, ROPE, 2, dtype=F32) / ROPE))
    pos_dense = jnp.broadcast_to(positions.reshape(s, 1), (s, half)).reshape(s // per_row, LANES)
    invf = jnp.tile(inv_freq, per_row).reshape(1, LANES)
    rows = s // per_row
    tr = rows // 4
    cos_d, sin_d = pl.pallas_call(
        _rope_body,
        grid=(rows // tr,),
        in_specs=[pl.BlockSpec((tr, LANES), lambda i: (i, 0)), _full((1, LANES))],
        out_specs=[pl.BlockSpec((tr, LANES), lambda i: (i, 0))] * 2,
        out_shape=[jax.ShapeDtypeStruct((rows, LANES), F32)] * 2,
        compiler_params=_params("arbitrary"),
        name="rope_tables",
    )(pos_dense, invf)
    cos = cos_d.reshape(s, half)
    sin = sin_d.reshape(s, half)
    ones = jnp.ones((s, NOPE), F32)
    zeros = jnp.zeros((s, NOPE), F32)
    pad = jnp.zeros((s, HEAD_PAD - NOPE - ROPE), F32)
    cos_t = jnp.concatenate([ones, cos, cos, pad], axis=1)
    sin_t = jnp.concatenate([zeros, sin, sin, pad], axis=1)
    return cos_t, sin_t


def _inproj_body(x_ref, g_ref, sc_ref, sh_ref, w_ref, uv_ref, cq_ref, ckv_ref, kr_ref, ga_ref, gb_ref):
    h = _rms(x_ref[...], g_ref[...]) * (1.0 + sc_ref[...]) + sh_ref[...]
    proj = _dot(h.astype(BF16), w_ref[...])
    col = 0
    for ref in (uv_ref, cq_ref, ckv_ref, kr_ref, ga_ref, gb_ref):
        n = ref.shape[1]
        ref[...] = proj[:, col:col + n].astype(ref.dtype)
        col += n


def _inproj(x, g, sc, sh, w):
    s, d = x.shape
    tm = ROW_TILE
    widths = (2 * GMLP_WIDTH, Q_RANK, KV_RANK, 2 * HEAD_PAD, d, d)
    row = lambda n: pl.BlockSpec((tm, n), lambda i: (i, 0))
    return pl.pallas_call(
        _inproj_body,
        grid=(s // tm,),
        in_specs=[row(d), _full((1, d)), _full((1, d)), _full((1, d)), _full(w.shape)],
        out_specs=[row(n) for n in widths],
        out_shape=[jax.ShapeDtypeStruct((s, n), BF16) for n in widths],
        compiler_params=_params("arbitrary"),
        name="mixer_inproj",
    )(x, g, sc, sh, w)


def _gmlp_body(uv_ref, ga_ref, lng_ref, lnb_ref, ws_ref, bias_ref, wa_ref, o_ref):
    tm = uv_ref.shape[0]
    u = uv_ref[:, :GMLP_WIDTH].astype(F32)
    v = uv_ref[:, GMLP_WIDTH:].astype(F32)
    gu = _gelu(u)
    gv = _gelu(v)
    mu = jnp.mean(gv, axis=-1, keepdims=True)
    xc = gv - mu
    vn = xc * lax.rsqrt(jnp.mean(xc * xc, axis=-1, keepdims=True) + EPS) * lng_ref[...] + lnb_ref[...]
    vb = vn.astype(BF16)
    t_idx = lax.broadcasted_iota(jnp.int32, (CHUNK, CHUNK), 0)
    s_idx = lax.broadcasted_iota(jnp.int32, (CHUNK, CHUNK), 1)
    causal = s_idx <= t_idx
    ws = [jnp.where(causal, ws_ref[g], 0.0).astype(BF16) for g in range(GMLP_GROUPS)]
    left = lax.broadcasted_iota(jnp.int32, (CHUNK, LANES), 1) < GMLP_GROUP_DIM
    bias = bias_ref[...]
    z_rows = []
    for c in range(tm // CHUNK):
        vc = vb[c * CHUNK:(c + 1) * CHUNK]
        z_cols = []
        for j in range(GMLP_WIDTH // LANES):
            vp = vc[:, j * LANES:(j + 1) * LANES]
            z_cols.append(jnp.where(left, _dot(ws[2 * j], vp), _dot(ws[2 * j + 1], vp)))
        z_rows.append(jnp.concatenate(z_cols, axis=1) + bias)
    z = jnp.concatenate(z_rows, axis=0)
    gated = (gu * z).astype(BF16)
    ya = _dot(gated, wa_ref[...])
    o_ref[...] = (jax.nn.sigmoid(ga_ref[...].astype(F32)) * ya).astype(o_ref.dtype)


def _gmlp(uv, ga, lng, lnb, ws, bias, wa):
    s, d = ga.shape
    tm = ROW_TILE
    row = lambda n: pl.BlockSpec((tm, n), lambda i: (i, 0))
    return pl.pallas_call(
        _gmlp_body,
        grid=(s // tm,),
        in_specs=[row(2 * GMLP_WIDTH), row(d), _full((1, GMLP_WIDTH)), _full((1, GMLP_WIDTH)),
                  _full(ws.shape), _full(bias.shape), _full(wa.shape)],
        out_specs=row(d),
        out_shape=jax.ShapeDtypeStruct((s, d), BF16),
        compiler_params=_params("arbitrary"),
        name="mixer_gmlp",
    )(uv, ga, lng, lnb, ws, bias, wa)


def _mla_prep_body(cq_ref, ckv_ref, kr_ref, cos_ref, sin_ref, qn_ref, kvn_ref, wqm_ref, wqs_ref,
                   wk_ref, wvt_ref, vone_ref, q_ref, k_ref, vt_ref, *, scale):
    cos = cos_ref[...]
    sin = sin_ref[...]
    cqn = _rms(cq_ref[...].astype(F32), qn_ref[...]).astype(BF16)
    qm = _dot(cqn, wqm_ref[...])
    qs = _dot(cqn, wqs_ref[...])
    ckn = _rms(ckv_ref[...].astype(F32), kvn_ref[...]).astype(BF16)
    km = _dot(ckn, wk_ref[...])
    kr = kr_ref[...].astype(F32)
    kpe = kr[:, :HEAD_PAD] * cos + kr[:, HEAD_PAD:] * sin
    for h in range(HEADS):
        sl = slice(h * HEAD_PAD, (h + 1) * HEAD_PAD)
        q_ref[:, sl] = ((qm[:, sl] * cos + qs[:, sl] * sin) * scale).astype(q_ref.dtype)
        k_ref[:, sl] = (km[:, sl] + kpe).astype(k_ref.dtype)
    vt = lax.dot_general(wvt_ref[...], ckn, (((1,), (1,)), ((), ())), preferred_element_type=F32)
    vt_ref[...] = (vt + vone_ref[...]).astype(vt_ref.dtype)


def _mla_prep(cq, ckv, kr, cos_t, sin_t, qn, kvn, wqm, wqs, wk, wvt, vone):
    s = cq.shape[0]
    tm = ROW_TILE
    width = HEADS * HEAD_PAD
    row = lambda n: pl.BlockSpec((tm, n), lambda i: (i, 0))
    scale = float((NOPE + ROPE) ** -0.5 * np.log2(np.e))
    return pl.pallas_call(
        functools.partial(_mla_prep_body, scale=scale),
        grid=(s // tm,),
        in_specs=[row(Q_RANK), row(KV_RANK), row(2 * HEAD_PAD), row(HEAD_PAD), row(HEAD_PAD),
                  _full((1, Q_RANK)), _full((1, KV_RANK)), _full(wqm.shape), _full(wqs.shape),
                  _full(wk.shape), _full(wvt.shape), _full((wvt.shape[0], 1))],
        out_specs=[row(width), row(width), pl.BlockSpec((wvt.shape[0], tm), lambda i: (0, i))],
        out_shape=[jax.ShapeDtypeStruct((s, width), BF16), jax.ShapeDtypeStruct((s, width), BF16),
                   jax.ShapeDtypeStruct((wvt.shape[0], s), BF16)],
        compiler_params=_params("arbitrary"),
        name="mla_prep",
    )(cq, ckv, kr, cos_t, sin_t, qn, kvn, wqm, wqs, wk, wvt, vone)


def _attn_body(q_ref, k_ref, vt_ref, o_ref, s_ref, mx_ref, m_ref, acc_ref):
    t = o_ref.shape[0]
    heads = q_ref.shape[1] // HEAD_PAD
    qi = pl.program_id(1)
    m_ref[...] = jnp.full(m_ref.shape, NEG, F32)
    acc_ref[...] = jnp.zeros(acc_ref.shape, F32)

    def scores(blk, slot, q_tile):
        off = pl.multiple_of(blk * t, t)
        q_off = pl.multiple_of(q_tile * t, t)
        for h in range(heads):
            sl = slice(h * HEAD_PAD, (h + 1) * HEAD_PAD)
            s = lax.dot_general(k_ref[pl.ds(off, t), sl], q_ref[pl.ds(q_off, t), sl],
                                (((1,), (1,)), ((), ())), preferred_element_type=F32)
            s_ref[slot, h] = s
            mx_ref[slot, h] = jnp.broadcast_to(jnp.max(s, axis=0, keepdims=True), (STAT_ROWS, t))

    def consume(blk, slot, mask):
        off = pl.multiple_of(blk * t, t)
        for h in range(heads):
            sl = slice(h * HEAD_PAD, (h + 1) * HEAD_PAD)
            s = s_ref[slot, h]
            if mask is None:
                mx = mx_ref[slot, h]
            else:
                s = jnp.where(mask, s, NEG)
                mx = jnp.max(s, axis=0, keepdims=True)
            m_prev = m_ref[h]
            m_new = jnp.maximum(m_prev, mx)
            alpha = jnp.exp2(m_prev - m_new)
            pv = alpha[0:1] * acc_ref[h]
            half = t // 2
            for part in range(2):
                p = jnp.exp2(s[part * half:(part + 1) * half] - m_new[0:1]).astype(BF16)
                keys = pl.ds(pl.multiple_of(off + part * half, half), half)
                pv = pv + _dot(vt_ref[h * VT_ROWS:(h + 1) * VT_ROWS, keys], p)
            acc_ref[h] = pv
            m_ref[h] = m_new

    @pl.when(qi == 0)
    def _():
        scores(0, 0, qi)

    def run(blk, n):
        for u in range(n):
            scores(blk + u + 1, (u + 1) % 2, qi)
            consume(blk + u, u % 2, None)

    def octet(j, carry):
        run(8 * j, 8)
        return carry

    lax.fori_loop(0, qi // 8, octet, 0)
    done = (qi // 8) * 8
    for n in (4, 2):
        more = qi - done >= n

        @pl.when(more)
        def _():
            run(done, n)

        done = done + jnp.where(more, n, 0)

    row = lax.broadcasted_iota(jnp.int32, (t, t), 0)
    col = lax.broadcasted_iota(jnp.int32, (t, t), 1)
    causal = row <= col
    odd = lax.rem(qi, 2) == 1

    @pl.when(odd)
    def _():
        scores(qi, 1, qi)
        consume(qi - 1, 0, None)
        consume(qi, 1, causal)

    @pl.when(jnp.logical_not(odd))
    def _():
        consume(qi, 0, causal)

    def finalize():
        lower = lax.broadcasted_iota(jnp.int32, (t, HEAD_PAD), 1) < VDIM
        for j in range(heads // 2):
            outs = []
            for h in (2 * j, 2 * j + 1):
                acc = acc_ref[h]
                out_t = jnp.concatenate([acc / acc[VDIM:VDIM + 1], jnp.zeros((HEAD_PAD - VT_ROWS, t), F32)],
                                        axis=0)
                outs.append(out_t.T)
            packed = jnp.where(lower, outs[0], pltpu.roll(outs[1], VDIM, 1))
            o_ref[:, j * HEAD_PAD:(j + 1) * HEAD_PAD] = packed.astype(o_ref.dtype)

    last = qi == pl.num_programs(1) - 1

    @pl.when(jnp.logical_not(last))
    def _():
        scores(0, 0, qi + 1)
        finalize()

    @pl.when(last)
    def _():
        finalize()


def _attention(q, k, vt):
    s, width = q.shape
    t = ATTN_TILE
    gw = ATTN_HEADS * HEAD_PAD
    return pl.pallas_call(
        _attn_body,
        grid=(width // gw, s // t),
        in_specs=[pl.BlockSpec((s, gw), lambda h, i: (0, h)),
                  pl.BlockSpec((s, gw), lambda h, i: (0, h)),
                  pl.BlockSpec((ATTN_HEADS * VT_ROWS, s), lambda h, i: (h, 0))],
        out_specs=pl.BlockSpec((t, ATTN_HEADS * VDIM), lambda h, i: (i, h)),
        out_shape=jax.ShapeDtypeStruct((s, HEADS * VDIM), BF16),
        scratch_shapes=[pltpu.VMEM((2, ATTN_HEADS, t, t), F32),
                        pltpu.VMEM((2, ATTN_HEADS, STAT_ROWS, t), F32),
                        pltpu.VMEM((ATTN_HEADS, STAT_ROWS, t), F32),
                        pltpu.VMEM((ATTN_HEADS, VT_ROWS, t), F32)],
        compiler_params=_params("arbitrary", "arbitrary"),
        name="mla_attention",
    )(q, k, vt)


def _merge_body(a_ref, gb_ref, o_ref, x_ref, wb_ref, wo_ref, gp_ref, gm_ref, out_ref):
    yb = _dot(o_ref[...], wb_ref[...])
    merged = a_ref[...].astype(F32) + jax.nn.sigmoid(gb_ref[...].astype(F32)) * yb
    y = _dot(merged.astype(BF16), wo_ref[...])
    out_ref[...] = x_ref[...] + gm_ref[...] * _rms(y, gp_ref[...])


def _merge(a, gb, o, x, wb, wo, gp, gm):
    s, d = x.shape
    tm = ROW_TILE
    row = pl.BlockSpec((tm, d), lambda i: (i, 0))
    return pl.pallas_call(
        _merge_body,
        grid=(s // tm,),
        in_specs=[row, row, pl.BlockSpec((tm, o.shape[1]), lambda i: (i, 0)), row,
                  _full(wb.shape), _full(wo.shape), _full((1, d)), _full((1, d))],
        out_specs=row,
        out_shape=jax.ShapeDtypeStruct((s, d), F32),
        compiler_params=_params("arbitrary"),
        name="mixer_merge",
    )(a, gb, o, x, wb, wo, gp, gm)


def _ffn_body(x_ref, g_ref, sc_ref, sh_ref, wg_ref, wu_ref, wd_ref, gp_ref, gf_ref, o_ref):
    x = x_ref[...]
    hb = (_rms(x, g_ref[...]) * (1.0 + sc_ref[...]) + sh_ref[...]).astype(BF16)
    a = _dot(hb, wg_ref[...])
    b = _dot(hb, wu_ref[...])
    y = _dot((a * jax.nn.sigmoid(a) * b).astype(BF16), wd_ref[...])
    o_ref[...] = x + gf_ref[...] * _rms(y, gp_ref[...])


def _dense_ffn(x, g, sc, sh, wg, wu, wd, gp, gf):
    s, d = x.shape
    tm = ROW_TILE
    row = pl.BlockSpec((tm, d), lambda i: (i, 0))
    vec = _full((1, d))
    return pl.pallas_call(
        _ffn_body,
        grid=(s // tm,),
        in_specs=[row, vec, vec, vec, _full(wg.shape), _full(wu.shape), _full(wd.shape), vec, vec],
        out_specs=row,
        out_shape=jax.ShapeDtypeStruct((s, d), F32),
        compiler_params=_params("arbitrary"),
        name="dense_ffn",
    )(x, g, sc, sh, wg, wu, wd, gp, gf)


def _route_body(x_ref, g_ref, sc_ref, sh_ref, wr_ref, br_ref, hb_ref, info_ref, infot_ref, cnt_ref):
    i = pl.program_id(0)
    tm = x_ref.shape[0]
    h = _rms(x_ref[...], g_ref[...]) * (1.0 + sc_ref[...]) + sh_ref[...]
    hb = h.astype(BF16)
    hb_ref[...] = hb
    h_lo = (h - hb.astype(F32)).astype(BF16)
    w = wr_ref[...]
    w_hi = w.astype(BF16)
    w_lo = (w - w_hi.astype(F32)).astype(BF16)
    logits = _dot(hb, w_hi) + (_dot(h_lo, w_hi) + _dot(hb, w_lo)) + br_ref[...]

    lane = lax.broadcasted_iota(jnp.int32, (tm, LANES), 1)
    m1 = jnp.max(logits, axis=-1, keepdims=True)
    i1 = jnp.min(jnp.where(logits == m1, lane, LANES), axis=-1, keepdims=True)
    oh1 = lane == i1
    rest = jnp.where(oh1, -3e38, logits)
    m2 = jnp.max(rest, axis=-1, keepdims=True)
    i2 = jnp.min(jnp.where(rest == m2, lane, LANES), axis=-1, keepdims=True)
    oh2 = lane == i2
    ex = jnp.exp(m2 - m1)
    g1 = 1.0 / (1.0 + ex)
    g2 = ex / (1.0 + ex)

    ohf = jnp.where(oh1 | oh2, 1.0, 0.0)
    r_idx = lax.broadcasted_iota(jnp.int32, (tm, tm), 0)
    c_idx = lax.broadcasted_iota(jnp.int32, (tm, tm), 1)
    earlier = jnp.where(c_idx < r_idx, 1.0, 0.0).astype(BF16)
    rank = _dot(earlier, ohf.astype(BF16))
    rank1 = jnp.sum(jnp.where(oh1, rank, 0.0), axis=-1, keepdims=True)
    rank2 = jnp.sum(jnp.where(oh2, rank, 0.0), axis=-1, keepdims=True)
    info = jnp.where(lane == 0, i1.astype(F32),
           jnp.where(lane == 1, i2.astype(F32),
           jnp.where(lane == 2, g1,
           jnp.where(lane == 3, g2,
           jnp.where(lane == 4, rank1,
           jnp.where(lane == 5, rank2, 0.0))))))
    info_ref[...] = info
    infot_ref[...] = info.T[:ROUTE_FIELDS]
    cnt = jnp.sum(ohf, axis=0, keepdims=True).astype(jnp.int32)
    cnt_al = ((cnt + (BF16_ROWS - 1)) // BF16_ROWS) * BF16_ROWS
    for e in range(N_EXPERTS):
        cnt_ref[i * N_EXPERTS + e] = cnt_al[0, e]


def _route(x, g, sc, sh, wr, br):
    s, d = x.shape
    tm = ROW_TILE
    nt = s // tm
    row = pl.BlockSpec((tm, d), lambda i: (i, 0))
    return pl.pallas_call(
        _route_body,
        grid=(nt,),
        in_specs=[row, _full((1, d)), _full((1, d)), _full((1, d)), _full(wr.shape), _full(br.shape)],
        out_specs=[row,
                   pl.BlockSpec((tm, LANES), lambda i: (i, 0)),
                   pl.BlockSpec((ROUTE_FIELDS, tm), lambda i: (0, i)),
                   pl.BlockSpec(memory_space=pltpu.SMEM)],
        out_shape=[jax.ShapeDtypeStruct((s, d), BF16),
                   jax.ShapeDtypeStruct((s, LANES), F32),
                   jax.ShapeDtypeStruct((ROUTE_FIELDS, s), F32),
                   jax.ShapeDtypeStruct((nt * N_EXPERTS,), jnp.int32)],
        compiler_params=_params("arbitrary"),
        name="moe_route",
    )(x, g, sc, sh, wr, br)


def _routing(info):
    return [(info[:, k:k + 1], info[:, 2 + k:3 + k], info[:, 4 + k:5 + k]) for k in range(2)]


def _dispatch_body(offs_ref, cnts_ref, fill_ref, nval_ref, hb_ref, infot_ref, xs_ref, xbuf, zbuf, sem, zsem, *, nblk):
    i = pl.program_id(0)
    tm = hb_ref.shape[0]
    tb = zbuf.shape[0]

    def zero_block(blk):
        cp = pltpu.make_async_copy(zbuf, xs_ref.at[pl.ds(pl.multiple_of(blk * tb, tb), tb)], zsem)
        cp.start()
        cp.wait()

    @pl.when(i == 0)
    def _():
        zbuf[...] = jnp.zeros(zbuf.shape, zbuf.dtype)
        for n in range(fill_ref.shape[0]):
            @pl.when(fill_ref[n] >= 0)
            def _():
                zero_block(fill_ref[n])

        def unused(blk, carry):
            zero_block(blk)
            return carry

        lax.fori_loop(nval_ref[0], nblk, unused, 0)

    hb = hb_ref[...]
    fields = infot_ref[...]
    e1, e2, r1, r2 = fields[0:1], fields[1:2], fields[4:5], fields[5:6]
    chunks = tm // DISPATCH_ROWS
    buf = lax.rem(i, 2)

    def first_slot(e, r):
        return jnp.where(r < float(FIRST_ROWS), e * float(FIRST_ROWS) + r, -1.0)

    p1 = first_slot(e1, r1)
    p2 = first_slot(e2, r2)
    slot = lax.broadcasted_iota(jnp.int32, (N_EXPERTS * FIRST_ROWS, tm), 0).astype(F32)
    sel = jnp.where((p1 == slot) | (p2 == slot), 1.0, 0.0).astype(BF16)
    xbuf[buf, 0] = _dot(sel, hb).astype(BF16)

    first_chunks = FIRST_ROWS // DISPATCH_ROWS

    def buf_rows(e, c):
        return c // first_chunks, e * FIRST_ROWS + (c % first_chunks) * DISPATCH_ROWS

    for e in range(N_EXPERTS):
        for c in range(first_chunks, chunks):
            @pl.when(cnts_ref[i * N_EXPERTS + e] > c * DISPATCH_ROWS)
            def _():
                first = e1 == float(e)
                rank = jnp.where(first, r1, r2)
                late = (lax.broadcasted_iota(jnp.int32, (DISPATCH_ROWS, tm), 0) + c * DISPATCH_ROWS).astype(F32)
                hit = (rank == late) & (first | (e2 == float(e)))
                part, row0 = buf_rows(e, c)
                xbuf[buf, part, row0:row0 + DISPATCH_ROWS] = _dot(jnp.where(hit, 1.0, 0.0).astype(BF16),
                                                                  hb).astype(BF16)

    def copy(b, e, c, dst):
        part, row0 = buf_rows(e, c)
        return pltpu.make_async_copy(xbuf.at[b, part, pl.ds(row0, DISPATCH_ROWS)],
                                     xs_ref.at[pl.ds(dst, DISPATCH_ROWS)], sem.at[b, e, c])

    def for_each_copy(step, fn):
        for e in range(N_EXPERTS):
            for c in range(chunks):
                @pl.when(cnts_ref[step * N_EXPERTS + e] > c * DISPATCH_ROWS)
                def _():
                    fn(e, c)

    @pl.when(i > 0)
    def _():
        for_each_copy(i - 1, lambda e, c: copy(1 - buf, e, c, 0).wait())

    def start(e, c):
        off = offs_ref[i * N_EXPERTS + e]
        copy(buf, e, c, pl.multiple_of(off + c * DISPATCH_ROWS, BF16_ROWS)).start()

    for_each_copy(i, start)

    @pl.when(i == pl.num_programs(0) - 1)
    def _():
        for_each_copy(i, lambda e, c: copy(buf, e, c, 0).wait())


def _dispatch(offs, cnts, fill, nval, hb, infot, nblk):
    s, d = hb.shape
    tm = ROW_TILE
    tb = EXPERT_ROWS
    grid_spec = pltpu.PrefetchScalarGridSpec(
        num_scalar_prefetch=4,
        grid=(s // tm,),
        in_specs=[pl.BlockSpec((tm, d), lambda i, *_: (i, 0)),
                  pl.BlockSpec((ROUTE_FIELDS, tm), lambda i, *_: (0, i))],
        out_specs=pl.BlockSpec(memory_space=pl.ANY),
        scratch_shapes=[pltpu.VMEM((2, tm // FIRST_ROWS, N_EXPERTS * FIRST_ROWS, d), BF16),
                        pltpu.VMEM((tb, d), BF16),
                        pltpu.SemaphoreType.DMA((2, N_EXPERTS, tm // DISPATCH_ROWS)),
                        pltpu.SemaphoreType.DMA(())],
    )
    return pl.pallas_call(
        functools.partial(_dispatch_body, nblk=nblk),
        grid_spec=grid_spec,
        out_shape=jax.ShapeDtypeStruct((nblk * tb, d), BF16),
        compiler_params=_params("arbitrary"),
        name="moe_dispatch",
    )(offs, cnts, fill, nval, hb, infot)


def _expert_body(bexp_ref, nval_ref, x_ref, w1_ref, w3_ref, w2_ref, o_ref, acc_ref):
    i = pl.program_id(0)
    f = pl.program_id(1)
    last = pl.num_programs(1) - 1
    used = i < nval_ref[0]

    @pl.when(used)
    def _():
        xb = x_ref[...]
        a = _dot(xb, w1_ref[0])
        b = _dot(xb, w3_ref[0])
        part = _dot((a * jax.nn.sigmoid(a) * b).astype(BF16), w2_ref[0])

        @pl.when(f == 0)
        def _():
            acc_ref[...] = part

        @pl.when(f > 0)
        def _():
            acc_ref[...] += part

        @pl.when(f == last)
        def _():
            o_ref[...] = acc_ref[...].astype(o_ref.dtype)

    @pl.when(jnp.logical_not(used) & (f == last))
    def _():
        o_ref[...] = jnp.zeros(o_ref.shape, o_ref.dtype)


def _experts(xs, w1, w3, w2, bexp, nval, tf):
    rows, d = xs.shape
    ff = w1.shape[2]
    tb = EXPERT_ROWS
    nff = ff // tf

    def xrow(i, f, be, nv):
        return (jnp.minimum(i, nv[0] - 1), 0)

    def fcol(i, f, nv):
        return jnp.where(i < nv[0], f, nff - 1)

    grid_spec = pltpu.PrefetchScalarGridSpec(
        num_scalar_prefetch=2,
        grid=(rows // tb, nff),
        in_specs=[pl.BlockSpec((tb, d), xrow),
                  pl.BlockSpec((1, d, tf), lambda i, f, be, nv: (be[i], 0, fcol(i, f, nv))),
                  pl.BlockSpec((1, d, tf), lambda i, f, be, nv: (be[i], 0, fcol(i, f, nv))),
                  pl.BlockSpec((1, tf, d), lambda i, f, be, nv: (be[i], fcol(i, f, nv), 0))],
        out_specs=pl.BlockSpec((tb, d), lambda i, f, be, nv: (i, 0)),
        scratch_shapes=[pltpu.VMEM((tb, d), F32)],
    )
    return pl.pallas_call(
        _expert_body,
        grid_spec=grid_spec,
        out_shape=jax.ShapeDtypeStruct((rows, d), BF16),
        compiler_params=_params("arbitrary", "arbitrary"),
        name="moe_experts",
    )(bexp, nval, xs, w1, w3, w2)


def _combine_body(offs_ref, cnts_ref, info_ref, x_ref, gp_ref, gf_ref, ys_ref, o_ref, ybuf, acc_ref, sem):
    i = pl.program_id(0)
    nt = pl.num_programs(0)
    tm = x_ref.shape[0]
    chunks = tm // COMBINE_ROWS
    buf = lax.rem(i, 2)

    def copy(b, e, c, src):
        dst = ybuf.at[b, pl.ds((c * N_EXPERTS + e) * COMBINE_ROWS, COMBINE_ROWS)]
        return pltpu.make_async_copy(ys_ref.at[pl.ds(src, COMBINE_ROWS)], dst, sem.at[b, e, c])

    def fetch(step, b):
        for e in range(N_EXPERTS):
            off = offs_ref[step * N_EXPERTS + e]
            copy(b, e, 0, pl.multiple_of(off, BF16_ROWS)).start()
            for c in range(1, chunks):
                @pl.when(cnts_ref[step * N_EXPERTS + e] > c * COMBINE_ROWS)
                def _():
                    copy(b, e, c, pl.multiple_of(off + c * COMBINE_ROWS, BF16_ROWS)).start()

    @pl.when(i == 0)
    def _():
        fetch(0, 0)

    @pl.when(i + 1 < nt)
    def _():
        fetch(i + 1, 1 - buf)

    (e1, g1, r1), (e2, g2, r2) = _routing(info_ref[...])
    for e in range(N_EXPERTS):
        copy(buf, e, 0, 0).wait()
    stack = N_EXPERTS * COMBINE_ROWS
    slot = lax.broadcasted_iota(jnp.int32, (tm, stack), 1).astype(F32)
    y_first = ybuf[buf, 0:stack]
    acc = None
    for ek, gk, rk in ((e1, g1, r1), (e2, g2, r2)):
        pos = jnp.where(rk < float(COMBINE_ROWS), ek * float(COMBINE_ROWS) + rk, -1.0)
        term = gk * _dot(jnp.where(pos == slot, 1.0, 0.0).astype(BF16), y_first)
        acc = term if acc is None else acc + term
    acc_ref[...] = acc

    late = lax.broadcasted_iota(jnp.int32, (tm, COMBINE_ROWS), 1).astype(F32)
    for e in range(N_EXPERTS):
        for c in range(1, chunks):
            @pl.when(cnts_ref[i * N_EXPERTS + e] > c * COMBINE_ROWS)
            def _():
                first = e1 == float(e)
                second = e2 == float(e)
                rank = jnp.where(first, r1, r2)
                gate = jnp.where(first, g1, jnp.where(second, g2, 0.0))
                copy(buf, e, c, 0).wait()
                hit = (rank == late + float(c * COMBINE_ROWS)) & (first | second)
                rows = ybuf[buf, pl.ds((c * N_EXPERTS + e) * COMBINE_ROWS, COMBINE_ROWS)]
                acc_ref[...] += gate * _dot(jnp.where(hit, 1.0, 0.0).astype(BF16), rows)

    o_ref[...] = x_ref[...] + gf_ref[...] * _rms(acc_ref[...], gp_ref[...])


def _combine(offs, cnts, info, x, gp, gf, ys):
    s, d = x.shape
    tm = ROW_TILE
    chunks = tm // COMBINE_ROWS
    grid_spec = pltpu.PrefetchScalarGridSpec(
        num_scalar_prefetch=2,
        grid=(s // tm,),
        in_specs=[pl.BlockSpec((tm, LANES), lambda i, *_: (i, 0)),
                  pl.BlockSpec((tm, d), lambda i, *_: (i, 0)),
                  pl.BlockSpec((1, d), lambda i, *_: (0, 0)),
                  pl.BlockSpec((1, d), lambda i, *_: (0, 0)),
                  pl.BlockSpec(memory_space=pl.ANY)],
        out_specs=pl.BlockSpec((tm, d), lambda i, *_: (i, 0)),
        scratch_shapes=[pltpu.VMEM((2, chunks * N_EXPERTS * COMBINE_ROWS, d), BF16),
                        pltpu.VMEM((tm, d), F32),
                        pltpu.SemaphoreType.DMA((2, N_EXPERTS, chunks))],
    )
    return pl.pallas_call(
        _combine_body,
        grid_spec=grid_spec,
        out_shape=jax.ShapeDtypeStruct((s, d), F32),
        compiler_params=_params("arbitrary"),
        name="moe_combine",
    )(offs, cnts, info, x, gp, gf, ys)


def _rot_half_cols(w):
    half = ROPE // 2
    return jnp.concatenate([-w[..., half:], w[..., :half]], axis=-1)


def _pad_cols(w, before, total):
    return jnp.pad(w, ((0, 0), (before, total - before - w.shape[1])))


def _mixer_weights(w_in, w_uq, w_ukv, w_branch_b):
    d = w_in.shape[0]
    o = 0
    parts = {}
    for name, n in (("u", GMLP_WIDTH), ("v", GMLP_WIDTH), ("cq", Q_RANK), ("ckv", KV_RANK),
                    ("kr", ROPE), ("ga", d), ("gb", d)):
        parts[name] = w_in[:, o:o + n]
        o += n
    kr_main = _pad_cols(parts["kr"], NOPE, HEAD_PAD)
    kr_swap = _pad_cols(_rot_half_cols(parts["kr"]), NOPE, HEAD_PAD)
    w_in_p = jnp.concatenate([parts["u"], parts["v"], parts["cq"], parts["ckv"], kr_main, kr_swap,
                              parts["ga"], parts["gb"]], axis=1).astype(BF16)

    wq = w_uq.reshape(Q_RANK, HEADS, NOPE + ROPE)
    zq = jnp.zeros((Q_RANK, HEADS, HEAD_PAD - NOPE - ROPE), w_uq.dtype)
    wq_main = jnp.concatenate([wq, zq], axis=-1).reshape(Q_RANK, HEADS * HEAD_PAD).astype(BF16)
    wq_swap = jnp.concatenate([jnp.zeros((Q_RANK, HEADS, NOPE), w_uq.dtype),
                               _rot_half_cols(wq[..., NOPE:]), zq], axis=-1)
    wq_swap = wq_swap.reshape(Q_RANK, HEADS * HEAD_PAD).astype(BF16)

    wkv = w_ukv.reshape(KV_RANK, HEADS, NOPE + VDIM)
    zk = jnp.zeros((KV_RANK, HEADS, HEAD_PAD - NOPE), w_ukv.dtype)
    wk = jnp.concatenate([wkv[..., :NOPE], zk], axis=-1).reshape(KV_RANK, HEADS * HEAD_PAD).astype(BF16)
    zv = jnp.zeros((KV_RANK, HEADS, VT_ROWS - VDIM), w_ukv.dtype)
    wv = jnp.concatenate([wkv[..., NOPE:], zv], axis=-1).reshape(KV_RANK, HEADS * VT_ROWS).T.astype(BF16)

    return w_in_p, wq_main, wq_swap, wk, wv, w_branch_b.astype(BF16)


def _expert_layout(cnts, nt, nblk):
    tb = EXPERT_ROWS
    c = cnts.reshape(nt, N_EXPERTS)
    total = jnp.sum(c, axis=0)
    per = jnp.where(total > 0, (total + REGION_SLACK + tb - 1) // tb, 0)
    stop = jnp.cumsum(per)
    start = stop - per
    offs = start[None, :] * tb + (jnp.cumsum(c, axis=0) - c)
    nval = stop[-1]
    blk = jnp.minimum(jnp.arange(nblk, dtype=jnp.int32), nval - 1)
    bexp = jnp.minimum(jnp.sum(blk[:, None] >= stop[None, :], axis=1), N_EXPERTS - 1)
    fill = jnp.concatenate([jnp.where(per > 0, stop - 1, -1), jnp.where(per > 1, stop - 2, -1)])
    i32 = lambda v: v.astype(jnp.int32)
    return i32(offs.reshape(-1)), i32(bexp), i32(nval.reshape(1)), i32(fill)


def kernel(x, c, positions, ada_w, ada_b, norm_mix_pre, norm_mix_post, norm_ffn_pre, norm_ffn_post, w_in, gmlp_ln_g, gmlp_ln_b, gmlp_ws, gmlp_bs, mla_q_norm, mla_w_uq, mla_kv_norm, mla_w_ukv, w_branch_a, w_branch_b, w_out, ffn_w_gate, ffn_w_up, ffn_w_down, moe_router, moe_router_bias, moe_w1, moe_w3, moe_w2):
    batch, s, d = x.shape
    assert batch == 1 and s % ROW_TILE == 0 and s % ATTN_TILE == 0
    depth = ada_w.shape[0]
    xs = x.reshape(s, d)
    mod = _ada_mod(c, ada_w, ada_b)
    cos_t, sin_t = _rope_tables(positions)
    vone = jnp.zeros((HEADS, VT_ROWS), F32).at[:, VDIM].set(1.0).reshape(HEADS * VT_ROWS, 1)
    row = lambda v: v.reshape(1, -1)

    for l in range(depth):
        sh_m, sc_m, g_m, sh_f, sc_f, g_f = [mod[l, :, k * d:(k + 1) * d] for k in range(6)]
        w_in_p, wq_main, wq_swap, wk, wv, wb = _mixer_weights(w_in[l], mla_w_uq[l], mla_w_ukv[l], w_branch_b[l])
        uv, cq, ckv, kr, ga, gb = _inproj(xs, row(norm_mix_pre[l]), sc_m, sh_m, w_in_p)
        bias = jnp.broadcast_to(gmlp_bs[l].T[:, :, None], (CHUNK, GMLP_GROUPS, GMLP_GROUP_DIM))
        bias = bias.reshape(CHUNK, GMLP_WIDTH)
        a = _gmlp(uv, ga, row(gmlp_ln_g[l]), row(gmlp_ln_b[l]), gmlp_ws[l], bias, w_branch_a[l].astype(BF16))
        q, k, v = _mla_prep(cq, ckv, kr, cos_t, sin_t, row(mla_q_norm[l]), row(mla_kv_norm[l]),
                            wq_main, wq_swap, wk, wv, vone)
        o = _attention(q, k, v)
        xs = _merge(a, gb, o, xs, wb, w_out[l].astype(BF16), row(norm_mix_post[l]), g_m)

        j = l // 2
        if l % 2 == 0:
            ff = ffn_w_gate.shape[2]
            ff_pad = -(-ff // (2 * LANES)) * (2 * LANES)
            wg = jnp.pad(ffn_w_gate[j], ((0, 0), (0, ff_pad - ff))).astype(BF16)
            wu = jnp.pad(ffn_w_up[j], ((0, 0), (0, ff_pad - ff))).astype(BF16)
            wd = jnp.pad(ffn_w_down[j], ((0, ff_pad - ff), (0, 0))).astype(BF16)
            xs = _dense_ffn(xs, row(norm_ffn_pre[l]), sc_f, sh_f, wg, wu, wd, row(norm_ffn_post[l]), g_f)
        else:
            nt = s // ROW_TILE
            tb = EXPERT_ROWS
            max_rows = 2 * s + nt * N_EXPERTS * (BF16_ROWS - 1) + N_EXPERTS * (REGION_SLACK + tb - 1)
            nblk = -(-max_rows // tb) + 1
            wr = jnp.pad(moe_router[j], ((0, 0), (0, LANES - N_EXPERTS)))
            br = jnp.pad(moe_router_bias[j], (0, LANES - N_EXPERTS), constant_values=NEG).reshape(1, LANES)
            hb, info, infot, cnts = _route(xs, row(norm_ffn_pre[l]), sc_f, sh_f, wr, br)
            offs, bexp, nval, fill = _expert_layout(cnts, nt, nblk)
            xsort = _dispatch(offs, cnts, fill, nval, hb, infot, nblk)
            ysort = _experts(xsort, moe_w1[j].astype(BF16), moe_w3[j].astype(BF16), moe_w2[j].astype(BF16),
                             bexp, nval, tf=moe_w1.shape[3] // 2)
            xs = _combine(offs, cnts, info, xs, row(norm_ffn_post[l]), g_f, ysort)
    return xs.reshape(batch, s, d)
```

```python
import functools

import jax
import jax.numpy as jnp
import numpy as np
from jax import lax
from jax.experimental import pallas as pl
from jax.experimental.pallas import tpu as pltpu

F32 = jnp.float32
BF16 = jnp.bfloat16

EPS = 1e-6
LANES = 128
BF16_ROWS = 16
GMLP_GROUPS = 8
GMLP_GROUP_DIM = 64
GMLP_WIDTH = GMLP_GROUPS * GMLP_GROUP_DIM
CHUNK = 128
HEADS = 8
NOPE = 64
ROPE = 32
VDIM = 64
HEAD_PAD = 128
VT_ROWS = 80
Q_RANK = 384
KV_RANK = 256
ROPE_THETA = 10000.0
N_EXPERTS = 8
NEG = -1e30

ROW_TILE = 512
ATTN_TILE = 512
ATTN_HEADS = 2
STAT_ROWS = 8
EXPERT_ROWS = 512
DISPATCH_ROWS = 128
COMBINE_ROWS = 256
FIRST_ROWS = 256
ROUTE_FIELDS = 8
REGION_SLACK = DISPATCH_ROWS - BF16_ROWS
VMEM_LIMIT = 52 * 1024 * 1024


def _params(*sem):
    return pltpu.CompilerParams(dimension_semantics=sem, vmem_limit_bytes=VMEM_LIMIT)


def _dot(a, b):
    return jnp.dot(a, b, preferred_element_type=F32)


def _rms(x, g):
    return x * lax.rsqrt(jnp.mean(x * x, axis=-1, keepdims=True) + EPS) * g


def _gelu(x):
    return 0.5 * x * (1.0 + lax.erf(x * np.float32(0.7071067811865476)))


def _full(shape):
    return pl.BlockSpec(shape, lambda *_: (0,) * len(shape), pipeline_mode=pl.Buffered(1))


def _mod_body(c_ref, w_ref, b_ref, o_ref):
    c = c_ref[...]
    ca = c * jax.nn.sigmoid(c)
    o_ref[0] = jnp.sum(ca * w_ref[0], axis=0, keepdims=True) + b_ref[0]


def _ada_mod(c, ada_w, ada_b):
    n_layers, d, n = ada_w.shape
    tn = n // 4
    return pl.pallas_call(
        _mod_body,
        grid=(n_layers, n // tn),
        in_specs=[pl.BlockSpec((d, 1), lambda l, j: (0, 0)),
                  pl.BlockSpec((1, d, tn), lambda l, j: (l, 0, j)),
                  pl.BlockSpec((1, 1, tn), lambda l, j: (l, 0, j))],
        out_specs=pl.BlockSpec((1, 1, tn), lambda l, j: (l, 0, j)),
        out_shape=jax.ShapeDtypeStruct((n_layers, 1, n), F32),
        compiler_params=_params("arbitrary", "arbitrary"),
        name="ada_mod",
    )(c.reshape(d, 1), ada_w, ada_b.reshape(n_layers, 1, n))


def _rope_body(pos_ref, invf_ref, cos_ref, sin_ref):
    ang = pos_ref[...].astype(F32) * invf_ref[...]
    cos_ref[...] = jnp.cos(ang)
    sin_ref[...] = jnp.sin(ang)


def _rope_tables(positions):
    s = positions.shape[-1]
    half = ROPE // 2
    per_row = LANES // half
    inv_freq = 1.0 / (ROPE_THETA ** (jnp.arange(0, ROPE, 2, dtype=F32) / ROPE))
    pos_dense = jnp.broadcast_to(positions.reshape(s, 1), (s, half)).reshape(s // per_row, LANES)
    invf = jnp.tile(inv_freq, per_row).reshape(1, LANES)
    rows = s // per_row
    tr = rows // 4
    cos_d, sin_d = pl.pallas_call(
        _rope_body,
        grid=(rows // tr,),
        in_specs=[pl.BlockSpec((tr, LANES), lambda i: (i, 0)), _full((1, LANES))],
        out_specs=[pl.BlockSpec((tr, LANES), lambda i: (i, 0))] * 2,
        out_shape=[jax.ShapeDtypeStruct((rows, LANES), F32)] * 2,
        compiler_params=_params("arbitrary"),
        name="rope_tables",
    )(pos_dense, invf)
    cos = cos_d.reshape(s, half)
    sin = sin_d.reshape(s, half)
    ones = jnp.ones((s, NOPE), F32)
    zeros = jnp.zeros((s, NOPE), F32)
    pad = jnp.zeros((s, HEAD_PAD - NOPE - ROPE), F32)
    cos_t = jnp.concatenate([ones, cos, cos, pad], axis=1)
    sin_t = jnp.concatenate([zeros, sin, sin, pad], axis=1)
    return cos_t, sin_t


def _gmlp_branch(u, v, ga, lng_ref, lnb_ref, ws_ref, bias_ref, wa_ref):
    tm = u.shape[0]
    gu = _gelu(u)
    gv = _gelu(v)
    mu = jnp.mean(gv, axis=-1, keepdims=True)
    xc = gv - mu
    vn = xc * lax.rsqrt(jnp.mean(xc * xc, axis=-1, keepdims=True) + EPS) * lng_ref[...] + lnb_ref[...]
    vb = vn.astype(BF16)
    t_idx = lax.broadcasted_iota(jnp.int32, (CHUNK, CHUNK), 0)
    s_idx = lax.broadcasted_iota(jnp.int32, (CHUNK, CHUNK), 1)
    causal = s_idx <= t_idx
    ws = [jnp.where(causal, ws_ref[g], 0.0).astype(BF16) for g in range(GMLP_GROUPS)]
    left = lax.broadcasted_iota(jnp.int32, (CHUNK, LANES), 1) < GMLP_GROUP_DIM
    bias = bias_ref[...]
    z_rows = []
    for c in range(tm // CHUNK):
        vc = vb[c * CHUNK:(c + 1) * CHUNK]
        z_cols = []
        for j in range(GMLP_WIDTH // LANES):
            vp = vc[:, j * LANES:(j + 1) * LANES]
            z_cols.append(jnp.where(left, _dot(ws[2 * j], vp), _dot(ws[2 * j + 1], vp)))
        z_rows.append(jnp.concatenate(z_cols, axis=1) + bias)
    z = jnp.concatenate(z_rows, axis=0)
    gated = (gu * z).astype(BF16)
    return jax.nn.sigmoid(ga) * _dot(gated, wa_ref[...])


def _mla_qkv(cq, ckv, kr, cos_ref, sin_ref, qn_ref, kvn_ref, wqm_ref, wqs_ref, wk_ref, wvt_ref, vone_ref,
             q_ref, k_ref, vt_ref, scale):
    cos = cos_ref[...]
    sin = sin_ref[...]
    cqn = _rms(cq, qn_ref[...]).astype(BF16)
    qm = _dot(cqn, wqm_ref[...])
    qs = _dot(cqn, wqs_ref[...])
    ckn = _rms(ckv, kvn_ref[...]).astype(BF16)
    km = _dot(ckn, wk_ref[...])
    kpe = kr[:, :HEAD_PAD] * cos + kr[:, HEAD_PAD:] * sin
    for h in range(HEADS):
        sl = slice(h * HEAD_PAD, (h + 1) * HEAD_PAD)
        q_ref[:, sl] = ((qm[:, sl] * cos + qs[:, sl] * sin) * scale).astype(q_ref.dtype)
        k_ref[:, sl] = (km[:, sl] + kpe).astype(k_ref.dtype)
    vt = lax.dot_general(wvt_ref[...], ckn, (((1,), (1,)), ((), ())), preferred_element_type=F32)
    vt_ref[...] = (vt + vone_ref[...]).astype(vt_ref.dtype)


def _front_body(x_ref, g_ref, sc_ref, sh_ref, w_ref, lng_ref, lnb_ref, ws_ref, bias_ref, wa_ref,
                cos_ref, sin_ref, qn_ref, kvn_ref, wqm_ref, wqs_ref, wk_ref, wvt_ref, vone_ref,
                a_ref, gb_ref, q_ref, k_ref, vt_ref, *, scale):
    h = _rms(x_ref[...], g_ref[...]) * (1.0 + sc_ref[...]) + sh_ref[...]
    proj = _dot(h.astype(BF16), w_ref[...])
    d = x_ref.shape[1]
    cuts = np.cumsum([0, GMLP_WIDTH, GMLP_WIDTH, Q_RANK, KV_RANK, 2 * HEAD_PAD, d, d])
    u, v, cq, ckv, kr, ga, gb = [proj[:, lo:hi] for lo, hi in zip(cuts[:-1], cuts[1:])]
    gb_ref[...] = gb.astype(gb_ref.dtype)
    a_ref[...] = _gmlp_branch(u, v, ga, lng_ref, lnb_ref, ws_ref, bias_ref, wa_ref).astype(a_ref.dtype)
    _mla_qkv(cq, ckv, kr, cos_ref, sin_ref, qn_ref, kvn_ref, wqm_ref, wqs_ref, wk_ref, wvt_ref, vone_ref,
             q_ref, k_ref, vt_ref, scale)


def _mixer_front(x, g, sc, sh, w, lng, lnb, ws, bias, wa, cos_t, sin_t, qn, kvn, wqm, wqs, wk, wvt, vone):
    s, d = x.shape
    tm = ROW_TILE
    width = HEADS * HEAD_PAD
    row = lambda n: pl.BlockSpec((tm, n), lambda i: (i, 0))
    scale = float((NOPE + ROPE) ** -0.5 * np.log2(np.e))
    consts = (g, sc, sh, w, lng, lnb, ws, bias, wa)
    mla_consts = (qn, kvn, wqm, wqs, wk, wvt, vone)
    return pl.pallas_call(
        functools.partial(_front_body, scale=scale),
        grid=(s // tm,),
        in_specs=[row(d)] + [_full(c.shape) for c in consts] + [row(HEAD_PAD), row(HEAD_PAD)]
                 + [_full(c.shape) for c in mla_consts],
        out_specs=[row(d), row(d), row(width), row(width), pl.BlockSpec((wvt.shape[0], tm), lambda i: (0, i))],
        out_shape=[jax.ShapeDtypeStruct((s, d), BF16), jax.ShapeDtypeStruct((s, d), BF16),
                   jax.ShapeDtypeStruct((s, width), BF16), jax.ShapeDtypeStruct((s, width), BF16),
                   jax.ShapeDtypeStruct((wvt.shape[0], s), BF16)],
        compiler_params=_params("arbitrary"),
        name="mixer_front",
    )(x, *consts, cos_t, sin_t, *mla_consts)


def _attn_body(q_ref, k_ref, vt_ref, o_ref, s_ref, mx_ref, m_ref, acc_ref):
    t = o_ref.shape[0]
    heads = q_ref.shape[1] // HEAD_PAD
    qi = pl.program_id(1)
    m_ref[...] = jnp.full(m_ref.shape, NEG, F32)
    acc_ref[...] = jnp.zeros(acc_ref.shape, F32)

    def scores(blk, slot, q_tile):
        off = pl.multiple_of(blk * t, t)
        q_off = pl.multiple_of(q_tile * t, t)
        for h in range(heads):
            sl = slice(h * HEAD_PAD, (h + 1) * HEAD_PAD)
            s = lax.dot_general(k_ref[pl.ds(off, t), sl], q_ref[pl.ds(q_off, t), sl],
                                (((1,), (1,)), ((), ())), preferred_element_type=F32)
            s_ref[slot, h] = s
            mx_ref[slot, h] = jnp.broadcast_to(jnp.max(s, axis=0, keepdims=True), (STAT_ROWS, t))

    def consume(blk, slot, mask):
        off = pl.multiple_of(blk * t, t)
        for h in range(heads):
            sl = slice(h * HEAD_PAD, (h + 1) * HEAD_PAD)
            s = s_ref[slot, h]
            if mask is None:
                mx = mx_ref[slot, h]
            else:
                s = jnp.where(mask, s, NEG)
                mx = jnp.max(s, axis=0, keepdims=True)
            m_prev = m_ref[h]
            m_new = jnp.maximum(m_prev, mx)
            alpha = jnp.exp2(m_prev - m_new)
            pv = alpha[0:1] * acc_ref[h]
            half = t // 2
            for part in range(2):
                p = jnp.exp2(s[part * half:(part + 1) * half] - m_new[0:1]).astype(BF16)
                keys = pl.ds(pl.multiple_of(off + part * half, half), half)
                pv = pv + _dot(vt_ref[h * VT_ROWS:(h + 1) * VT_ROWS, keys], p)
            acc_ref[h] = pv
            m_ref[h] = m_new

    @pl.when(qi == 0)
    def _():
        scores(0, 0, qi)

    def run(blk, n):
        for u in range(n):
            scores(blk + u + 1, (u + 1) % 2, qi)
            consume(blk + u, u % 2, None)

    def octet(j, carry):
        run(8 * j, 8)
        return carry

    lax.fori_loop(0, qi // 8, octet, 0)
    done = (qi // 8) * 8
    for n in (4, 2):
        more = qi - done >= n

        @pl.when(more)
        def _():
            run(done, n)

        done = done + jnp.where(more, n, 0)

    row = lax.broadcasted_iota(jnp.int32, (t, t), 0)
    col = lax.broadcasted_iota(jnp.int32, (t, t), 1)
    causal = row <= col
    odd = lax.rem(qi, 2) == 1

    @pl.when(odd)
    def _():
        scores(qi, 1, qi)
        consume(qi - 1, 0, None)
        consume(qi, 1, causal)

    @pl.when(jnp.logical_not(odd))
    def _():
        consume(qi, 0, causal)

    def finalize():
        lower = lax.broadcasted_iota(jnp.int32, (t, HEAD_PAD), 1) < VDIM
        for j in range(heads // 2):
            outs = []
            for h in (2 * j, 2 * j + 1):
                acc = acc_ref[h]
                out_t = jnp.concatenate([acc / acc[VDIM:VDIM + 1], jnp.zeros((HEAD_PAD - VT_ROWS, t), F32)],
                                        axis=0)
                outs.append(out_t.T)
            packed = jnp.where(lower, outs[0], pltpu.roll(outs[1], VDIM, 1))
            o_ref[:, j * HEAD_PAD:(j + 1) * HEAD_PAD] = packed.astype(o_ref.dtype)

    last = qi == pl.num_programs(1) - 1

    @pl.when(jnp.logical_not(last))
    def _():
        scores(0, 0, qi + 1)
        finalize()

    @pl.when(last)
    def _():
        finalize()


def _attention(q, k, vt):
    s, width = q.shape
    t = ATTN_TILE
    gw = ATTN_HEADS * HEAD_PAD
    return pl.pallas_call(
        _attn_body,
        grid=(width // gw, s // t),
        in_specs=[pl.BlockSpec((s, gw), lambda h, i: (0, h)),
                  pl.BlockSpec((s, gw), lambda h, i: (0, h)),
                  pl.BlockSpec((ATTN_HEADS * VT_ROWS, s), lambda h, i: (h, 0))],
        out_specs=pl.BlockSpec((t, ATTN_HEADS * VDIM), lambda h, i: (i, h)),
        out_shape=jax.ShapeDtypeStruct((s, HEADS * VDIM), BF16),
        scratch_shapes=[pltpu.VMEM((2, ATTN_HEADS, t, t), F32),
                        pltpu.VMEM((2, ATTN_HEADS, STAT_ROWS, t), F32),
                        pltpu.VMEM((ATTN_HEADS, STAT_ROWS, t), F32),
                        pltpu.VMEM((ATTN_HEADS, VT_ROWS, t), F32)],
        compiler_params=_params("arbitrary", "arbitrary"),
        name="mla_attention",
    )(q, k, vt)


MERGE_ARGS = 8


def _merge_math(a_ref, gb_ref, o_ref, x_ref, wb_ref, wo_ref, gp_ref, gm_ref):
    yb = _dot(o_ref[...], wb_ref[...])
    merged = a_ref[...].astype(F32) + jax.nn.sigmoid(gb_ref[...].astype(F32)) * yb
    y = _dot(merged.astype(BF16), wo_ref[...])
    return x_ref[...] + gm_ref[...] * _rms(y, gp_ref[...])


def _merge_specs(a, gb, o, x, wb, wo, gp, gm):
    tm = ROW_TILE
    d = x.shape[1]
    row = pl.BlockSpec((tm, d), lambda i: (i, 0))
    return [row, row, pl.BlockSpec((tm, o.shape[1]), lambda i: (i, 0)), row,
            _full(wb.shape), _full(wo.shape), _full((1, d)), _full((1, d))]


def _ffn_body(*refs):
    g_ref, sc_ref, sh_ref, wg_ref, wu_ref, wd_ref, gp_ref, gf_ref, o_ref = refs[MERGE_ARGS:]
    x = _merge_math(*refs[:MERGE_ARGS])
    hb = (_rms(x, g_ref[...]) * (1.0 + sc_ref[...]) + sh_ref[...]).astype(BF16)
    a = _dot(hb, wg_ref[...])
    b = _dot(hb, wu_ref[...])
    y = _dot((a * jax.nn.sigmoid(a) * b).astype(BF16), wd_ref[...])
    o_ref[...] = x + gf_ref[...] * _rms(y, gp_ref[...])


def _dense_ffn(merge_args, g, sc, sh, wg, wu, wd, gp, gf):
    s, d = merge_args[3].shape
    tm = ROW_TILE
    row = pl.BlockSpec((tm, d), lambda i: (i, 0))
    vec = _full((1, d))
    return pl.pallas_call(
        _ffn_body,
        grid=(s // tm,),
        in_specs=_merge_specs(*merge_args)
                 + [vec, vec, vec, _full(wg.shape), _full(wu.shape), _full(wd.shape), vec, vec],
        out_specs=row,
        out_shape=jax.ShapeDtypeStruct((s, d), F32),
        compiler_params=_params("arbitrary"),
        name="merge_dense_ffn",
    )(*merge_args, g, sc, sh, wg, wu, wd, gp, gf)


def _route_body(*refs):
    g_ref, sc_ref, sh_ref, wr_ref, br_ref, x_ref, hb_ref, info_ref, infot_ref, cnt_ref = refs[MERGE_ARGS:]
    i = pl.program_id(0)
    tm = x_ref.shape[0]
    x = _merge_math(*refs[:MERGE_ARGS])
    x_ref[...] = x
    h = _rms(x, g_ref[...]) * (1.0 + sc_ref[...]) + sh_ref[...]
    hb = h.astype(BF16)
    hb_ref[...] = hb
    h_lo = (h - hb.astype(F32)).astype(BF16)
    w = wr_ref[...]
    w_hi = w.astype(BF16)
    w_lo = (w - w_hi.astype(F32)).astype(BF16)
    logits = _dot(hb, w_hi) + (_dot(h_lo, w_hi) + _dot(hb, w_lo)) + br_ref[...]

    lane = lax.broadcasted_iota(jnp.int32, (tm, LANES), 1)
    m1 = jnp.max(logits, axis=-1, keepdims=True)
    i1 = jnp.min(jnp.where(logits == m1, lane, LANES), axis=-1, keepdims=True)
    oh1 = lane == i1
    rest = jnp.where(oh1, -3e38, logits)
    m2 = jnp.max(rest, axis=-1, keepdims=True)
    i2 = jnp.min(jnp.where(rest == m2, lane, LANES), axis=-1, keepdims=True)
    oh2 = lane == i2
    ex = jnp.exp(m2 - m1)
    g1 = 1.0 / (1.0 + ex)
    g2 = ex / (1.0 + ex)

    ohf = jnp.where(oh1 | oh2, 1.0, 0.0)
    r_idx = lax.broadcasted_iota(jnp.int32, (tm, tm), 0)
    c_idx = lax.broadcasted_iota(jnp.int32, (tm, tm), 1)
    earlier = jnp.where(c_idx < r_idx, 1.0, 0.0).astype(BF16)
    rank = _dot(earlier, ohf.astype(BF16))
    rank1 = jnp.sum(jnp.where(oh1, rank, 0.0), axis=-1, keepdims=True)
    rank2 = jnp.sum(jnp.where(oh2, rank, 0.0), axis=-1, keepdims=True)
    info = jnp.where(lane == 0, i1.astype(F32),
           jnp.where(lane == 1, i2.astype(F32),
           jnp.where(lane == 2, g1,
           jnp.where(lane == 3, g2,
           jnp.where(lane == 4, rank1,
           jnp.where(lane == 5, rank2, 0.0))))))
    info_ref[...] = info
    infot_ref[...] = info.T[:ROUTE_FIELDS]
    cnt = jnp.sum(ohf, axis=0, keepdims=True).astype(jnp.int32)
    cnt_al = ((cnt + (BF16_ROWS - 1)) // BF16_ROWS) * BF16_ROWS
    for e in range(N_EXPERTS):
        cnt_ref[i * N_EXPERTS + e] = cnt_al[0, e]


def _route(merge_args, g, sc, sh, wr, br):
    s, d = merge_args[3].shape
    tm = ROW_TILE
    nt = s // tm
    row = pl.BlockSpec((tm, d), lambda i: (i, 0))
    return pl.pallas_call(
        _route_body,
        grid=(nt,),
        in_specs=_merge_specs(*merge_args)
                 + [_full((1, d)), _full((1, d)), _full((1, d)), _full(wr.shape), _full(br.shape)],
        out_specs=[row, row,
                   pl.BlockSpec((tm, LANES), lambda i: (i, 0)),
                   pl.BlockSpec((ROUTE_FIELDS, tm), lambda i: (0, i)),
                   pl.BlockSpec(memory_space=pltpu.SMEM)],
        out_shape=[jax.ShapeDtypeStruct((s, d), F32),
                   jax.ShapeDtypeStruct((s, d), BF16),
                   jax.ShapeDtypeStruct((s, LANES), F32),
                   jax.ShapeDtypeStruct((ROUTE_FIELDS, s), F32),
                   jax.ShapeDtypeStruct((nt * N_EXPERTS,), jnp.int32)],
        compiler_params=_params("arbitrary"),
        name="merge_moe_route",
    )(*merge_args, g, sc, sh, wr, br)


def _routing(info):
    return [(info[:, k:k + 1], info[:, 2 + k:3 + k], info[:, 4 + k:5 + k]) for k in range(2)]


def _dispatch_body(offs_ref, cnts_ref, fill_ref, nval_ref, hb_ref, infot_ref, xs_ref, xbuf, zbuf, sem, zsem, *, nblk):
    i = pl.program_id(0)
    tm = hb_ref.shape[0]
    tb = zbuf.shape[0]

    def zero_block(blk):
        cp = pltpu.make_async_copy(zbuf, xs_ref.at[pl.ds(pl.multiple_of(blk * tb, tb), tb)], zsem)
        cp.start()
        cp.wait()

    @pl.when(i == 0)
    def _():
        zbuf[...] = jnp.zeros(zbuf.shape, zbuf.dtype)
        for n in range(fill_ref.shape[0]):
            @pl.when(fill_ref[n] >= 0)
            def _():
                zero_block(fill_ref[n])

        def unused(blk, carry):
            zero_block(blk)
            return carry

        lax.fori_loop(nval_ref[0], nblk, unused, 0)

    hb = hb_ref[...]
    fields = infot_ref[...]
    e1, e2, r1, r2 = fields[0:1], fields[1:2], fields[4:5], fields[5:6]
    chunks = tm // DISPATCH_ROWS
    buf = lax.rem(i, 2)

    def first_slot(e, r):
        return jnp.where(r < float(FIRST_ROWS), e * float(FIRST_ROWS) + r, -1.0)

    p1 = first_slot(e1, r1)
    p2 = first_slot(e2, r2)
    slot = lax.broadcasted_iota(jnp.int32, (N_EXPERTS * FIRST_ROWS, tm), 0).astype(F32)
    sel = jnp.where((p1 == slot) | (p2 == slot), 1.0, 0.0).astype(BF16)
    xbuf[buf, 0] = _dot(sel, hb).astype(BF16)

    first_chunks = FIRST_ROWS // DISPATCH_ROWS

    def buf_rows(e, c):
        return c // first_chunks, e * FIRST_ROWS + (c % first_chunks) * DISPATCH_ROWS

    for e in range(N_EXPERTS):
        for c in range(first_chunks, chunks):
            @pl.when(cnts_ref[i * N_EXPERTS + e] > c * DISPATCH_ROWS)
            def _():
                first = e1 == float(e)
                rank = jnp.where(first, r1, r2)
                late = (lax.broadcasted_iota(jnp.int32, (DISPATCH_ROWS, tm), 0) + c * DISPATCH_ROWS).astype(F32)
                hit = (rank == late) & (first | (e2 == float(e)))
                part, row0 = buf_rows(e, c)
                xbuf[buf, part, row0:row0 + DISPATCH_ROWS] = _dot(jnp.where(hit, 1.0, 0.0).astype(BF16),
                                                                  hb).astype(BF16)

    def copy(b, e, c, dst):
        part, row0 = buf_rows(e, c)
        return pltpu.make_async_copy(xbuf.at[b, part, pl.ds(row0, DISPATCH_ROWS)],
                                     xs_ref.at[pl.ds(dst, DISPATCH_ROWS)], sem.at[b, e, c])

    def for_each_copy(step, fn):
        for e in range(N_EXPERTS):
            for c in range(chunks):
                @pl.when(cnts_ref[step * N_EXPERTS + e] > c * DISPATCH_ROWS)
                def _():
                    fn(e, c)

    @pl.when(i > 0)
    def _():
        for_each_copy(i - 1, lambda e, c: copy(1 - buf, e, c, 0).wait())

    def start(e, c):
        off = offs_ref[i * N_EXPERTS + e]
        copy(buf, e, c, pl.multiple_of(off + c * DISPATCH_ROWS, BF16_ROWS)).start()

    for_each_copy(i, start)

    @pl.when(i == pl.num_programs(0) - 1)
    def _():
        for_each_copy(i, lambda e, c: copy(buf, e, c, 0).wait())


def _dispatch(offs, cnts, fill, nval, hb, infot, nblk):
    s, d = hb.shape
    tm = ROW_TILE
    tb = EXPERT_ROWS
    grid_spec = pltpu.PrefetchScalarGridSpec(
        num_scalar_prefetch=4,
        grid=(s // tm,),
        in_specs=[pl.BlockSpec((tm, d), lambda i, *_: (i, 0)),
                  pl.BlockSpec((ROUTE_FIELDS, tm), lambda i, *_: (0, i))],
        out_specs=pl.BlockSpec(memory_space=pl.ANY),
        scratch_shapes=[pltpu.VMEM((2, tm // FIRST_ROWS, N_EXPERTS * FIRST_ROWS, d), BF16),
                        pltpu.VMEM((tb, d), BF16),
                        pltpu.SemaphoreType.DMA((2, N_EXPERTS, tm // DISPATCH_ROWS)),
                        pltpu.SemaphoreType.DMA(())],
    )
    return pl.pallas_call(
        functools.partial(_dispatch_body, nblk=nblk),
        grid_spec=grid_spec,
        out_shape=jax.ShapeDtypeStruct((nblk * tb, d), BF16),
        compiler_params=_params("arbitrary"),
        name="moe_dispatch",
    )(offs, cnts, fill, nval, hb, infot)


def _expert_body(bexp_ref, nval_ref, x_ref, w1_ref, w3_ref, w2_ref, o_ref, acc_ref):
    i = pl.program_id(0)
    f = pl.program_id(1)
    last = pl.num_programs(1) - 1
    used = i < nval_ref[0]

    @pl.when(used)
    def _():
        xb = x_ref[...]
        a = _dot(xb, w1_ref[0])
        b = _dot(xb, w3_ref[0])
        part = _dot((a * jax.nn.sigmoid(a) * b).astype(BF16), w2_ref[0])

        @pl.when(f == 0)
        def _():
            acc_ref[...] = part

        @pl.when(f > 0)
        def _():
            acc_ref[...] += part

        @pl.when(f == last)
        def _():
            o_ref[...] = acc_ref[...].astype(o_ref.dtype)

    @pl.when(jnp.logical_not(used) & (f == last))
    def _():
        o_ref[...] = jnp.zeros(o_ref.shape, o_ref.dtype)


def _experts(xs, w1, w3, w2, bexp, nval, tf):
    rows, d = xs.shape
    ff = w1.shape[2]
    tb = EXPERT_ROWS
    nff = ff // tf

    def xrow(i, f, be, nv):
        return (jnp.minimum(i, nv[0] - 1), 0)

    def fcol(i, f, nv):
        return jnp.where(i < nv[0], f, nff - 1)

    grid_spec = pltpu.PrefetchScalarGridSpec(
        num_scalar_prefetch=2,
        grid=(rows // tb, nff),
        in_specs=[pl.BlockSpec((tb, d), xrow),
                  pl.BlockSpec((1, d, tf), lambda i, f, be, nv: (be[i], 0, fcol(i, f, nv))),
                  pl.BlockSpec((1, d, tf), lambda i, f, be, nv: (be[i], 0, fcol(i, f, nv))),
                  pl.BlockSpec((1, tf, d), lambda i, f, be, nv: (be[i], fcol(i, f, nv), 0))],
        out_specs=pl.BlockSpec((tb, d), lambda i, f, be, nv: (i, 0)),
        scratch_shapes=[pltpu.VMEM((tb, d), F32)],
    )
    return pl.pallas_call(
        _expert_body,
        grid_spec=grid_spec,
        out_shape=jax.ShapeDtypeStruct((rows, d), BF16),
        compiler_params=_params("arbitrary", "arbitrary"),
        name="moe_experts",
    )(bexp, nval, xs, w1, w3, w2)


def _combine_body(offs_ref, cnts_ref, info_ref, x_ref, gp_ref, gf_ref, ys_ref, o_ref, ybuf, acc_ref, sem):
    i = pl.program_id(0)
    nt = pl.num_programs(0)
    tm = x_ref.shape[0]
    chunks = tm // COMBINE_ROWS
    buf = lax.rem(i, 2)

    def copy(b, e, c, src):
        dst = ybuf.at[b, pl.ds((c * N_EXPERTS + e) * COMBINE_ROWS, COMBINE_ROWS)]
        return pltpu.make_async_copy(ys_ref.at[pl.ds(src, COMBINE_ROWS)], dst, sem.at[b, e, c])

    def fetch(step, b):
        for e in range(N_EXPERTS):
            off = offs_ref[step * N_EXPERTS + e]
            copy(b, e, 0, pl.multiple_of(off, BF16_ROWS)).start()
            for c in range(1, chunks):
                @pl.when(cnts_ref[step * N_EXPERTS + e] > c * COMBINE_ROWS)
                def _():
                    copy(b, e, c, pl.multiple_of(off + c * COMBINE_ROWS, BF16_ROWS)).start()

    @pl.when(i == 0)
    def _():
        fetch(0, 0)

    @pl.when(i + 1 < nt)
    def _():
        fetch(i + 1, 1 - buf)

    (e1, g1, r1), (e2, g2, r2) = _routing(info_ref[...])
    for e in range(N_EXPERTS):
        copy(buf, e, 0, 0).wait()
    stack = N_EXPERTS * COMBINE_ROWS
    slot = lax.broadcasted_iota(jnp.int32, (tm, stack), 1).astype(F32)
    y_first = ybuf[buf, 0:stack]
    acc = None
    for ek, gk, rk in ((e1, g1, r1), (e2, g2, r2)):
        pos = jnp.where(rk < float(COMBINE_ROWS), ek * float(COMBINE_ROWS) + rk, -1.0)
        term = gk * _dot(jnp.where(pos == slot, 1.0, 0.0).astype(BF16), y_first)
        acc = term if acc is None else acc + term
    acc_ref[...] = acc

    late = lax.broadcasted_iota(jnp.int32, (tm, COMBINE_ROWS), 1).astype(F32)
    for e in range(N_EXPERTS):
        for c in range(1, chunks):
            @pl.when(cnts_ref[i * N_EXPERTS + e] > c * COMBINE_ROWS)
            def _():
                first = e1 == float(e)
                second = e2 == float(e)
                rank = jnp.where(first, r1, r2)
                gate = jnp.where(first, g1, jnp.where(second, g2, 0.0))
                copy(buf, e, c, 0).wait()
                hit = (rank == late + float(c * COMBINE_ROWS)) & (first | second)
                rows = ybuf[buf, pl.ds((c * N_EXPERTS + e) * COMBINE_ROWS, COMBINE_ROWS)]
                acc_ref[...] += gate * _dot(jnp.where(hit, 1.0, 0.0).astype(BF16), rows)

    o_ref[...] = x_ref[...] + gf_ref[...] * _rms(acc_ref[...], gp_ref[...])


def _combine(offs, cnts, info, x, gp, gf, ys):
    s, d = x.shape
    tm = ROW_TILE
    chunks = tm // COMBINE_ROWS
    grid_spec = pltpu.PrefetchScalarGridSpec(
        num_scalar_prefetch=2,
        grid=(s // tm,),
        in_specs=[pl.BlockSpec((tm, LANES), lambda i, *_: (i, 0)),
                  pl.BlockSpec((tm, d), lambda i, *_: (i, 0)),
                  pl.BlockSpec((1, d), lambda i, *_: (0, 0)),
                  pl.BlockSpec((1, d), lambda i, *_: (0, 0)),
                  pl.BlockSpec(memory_space=pl.ANY)],
        out_specs=pl.BlockSpec((tm, d), lambda i, *_: (i, 0)),
        scratch_shapes=[pltpu.VMEM((2, chunks * N_EXPERTS * COMBINE_ROWS, d), BF16),
                        pltpu.VMEM((tm, d), F32),
                        pltpu.SemaphoreType.DMA((2, N_EXPERTS, chunks))],
    )
    return pl.pallas_call(
        _combine_body,
        grid_spec=grid_spec,
        out_shape=jax.ShapeDtypeStruct((s, d), F32),
        compiler_params=_params("arbitrary"),
        name="moe_combine",
    )(offs, cnts, info, x, gp, gf, ys)


def _rot_half_cols(w):
    half = ROPE // 2
    return jnp.concatenate([-w[..., half:], w[..., :half]], axis=-1)


def _pad_cols(w, before, total):
    return jnp.pad(w, ((0, 0), (before, total - before - w.shape[1])))


def _mixer_weights(w_in, w_uq, w_ukv, w_branch_b):
    d = w_in.shape[0]
    o = 0
    parts = {}
    for name, n in (("u", GMLP_WIDTH), ("v", GMLP_WIDTH), ("cq", Q_RANK), ("ckv", KV_RANK),
                    ("kr", ROPE), ("ga", d), ("gb", d)):
        parts[name] = w_in[:, o:o + n]
        o += n
    kr_main = _pad_cols(parts["kr"], NOPE, HEAD_PAD)
    kr_swap = _pad_cols(_rot_half_cols(parts["kr"]), NOPE, HEAD_PAD)
    w_in_p = jnp.concatenate([parts["u"], parts["v"], parts["cq"], parts["ckv"], kr_main, kr_swap,
                              parts["ga"], parts["gb"]], axis=1).astype(BF16)

    wq = w_uq.reshape(Q_RANK, HEADS, NOPE + ROPE)
    zq = jnp.zeros((Q_RANK, HEADS, HEAD_PAD - NOPE - ROPE), w_uq.dtype)
    wq_main = jnp.concatenate([wq, zq], axis=-1).reshape(Q_RANK, HEADS * HEAD_PAD).astype(BF16)
    wq_swap = jnp.concatenate([jnp.zeros((Q_RANK, HEADS, NOPE), w_uq.dtype),
                               _rot_half_cols(wq[..., NOPE:]), zq], axis=-1)
    wq_swap = wq_swap.reshape(Q_RANK, HEADS * HEAD_PAD).astype(BF16)

    wkv = w_ukv.reshape(KV_RANK, HEADS, NOPE + VDIM)
    zk = jnp.zeros((KV_RANK, HEADS, HEAD_PAD - NOPE), w_ukv.dtype)
    wk = jnp.concatenate([wkv[..., :NOPE], zk], axis=-1).reshape(KV_RANK, HEADS * HEAD_PAD).astype(BF16)
    zv = jnp.zeros((KV_RANK, HEADS, VT_ROWS - VDIM), w_ukv.dtype)
    wv = jnp.concatenate([wkv[..., NOPE:], zv], axis=-1).reshape(KV_RANK, HEADS * VT_ROWS).T.astype(BF16)

    return w_in_p, wq_main, wq_swap, wk, wv, w_branch_b.astype(BF16)


def _expert_layout(cnts, nt, nblk):
    tb = EXPERT_ROWS
    c = cnts.reshape(nt, N_EXPERTS)
    total = jnp.sum(c, axis=0)
    per = jnp.where(total > 0, (total + REGION_SLACK + tb - 1) // tb, 0)
    stop = jnp.cumsum(per)
    start = stop - per
    offs = start[None, :] * tb + (jnp.cumsum(c, axis=0) - c)
    nval = stop[-1]
    blk = jnp.minimum(jnp.arange(nblk, dtype=jnp.int32), nval - 1)
    bexp = jnp.minimum(jnp.sum(blk[:, None] >= stop[None, :], axis=1), N_EXPERTS - 1)
    fill = jnp.concatenate([jnp.where(per > 0, stop - 1, -1), jnp.where(per > 1, stop - 2, -1)])
    i32 = lambda v: v.astype(jnp.int32)
    return i32(offs.reshape(-1)), i32(bexp), i32(nval.reshape(1)), i32(fill)


def kernel(x, c, positions, ada_w, ada_b, norm_mix_pre, norm_mix_post, norm_ffn_pre, norm_ffn_post, w_in, gmlp_ln_g, gmlp_ln_b, gmlp_ws, gmlp_bs, mla_q_norm, mla_w_uq, mla_kv_norm, mla_w_ukv, w_branch_a, w_branch_b, w_out, ffn_w_gate, ffn_w_up, ffn_w_down, moe_router, moe_router_bias, moe_w1, moe_w3, moe_w2):
    batch, s, d = x.shape
    assert batch == 1 and s % ROW_TILE == 0 and s % ATTN_TILE == 0
    depth = ada_w.shape[0]
    xs = x.reshape(s, d)
    mod = _ada_mod(c, ada_w, ada_b)
    cos_t, sin_t = _rope_tables(positions)
    vone = jnp.zeros((HEADS, VT_ROWS), F32).at[:, VDIM].set(1.0).reshape(HEADS * VT_ROWS, 1)
    row = lambda v: v.reshape(1, -1)

    for l in range(depth):
        sh_m, sc_m, g_m, sh_f, sc_f, g_f = [mod[l, :, k * d:(k + 1) * d] for k in range(6)]
        w_in_p, wq_main, wq_swap, wk, wv, wb = _mixer_weights(w_in[l], mla_w_uq[l], mla_w_ukv[l], w_branch_b[l])
        bias = jnp.broadcast_to(gmlp_bs[l].T[:, :, None], (CHUNK, GMLP_GROUPS, GMLP_GROUP_DIM))
        bias = bias.reshape(CHUNK, GMLP_WIDTH)
        a, gb, q, k, vt = _mixer_front(xs, row(norm_mix_pre[l]), sc_m, sh_m, w_in_p,
                                       row(gmlp_ln_g[l]), row(gmlp_ln_b[l]), gmlp_ws[l], bias,
                                       w_branch_a[l].astype(BF16), cos_t, sin_t,
                                       row(mla_q_norm[l]), row(mla_kv_norm[l]), wq_main, wq_swap, wk, wv, vone)
        o = _attention(q, k, vt)
        merge_args = (a, gb, o, xs, wb, w_out[l].astype(BF16), row(norm_mix_post[l]), g_m)

        j = l // 2
        if l % 2 == 0:
            ff = ffn_w_gate.shape[2]
            ff_pad = -(-ff // (2 * LANES)) * (2 * LANES)
            wg = jnp.pad(ffn_w_gate[j], ((0, 0), (0, ff_pad - ff))).astype(BF16)
            wu = jnp.pad(ffn_w_up[j], ((0, 0), (0, ff_pad - ff))).astype(BF16)
            wd = jnp.pad(ffn_w_down[j], ((0, ff_pad - ff), (0, 0))).astype(BF16)
            xs = _dense_ffn(merge_args, row(norm_ffn_pre[l]), sc_f, sh_f, wg, wu, wd, row(norm_ffn_post[l]), g_f)
        else:
            nt = s // ROW_TILE
            tb = EXPERT_ROWS
            max_rows = 2 * s + nt * N_EXPERTS * (BF16_ROWS - 1) + N_EXPERTS * (REGION_SLACK + tb - 1)
            nblk = -(-max_rows // tb) + 1
            wr = jnp.pad(moe_router[j], ((0, 0), (0, LANES - N_EXPERTS)))
            br = jnp.pad(moe_router_bias[j], (0, LANES - N_EXPERTS), constant_values=NEG).reshape(1, LANES)
            xs, hb, info, infot, cnts = _route(merge_args, row(norm_ffn_pre[l]), sc_f, sh_f, wr, br)
            offs, bexp, nval, fill = _expert_layout(cnts, nt, nblk)
            xsort = _dispatch(offs, cnts, fill, nval, hb, infot, nblk)
            ysort = _experts(xsort, moe_w1[j].astype(BF16), moe_w3[j].astype(BF16), moe_w2[j].astype(BF16),
                             bexp, nval, tf=moe_w1.shape[3] // 2)
            xs = _combine(offs, cnts, info, xs, row(norm_ffn_post[l]), g_f, ysort)
    return xs.reshape(batch, s, d)
```

```python
import functools

import jax
import jax.numpy as jnp
import numpy as np
from jax import lax
from jax.experimental import pallas as pl
from jax.experimental.pallas import tpu as pltpu

F32 = jnp.float32
BF16 = jnp.bfloat16

EPS = 1e-6
LANES = 128
BF16_ROWS = 16
GMLP_GROUPS = 8
GMLP_GROUP_DIM = 64
GMLP_WIDTH = GMLP_GROUPS * GMLP_GROUP_DIM
CHUNK = 128
HEADS = 8
NOPE = 64
ROPE = 32
VDIM = 64
HEAD_PAD = 128
VT_ROWS = 80
Q_RANK = 384
KV_RANK = 256
ROPE_THETA = 10000.0
N_EXPERTS = 8
NEG = -1e30

ROW_TILE = 512
ATTN_TILE = 512
ATTN_HEADS = 2
STAT_ROWS = 8
EXPERT_ROWS = 512
DISPATCH_ROWS = 128
COMBINE_ROWS = 256
FIRST_ROWS = 256
ROUTE_FIELDS = 8
REGION_SLACK = DISPATCH_ROWS - BF16_ROWS
VMEM_LIMIT = 52 * 1024 * 1024


def _params(*sem):
    return pltpu.CompilerParams(dimension_semantics=sem, vmem_limit_bytes=VMEM_LIMIT)


def _dot(a, b):
    return jnp.dot(a, b, preferred_element_type=F32)


def _rms(x, g):
    return x * lax.rsqrt(jnp.mean(x * x, axis=-1, keepdims=True) + EPS) * g


def _gelu(x):
    return 0.5 * x * (1.0 + lax.erf(x * np.float32(0.7071067811865476)))


def _full(shape):
    return pl.BlockSpec(shape, lambda *_: (0,) * len(shape), pipeline_mode=pl.Buffered(1))


def _mod_body(c_ref, w_ref, b_ref, o_ref):
    c = c_ref[...]
    ca = c * jax.nn.sigmoid(c)
    o_ref[0] = jnp.sum(ca * w_ref[0], axis=0, keepdims=True) + b_ref[0]


def _ada_mod(c, ada_w, ada_b):
    n_layers, d, n = ada_w.shape
    tn = n // 4
    return pl.pallas_call(
        _mod_body,
        grid=(n_layers, n // tn),
        in_specs=[pl.BlockSpec((d, 1), lambda l, j: (0, 0)),
                  pl.BlockSpec((1, d, tn), lambda l, j: (l, 0, j)),
                  pl.BlockSpec((1, 1, tn), lambda l, j: (l, 0, j))],
        out_specs=pl.BlockSpec((1, 1, tn), lambda l, j: (l, 0, j)),
        out_shape=jax.ShapeDtypeStruct((n_layers, 1, n), F32),
        compiler_params=_params("arbitrary", "arbitrary"),
        name="ada_mod",
    )(c.reshape(d, 1), ada_w, ada_b.reshape(n_layers, 1, n))


def _rope_body(pos_ref, invf_ref, cos_ref, sin_ref):
    ang = pos_ref[...].astype(F32) * invf_ref[...]
    cos_ref[...] = jnp.cos(ang)
    sin_ref[...] = jnp.sin(ang)


def _rope_tables(positions):
    s = positions.shape[-1]
    half = ROPE // 2
    per_row = LANES // half
    inv_freq = 1.0 / (ROPE_THETA ** (jnp.arange(0, ROPE, 2, dtype=F32) / ROPE))
    pos_dense = jnp.broadcast_to(positions.reshape(s, 1), (s, half)).reshape(s // per_row, LANES)
    invf = jnp.tile(inv_freq, per_row).reshape(1, LANES)
    rows = s // per_row
    tr = rows // 4
    cos_d, sin_d = pl.pallas_call(
        _rope_body,
        grid=(rows // tr,),
        in_specs=[pl.BlockSpec((tr, LANES), lambda i: (i, 0)), _full((1, LANES))],
        out_specs=[pl.BlockSpec((tr, LANES), lambda i: (i, 0))] * 2,
        out_shape=[jax.ShapeDtypeStruct((rows, LANES), F32)] * 2,
        compiler_params=_params("arbitrary"),
        name="rope_tables",
    )(pos_dense, invf)
    cos = cos_d.reshape(s, half)
    sin = sin_d.reshape(s, half)
    ones = jnp.ones((s, NOPE), F32)
    zeros = jnp.zeros((s, NOPE), F32)
    pad = jnp.zeros((s, HEAD_PAD - NOPE - ROPE), F32)
    cos_t = jnp.concatenate([ones, cos, cos, pad], axis=1)
    sin_t = jnp.concatenate([zeros, sin, sin, pad], axis=1)
    return cos_t, sin_t


def _gmlp_branch(u, v, ga, lng_ref, lnb_ref, ws_ref, bias_ref, wa_ref):
    tm = u.shape[0]
    gu = _gelu(u)
    gv = _gelu(v)
    mu = jnp.mean(gv, axis=-1, keepdims=True)
    xc = gv - mu
    vn = xc * lax.rsqrt(jnp.mean(xc * xc, axis=-1, keepdims=True) + EPS) * lng_ref[...] + lnb_ref[...]
    vb = vn.astype(BF16)
    t_idx = lax.broadcasted_iota(jnp.int32, (CHUNK, CHUNK), 0)
    s_idx = lax.broadcasted_iota(jnp.int32, (CHUNK, CHUNK), 1)
    causal = s_idx <= t_idx
    ws = [jnp.where(causal, ws_ref[g], 0.0).astype(BF16) for g in range(GMLP_GROUPS)]
    left = lax.broadcasted_iota(jnp.int32, (CHUNK, LANES), 1) < GMLP_GROUP_DIM
    bias = bias_ref[...]
    z_rows = []
    for c in range(tm // CHUNK):
        vc = vb[c * CHUNK:(c + 1) * CHUNK]
        z_cols = []
        for j in range(GMLP_WIDTH // LANES):
            vp = vc[:, j * LANES:(j + 1) * LANES]
            z_cols.append(jnp.where(left, _dot(ws[2 * j], vp), _dot(ws[2 * j + 1], vp)))
        z_rows.append(jnp.concatenate(z_cols, axis=1) + bias)
    z = jnp.concatenate(z_rows, axis=0)
    gated = (gu * z).astype(BF16)
    return jax.nn.sigmoid(ga) * _dot(gated, wa_ref[...])


def _mla_qkv(cq, ckv, kr, cos_ref, sin_ref, qn_ref, kvn_ref, wqm_ref, wqs_ref, wk_ref, wvt_ref, vone_ref,
             q_ref, k_ref, vt_ref, scale):
    cos = cos_ref[...]
    sin = sin_ref[...]
    cqn = _rms(cq, qn_ref[...]).astype(BF16)
    qm = _dot(cqn, wqm_ref[...])
    qs = _dot(cqn, wqs_ref[...])
    ckn = _rms(ckv, kvn_ref[...]).astype(BF16)
    km = _dot(ckn, wk_ref[...])
    kpe = kr[:, :HEAD_PAD] * cos + kr[:, HEAD_PAD:] * sin
    for h in range(HEADS):
        sl = slice(h * HEAD_PAD, (h + 1) * HEAD_PAD)
        q_ref[:, sl] = ((qm[:, sl] * cos + qs[:, sl] * sin) * scale).astype(q_ref.dtype)
        k_ref[:, sl] = (km[:, sl] + kpe).astype(k_ref.dtype)
    vt = lax.dot_general(wvt_ref[...], ckn, (((1,), (1,)), ((), ())), preferred_element_type=F32)
    vt_ref[...] = (vt + vone_ref[...]).astype(vt_ref.dtype)


def _front_body(x_ref, g_ref, sc_ref, sh_ref, w_ref, lng_ref, lnb_ref, ws_ref, bias_ref, wa_ref,
                cos_ref, sin_ref, qn_ref, kvn_ref, wqm_ref, wqs_ref, wk_ref, wvt_ref, vone_ref,
                a_ref, gb_ref, q_ref, k_ref, vt_ref, *, scale):
    h = _rms(x_ref[...], g_ref[...]) * (1.0 + sc_ref[...]) + sh_ref[...]
    proj = _dot(h.astype(BF16), w_ref[...])
    d = x_ref.shape[1]
    cuts = np.cumsum([0, GMLP_WIDTH, GMLP_WIDTH, Q_RANK, KV_RANK, 2 * HEAD_PAD, d, d])
    u, v, cq, ckv, kr, ga, gb = [proj[:, lo:hi] for lo, hi in zip(cuts[:-1], cuts[1:])]
    gb_ref[...] = gb.astype(gb_ref.dtype)
    a_ref[...] = _gmlp_branch(u, v, ga, lng_ref, lnb_ref, ws_ref, bias_ref, wa_ref).astype(a_ref.dtype)
    _mla_qkv(cq, ckv, kr, cos_ref, sin_ref, qn_ref, kvn_ref, wqm_ref, wqs_ref, wk_ref, wvt_ref, vone_ref,
             q_ref, k_ref, vt_ref, scale)


def _mixer_front(x, g, sc, sh, w, lng, lnb, ws, bias, wa, cos_t, sin_t, qn, kvn, wqm, wqs, wk, wvt, vone):
    s, d = x.shape
    tm = ROW_TILE
    width = HEADS * HEAD_PAD
    row = lambda n: pl.BlockSpec((tm, n), lambda i: (i, 0))
    scale = float((NOPE + ROPE) ** -0.5 * np.log2(np.e))
    consts = (g, sc, sh, w, lng, lnb, ws, bias, wa)
    mla_consts = (qn, kvn, wqm, wqs, wk, wvt, vone)
    return pl.pallas_call(
        functools.partial(_front_body, scale=scale),
        grid=(s // tm,),
        in_specs=[row(d)] + [_full(c.shape) for c in consts] + [row(HEAD_PAD), row(HEAD_PAD)]
                 + [_full(c.shape) for c in mla_consts],
        out_specs=[row(d), row(d), row(width), row(width), pl.BlockSpec((wvt.shape[0], tm), lambda i: (0, i))],
        out_shape=[jax.ShapeDtypeStruct((s, d), BF16), jax.ShapeDtypeStruct((s, d), BF16),
                   jax.ShapeDtypeStruct((s, width), BF16), jax.ShapeDtypeStruct((s, width), BF16),
                   jax.ShapeDtypeStruct((wvt.shape[0], s), BF16)],
        compiler_params=_params("arbitrary"),
        name="mixer_front",
    )(x, *consts, cos_t, sin_t, *mla_consts)


def _attn_body(q_ref, k_ref, vt_ref, o_ref, s_ref, mx_ref, m_ref, acc_ref):
    t = o_ref.shape[0]
    heads = q_ref.shape[1] // HEAD_PAD
    qi = pl.program_id(1)
    m_ref[...] = jnp.full(m_ref.shape, NEG, F32)
    acc_ref[...] = jnp.zeros(acc_ref.shape, F32)

    def scores(blk, slot, q_tile):
        off = pl.multiple_of(blk * t, t)
        q_off = pl.multiple_of(q_tile * t, t)
        for h in range(heads):
            sl = slice(h * HEAD_PAD, (h + 1) * HEAD_PAD)
            s = lax.dot_general(k_ref[pl.ds(off, t), sl], q_ref[pl.ds(q_off, t), sl],
                                (((1,), (1,)), ((), ())), preferred_element_type=F32)
            s_ref[slot, h] = s
            mx_ref[slot, h] = jnp.broadcast_to(jnp.max(s, axis=0, keepdims=True), (STAT_ROWS, t))

    def consume(blk, slot, mask):
        off = pl.multiple_of(blk * t, t)
        for h in range(heads):
            sl = slice(h * HEAD_PAD, (h + 1) * HEAD_PAD)
            s = s_ref[slot, h]
            if mask is None:
                mx = mx_ref[slot, h]
            else:
                s = jnp.where(mask, s, NEG)
                mx = jnp.max(s, axis=0, keepdims=True)
            m_prev = m_ref[h]
            m_new = jnp.maximum(m_prev, mx)
            alpha = jnp.exp2(m_prev - m_new)
            pv = alpha[0:1] * acc_ref[h]
            half = t // 2
            for part in range(2):
                p = jnp.exp2(s[part * half:(part + 1) * half] - m_new[0:1]).astype(BF16)
                keys = pl.ds(pl.multiple_of(off + part * half, half), half)
                pv = pv + _dot(vt_ref[h * VT_ROWS:(h + 1) * VT_ROWS, keys], p)
            acc_ref[h] = pv
            m_ref[h] = m_new

    @pl.when(qi == 0)
    def _():
        scores(0, 0, qi)

    def run(blk, n):
        for u in range(n):
            scores(blk + u + 1, (u + 1) % 2, qi)
            consume(blk + u, u % 2, None)

    def octet(j, carry):
        run(8 * j, 8)
        return carry

    lax.fori_loop(0, qi // 8, octet, 0)
    done = (qi // 8) * 8
    for n in (4, 2):
        more = qi - done >= n

        @pl.when(more)
        def _():
            run(done, n)

        done = done + jnp.where(more, n, 0)

    row = lax.broadcasted_iota(jnp.int32, (t, t), 0)
    col = lax.broadcasted_iota(jnp.int32, (t, t), 1)
    causal = row <= col
    odd = lax.rem(qi, 2) == 1

    @pl.when(odd)
    def _():
        scores(qi, 1, qi)
        consume(qi - 1, 0, None)
        consume(qi, 1, causal)

    @pl.when(jnp.logical_not(odd))
    def _():
        consume(qi, 0, causal)

    def finalize():
        lower = lax.broadcasted_iota(jnp.int32, (t, HEAD_PAD), 1) < VDIM
        for j in range(heads // 2):
            outs = []
            for h in (2 * j, 2 * j + 1):
                acc = acc_ref[h]
                out_t = jnp.concatenate([acc / acc[VDIM:VDIM + 1], jnp.zeros((HEAD_PAD - VT_ROWS, t), F32)],
                                        axis=0)
                outs.append(out_t.T)
            packed = jnp.where(lower, outs[0], pltpu.roll(outs[1], VDIM, 1))
            o_ref[:, j * HEAD_PAD:(j + 1) * HEAD_PAD] = packed.astype(o_ref.dtype)

    last = qi == pl.num_programs(1) - 1

    @pl.when(jnp.logical_not(last))
    def _():
        scores(0, 0, qi + 1)
        finalize()

    @pl.when(last)
    def _():
        finalize()


def _attention(q, k, vt):
    s, width = q.shape
    t = ATTN_TILE
    gw = ATTN_HEADS * HEAD_PAD
    return pl.pallas_call(
        _attn_body,
        grid=(width // gw, s // t),
        in_specs=[pl.BlockSpec((s, gw), lambda h, i: (0, h)),
                  pl.BlockSpec((s, gw), lambda h, i: (0, h)),
                  pl.BlockSpec((ATTN_HEADS * VT_ROWS, s), lambda h, i: (h, 0))],
        out_specs=pl.BlockSpec((t, ATTN_HEADS * VDIM), lambda h, i: (i, h)),
        out_shape=jax.ShapeDtypeStruct((s, HEADS * VDIM), BF16),
        scratch_shapes=[pltpu.VMEM((2, ATTN_HEADS, t, t), F32),
                        pltpu.VMEM((2, ATTN_HEADS, STAT_ROWS, t), F32),
                        pltpu.VMEM((ATTN_HEADS, STAT_ROWS, t), F32),
                        pltpu.VMEM((ATTN_HEADS, VT_ROWS, t), F32)],
        compiler_params=_params("arbitrary", "arbitrary"),
        name="mla_attention",
    )(q, k, vt)


MERGE_ARGS = 8


def _merge_math(a_ref, gb_ref, o_ref, x_ref, wb_ref, wo_ref, gp_ref, gm_ref):
    yb = _dot(o_ref[...], wb_ref[...])
    merged = a_ref[...].astype(F32) + jax.nn.sigmoid(gb_ref[...].astype(F32)) * yb
    y = _dot(merged.astype(BF16), wo_ref[...])
    return x_ref[...] + gm_ref[...] * _rms(y, gp_ref[...])


def _merge_specs(a, gb, o, x, wb, wo, gp, gm):
    tm = ROW_TILE
    d = x.shape[1]
    row = pl.BlockSpec((tm, d), lambda i: (i, 0))
    return [row, row, pl.BlockSpec((tm, o.shape[1]), lambda i: (i, 0)), row,
            _full(wb.shape), _full(wo.shape), _full((1, d)), _full((1, d))]


def _ffn_body(*refs):
    g_ref, sc_ref, sh_ref, wg_ref, wu_ref, wd_ref, gp_ref, gf_ref, o_ref = refs[MERGE_ARGS:]
    x = _merge_math(*refs[:MERGE_ARGS])
    hb = (_rms(x, g_ref[...]) * (1.0 + sc_ref[...]) + sh_ref[...]).astype(BF16)
    a = _dot(hb, wg_ref[...])
    b = _dot(hb, wu_ref[...])
    y = _dot((a * jax.nn.sigmoid(a) * b).astype(BF16), wd_ref[...])
    o_ref[...] = x + gf_ref[...] * _rms(y, gp_ref[...])


def _dense_ffn(merge_args, g, sc, sh, wg, wu, wd, gp, gf):
    s, d = merge_args[3].shape
    tm = ROW_TILE
    row = pl.BlockSpec((tm, d), lambda i: (i, 0))
    vec = _full((1, d))
    return pl.pallas_call(
        _ffn_body,
        grid=(s // tm,),
        in_specs=_merge_specs(*merge_args)
                 + [vec, vec, vec, _full(wg.shape), _full(wu.shape), _full(wd.shape), vec, vec],
        out_specs=row,
        out_shape=jax.ShapeDtypeStruct((s, d), F32),
        compiler_params=_params("arbitrary"),
        name="merge_dense_ffn",
    )(*merge_args, g, sc, sh, wg, wu, wd, gp, gf)


def _route_body(*refs):
    g_ref, sc_ref, sh_ref, wr_ref, br_ref, x_ref, hb_ref, info_ref, infot_ref, cnt_ref = refs[MERGE_ARGS:]
    i = pl.program_id(0)
    tm = x_ref.shape[0]
    x = _merge_math(*refs[:MERGE_ARGS])
    x_ref[...] = x
    h = _rms(x, g_ref[...]) * (1.0 + sc_ref[...]) + sh_ref[...]
    hb = h.astype(BF16)
    hb_ref[...] = hb
    h_lo = (h - hb.astype(F32)).astype(BF16)
    w = wr_ref[...]
    w_hi = w.astype(BF16)
    w_lo = (w - w_hi.astype(F32)).astype(BF16)
    logits = _dot(hb, w_hi) + (_dot(h_lo, w_hi) + _dot(hb, w_lo)) + br_ref[...]

    lane = lax.broadcasted_iota(jnp.int32, (tm, LANES), 1)
    m1 = jnp.max(logits, axis=-1, keepdims=True)
    i1 = jnp.min(jnp.where(logits == m1, lane, LANES), axis=-1, keepdims=True)
    oh1 = lane == i1
    rest = jnp.where(oh1, -3e38, logits)
    m2 = jnp.max(rest, axis=-1, keepdims=True)
    i2 = jnp.min(jnp.where(rest == m2, lane, LANES), axis=-1, keepdims=True)
    oh2 = lane == i2
    ex = jnp.exp(m2 - m1)
    g1 = 1.0 / (1.0 + ex)
    g2 = ex / (1.0 + ex)

    ohf = jnp.where(oh1 | oh2, 1.0, 0.0)
    r_idx = lax.broadcasted_iota(jnp.int32, (tm, tm), 0)
    c_idx = lax.broadcasted_iota(jnp.int32, (tm, tm), 1)
    earlier = jnp.where(c_idx < r_idx, 1.0, 0.0).astype(BF16)
    rank = _dot(earlier, ohf.astype(BF16))
    rank1 = jnp.sum(jnp.where(oh1, rank, 0.0), axis=-1, keepdims=True)
    rank2 = jnp.sum(jnp.where(oh2, rank, 0.0), axis=-1, keepdims=True)
    info = jnp.where(lane == 0, i1.astype(F32),
           jnp.where(lane == 1, i2.astype(F32),
           jnp.where(lane == 2, g1,
           jnp.where(lane == 3, g2,
           jnp.where(lane == 4, rank1,
           jnp.where(lane == 5, rank2, 0.0))))))
    info_ref[...] = info
    infot_ref[...] = info.T[:ROUTE_FIELDS]
    cnt = jnp.sum(ohf, axis=0, keepdims=True).astype(jnp.int32)
    cnt_al = ((cnt + (BF16_ROWS - 1)) // BF16_ROWS) * BF16_ROWS
    for e in range(N_EXPERTS):
        cnt_ref[i * N_EXPERTS + e] = cnt_al[0, e]


def _route(merge_args, g, sc, sh, wr, br):
    s, d = merge_args[3].shape
    tm = ROW_TILE
    nt = s // tm
    row = pl.BlockSpec((tm, d), lambda i: (i, 0))
    return pl.pallas_call(
        _route_body,
        grid=(nt,),
        in_specs=_merge_specs(*merge_args)
                 + [_full((1, d)), _full((1, d)), _full((1, d)), _full(wr.shape), _full(br.shape)],
        out_specs=[row, row,
                   pl.BlockSpec((tm, LANES), lambda i: (i, 0)),
                   pl.BlockSpec((ROUTE_FIELDS, tm), lambda i: (0, i)),
                   pl.BlockSpec(memory_space=pltpu.SMEM)],
        out_shape=[jax.ShapeDtypeStruct((s, d), F32),
                   jax.ShapeDtypeStruct((s, d), BF16),
                   jax.ShapeDtypeStruct((s, LANES), F32),
                   jax.ShapeDtypeStruct((ROUTE_FIELDS, s), F32),
                   jax.ShapeDtypeStruct((nt * N_EXPERTS,), jnp.int32)],
        compiler_params=_params("arbitrary"),
        name="merge_moe_route",
    )(*merge_args, g, sc, sh, wr, br)


def _routing(info):
    return [(info[:, k:k + 1], info[:, 2 + k:3 + k], info[:, 4 + k:5 + k]) for k in range(2)]


def _dispatch_body(offs_ref, cnts_ref, fill_ref, nval_ref, hb_ref, infot_ref, xs_ref, xbuf, zbuf, sem, zsem, *, nblk):
    i = pl.program_id(0)
    tm = hb_ref.shape[0]
    tb = zbuf.shape[0]

    def zero_block(blk):
        cp = pltpu.make_async_copy(zbuf, xs_ref.at[pl.ds(pl.multiple_of(blk * tb, tb), tb)], zsem)
        cp.start()
        cp.wait()

    @pl.when(i == 0)
    def _():
        zbuf[...] = jnp.zeros(zbuf.shape, zbuf.dtype)
        for n in range(fill_ref.shape[0]):
            @pl.when(fill_ref[n] >= 0)
            def _():
                zero_block(fill_ref[n])

        def unused(blk, carry):
            zero_block(blk)
            return carry

        lax.fori_loop(nval_ref[0], nblk, unused, 0)

    hb = hb_ref[...]
    fields = infot_ref[...]
    e1, e2, r1, r2 = fields[0:1], fields[1:2], fields[4:5], fields[5:6]
    chunks = tm // DISPATCH_ROWS
    buf = lax.rem(i, 2)

    def first_slot(e, r):
        return jnp.where(r < float(FIRST_ROWS), e * float(FIRST_ROWS) + r, -1.0)

    p1 = first_slot(e1, r1)
    p2 = first_slot(e2, r2)
    slot = lax.broadcasted_iota(jnp.int32, (N_EXPERTS * FIRST_ROWS, tm), 0).astype(F32)
    sel = jnp.where((p1 == slot) | (p2 == slot), 1.0, 0.0).astype(BF16)
    xbuf[buf, 0] = _dot(sel, hb).astype(BF16)

    first_chunks = FIRST_ROWS // DISPATCH_ROWS

    def buf_rows(e, c):
        return c // first_chunks, e * FIRST_ROWS + (c % first_chunks) * DISPATCH_ROWS

    for e in range(N_EXPERTS):
        for c in range(first_chunks, chunks):
            @pl.when(cnts_ref[i * N_EXPERTS + e] > c * DISPATCH_ROWS)
            def _():
                first = e1 == float(e)
                rank = jnp.where(first, r1, r2)
                late = (lax.broadcasted_iota(jnp.int32, (DISPATCH_ROWS, tm), 0) + c * DISPATCH_ROWS).astype(F32)
                hit = (rank == late) & (first | (e2 == float(e)))
                part, row0 = buf_rows(e, c)
                xbuf[buf, part, row0:row0 + DISPATCH_ROWS] = _dot(jnp.where(hit, 1.0, 0.0).astype(BF16),
                                                                  hb).astype(BF16)

    def copy(b, e, c, dst):
        part, row0 = buf_rows(e, c)
        return pltpu.make_async_copy(xbuf.at[b, part, pl.ds(row0, DISPATCH_ROWS)],
                                     xs_ref.at[pl.ds(dst, DISPATCH_ROWS)], sem.at[b, e, c])

    def for_each_copy(step, fn):
        for e in range(N_EXPERTS):
            for c in range(chunks):
                @pl.when(cnts_ref[step * N_EXPERTS + e] > c * DISPATCH_ROWS)
                def _():
                    fn(e, c)

    @pl.when(i > 0)
    def _():
        for_each_copy(i - 1, lambda e, c: copy(1 - buf, e, c, 0).wait())

    def start(e, c):
        off = offs_ref[i * N_EXPERTS + e]
        copy(buf, e, c, pl.multiple_of(off + c * DISPATCH_ROWS, BF16_ROWS)).start()

    for_each_copy(i, start)

    @pl.when(i == pl.num_programs(0) - 1)
    def _():
        for_each_copy(i, lambda e, c: copy(buf, e, c, 0).wait())


def _dispatch(offs, cnts, fill, nval, hb, infot, nblk):
    s, d = hb.shape
    tm = ROW_TILE
    tb = EXPERT_ROWS
    grid_spec = pltpu.PrefetchScalarGridSpec(
        num_scalar_prefetch=4,
        grid=(s // tm,),
        in_specs=[pl.BlockSpec((tm, d), lambda i, *_: (i, 0)),
                  pl.BlockSpec((ROUTE_FIELDS, tm), lambda i, *_: (0, i))],
        out_specs=pl.BlockSpec(memory_space=pl.ANY),
        scratch_shapes=[pltpu.VMEM((2, tm // FIRST_ROWS, N_EXPERTS * FIRST_ROWS, d), BF16),
                        pltpu.VMEM((tb, d), BF16),
                        pltpu.SemaphoreType.DMA((2, N_EXPERTS, tm // DISPATCH_ROWS)),
                        pltpu.SemaphoreType.DMA(())],
    )
    return pl.pallas_call(
        functools.partial(_dispatch_body, nblk=nblk),
        grid_spec=grid_spec,
        out_shape=jax.ShapeDtypeStruct((nblk * tb, d), BF16),
        compiler_params=_params("arbitrary"),
        name="moe_dispatch",
    )(offs, cnts, fill, nval, hb, infot)


def _expert_body(bexp_ref, nval_ref, x_ref, w1_ref, w3_ref, w2_ref, o_ref, acc_ref, *, nff):
    i = pl.program_id(0)
    f = pl.program_id(1)
    used = i < nval_ref[0]

    def ff_slice(first, final):
        xb = x_ref[...]
        a = _dot(xb, w1_ref[0])
        b = _dot(xb, w3_ref[0])
        acc = _dot((a * jax.nn.sigmoid(a) * b).astype(BF16), w2_ref[0])
        if not first:
            acc = acc_ref[...] + acc
        if final:
            o_ref[...] = acc.astype(o_ref.dtype)
        else:
            acc_ref[...] = acc

    for step in sorted({0, min(1, nff - 1), nff - 1}):
        pick = f == step if step in (0, nff - 1) else (f > 0) & (f < nff - 1)

        @pl.when(used & pick)
        def _():
            ff_slice(step == 0, step == nff - 1)

    @pl.when(jnp.logical_not(used) & (f == nff - 1))
    def _():
        o_ref[...] = jnp.zeros(o_ref.shape, o_ref.dtype)


def _experts(xs, w1, w3, w2, bexp, nval, tf):
    rows, d = xs.shape
    ff = w1.shape[2]
    tb = EXPERT_ROWS
    nff = ff // tf

    def xrow(i, f, be, nv):
        return (jnp.minimum(i, nv[0] - 1), 0)

    def fcol(i, f, nv):
        return jnp.where(i < nv[0], f, nff - 1)

    grid_spec = pltpu.PrefetchScalarGridSpec(
        num_scalar_prefetch=2,
        grid=(rows // tb, nff),
        in_specs=[pl.BlockSpec((tb, d), xrow),
                  pl.BlockSpec((1, d, tf), lambda i, f, be, nv: (be[i], 0, fcol(i, f, nv))),
                  pl.BlockSpec((1, d, tf), lambda i, f, be, nv: (be[i], 0, fcol(i, f, nv))),
                  pl.BlockSpec((1, tf, d), lambda i, f, be, nv: (be[i], fcol(i, f, nv), 0))],
        out_specs=pl.BlockSpec((tb, d), lambda i, f, be, nv: (i, 0)),
        scratch_shapes=[pltpu.VMEM((tb, d), F32)],
    )
    return pl.pallas_call(
        functools.partial(_expert_body, nff=nff),
        grid_spec=grid_spec,
        out_shape=jax.ShapeDtypeStruct((rows, d), BF16),
        compiler_params=_params("arbitrary", "arbitrary"),
        name="moe_experts",
    )(bexp, nval, xs, w1, w3, w2)


def _combine_body(offs_ref, cnts_ref, info_ref, x_ref, gp_ref, gf_ref, ys_ref, o_ref, ybuf, acc_ref, sem):
    i = pl.program_id(0)
    nt = pl.num_programs(0)
    tm = x_ref.shape[0]
    chunks = tm // COMBINE_ROWS
    buf = lax.rem(i, 2)

    def copy(b, e, c, src):
        dst = ybuf.at[b, pl.ds((c * N_EXPERTS + e) * COMBINE_ROWS, COMBINE_ROWS)]
        return pltpu.make_async_copy(ys_ref.at[pl.ds(src, COMBINE_ROWS)], dst, sem.at[b, e, c])

    def fetch(step, b):
        for e in range(N_EXPERTS):
            off = offs_ref[step * N_EXPERTS + e]
            copy(b, e, 0, pl.multiple_of(off, BF16_ROWS)).start()
            for c in range(1, chunks):
                @pl.when(cnts_ref[step * N_EXPERTS + e] > c * COMBINE_ROWS)
                def _():
                    copy(b, e, c, pl.multiple_of(off + c * COMBINE_ROWS, BF16_ROWS)).start()

    @pl.when(i == 0)
    def _():
        fetch(0, 0)

    @pl.when(i + 1 < nt)
    def _():
        fetch(i + 1, 1 - buf)

    (e1, g1, r1), (e2, g2, r2) = _routing(info_ref[...])
    for e in range(N_EXPERTS):
        copy(buf, e, 0, 0).wait()
    stack = N_EXPERTS * COMBINE_ROWS
    slot = lax.broadcasted_iota(jnp.int32, (tm, stack), 1).astype(F32)
    y_first = ybuf[buf, 0:stack]
    acc = None
    for ek, gk, rk in ((e1, g1, r1), (e2, g2, r2)):
        pos = jnp.where(rk < float(COMBINE_ROWS), ek * float(COMBINE_ROWS) + rk, -1.0)
        term = gk * _dot(jnp.where(pos == slot, 1.0, 0.0).astype(BF16), y_first)
        acc = term if acc is None else acc + term
    acc_ref[...] = acc

    late = lax.broadcasted_iota(jnp.int32, (tm, COMBINE_ROWS), 1).astype(F32)
    for e in range(N_EXPERTS):
        for c in range(1, chunks):
            @pl.when(cnts_ref[i * N_EXPERTS + e] > c * COMBINE_ROWS)
            def _():
                first = e1 == float(e)
                second = e2 == float(e)
                rank = jnp.where(first, r1, r2)
                gate = jnp.where(first, g1, jnp.where(second, g2, 0.0))
                copy(buf, e, c, 0).wait()
                hit = (rank == late + float(c * COMBINE_ROWS)) & (first | second)
                rows = ybuf[buf, pl.ds((c * N_EXPERTS + e) * COMBINE_ROWS, COMBINE_ROWS)]
                acc_ref[...] += gate * _dot(jnp.where(hit, 1.0, 0.0).astype(BF16), rows)

    o_ref[...] = x_ref[...] + gf_ref[...] * _rms(acc_ref[...], gp_ref[...])


def _combine(offs, cnts, info, x, gp, gf, ys):
    s, d = x.shape
    tm = ROW_TILE
    chunks = tm // COMBINE_ROWS
    grid_spec = pltpu.PrefetchScalarGridSpec(
        num_scalar_prefetch=2,
        grid=(s // tm,),
        in_specs=[pl.BlockSpec((tm, LANES), lambda i, *_: (i, 0)),
                  pl.BlockSpec((tm, d), lambda i, *_: (i, 0)),
                  pl.BlockSpec((1, d), lambda i, *_: (0, 0)),
                  pl.BlockSpec((1, d), lambda i, *_: (0, 0)),
                  pl.BlockSpec(memory_space=pl.ANY)],
        out_specs=pl.BlockSpec((tm, d), lambda i, *_: (i, 0)),
        scratch_shapes=[pltpu.VMEM((2, chunks * N_EXPERTS * COMBINE_ROWS, d), BF16),
                        pltpu.VMEM((tm, d), F32),
                        pltpu.SemaphoreType.DMA((2, N_EXPERTS, chunks))],
    )
    return pl.pallas_call(
        _combine_body,
        grid_spec=grid_spec,
        out_shape=jax.ShapeDtypeStruct((s, d), F32),
        compiler_params=_params("arbitrary"),
        name="moe_combine",
    )(offs, cnts, info, x, gp, gf, ys)


def _rot_half_cols(w):
    half = ROPE // 2
    return jnp.concatenate([-w[..., half:], w[..., :half]], axis=-1)


def _pad_cols(w, before, total):
    return jnp.pad(w, ((0, 0), (before, total - before - w.shape[1])))


def _mixer_weights(w_in, w_uq, w_ukv, w_branch_b):
    d = w_in.shape[0]
    o = 0
    parts = {}
    for name, n in (("u", GMLP_WIDTH), ("v", GMLP_WIDTH), ("cq", Q_RANK), ("ckv", KV_RANK),
                    ("kr", ROPE), ("ga", d), ("gb", d)):
        parts[name] = w_in[:, o:o + n]
        o += n
    kr_main = _pad_cols(parts["kr"], NOPE, HEAD_PAD)
    kr_swap = _pad_cols(_rot_half_cols(parts["kr"]), NOPE, HEAD_PAD)
    w_in_p = jnp.concatenate([parts["u"], parts["v"], parts["cq"], parts["ckv"], kr_main, kr_swap,
                              parts["ga"], parts["gb"]], axis=1).astype(BF16)

    wq = w_uq.reshape(Q_RANK, HEADS, NOPE + ROPE)
    zq = jnp.zeros((Q_RANK, HEADS, HEAD_PAD - NOPE - ROPE), w_uq.dtype)
    wq_main = jnp.concatenate([wq, zq], axis=-1).reshape(Q_RANK, HEADS * HEAD_PAD).astype(BF16)
    wq_swap = jnp.concatenate([jnp.zeros((Q_RANK, HEADS, NOPE), w_uq.dtype),
                               _rot_half_cols(wq[..., NOPE:]), zq], axis=-1)
    wq_swap = wq_swap.reshape(Q_RANK, HEADS * HEAD_PAD).astype(BF16)

    wkv = w_ukv.reshape(KV_RANK, HEADS, NOPE + VDIM)
    zk = jnp.zeros((KV_RANK, HEADS, HEAD_PAD - NOPE), w_ukv.dtype)
    wk = jnp.concatenate([wkv[..., :NOPE], zk], axis=-1).reshape(KV_RANK, HEADS * HEAD_PAD).astype(BF16)
    zv = jnp.zeros((KV_RANK, HEADS, VT_ROWS - VDIM), w_ukv.dtype)
    wv = jnp.concatenate([wkv[..., NOPE:], zv], axis=-1).reshape(KV_RANK, HEADS * VT_ROWS).T.astype(BF16)

    return w_in_p, wq_main, wq_swap, wk, wv, w_branch_b.astype(BF16)


def _expert_layout(cnts, nt, nblk):
    tb = EXPERT_ROWS
    c = cnts.reshape(nt, N_EXPERTS)
    total = jnp.sum(c, axis=0)
    per = jnp.where(total > 0, (total + REGION_SLACK + tb - 1) // tb, 0)
    stop = jnp.cumsum(per)
    start = stop - per
    offs = start[None, :] * tb + (jnp.cumsum(c, axis=0) - c)
    nval = stop[-1]
    blk = jnp.minimum(jnp.arange(nblk, dtype=jnp.int32), nval - 1)
    bexp = jnp.minimum(jnp.sum(blk[:, None] >= stop[None, :], axis=1), N_EXPERTS - 1)
    fill = jnp.concatenate([jnp.where(per > 0, stop - 1, -1), jnp.where(per > 1, stop - 2, -1)])
    i32 = lambda v: v.astype(jnp.int32)
    return i32(offs.reshape(-1)), i32(bexp), i32(nval.reshape(1)), i32(fill)


def kernel(x, c, positions, ada_w, ada_b, norm_mix_pre, norm_mix_post, norm_ffn_pre, norm_ffn_post, w_in, gmlp_ln_g, gmlp_ln_b, gmlp_ws, gmlp_bs, mla_q_norm, mla_w_uq, mla_kv_norm, mla_w_ukv, w_branch_a, w_branch_b, w_out, ffn_w_gate, ffn_w_up, ffn_w_down, moe_router, moe_router_bias, moe_w1, moe_w3, moe_w2):
    batch, s, d = x.shape
    assert batch == 1 and s % ROW_TILE == 0 and s % ATTN_TILE == 0
    depth = ada_w.shape[0]
    xs = x.reshape(s, d)
    mod = _ada_mod(c, ada_w, ada_b)
    cos_t, sin_t = _rope_tables(positions)
    vone = jnp.zeros((HEADS, VT_ROWS), F32).at[:, VDIM].set(1.0).reshape(HEADS * VT_ROWS, 1)
    row = lambda v: v.reshape(1, -1)

    for l in range(depth):
        sh_m, sc_m, g_m, sh_f, sc_f, g_f = [mod[l, :, k * d:(k + 1) * d] for k in range(6)]
        w_in_p, wq_main, wq_swap, wk, wv, wb = _mixer_weights(w_in[l], mla_w_uq[l], mla_w_ukv[l], w_branch_b[l])
        bias = jnp.broadcast_to(gmlp_bs[l].T[:, :, None], (CHUNK, GMLP_GROUPS, GMLP_GROUP_DIM))
        bias = bias.reshape(CHUNK, GMLP_WIDTH)
        a, gb, q, k, vt = _mixer_front(xs, row(norm_mix_pre[l]), sc_m, sh_m, w_in_p,
                                       row(gmlp_ln_g[l]), row(gmlp_ln_b[l]), gmlp_ws[l], bias,
                                       w_branch_a[l].astype(BF16), cos_t, sin_t,
                                       row(mla_q_norm[l]), row(mla_kv_norm[l]), wq_main, wq_swap, wk, wv, vone)
        o = _attention(q, k, vt)
        merge_args = (a, gb, o, xs, wb, w_out[l].astype(BF16), row(norm_mix_post[l]), g_m)

        j = l // 2
        if l % 2 == 0:
            ff = ffn_w_gate.shape[2]
            ff_pad = -(-ff // (2 * LANES)) * (2 * LANES)
            wg = jnp.pad(ffn_w_gate[j], ((0, 0), (0, ff_pad - ff))).astype(BF16)
            wu = jnp.pad(ffn_w_up[j], ((0, 0), (0, ff_pad - ff))).astype(BF16)
            wd = jnp.pad(ffn_w_down[j], ((0, ff_pad - ff), (0, 0))).astype(BF16)
            xs = _dense_ffn(merge_args, row(norm_ffn_pre[l]), sc_f, sh_f, wg, wu, wd, row(norm_ffn_post[l]), g_f)
        else:
            nt = s // ROW_TILE
            tb = EXPERT_ROWS
            max_rows = 2 * s + nt * N_EXPERTS * (BF16_ROWS - 1) + N_EXPERTS * (REGION_SLACK + tb - 1)
            nblk = -(-max_rows // tb) + 1
            wr = jnp.pad(moe_router[j], ((0, 0), (0, LANES - N_EXPERTS)))
            br = jnp.pad(moe_router_bias[j], (0, LANES - N_EXPERTS), constant_values=NEG).reshape(1, LANES)
            xs, hb, info, infot, cnts = _route(merge_args, row(norm_ffn_pre[l]), sc_f, sh_f, wr, br)
            offs, bexp, nval, fill = _expert_layout(cnts, nt, nblk)
            xsort = _dispatch(offs, cnts, fill, nval, hb, infot, nblk)
            ysort = _experts(xsort, moe_w1[j].astype(BF16), moe_w3[j].astype(BF16), moe_w2[j].astype(BF16),
                             bexp, nval, tf=moe_w1.shape[3] // 2)
            xs = _combine(offs, cnts, info, xs, row(norm_ffn_post[l]), g_f, ysort)
    return xs.reshape(batch, s, d)
```

```python
import functools

import jax
import jax.numpy as jnp
import numpy as np
from jax import lax
from jax.experimental import pallas as pl
from jax.experimental.pallas import tpu as pltpu

F32 = jnp.float32
BF16 = jnp.bfloat16

EPS = 1e-6
LANES = 128
BF16_ROWS = 16
GMLP_GROUPS = 8
GMLP_GROUP_DIM = 64
GMLP_WIDTH = GMLP_GROUPS * GMLP_GROUP_DIM
CHUNK = 128
HEADS = 8
NOPE = 64
ROPE = 32
VDIM = 64
HEAD_PAD = 128
VT_ROWS = 80
Q_RANK = 384
KV_RANK = 256
ROPE_THETA = 10000.0
N_EXPERTS = 8
NEG = -1e30

ROW_TILE = 512
ATTN_TILE = 512
ATTN_HEADS = 2
STAT_ROWS = 8
EXPERT_ROWS = 512
DISPATCH_ROWS = 128
COMBINE_ROWS = 256
FIRST_ROWS = 256
ROUTE_FIELDS = 8
REGION_SLACK = DISPATCH_ROWS - BF16_ROWS
VMEM_LIMIT = 52 * 1024 * 1024


def _params(*sem):
    return pltpu.CompilerParams(dimension_semantics=sem, vmem_limit_bytes=VMEM_LIMIT)


def _dot(a, b):
    return jnp.dot(a, b, preferred_element_type=F32)


def _rms(x, g):
    return x * lax.rsqrt(jnp.mean(x * x, axis=-1, keepdims=True) + EPS) * g


def _gelu(x):
    return 0.5 * x * (1.0 + lax.erf(x * np.float32(0.7071067811865476)))


def _full(shape):
    return pl.BlockSpec(shape, lambda *_: (0,) * len(shape), pipeline_mode=pl.Buffered(1))


def _mod_body(c_ref, w_ref, b_ref, o_ref):
    c = c_ref[...]
    ca = c * jax.nn.sigmoid(c)
    o_ref[0] = jnp.sum(ca * w_ref[0], axis=0, keepdims=True) + b_ref[0]


def _ada_mod(c, ada_w, ada_b):
    n_layers, d, n = ada_w.shape
    tn = n // 4
    return pl.pallas_call(
        _mod_body,
        grid=(n_layers, n // tn),
        in_specs=[pl.BlockSpec((d, 1), lambda l, j: (0, 0)),
                  pl.BlockSpec((1, d, tn), lambda l, j: (l, 0, j)),
                  pl.BlockSpec((1, 1, tn), lambda l, j: (l, 0, j))],
        out_specs=pl.BlockSpec((1, 1, tn), lambda l, j: (l, 0, j)),
        out_shape=jax.ShapeDtypeStruct((n_layers, 1, n), F32),
        compiler_params=_params("arbitrary", "arbitrary"),
        name="ada_mod",
    )(c.reshape(d, 1), ada_w, ada_b.reshape(n_layers, 1, n))


def _rope_body(pos_ref, invf_ref, cos_ref, sin_ref):
    ang = pos_ref[...].astype(F32) * invf_ref[...]
    cos_ref[...] = jnp.cos(ang)
    sin_ref[...] = jnp.sin(ang)


def _rope_tables(positions):
    s = positions.shape[-1]
    half = ROPE // 2
    per_row = LANES // half
    inv_freq = 1.0 / (ROPE_THETA ** (jnp.arange(0, ROPE, 2, dtype=F32) / ROPE))
    pos_dense = jnp.broadcast_to(positions.reshape(s, 1), (s, half)).reshape(s // per_row, LANES)
    invf = jnp.tile(inv_freq, per_row).reshape(1, LANES)
    rows = s // per_row
    tr = rows // 4
    cos_d, sin_d = pl.pallas_call(
        _rope_body,
        grid=(rows // tr,),
        in_specs=[pl.BlockSpec((tr, LANES), lambda i: (i, 0)), _full((1, LANES))],
        out_specs=[pl.BlockSpec((tr, LANES), lambda i: (i, 0))] * 2,
        out_shape=[jax.ShapeDtypeStruct((rows, LANES), F32)] * 2,
        compiler_params=_params("arbitrary"),
        name="rope_tables",
    )(pos_dense, invf)
    cos = cos_d.reshape(s, half)
    sin = sin_d.reshape(s, half)
    ones = jnp.ones((s, NOPE), F32)
    zeros = jnp.zeros((s, NOPE), F32)
    pad = jnp.zeros((s, HEAD_PAD - NOPE - ROPE), F32)
    cos_t = jnp.concatenate([ones, cos, cos, pad], axis=1)
    sin_t = jnp.concatenate([zeros, sin, sin, pad], axis=1)
    return cos_t, sin_t


def _gmlp_branch(u, v, ga, lng_ref, lnb_ref, ws_ref, bias_ref, wa_ref):
    tm = u.shape[0]
    gu = _gelu(u)
    gv = _gelu(v)
    mu = jnp.mean(gv, axis=-1, keepdims=True)
    xc = gv - mu
    vn = xc * lax.rsqrt(jnp.mean(xc * xc, axis=-1, keepdims=True) + EPS) * lng_ref[...] + lnb_ref[...]
    vb = vn.astype(BF16)
    t_idx = lax.broadcasted_iota(jnp.int32, (CHUNK, CHUNK), 0)
    s_idx = lax.broadcasted_iota(jnp.int32, (CHUNK, CHUNK), 1)
    causal = s_idx <= t_idx
    ws = [jnp.where(causal, ws_ref[g], 0.0).astype(BF16) for g in range(GMLP_GROUPS)]
    left = lax.broadcasted_iota(jnp.int32, (CHUNK, LANES), 1) < GMLP_GROUP_DIM
    bias = bias_ref[...]
    z_rows = []
    for c in range(tm // CHUNK):
        vc = vb[c * CHUNK:(c + 1) * CHUNK]
        z_cols = []
        for j in range(GMLP_WIDTH // LANES):
            vp = vc[:, j * LANES:(j + 1) * LANES]
            z_cols.append(jnp.where(left, _dot(ws[2 * j], vp), _dot(ws[2 * j + 1], vp)))
        z_rows.append(jnp.concatenate(z_cols, axis=1) + bias)
    z = jnp.concatenate(z_rows, axis=0)
    gated = (gu * z).astype(BF16)
    return jax.nn.sigmoid(ga) * _dot(gated, wa_ref[...])


def _mla_qkv(cq, ckv, kr, cos_ref, sin_ref, qn_ref, kvn_ref, wqm_ref, wqs_ref, wk_ref, wvt_ref, vone_ref,
             q_ref, k_ref, vt_ref, scale):
    cos = cos_ref[...]
    sin = sin_ref[...]
    cqn = _rms(cq, qn_ref[...]).astype(BF16)
    qm = _dot(cqn, wqm_ref[...])
    qs = _dot(cqn, wqs_ref[...])
    ckn = _rms(ckv, kvn_ref[...]).astype(BF16)
    km = _dot(ckn, wk_ref[...])
    kpe = kr[:, :HEAD_PAD] * cos + kr[:, HEAD_PAD:] * sin
    for h in range(HEADS):
        sl = slice(h * HEAD_PAD, (h + 1) * HEAD_PAD)
        q_ref[:, sl] = ((qm[:, sl] * cos + qs[:, sl] * sin) * scale).astype(q_ref.dtype)
        k_ref[:, sl] = (km[:, sl] + kpe).astype(k_ref.dtype)
    vt = lax.dot_general(wvt_ref[...], ckn, (((1,), (1,)), ((), ())), preferred_element_type=F32)
    vt_ref[...] = (vt + vone_ref[...]).astype(vt_ref.dtype)


def _front_body(x_ref, g_ref, sc_ref, sh_ref, w_ref, lng_ref, lnb_ref, ws_ref, bias_ref, wa_ref,
                cos_ref, sin_ref, qn_ref, kvn_ref, wqm_ref, wqs_ref, wk_ref, wvt_ref, vone_ref,
                a_ref, gb_ref, q_ref, k_ref, vt_ref, *, scale):
    h = _rms(x_ref[...], g_ref[...]) * (1.0 + sc_ref[...]) + sh_ref[...]
    proj = _dot(h.astype(BF16), w_ref[...])
    d = x_ref.shape[1]
    cuts = np.cumsum([0, GMLP_WIDTH, GMLP_WIDTH, Q_RANK, KV_RANK, 2 * HEAD_PAD, d, d])
    u, v, cq, ckv, kr, ga, gb = [proj[:, lo:hi] for lo, hi in zip(cuts[:-1], cuts[1:])]
    gb_ref[...] = gb.astype(gb_ref.dtype)
    a_ref[...] = _gmlp_branch(u, v, ga, lng_ref, lnb_ref, ws_ref, bias_ref, wa_ref).astype(a_ref.dtype)
    _mla_qkv(cq, ckv, kr, cos_ref, sin_ref, qn_ref, kvn_ref, wqm_ref, wqs_ref, wk_ref, wvt_ref, vone_ref,
             q_ref, k_ref, vt_ref, scale)


def _mixer_front(x, g, sc, sh, w, lng, lnb, ws, bias, wa, cos_t, sin_t, qn, kvn, wqm, wqs, wk, wvt, vone):
    s, d = x.shape
    tm = ROW_TILE
    width = HEADS * HEAD_PAD
    row = lambda n: pl.BlockSpec((tm, n), lambda i: (i, 0))
    scale = float((NOPE + ROPE) ** -0.5 * np.log2(np.e))
    consts = (g, sc, sh, w, lng, lnb, ws, bias, wa)
    mla_consts = (qn, kvn, wqm, wqs, wk, wvt, vone)
    return pl.pallas_call(
        functools.partial(_front_body, scale=scale),
        grid=(s // tm,),
        in_specs=[row(d)] + [_full(c.shape) for c in consts] + [row(HEAD_PAD), row(HEAD_PAD)]
                 + [_full(c.shape) for c in mla_consts],
        out_specs=[row(d), row(d), row(width), row(width), pl.BlockSpec((wvt.shape[0], tm), lambda i: (0, i))],
        out_shape=[jax.ShapeDtypeStruct((s, d), BF16), jax.ShapeDtypeStruct((s, d), BF16),
                   jax.ShapeDtypeStruct((s, width), BF16), jax.ShapeDtypeStruct((s, width), BF16),
                   jax.ShapeDtypeStruct((wvt.shape[0], s), BF16)],
        compiler_params=_params("arbitrary"),
        name="mixer_front",
    )(x, *consts, cos_t, sin_t, *mla_consts)


def _attn_body(q_ref, k_ref, vt_ref, o_ref, s_ref, mx_ref, m_ref, acc_ref):
    t = o_ref.shape[0]
    heads = q_ref.shape[1] // HEAD_PAD
    qi = pl.program_id(1)
    m_ref[...] = jnp.full(m_ref.shape, NEG, F32)
    acc_ref[...] = jnp.zeros(acc_ref.shape, F32)

    def scores(blk, slot, q_tile):
        off = pl.multiple_of(blk * t, t)
        q_off = pl.multiple_of(q_tile * t, t)
        for h in range(heads):
            sl = slice(h * HEAD_PAD, (h + 1) * HEAD_PAD)
            s = lax.dot_general(k_ref[pl.ds(off, t), sl], q_ref[pl.ds(q_off, t), sl],
                                (((1,), (1,)), ((), ())), preferred_element_type=F32)
            s_ref[slot, h, :, 0:t] = s
            mx_ref[slot, h] = jnp.broadcast_to(jnp.max(s, axis=0, keepdims=True), (STAT_ROWS, t))

    def consume(blk, slot, mask):
        off = pl.multiple_of(blk * t, t)
        for h in range(heads):
            sl = slice(h * HEAD_PAD, (h + 1) * HEAD_PAD)
            s = s_ref[slot, h, :, 0:t]
            if mask is None:
                mx = mx_ref[slot, h]
            else:
                s = jnp.where(mask, s, NEG)
                mx = jnp.max(s, axis=0, keepdims=True)
            m_prev = m_ref[h]
            m_new = jnp.maximum(m_prev, mx)
            alpha = jnp.exp2(m_prev - m_new)
            pv = alpha[0:1] * acc_ref[h]
            half = t // 2
            for part in range(2):
                p = jnp.exp2(s[part * half:(part + 1) * half] - m_new[0:1]).astype(BF16)
                keys = pl.ds(pl.multiple_of(off + part * half, half), half)
                pv = pv + _dot(vt_ref[h * VT_ROWS:(h + 1) * VT_ROWS, keys], p)
            acc_ref[h] = pv
            m_ref[h] = m_new

    @pl.when(qi == 0)
    def _():
        scores(0, 0, qi)

    def run(blk, n):
        for u in range(n):
            scores(blk + u + 1, (u + 1) % 2, qi)
            consume(blk + u, u % 2, None)

    def octet(j, carry):
        run(8 * j, 8)
        return carry

    lax.fori_loop(0, qi // 8, octet, 0)
    done = (qi // 8) * 8
    for n in (4, 2):
        more = qi - done >= n

        @pl.when(more)
        def _():
            run(done, n)

        done = done + jnp.where(more, n, 0)

    row = lax.broadcasted_iota(jnp.int32, (t, t), 0)
    col = lax.broadcasted_iota(jnp.int32, (t, t), 1)
    causal = row <= col
    odd = lax.rem(qi, 2) == 1

    @pl.when(odd)
    def _():
        scores(qi, 1, qi)
        consume(qi - 1, 0, None)
        consume(qi, 1, causal)

    @pl.when(jnp.logical_not(odd))
    def _():
        consume(qi, 0, causal)

    def finalize():
        lower = lax.broadcasted_iota(jnp.int32, (t, HEAD_PAD), 1) < VDIM
        for j in range(heads // 2):
            outs = []
            for h in (2 * j, 2 * j + 1):
                acc = acc_ref[h]
                out_t = jnp.concatenate([acc / acc[VDIM:VDIM + 1], jnp.zeros((HEAD_PAD - VT_ROWS, t), F32)],
                                        axis=0)
                outs.append(out_t.T)
            packed = jnp.where(lower, outs[0], pltpu.roll(outs[1], VDIM, 1))
            o_ref[:, j * HEAD_PAD:(j + 1) * HEAD_PAD] = packed.astype(o_ref.dtype)

    last = qi == pl.num_programs(1) - 1

    @pl.when(jnp.logical_not(last))
    def _():
        scores(0, 0, qi + 1)
        finalize()

    @pl.when(last)
    def _():
        finalize()


def _attention(q, k, vt):
    s, width = q.shape
    t = ATTN_TILE
    gw = ATTN_HEADS * HEAD_PAD
    return pl.pallas_call(
        _attn_body,
        grid=(width // gw, s // t),
        in_specs=[pl.BlockSpec((s, gw), lambda h, i: (0, h)),
                  pl.BlockSpec((s, gw), lambda h, i: (0, h)),
                  pl.BlockSpec((ATTN_HEADS * VT_ROWS, s), lambda h, i: (h, 0))],
        out_specs=pl.BlockSpec((t, ATTN_HEADS * VDIM), lambda h, i: (i, h)),
        out_shape=jax.ShapeDtypeStruct((s, HEADS * VDIM), BF16),
        scratch_shapes=[pltpu.VMEM((2, ATTN_HEADS, t, t + LANES), F32),
                        pltpu.VMEM((2, ATTN_HEADS, STAT_ROWS, t), F32),
                        pltpu.VMEM((ATTN_HEADS, STAT_ROWS, t), F32),
                        pltpu.VMEM((ATTN_HEADS, VT_ROWS, t), F32)],
        compiler_params=_params("arbitrary", "arbitrary"),
        name="mla_attention",
    )(q, k, vt)


MERGE_ARGS = 8


def _merge_math(a_ref, gb_ref, o_ref, x_ref, wb_ref, wo_ref, gp_ref, gm_ref):
    yb = _dot(o_ref[...], wb_ref[...])
    merged = a_ref[...].astype(F32) + jax.nn.sigmoid(gb_ref[...].astype(F32)) * yb
    y = _dot(merged.astype(BF16), wo_ref[...])
    return x_ref[...] + gm_ref[...] * _rms(y, gp_ref[...])


def _merge_specs(a, gb, o, x, wb, wo, gp, gm):
    tm = ROW_TILE
    d = x.shape[1]
    row = pl.BlockSpec((tm, d), lambda i: (i, 0))
    return [row, row, pl.BlockSpec((tm, o.shape[1]), lambda i: (i, 0)), row,
            _full(wb.shape), _full(wo.shape), _full((1, d)), _full((1, d))]


def _ffn_body(*refs):
    g_ref, sc_ref, sh_ref, wg_ref, wu_ref, wd_ref, gp_ref, gf_ref, o_ref = refs[MERGE_ARGS:]
    x = _merge_math(*refs[:MERGE_ARGS])
    hb = (_rms(x, g_ref[...]) * (1.0 + sc_ref[...]) + sh_ref[...]).astype(BF16)
    a = _dot(hb, wg_ref[...])
    b = _dot(hb, wu_ref[...])
    y = _dot((a * jax.nn.sigmoid(a) * b).astype(BF16), wd_ref[...])
    o_ref[...] = x + gf_ref[...] * _rms(y, gp_ref[...])


def _dense_ffn(merge_args, g, sc, sh, wg, wu, wd, gp, gf):
    s, d = merge_args[3].shape
    tm = ROW_TILE
    row = pl.BlockSpec((tm, d), lambda i: (i, 0))
    vec = _full((1, d))
    return pl.pallas_call(
        _ffn_body,
        grid=(s // tm,),
        in_specs=_merge_specs(*merge_args)
                 + [vec, vec, vec, _full(wg.shape), _full(wu.shape), _full(wd.shape), vec, vec],
        out_specs=row,
        out_shape=jax.ShapeDtypeStruct((s, d), F32),
        compiler_params=_params("arbitrary"),
        name="merge_dense_ffn",
    )(*merge_args, g, sc, sh, wg, wu, wd, gp, gf)


def _route_body(*refs):
    g_ref, sc_ref, sh_ref, wr_ref, br_ref, x_ref, hb_ref, info_ref, infot_ref, cnt_ref = refs[MERGE_ARGS:]
    i = pl.program_id(0)
    tm = x_ref.shape[0]
    x = _merge_math(*refs[:MERGE_ARGS])
    x_ref[...] = x
    h = _rms(x, g_ref[...]) * (1.0 + sc_ref[...]) + sh_ref[...]
    hb = h.astype(BF16)
    hb_ref[...] = hb
    h_lo = (h - hb.astype(F32)).astype(BF16)
    w = wr_ref[...]
    w_hi = w.astype(BF16)
    w_lo = (w - w_hi.astype(F32)).astype(BF16)
    logits = _dot(hb, w_hi) + (_dot(h_lo, w_hi) + _dot(hb, w_lo)) + br_ref[...]

    lane = lax.broadcasted_iota(jnp.int32, (tm, LANES), 1)
    m1 = jnp.max(logits, axis=-1, keepdims=True)
    i1 = jnp.min(jnp.where(logits == m1, lane, LANES), axis=-1, keepdims=True)
    oh1 = lane == i1
    rest = jnp.where(oh1, -3e38, logits)
    m2 = jnp.max(rest, axis=-1, keepdims=True)
    i2 = jnp.min(jnp.where(rest == m2, lane, LANES), axis=-1, keepdims=True)
    oh2 = lane == i2
    ex = jnp.exp(m2 - m1)
    g1 = 1.0 / (1.0 + ex)
    g2 = ex / (1.0 + ex)

    ohf = jnp.where(oh1 | oh2, 1.0, 0.0)
    r_idx = lax.broadcasted_iota(jnp.int32, (tm, tm), 0)
    c_idx = lax.broadcasted_iota(jnp.int32, (tm, tm), 1)
    earlier = jnp.where(c_idx < r_idx, 1.0, 0.0).astype(BF16)
    rank = _dot(earlier, ohf.astype(BF16))
    rank1 = jnp.sum(jnp.where(oh1, rank, 0.0), axis=-1, keepdims=True)
    rank2 = jnp.sum(jnp.where(oh2, rank, 0.0), axis=-1, keepdims=True)
    info = jnp.where(lane == 0, i1.astype(F32),
           jnp.where(lane == 1, i2.astype(F32),
           jnp.where(lane == 2, g1,
           jnp.where(lane == 3, g2,
           jnp.where(lane == 4, rank1,
           jnp.where(lane == 5, rank2, 0.0))))))
    info_ref[...] = info
    infot_ref[...] = info.T[:ROUTE_FIELDS]
    cnt = jnp.sum(ohf, axis=0, keepdims=True).astype(jnp.int32)
    cnt_al = ((cnt + (BF16_ROWS - 1)) // BF16_ROWS) * BF16_ROWS
    for e in range(N_EXPERTS):
        cnt_ref[i * N_EXPERTS + e] = cnt_al[0, e]


def _route(merge_args, g, sc, sh, wr, br):
    s, d = merge_args[3].shape
    tm = ROW_TILE
    nt = s // tm
    row = pl.BlockSpec((tm, d), lambda i: (i, 0))
    return pl.pallas_call(
        _route_body,
        grid=(nt,),
        in_specs=_merge_specs(*merge_args)
                 + [_full((1, d)), _full((1, d)), _full((1, d)), _full(wr.shape), _full(br.shape)],
        out_specs=[row, row,
                   pl.BlockSpec((tm, LANES), lambda i: (i, 0)),
                   pl.BlockSpec((ROUTE_FIELDS, tm), lambda i: (0, i)),
                   pl.BlockSpec(memory_space=pltpu.SMEM)],
        out_shape=[jax.ShapeDtypeStruct((s, d), F32),
                   jax.ShapeDtypeStruct((s, d), BF16),
                   jax.ShapeDtypeStruct((s, LANES), F32),
                   jax.ShapeDtypeStruct((ROUTE_FIELDS, s), F32),
                   jax.ShapeDtypeStruct((nt * N_EXPERTS,), jnp.int32)],
        compiler_params=_params("arbitrary"),
        name="merge_moe_route",
    )(*merge_args, g, sc, sh, wr, br)


def _routing(info):
    return [(info[:, k:k + 1], info[:, 2 + k:3 + k], info[:, 4 + k:5 + k]) for k in range(2)]


def _dispatch_body(offs_ref, cnts_ref, fill_ref, nval_ref, hb_ref, infot_ref, xs_ref, xbuf, zbuf, sem, zsem, *, nblk):
    i = pl.program_id(0)
    tm = hb_ref.shape[0]
    tb = zbuf.shape[0]

    def zero_block(blk):
        cp = pltpu.make_async_copy(zbuf, xs_ref.at[pl.ds(pl.multiple_of(blk * tb, tb), tb)], zsem)
        cp.start()
        cp.wait()

    @pl.when(i == 0)
    def _():
        zbuf[...] = jnp.zeros(zbuf.shape, zbuf.dtype)
        for n in range(fill_ref.shape[0]):
            @pl.when(fill_ref[n] >= 0)
            def _():
                zero_block(fill_ref[n])

        def unused(blk, carry):
            zero_block(blk)
            return carry

        lax.fori_loop(nval_ref[0], nblk, unused, 0)

    hb = hb_ref[...]
    fields = infot_ref[...]
    e1, e2, r1, r2 = fields[0:1], fields[1:2], fields[4:5], fields[5:6]
    chunks = tm // DISPATCH_ROWS
    buf = lax.rem(i, 2)

    def first_slot(e, r):
        return jnp.where(r < float(FIRST_ROWS), e * float(FIRST_ROWS) + r, -1.0)

    p1 = first_slot(e1, r1)
    p2 = first_slot(e2, r2)
    slot = lax.broadcasted_iota(jnp.int32, (N_EXPERTS * FIRST_ROWS, tm), 0).astype(F32)
    sel = jnp.where((p1 == slot) | (p2 == slot), 1.0, 0.0).astype(BF16)
    xbuf[buf, 0] = _dot(sel, hb).astype(BF16)

    first_chunks = FIRST_ROWS // DISPATCH_ROWS

    def buf_rows(e, c):
        return c // first_chunks, e * FIRST_ROWS + (c % first_chunks) * DISPATCH_ROWS

    for e in range(N_EXPERTS):
        for c in range(first_chunks, chunks):
            @pl.when(cnts_ref[i * N_EXPERTS + e] > c * DISPATCH_ROWS)
            def _():
                first = e1 == float(e)
                rank = jnp.where(first, r1, r2)
                late = (lax.broadcasted_iota(jnp.int32, (DISPATCH_ROWS, tm), 0) + c * DISPATCH_ROWS).astype(F32)
                hit = (rank == late) & (first | (e2 == float(e)))
                part, row0 = buf_rows(e, c)
                xbuf[buf, part, row0:row0 + DISPATCH_ROWS] = _dot(jnp.where(hit, 1.0, 0.0).astype(BF16),
                                                                  hb).astype(BF16)

    def copy(b, e, c, dst):
        part, row0 = buf_rows(e, c)
        return pltpu.make_async_copy(xbuf.at[b, part, pl.ds(row0, DISPATCH_ROWS)],
                                     xs_ref.at[pl.ds(dst, DISPATCH_ROWS)], sem.at[b, e, c])

    def for_each_copy(step, fn):
        for e in range(N_EXPERTS):
            for c in range(chunks):
                @pl.when(cnts_ref[step * N_EXPERTS + e] > c * DISPATCH_ROWS)
                def _():
                    fn(e, c)

    @pl.when(i > 0)
    def _():
        for_each_copy(i - 1, lambda e, c: copy(1 - buf, e, c, 0).wait())

    def start(e, c):
        off = offs_ref[i * N_EXPERTS + e]
        copy(buf, e, c, pl.multiple_of(off + c * DISPATCH_ROWS, BF16_ROWS)).start()

    for_each_copy(i, start)

    @pl.when(i == pl.num_programs(0) - 1)
    def _():
        for_each_copy(i, lambda e, c: copy(buf, e, c, 0).wait())


def _dispatch(offs, cnts, fill, nval, hb, infot, nblk):
    s, d = hb.shape
    tm = ROW_TILE
    tb = EXPERT_ROWS
    grid_spec = pltpu.PrefetchScalarGridSpec(
        num_scalar_prefetch=4,
        grid=(s // tm,),
        in_specs=[pl.BlockSpec((tm, d), lambda i, *_: (i, 0)),
                  pl.BlockSpec((ROUTE_FIELDS, tm), lambda i, *_: (0, i))],
        out_specs=pl.BlockSpec(memory_space=pl.ANY),
        scratch_shapes=[pltpu.VMEM((2, tm // FIRST_ROWS, N_EXPERTS * FIRST_ROWS, d), BF16),
                        pltpu.VMEM((tb, d), BF16),
                        pltpu.SemaphoreType.DMA((2, N_EXPERTS, tm // DISPATCH_ROWS)),
                        pltpu.SemaphoreType.DMA(())],
    )
    return pl.pallas_call(
        functools.partial(_dispatch_body, nblk=nblk),
        grid_spec=grid_spec,
        out_shape=jax.ShapeDtypeStruct((nblk * tb, d), BF16),
        compiler_params=_params("arbitrary"),
        name="moe_dispatch",
    )(offs, cnts, fill, nval, hb, infot)


def _expert_body(bexp_ref, nval_ref, x_ref, w1_ref, w3_ref, w2_ref, o_ref, acc_ref, *, nff):
    i = pl.program_id(0)
    f = pl.program_id(1)
    used = i < nval_ref[0]

    def ff_slice(first, final):
        xb = x_ref[...]
        a = _dot(xb, w1_ref[0])
        b = _dot(xb, w3_ref[0])
        acc = _dot((a * jax.nn.sigmoid(a) * b).astype(BF16), w2_ref[0])
        if not first:
            acc = acc_ref[...] + acc
        if final:
            o_ref[...] = acc.astype(o_ref.dtype)
        else:
            acc_ref[...] = acc

    for step in sorted({0, min(1, nff - 1), nff - 1}):
        pick = f == step if step in (0, nff - 1) else (f > 0) & (f < nff - 1)

        @pl.when(used & pick)
        def _():
            ff_slice(step == 0, step == nff - 1)

    @pl.when(jnp.logical_not(used) & (f == nff - 1))
    def _():
        o_ref[...] = jnp.zeros(o_ref.shape, o_ref.dtype)


def _experts(xs, w1, w3, w2, bexp, nval, tf):
    rows, d = xs.shape
    ff = w1.shape[2]
    tb = EXPERT_ROWS
    nff = ff // tf

    def xrow(i, f, be, nv):
        return (jnp.minimum(i, nv[0] - 1), 0)

    def fcol(i, f, nv):
        return jnp.where(i < nv[0], f, nff - 1)

    grid_spec = pltpu.PrefetchScalarGridSpec(
        num_scalar_prefetch=2,
        grid=(rows // tb, nff),
        in_specs=[pl.BlockSpec((tb, d), xrow),
                  pl.BlockSpec((1, d, tf), lambda i, f, be, nv: (be[i], 0, fcol(i, f, nv))),
                  pl.BlockSpec((1, d, tf), lambda i, f, be, nv: (be[i], 0, fcol(i, f, nv))),
                  pl.BlockSpec((1, tf, d), lambda i, f, be, nv: (be[i], fcol(i, f, nv), 0))],
        out_specs=pl.BlockSpec((tb, d), lambda i, f, be, nv: (i, 0)),
        scratch_shapes=[pltpu.VMEM((tb, d), F32)],
    )
    return pl.pallas_call(
        functools.partial(_expert_body, nff=nff),
        grid_spec=grid_spec,
        out_shape=jax.ShapeDtypeStruct((rows, d), BF16),
        compiler_params=_params("arbitrary", "arbitrary"),
        name="moe_experts",
    )(bexp, nval, xs, w1, w3, w2)


def _combine_body(offs_ref, cnts_ref, info_ref, x_ref, gp_ref, gf_ref, ys_ref, o_ref, ybuf, acc_ref, sem):
    i = pl.program_id(0)
    nt = pl.num_programs(0)
    tm = x_ref.shape[0]
    chunks = tm // COMBINE_ROWS
    buf = lax.rem(i, 2)

    def copy(b, e, c, src):
        dst = ybuf.at[b, pl.ds((c * N_EXPERTS + e) * COMBINE_ROWS, COMBINE_ROWS)]
        return pltpu.make_async_copy(ys_ref.at[pl.ds(src, COMBINE_ROWS)], dst, sem.at[b, e, c])

    def fetch(step, b):
        for e in range(N_EXPERTS):
            off = offs_ref[step * N_EXPERTS + e]
            copy(b, e, 0, pl.multiple_of(off, BF16_ROWS)).start()
            for c in range(1, chunks):
                @pl.when(cnts_ref[step * N_EXPERTS + e] > c * COMBINE_ROWS)
                def _():
                    copy(b, e, c, pl.multiple_of(off + c * COMBINE_ROWS, BF16_ROWS)).start()

    @pl.when(i == 0)
    def _():
        fetch(0, 0)

    @pl.when(i + 1 < nt)
    def _():
        fetch(i + 1, 1 - buf)

    (e1, g1, r1), (e2, g2, r2) = _routing(info_ref[...])
    for e in range(N_EXPERTS):
        copy(buf, e, 0, 0).wait()
    stack = N_EXPERTS * COMBINE_ROWS
    slot = lax.broadcasted_iota(jnp.int32, (tm, stack), 1).astype(F32)
    y_first = ybuf[buf, 0:stack]
    acc = None
    for ek, gk, rk in ((e1, g1, r1), (e2, g2, r2)):
        pos = jnp.where(rk < float(COMBINE_ROWS), ek * float(COMBINE_ROWS) + rk, -1.0)
        term = gk * _dot(jnp.where(pos == slot, 1.0, 0.0).astype(BF16), y_first)
        acc = term if acc is None else acc + term
    acc_ref[...] = acc

    late = lax.broadcasted_iota(jnp.int32, (tm, COMBINE_ROWS), 1).astype(F32)
    for e in range(N_EXPERTS):
        for c in range(1, chunks):
            @pl.when(cnts_ref[i * N_EXPERTS + e] > c * COMBINE_ROWS)
            def _():
                first = e1 == float(e)
                second = e2 == float(e)
                rank = jnp.where(first, r1, r2)
                gate = jnp.where(first, g1, jnp.where(second, g2, 0.0))
                copy(buf, e, c, 0).wait()
                hit = (rank == late + float(c * COMBINE_ROWS)) & (first | second)
                rows = ybuf[buf, pl.ds((c * N_EXPERTS + e) * COMBINE_ROWS, COMBINE_ROWS)]
                acc_ref[...] += gate * _dot(jnp.where(hit, 1.0, 0.0).astype(BF16), rows)

    o_ref[...] = x_ref[...] + gf_ref[...] * _rms(acc_ref[...], gp_ref[...])


def _combine(offs, cnts, info, x, gp, gf, ys):
    s, d = x.shape
    tm = ROW_TILE
    chunks = tm // COMBINE_ROWS
    grid_spec = pltpu.PrefetchScalarGridSpec(
        num_scalar_prefetch=2,
        grid=(s // tm,),
        in_specs=[pl.BlockSpec((tm, LANES), lambda i, *_: (i, 0)),
                  pl.BlockSpec((tm, d), lambda i, *_: (i, 0)),
                  pl.BlockSpec((1, d), lambda i, *_: (0, 0)),
                  pl.BlockSpec((1, d), lambda i, *_: (0, 0)),
                  pl.BlockSpec(memory_space=pl.ANY)],
        out_specs=pl.BlockSpec((tm, d), lambda i, *_: (i, 0)),
        scratch_shapes=[pltpu.VMEM((2, chunks * N_EXPERTS * COMBINE_ROWS, d), BF16),
                        pltpu.VMEM((tm, d), F32),
                        pltpu.SemaphoreType.DMA((2, N_EXPERTS, chunks))],
    )
    return pl.pallas_call(
        _combine_body,
        grid_spec=grid_spec,
        out_shape=jax.ShapeDtypeStruct((s, d), F32),
        compiler_params=_params("arbitrary"),
        name="moe_combine",
    )(offs, cnts, info, x, gp, gf, ys)


def _rot_half_cols(w):
    half = ROPE // 2
    return jnp.concatenate([-w[..., half:], w[..., :half]], axis=-1)


def _pad_cols(w, before, total):
    return jnp.pad(w, ((0, 0), (before, total - before - w.shape[1])))


def _mixer_weights(w_in, w_uq, w_ukv, w_branch_b):
    d = w_in.shape[0]
    o = 0
    parts = {}
    for name, n in (("u", GMLP_WIDTH), ("v", GMLP_WIDTH), ("cq", Q_RANK), ("ckv", KV_RANK),
                    ("kr", ROPE), ("ga", d), ("gb", d)):
        parts[name] = w_in[:, o:o + n]
        o += n
    kr_main = _pad_cols(parts["kr"], NOPE, HEAD_PAD)
    kr_swap = _pad_cols(_rot_half_cols(parts["kr"]), NOPE, HEAD_PAD)
    w_in_p = jnp.concatenate([parts["u"], parts["v"], parts["cq"], parts["ckv"], kr_main, kr_swap,
                              parts["ga"], parts["gb"]], axis=1).astype(BF16)

    wq = w_uq.reshape(Q_RANK, HEADS, NOPE + ROPE)
    zq = jnp.zeros((Q_RANK, HEADS, HEAD_PAD - NOPE - ROPE), w_uq.dtype)
    wq_main = jnp.concatenate([wq, zq], axis=-1).reshape(Q_RANK, HEADS * HEAD_PAD).astype(BF16)
    wq_swap = jnp.concatenate([jnp.zeros((Q_RANK, HEADS, NOPE), w_uq.dtype),
                               _rot_half_cols(wq[..., NOPE:]), zq], axis=-1)
    wq_swap = wq_swap.reshape(Q_RANK, HEADS * HEAD_PAD).astype(BF16)

    wkv = w_ukv.reshape(KV_RANK, HEADS, NOPE + VDIM)
    zk = jnp.zeros((KV_RANK, HEADS, HEAD_PAD - NOPE), w_ukv.dtype)
    wk = jnp.concatenate([wkv[..., :NOPE], zk], axis=-1).reshape(KV_RANK, HEADS * HEAD_PAD).astype(BF16)
    zv = jnp.zeros((KV_RANK, HEADS, VT_ROWS - VDIM), w_ukv.dtype)
    wv = jnp.concatenate([wkv[..., NOPE:], zv], axis=-1).reshape(KV_RANK, HEADS * VT_ROWS).T.astype(BF16)

    return w_in_p, wq_main, wq_swap, wk, wv, w_branch_b.astype(BF16)


def _expert_layout(cnts, nt, nblk):
    tb = EXPERT_ROWS
    c = cnts.reshape(nt, N_EXPERTS)
    total = jnp.sum(c, axis=0)
    per = jnp.where(total > 0, (total + REGION_SLACK + tb - 1) // tb, 0)
    stop = jnp.cumsum(per)
    start = stop - per
    offs = start[None, :] * tb + (jnp.cumsum(c, axis=0) - c)
    nval = stop[-1]
    blk = jnp.minimum(jnp.arange(nblk, dtype=jnp.int32), nval - 1)
    bexp = jnp.minimum(jnp.sum(blk[:, None] >= stop[None, :], axis=1), N_EXPERTS - 1)
    fill = jnp.concatenate([jnp.where(per > 0, stop - 1, -1), jnp.where(per > 1, stop - 2, -1)])
    i32 = lambda v: v.astype(jnp.int32)
    return i32(offs.reshape(-1)), i32(bexp), i32(nval.reshape(1)), i32(fill)


def kernel(x, c, positions, ada_w, ada_b, norm_mix_pre, norm_mix_post, norm_ffn_pre, norm_ffn_post, w_in, gmlp_ln_g, gmlp_ln_b, gmlp_ws, gmlp_bs, mla_q_norm, mla_w_uq, mla_kv_norm, mla_w_ukv, w_branch_a, w_branch_b, w_out, ffn_w_gate, ffn_w_up, ffn_w_down, moe_router, moe_router_bias, moe_w1, moe_w3, moe_w2):
    batch, s, d = x.shape
    assert batch == 1 and s % ROW_TILE == 0 and s % ATTN_TILE == 0
    depth = ada_w.shape[0]
    xs = x.reshape(s, d)
    mod = _ada_mod(c, ada_w, ada_b)
    cos_t, sin_t = _rope_tables(positions)
    vone = jnp.zeros((HEADS, VT_ROWS), F32).at[:, VDIM].set(1.0).reshape(HEADS * VT_ROWS, 1)
    row = lambda v: v.reshape(1, -1)

    for l in range(depth):
        sh_m, sc_m, g_m, sh_f, sc_f, g_f = [mod[l, :, k * d:(k + 1) * d] for k in range(6)]
        w_in_p, wq_main, wq_swap, wk, wv, wb = _mixer_weights(w_in[l], mla_w_uq[l], mla_w_ukv[l], w_branch_b[l])
        bias = jnp.broadcast_to(gmlp_bs[l].T[:, :, None], (CHUNK, GMLP_GROUPS, GMLP_GROUP_DIM))
        bias = bias.reshape(CHUNK, GMLP_WIDTH)
        a, gb, q, k, vt = _mixer_front(xs, row(norm_mix_pre[l]), sc_m, sh_m, w_in_p,
                                       row(gmlp_ln_g[l]), row(gmlp_ln_b[l]), gmlp_ws[l], bias,
                                       w_branch_a[l].astype(BF16), cos_t, sin_t,
                                       row(mla_q_norm[l]), row(mla_kv_norm[l]), wq_main, wq_swap, wk, wv, vone)
        o = _attention(q, k, vt)
        merge_args = (a, gb, o, xs, wb, w_out[l].astype(BF16), row(norm_mix_post[l]), g_m)

        j = l // 2
        if l % 2 == 0:
            ff = ffn_w_gate.shape[2]
            ff_pad = -(-ff // (2 * LANES)) * (2 * LANES)
            wg = jnp.pad(ffn_w_gate[j], ((0, 0), (0, ff_pad - ff))).astype(BF16)
            wu = jnp.pad(ffn_w_up[j], ((0, 0), (0, ff_pad - ff))).astype(BF16)
            wd = jnp.pad(ffn_w_down[j], ((0, ff_pad - ff), (0, 0))).astype(BF16)
            xs = _dense_ffn(merge_args, row(norm_ffn_pre[l]), sc_f, sh_f, wg, wu, wd, row(norm_ffn_post[l]), g_f)
        else:
            nt = s // ROW_TILE
            tb = EXPERT_ROWS
            max_rows = 2 * s + nt * N_EXPERTS * (BF16_ROWS - 1) + N_EXPERTS * (REGION_SLACK + tb - 1)
            nblk = -(-max_rows // tb) + 1
            wr = jnp.pad(moe_router[j], ((0, 0), (0, LANES - N_EXPERTS)))
            br = jnp.pad(moe_router_bias[j], (0, LANES - N_EXPERTS), constant_values=NEG).reshape(1, LANES)
            xs, hb, info, infot, cnts = _route(merge_args, row(norm_ffn_pre[l]), sc_f, sh_f, wr, br)
            offs, bexp, nval, fill = _expert_layout(cnts, nt, nblk)
            xsort = _dispatch(offs, cnts, fill, nval, hb, infot, nblk)
            ysort = _experts(xsort, moe_w1[j].astype(BF16), moe_w3[j].astype(BF16), moe_w2[j].astype(BF16),
                             bexp, nval, tf=moe_w1.shape[3] // 2)
            xs = _combine(offs, cnts, info, xs, row(norm_ffn_post[l]), g_f, ysort)
    return xs.reshape(batch, s, d)
```

```python
import functools

import jax
import jax.numpy as jnp
import numpy as np
from jax import lax
from jax.experimental import pallas as pl
from jax.experimental.pallas import tpu as pltpu

F32 = jnp.float32
BF16 = jnp.bfloat16

EPS = 1e-6
LANES = 128
BF16_ROWS = 16
GMLP_GROUPS = 8
GMLP_GROUP_DIM = 64
GMLP_WIDTH = GMLP_GROUPS * GMLP_GROUP_DIM
CHUNK = 128
HEADS = 8
NOPE = 64
ROPE = 32
VDIM = 64
HEAD_PAD = 128
VT_ROWS = 128
Q_RANK = 384
KV_RANK = 256
ROPE_THETA = 10000.0
N_EXPERTS = 8
NEG = -1e30

ROW_TILE = 512
ATTN_TILE = 512
ATTN_HEADS = 2
STAT_ROWS = 8
EXPERT_ROWS = 512
DISPATCH_ROWS = 128
COMBINE_ROWS = 256
FIRST_ROWS = 256
ROUTE_FIELDS = 8
REGION_SLACK = DISPATCH_ROWS - BF16_ROWS
VMEM_LIMIT = 58 * 1024 * 1024


def _params(*sem):
    return pltpu.CompilerParams(dimension_semantics=sem, vmem_limit_bytes=VMEM_LIMIT)


def _dot(a, b):
    return jnp.dot(a, b, preferred_element_type=F32)


def _rms(x, g):
    return x * lax.rsqrt(jnp.mean(x * x, axis=-1, keepdims=True) + EPS) * g


def _gelu(x):
    return 0.5 * x * (1.0 + lax.erf(x * np.float32(0.7071067811865476)))


def _full(shape):
    return pl.BlockSpec(shape, lambda *_: (0,) * len(shape), pipeline_mode=pl.Buffered(1))


def _mod_body(c_ref, w_ref, b_ref, o_ref):
    c = c_ref[...]
    ca = c * jax.nn.sigmoid(c)
    o_ref[0] = jnp.sum(ca * w_ref[0], axis=0, keepdims=True) + b_ref[0]


def _ada_mod(c, ada_w, ada_b):
    n_layers, d, n = ada_w.shape
    tn = n // 4
    return pl.pallas_call(
        _mod_body,
        grid=(n_layers, n // tn),
        in_specs=[pl.BlockSpec((d, 1), lambda l, j: (0, 0)),
                  pl.BlockSpec((1, d, tn), lambda l, j: (l, 0, j)),
                  pl.BlockSpec((1, 1, tn), lambda l, j: (l, 0, j))],
        out_specs=pl.BlockSpec((1, 1, tn), lambda l, j: (l, 0, j)),
        out_shape=jax.ShapeDtypeStruct((n_layers, 1, n), F32),
        compiler_params=_params("arbitrary", "arbitrary"),
        name="ada_mod",
    )(c.reshape(d, 1), ada_w, ada_b.reshape(n_layers, 1, n))


def _rope_body(pos_ref, invf_ref, cos_ref, sin_ref):
    ang = pos_ref[...].astype(F32) * invf_ref[...]
    cos_ref[...] = jnp.cos(ang)
    sin_ref[...] = jnp.sin(ang)


def _rope_tables(positions):
    s = positions.shape[-1]
    half = ROPE // 2
    per_row = LANES // half
    inv_freq = 1.0 / (ROPE_THETA ** (jnp.arange(0, ROPE, 2, dtype=F32) / ROPE))
    pos_dense = jnp.broadcast_to(positions.reshape(s, 1), (s, half)).reshape(s // per_row, LANES)
    invf = jnp.tile(inv_freq, per_row).reshape(1, LANES)
    rows = s // per_row
    tr = rows // 4
    cos_d, sin_d = pl.pallas_call(
        _rope_body,
        grid=(rows // tr,),
        in_specs=[pl.BlockSpec((tr, LANES), lambda i: (i, 0)), _full((1, LANES))],
        out_specs=[pl.BlockSpec((tr, LANES), lambda i: (i, 0))] * 2,
        out_shape=[jax.ShapeDtypeStruct((rows, LANES), F32)] * 2,
        compiler_params=_params("arbitrary"),
        name="rope_tables",
    )(pos_dense, invf)
    cos = cos_d.reshape(s, half)
    sin = sin_d.reshape(s, half)
    ones = jnp.ones((s, NOPE), F32)
    zeros = jnp.zeros((s, NOPE), F32)
    pad = jnp.zeros((s, HEAD_PAD - NOPE - ROPE), F32)
    cos_t = jnp.concatenate([ones, cos, cos, pad], axis=1)
    sin_t = jnp.concatenate([zeros, sin, sin, pad], axis=1)
    return cos_t, sin_t


def _gmlp_branch(u, v, ga, lng_ref, lnb_ref, ws_ref, bias_ref, wa_ref):
    tm = u.shape[0]
    gu = _gelu(u)
    gv = _gelu(v)
    mu = jnp.mean(gv, axis=-1, keepdims=True)
    xc = gv - mu
    vn = xc * lax.rsqrt(jnp.mean(xc * xc, axis=-1, keepdims=True) + EPS) * lng_ref[...] + lnb_ref[...]
    vb = vn.astype(BF16)
    t_idx = lax.broadcasted_iota(jnp.int32, (CHUNK, CHUNK), 0)
    s_idx = lax.broadcasted_iota(jnp.int32, (CHUNK, CHUNK), 1)
    causal = s_idx <= t_idx
    ws = [jnp.where(causal, ws_ref[g], 0.0).astype(BF16) for g in range(GMLP_GROUPS)]
    left = lax.broadcasted_iota(jnp.int32, (CHUNK, LANES), 1) < GMLP_GROUP_DIM
    bias = bias_ref[...]
    z_rows = []
    for c in range(tm // CHUNK):
        vc = vb[c * CHUNK:(c + 1) * CHUNK]
        z_cols = []
        for j in range(GMLP_WIDTH // LANES):
            vp = vc[:, j * LANES:(j + 1) * LANES]
            z_cols.append(jnp.where(left, _dot(ws[2 * j], vp), _dot(ws[2 * j + 1], vp)))
        z_rows.append(jnp.concatenate(z_cols, axis=1) + bias)
    z = jnp.concatenate(z_rows, axis=0)
    gated = (gu * z).astype(BF16)
    return jax.nn.sigmoid(ga) * _dot(gated, wa_ref[...])


def _mla_qkv(cq, ckv, kr, cos_ref, sin_ref, qn_ref, kvn_ref, wqm_ref, wqs_ref, wk_ref, wvt_ref, vone_ref,
             q_ref, k_ref, vt_ref, scale):
    cos = cos_ref[...]
    sin = sin_ref[...]
    cqn = _rms(cq, qn_ref[...]).astype(BF16)
    qm = _dot(cqn, wqm_ref[...])
    qs = _dot(cqn, wqs_ref[...])
    ckn = _rms(ckv, kvn_ref[...]).astype(BF16)
    km = _dot(ckn, wk_ref[...])
    kpe = kr[:, :HEAD_PAD] * cos + kr[:, HEAD_PAD:] * sin
    for h in range(HEADS):
        sl = slice(h * HEAD_PAD, (h + 1) * HEAD_PAD)
        q_ref[:, sl] = ((qm[:, sl] * cos + qs[:, sl] * sin) * scale).astype(q_ref.dtype)
        k_ref[:, sl] = (km[:, sl] + kpe).astype(k_ref.dtype)
    vt = lax.dot_general(wvt_ref[...], ckn, (((1,), (1,)), ((), ())), preferred_element_type=F32)
    vt_ref[...] = (vt + vone_ref[...]).astype(vt_ref.dtype)


def _front_body(x_ref, g_ref, sc_ref, sh_ref, w_ref, lng_ref, lnb_ref, ws_ref, bias_ref, wa_ref,
                cos_ref, sin_ref, qn_ref, kvn_ref, wqm_ref, wqs_ref, wk_ref, wvt_ref, vone_ref,
                a_ref, gb_ref, q_ref, k_ref, vt_ref, *, scale):
    h = _rms(x_ref[...], g_ref[...]) * (1.0 + sc_ref[...]) + sh_ref[...]
    proj = _dot(h.astype(BF16), w_ref[...])
    d = x_ref.shape[1]
    cuts = np.cumsum([0, GMLP_WIDTH, GMLP_WIDTH, Q_RANK, KV_RANK, 2 * HEAD_PAD, d, d])
    u, v, cq, ckv, kr, ga, gb = [proj[:, lo:hi] for lo, hi in zip(cuts[:-1], cuts[1:])]
    gb_ref[...] = gb.astype(gb_ref.dtype)
    a_ref[...] = _gmlp_branch(u, v, ga, lng_ref, lnb_ref, ws_ref, bias_ref, wa_ref).astype(a_ref.dtype)
    _mla_qkv(cq, ckv, kr, cos_ref, sin_ref, qn_ref, kvn_ref, wqm_ref, wqs_ref, wk_ref, wvt_ref, vone_ref,
             q_ref, k_ref, vt_ref, scale)


def _mixer_front(x, g, sc, sh, w, lng, lnb, ws, bias, wa, cos_t, sin_t, qn, kvn, wqm, wqs, wk, wvt, vone):
    s, d = x.shape
    tm = ROW_TILE
    width = HEADS * HEAD_PAD
    row = lambda n: pl.BlockSpec((tm, n), lambda i: (i, 0))
    scale = float((NOPE + ROPE) ** -0.5 * np.log2(np.e))
    consts = (g, sc, sh, w, lng, lnb, ws, bias, wa)
    mla_consts = (qn, kvn, wqm, wqs, wk, wvt, vone)
    return pl.pallas_call(
        functools.partial(_front_body, scale=scale),
        grid=(s // tm,),
        in_specs=[row(d)] + [_full(c.shape) for c in consts] + [row(HEAD_PAD), row(HEAD_PAD)]
                 + [_full(c.shape) for c in mla_consts],
        out_specs=[row(d), row(d), row(width), row(width), pl.BlockSpec((wvt.shape[0], tm), lambda i: (0, i))],
        out_shape=[jax.ShapeDtypeStruct((s, d), BF16), jax.ShapeDtypeStruct((s, d), BF16),
                   jax.ShapeDtypeStruct((s, width), BF16), jax.ShapeDtypeStruct((s, width), BF16),
                   jax.ShapeDtypeStruct((wvt.shape[0], s), BF16)],
        compiler_params=_params("arbitrary"),
        name="mixer_front",
    )(x, *consts, cos_t, sin_t, *mla_consts)


def _attn_body(q_ref, k_ref, vt_ref, o_ref, s_ref, mx_ref, m_ref, acc_ref):
    t = o_ref.shape[0]
    heads = q_ref.shape[1] // HEAD_PAD
    qi = pl.program_id(1)
    m_ref[...] = jnp.full(m_ref.shape, NEG, F32)
    acc_ref[...] = jnp.zeros(acc_ref.shape, F32)

    def scores(blk, slot, q_tile):
        off = pl.multiple_of(blk * t, t)
        q_off = pl.multiple_of(q_tile * t, t)
        for h in range(heads):
            sl = slice(h * HEAD_PAD, (h + 1) * HEAD_PAD)
            s = lax.dot_general(k_ref[pl.ds(off, t), sl], q_ref[pl.ds(q_off, t), sl],
                                (((1,), (1,)), ((), ())), preferred_element_type=F32)
            s_ref[slot, h] = s
            mx_ref[slot, h] = jnp.broadcast_to(jnp.max(s, axis=0, keepdims=True), (STAT_ROWS, t))

    def consume(blk, slot, mask):
        off = pl.multiple_of(blk * t, t)
        for h in range(heads):
            sl = slice(h * HEAD_PAD, (h + 1) * HEAD_PAD)
            s = s_ref[slot, h]
            if mask is None:
                mx = mx_ref[slot, h]
            else:
                s = jnp.where(mask, s, NEG)
                mx = jnp.max(s, axis=0, keepdims=True)
            m_prev = m_ref[h]
            m_new = jnp.maximum(m_prev, mx)
            alpha = jnp.exp2(m_prev - m_new)
            pv = alpha[0:1] * acc_ref[h]
            half = t // 2
            for part in range(2):
                p = jnp.exp2(s[part * half:(part + 1) * half] - m_new[0:1]).astype(BF16)
                keys = pl.ds(pl.multiple_of(off + part * half, half), half)
                pv = pv + _dot(vt_ref[h * VT_ROWS:(h + 1) * VT_ROWS, keys], p)
            acc_ref[h] = pv
            m_ref[h] = m_new

    @pl.when(qi == 0)
    def _():
        scores(0, 0, qi)

    def run(blk, n):
        for u in range(n):
            scores(blk + u + 1, (u + 1) % 2, qi)
            consume(blk + u, u % 2, None)

    def octet(j, carry):
        run(8 * j, 8)
        return carry

    lax.fori_loop(0, qi // 8, octet, 0)
    done = (qi // 8) * 8
    for n in (4, 2):
        more = qi - done >= n

        @pl.when(more)
        def _():
            run(done, n)

        done = done + jnp.where(more, n, 0)

    row = lax.broadcasted_iota(jnp.int32, (t, t), 0)
    col = lax.broadcasted_iota(jnp.int32, (t, t), 1)
    causal = row <= col
    odd = lax.rem(qi, 2) == 1

    @pl.when(odd)
    def _():
        scores(qi, 1, qi)
        consume(qi - 1, 0, None)
        consume(qi, 1, causal)

    @pl.when(jnp.logical_not(odd))
    def _():
        consume(qi, 0, causal)

    def finalize():
        lower = lax.broadcasted_iota(jnp.int32, (t, HEAD_PAD), 1) < VDIM
        for j in range(heads // 2):
            outs = []
            for h in (2 * j, 2 * j + 1):
                acc = acc_ref[h]
                out_t = acc / acc[VDIM:VDIM + 1]
                if VT_ROWS < HEAD_PAD:
                    out_t = jnp.concatenate([out_t, jnp.zeros((HEAD_PAD - VT_ROWS, t), F32)], axis=0)
                outs.append(out_t.T)
            packed = jnp.where(lower, outs[0], pltpu.roll(outs[1], VDIM, 1))
            o_ref[:, j * HEAD_PAD:(j + 1) * HEAD_PAD] = packed.astype(o_ref.dtype)

    last = qi == pl.num_programs(1) - 1

    @pl.when(jnp.logical_not(last))
    def _():
        scores(0, 0, qi + 1)
        finalize()

    @pl.when(last)
    def _():
        finalize()


def _attention(q, k, vt):
    s, width = q.shape
    t = ATTN_TILE
    gw = ATTN_HEADS * HEAD_PAD
    return pl.pallas_call(
        _attn_body,
        grid=(width // gw, s // t),
        in_specs=[pl.BlockSpec((s, gw), lambda h, i: (0, h)),
                  pl.BlockSpec((s, gw), lambda h, i: (0, h)),
                  pl.BlockSpec((ATTN_HEADS * VT_ROWS, s), lambda h, i: (h, 0))],
        out_specs=pl.BlockSpec((t, ATTN_HEADS * VDIM), lambda h, i: (i, h)),
        out_shape=jax.ShapeDtypeStruct((s, HEADS * VDIM), BF16),
        scratch_shapes=[pltpu.VMEM((2, ATTN_HEADS, t, t), F32),
                        pltpu.VMEM((2, ATTN_HEADS, STAT_ROWS, t), F32),
                        pltpu.VMEM((ATTN_HEADS, STAT_ROWS, t), F32),
                        pltpu.VMEM((ATTN_HEADS, VT_ROWS, t), F32)],
        compiler_params=_params("arbitrary", "arbitrary"),
        name="mla_attention",
    )(q, k, vt)


MERGE_ARGS = 8


def _merge_math(a_ref, gb_ref, o_ref, x_ref, wb_ref, wo_ref, gp_ref, gm_ref):
    yb = _dot(o_ref[...], wb_ref[...])
    merged = a_ref[...].astype(F32) + jax.nn.sigmoid(gb_ref[...].astype(F32)) * yb
    y = _dot(merged.astype(BF16), wo_ref[...])
    return x_ref[...] + gm_ref[...] * _rms(y, gp_ref[...])


def _merge_specs(a, gb, o, x, wb, wo, gp, gm):
    tm = ROW_TILE
    d = x.shape[1]
    row = pl.BlockSpec((tm, d), lambda i: (i, 0))
    return [row, row, pl.BlockSpec((tm, o.shape[1]), lambda i: (i, 0)), row,
            _full(wb.shape), _full(wo.shape), _full((1, d)), _full((1, d))]


def _ffn_body(*refs):
    g_ref, sc_ref, sh_ref, wg_ref, wu_ref, wd_ref, gp_ref, gf_ref, o_ref = refs[MERGE_ARGS:]
    x = _merge_math(*refs[:MERGE_ARGS])
    hb = (_rms(x, g_ref[...]) * (1.0 + sc_ref[...]) + sh_ref[...]).astype(BF16)
    a = _dot(hb, wg_ref[...])
    b = _dot(hb, wu_ref[...])
    y = _dot((a * jax.nn.sigmoid(a) * b).astype(BF16), wd_ref[...])
    o_ref[...] = x + gf_ref[...] * _rms(y, gp_ref[...])


def _dense_ffn(merge_args, g, sc, sh, wg, wu, wd, gp, gf):
    s, d = merge_args[3].shape
    tm = ROW_TILE
    row = pl.BlockSpec((tm, d), lambda i: (i, 0))
    vec = _full((1, d))
    return pl.pallas_call(
        _ffn_body,
        grid=(s // tm,),
        in_specs=_merge_specs(*merge_args)
                 + [vec, vec, vec, _full(wg.shape), _full(wu.shape), _full(wd.shape), vec, vec],
        out_specs=row,
        out_shape=jax.ShapeDtypeStruct((s, d), F32),
        compiler_params=_params("arbitrary"),
        name="merge_dense_ffn",
    )(*merge_args, g, sc, sh, wg, wu, wd, gp, gf)


def _route_body(*refs):
    g_ref, sc_ref, sh_ref, wr_ref, br_ref, x_ref, hb_ref, info_ref, infot_ref, cnt_ref = refs[MERGE_ARGS:]
    i = pl.program_id(0)
    tm = x_ref.shape[0]
    x = _merge_math(*refs[:MERGE_ARGS])
    x_ref[...] = x
    h = _rms(x, g_ref[...]) * (1.0 + sc_ref[...]) + sh_ref[...]
    hb = h.astype(BF16)
    hb_ref[...] = hb
    h_lo = (h - hb.astype(F32)).astype(BF16)
    w = wr_ref[...]
    w_hi = w.astype(BF16)
    w_lo = (w - w_hi.astype(F32)).astype(BF16)
    logits = _dot(hb, w_hi) + (_dot(h_lo, w_hi) + _dot(hb, w_lo)) + br_ref[...]

    lane = lax.broadcasted_iota(jnp.int32, (tm, LANES), 1)
    m1 = jnp.max(logits, axis=-1, keepdims=True)
    i1 = jnp.min(jnp.where(logits == m1, lane, LANES), axis=-1, keepdims=True)
    oh1 = lane == i1
    rest = jnp.where(oh1, -3e38, logits)
    m2 = jnp.max(rest, axis=-1, keepdims=True)
    i2 = jnp.min(jnp.where(rest == m2, lane, LANES), axis=-1, keepdims=True)
    oh2 = lane == i2
    ex = jnp.exp(m2 - m1)
    g1 = 1.0 / (1.0 + ex)
    g2 = ex / (1.0 + ex)

    ohf = jnp.where(oh1 | oh2, 1.0, 0.0)
    r_idx = lax.broadcasted_iota(jnp.int32, (tm, tm), 0)
    c_idx = lax.broadcasted_iota(jnp.int32, (tm, tm), 1)
    earlier = jnp.where(c_idx < r_idx, 1.0, 0.0).astype(BF16)
    rank = _dot(earlier, ohf.astype(BF16))
    rank1 = jnp.sum(jnp.where(oh1, rank, 0.0), axis=-1, keepdims=True)
    rank2 = jnp.sum(jnp.where(oh2, rank, 0.0), axis=-1, keepdims=True)
    info = jnp.where(lane == 0, i1.astype(F32),
           jnp.where(lane == 1, i2.astype(F32),
           jnp.where(lane == 2, g1,
           jnp.where(lane == 3, g2,
           jnp.where(lane == 4, rank1,
           jnp.where(lane == 5, rank2, 0.0))))))
    info_ref[...] = info
    infot_ref[...] = info.T[:ROUTE_FIELDS]
    cnt = jnp.sum(ohf, axis=0, keepdims=True).astype(jnp.int32)
    cnt_al = ((cnt + (BF16_ROWS - 1)) // BF16_ROWS) * BF16_ROWS
    for e in range(N_EXPERTS):
        cnt_ref[i * N_EXPERTS + e] = cnt_al[0, e]


def _route(merge_args, g, sc, sh, wr, br):
    s, d = merge_args[3].shape
    tm = ROW_TILE
    nt = s // tm
    row = pl.BlockSpec((tm, d), lambda i: (i, 0))
    return pl.pallas_call(
        _route_body,
        grid=(nt,),
        in_specs=_merge_specs(*merge_args)
                 + [_full((1, d)), _full((1, d)), _full((1, d)), _full(wr.shape), _full(br.shape)],
        out_specs=[row, row,
                   pl.BlockSpec((tm, LANES), lambda i: (i, 0)),
                   pl.BlockSpec((ROUTE_FIELDS, tm), lambda i: (0, i)),
                   pl.BlockSpec(memory_space=pltpu.SMEM)],
        out_shape=[jax.ShapeDtypeStruct((s, d), F32),
                   jax.ShapeDtypeStruct((s, d), BF16),
                   jax.ShapeDtypeStruct((s, LANES), F32),
                   jax.ShapeDtypeStruct((ROUTE_FIELDS, s), F32),
                   jax.ShapeDtypeStruct((nt * N_EXPERTS,), jnp.int32)],
        compiler_params=_params("arbitrary"),
        name="merge_moe_route",
    )(*merge_args, g, sc, sh, wr, br)


def _routing(info):
    return [(info[:, k:k + 1], info[:, 2 + k:3 + k], info[:, 4 + k:5 + k]) for k in range(2)]


def _dispatch_body(offs_ref, cnts_ref, fill_ref, nval_ref, hb_ref, infot_ref, xs_ref, xbuf, zbuf, sem, zsem, *, nblk):
    i = pl.program_id(0)
    tm = hb_ref.shape[0]
    tb = zbuf.shape[0]

    def zero_block(blk):
        cp = pltpu.make_async_copy(zbuf, xs_ref.at[pl.ds(pl.multiple_of(blk * tb, tb), tb)], zsem)
        cp.start()
        cp.wait()

    @pl.when(i == 0)
    def _():
        zbuf[...] = jnp.zeros(zbuf.shape, zbuf.dtype)
        for n in range(fill_ref.shape[0]):
            @pl.when(fill_ref[n] >= 0)
            def _():
                zero_block(fill_ref[n])

        def unused(blk, carry):
            zero_block(blk)
            return carry

        lax.fori_loop(nval_ref[0], nblk, unused, 0)

    hb = hb_ref[...]
    fields = infot_ref[...]
    e1, e2, r1, r2 = fields[0:1], fields[1:2], fields[4:5], fields[5:6]
    chunks = tm // DISPATCH_ROWS
    buf = lax.rem(i, 2)

    def first_slot(e, r):
        return jnp.where(r < float(FIRST_ROWS), e * float(FIRST_ROWS) + r, -1.0)

    p1 = first_slot(e1, r1)
    p2 = first_slot(e2, r2)
    slot = lax.broadcasted_iota(jnp.int32, (N_EXPERTS * FIRST_ROWS, tm), 0).astype(F32)
    sel = jnp.where((p1 == slot) | (p2 == slot), 1.0, 0.0).astype(BF16)
    xbuf[buf, 0] = _dot(sel, hb).astype(BF16)

    first_chunks = FIRST_ROWS // DISPATCH_ROWS

    def buf_rows(e, c):
        return c // first_chunks, e * FIRST_ROWS + (c % first_chunks) * DISPATCH_ROWS

    for e in range(N_EXPERTS):
        for c in range(first_chunks, chunks):
            @pl.when(cnts_ref[i * N_EXPERTS + e] > c * DISPATCH_ROWS)
            def _():
                first = e1 == float(e)
                rank = jnp.where(first, r1, r2)
                late = (lax.broadcasted_iota(jnp.int32, (DISPATCH_ROWS, tm), 0) + c * DISPATCH_ROWS).astype(F32)
                hit = (rank == late) & (first | (e2 == float(e)))
                part, row0 = buf_rows(e, c)
                xbuf[buf, part, row0:row0 + DISPATCH_ROWS] = _dot(jnp.where(hit, 1.0, 0.0).astype(BF16),
                                                                  hb).astype(BF16)

    def copy(b, e, c, dst):
        part, row0 = buf_rows(e, c)
        return pltpu.make_async_copy(xbuf.at[b, part, pl.ds(row0, DISPATCH_ROWS)],
                                     xs_ref.at[pl.ds(dst, DISPATCH_ROWS)], sem.at[b, e, c])

    def for_each_copy(step, fn):
        for e in range(N_EXPERTS):
            for c in range(chunks):
                @pl.when(cnts_ref[step * N_EXPERTS + e] > c * DISPATCH_ROWS)
                def _():
                    fn(e, c)

    @pl.when(i > 0)
    def _():
        for_each_copy(i - 1, lambda e, c: copy(1 - buf, e, c, 0).wait())

    def start(e, c):
        off = offs_ref[i * N_EXPERTS + e]
        copy(buf, e, c, pl.multiple_of(off + c * DISPATCH_ROWS, BF16_ROWS)).start()

    for_each_copy(i, start)

    @pl.when(i == pl.num_programs(0) - 1)
    def _():
        for_each_copy(i, lambda e, c: copy(buf, e, c, 0).wait())


def _dispatch(offs, cnts, fill, nval, hb, infot, nblk):
    s, d = hb.shape
    tm = ROW_TILE
    tb = EXPERT_ROWS
    grid_spec = pltpu.PrefetchScalarGridSpec(
        num_scalar_prefetch=4,
        grid=(s // tm,),
        in_specs=[pl.BlockSpec((tm, d), lambda i, *_: (i, 0)),
                  pl.BlockSpec((ROUTE_FIELDS, tm), lambda i, *_: (0, i))],
        out_specs=pl.BlockSpec(memory_space=pl.ANY),
        scratch_shapes=[pltpu.VMEM((2, tm // FIRST_ROWS, N_EXPERTS * FIRST_ROWS, d), BF16),
                        pltpu.VMEM((tb, d), BF16),
                        pltpu.SemaphoreType.DMA((2, N_EXPERTS, tm // DISPATCH_ROWS)),
                        pltpu.SemaphoreType.DMA(())],
    )
    return pl.pallas_call(
        functools.partial(_dispatch_body, nblk=nblk),
        grid_spec=grid_spec,
        out_shape=jax.ShapeDtypeStruct((nblk * tb, d), BF16),
        compiler_params=_params("arbitrary"),
        name="moe_dispatch",
    )(offs, cnts, fill, nval, hb, infot)


def _expert_body(bexp_ref, nval_ref, x_ref, w1_ref, w3_ref, w2_ref, o_ref, acc_ref, *, nff):
    i = pl.program_id(0)
    f = pl.program_id(1)
    used = i < nval_ref[0]

    def ff_slice(first, final):
        xb = x_ref[...]
        a = _dot(xb, w1_ref[0])
        b = _dot(xb, w3_ref[0])
        acc = _dot((a * jax.nn.sigmoid(a) * b).astype(BF16), w2_ref[0])
        if not first:
            acc = acc_ref[...] + acc
        if final:
            o_ref[...] = acc.astype(o_ref.dtype)
        else:
            acc_ref[...] = acc

    for step in sorted({0, min(1, nff - 1), nff - 1}):
        pick = f == step if step in (0, nff - 1) else (f > 0) & (f < nff - 1)

        @pl.when(used & pick)
        def _():
            ff_slice(step == 0, step == nff - 1)

    @pl.when(jnp.logical_not(used) & (f == nff - 1))
    def _():
        o_ref[...] = jnp.zeros(o_ref.shape, o_ref.dtype)


def _experts(xs, w1, w3, w2, bexp, nval, tf):
    rows, d = xs.shape
    ff = w1.shape[2]
    tb = EXPERT_ROWS
    nff = ff // tf

    def xrow(i, f, be, nv):
        return (jnp.minimum(i, nv[0] - 1), 0)

    def fcol(i, f, nv):
        return jnp.where(i < nv[0], f, nff - 1)

    grid_spec = pltpu.PrefetchScalarGridSpec(
        num_scalar_prefetch=2,
        grid=(rows // tb, nff),
        in_specs=[pl.BlockSpec((tb, d), xrow),
                  pl.BlockSpec((1, d, tf), lambda i, f, be, nv: (be[i], 0, fcol(i, f, nv))),
                  pl.BlockSpec((1, d, tf), lambda i, f, be, nv: (be[i], 0, fcol(i, f, nv))),
                  pl.BlockSpec((1, tf, d), lambda i, f, be, nv: (be[i], fcol(i, f, nv), 0))],
        out_specs=pl.BlockSpec((tb, d), lambda i, f, be, nv: (i, 0)),
        scratch_shapes=[pltpu.VMEM((tb, d), F32)],
    )
    return pl.pallas_call(
        functools.partial(_expert_body, nff=nff),
        grid_spec=grid_spec,
        out_shape=jax.ShapeDtypeStruct((rows, d), BF16),
        compiler_params=_params("arbitrary", "arbitrary"),
        name="moe_experts",
    )(bexp, nval, xs, w1, w3, w2)


def _combine_body(offs_ref, cnts_ref, info_ref, x_ref, gp_ref, gf_ref, ys_ref, o_ref, ybuf, acc_ref, sem):
    i = pl.program_id(0)
    nt = pl.num_programs(0)
    tm = x_ref.shape[0]
    chunks = tm // COMBINE_ROWS
    buf = lax.rem(i, 2)

    def copy(b, e, c, src):
        dst = ybuf.at[b, pl.ds((c * N_EXPERTS + e) * COMBINE_ROWS, COMBINE_ROWS)]
        return pltpu.make_async_copy(ys_ref.at[pl.ds(src, COMBINE_ROWS)], dst, sem.at[b, e, c])

    def fetch(step, b):
        for e in range(N_EXPERTS):
            off = offs_ref[step * N_EXPERTS + e]
            copy(b, e, 0, pl.multiple_of(off, BF16_ROWS)).start()
            for c in range(1, chunks):
                @pl.when(cnts_ref[step * N_EXPERTS + e] > c * COMBINE_ROWS)
                def _():
                    copy(b, e, c, pl.multiple_of(off + c * COMBINE_ROWS, BF16_ROWS)).start()

    @pl.when(i == 0)
    def _():
        fetch(0, 0)

    @pl.when(i + 1 < nt)
    def _():
        fetch(i + 1, 1 - buf)

    (e1, g1, r1), (e2, g2, r2) = _routing(info_ref[...])
    for e in range(N_EXPERTS):
        copy(buf, e, 0, 0).wait()
    stack = N_EXPERTS * COMBINE_ROWS
    slot = lax.broadcasted_iota(jnp.int32, (tm, stack), 1).astype(F32)
    y_first = ybuf[buf, 0:stack]
    acc = None
    for ek, gk, rk in ((e1, g1, r1), (e2, g2, r2)):
        pos = jnp.where(rk < float(COMBINE_ROWS), ek * float(COMBINE_ROWS) + rk, -1.0)
        term = gk * _dot(jnp.where(pos == slot, 1.0, 0.0).astype(BF16), y_first)
        acc = term if acc is None else acc + term
    acc_ref[...] = acc

    late = lax.broadcasted_iota(jnp.int32, (tm, COMBINE_ROWS), 1).astype(F32)
    for e in range(N_EXPERTS):
        for c in range(1, chunks):
            @pl.when(cnts_ref[i * N_EXPERTS + e] > c * COMBINE_ROWS)
            def _():
                first = e1 == float(e)
                second = e2 == float(e)
                rank = jnp.where(first, r1, r2)
                gate = jnp.where(first, g1, jnp.where(second, g2, 0.0))
                copy(buf, e, c, 0).wait()
                hit = (rank == late + float(c * COMBINE_ROWS)) & (first | second)
                rows = ybuf[buf, pl.ds((c * N_EXPERTS + e) * COMBINE_ROWS, COMBINE_ROWS)]
                acc_ref[...] += gate * _dot(jnp.where(hit, 1.0, 0.0).astype(BF16), rows)

    o_ref[...] = x_ref[...] + gf_ref[...] * _rms(acc_ref[...], gp_ref[...])


def _combine(offs, cnts, info, x, gp, gf, ys):
    s, d = x.shape
    tm = ROW_TILE
    chunks = tm // COMBINE_ROWS
    grid_spec = pltpu.PrefetchScalarGridSpec(
        num_scalar_prefetch=2,
        grid=(s // tm,),
        in_specs=[pl.BlockSpec((tm, LANES), lambda i, *_: (i, 0)),
                  pl.BlockSpec((tm, d), lambda i, *_: (i, 0)),
                  pl.BlockSpec((1, d), lambda i, *_: (0, 0)),
                  pl.BlockSpec((1, d), lambda i, *_: (0, 0)),
                  pl.BlockSpec(memory_space=pl.ANY)],
        out_specs=pl.BlockSpec((tm, d), lambda i, *_: (i, 0)),
        scratch_shapes=[pltpu.VMEM((2, chunks * N_EXPERTS * COMBINE_ROWS, d), BF16),
                        pltpu.VMEM((tm, d), F32),
                        pltpu.SemaphoreType.DMA((2, N_EXPERTS, chunks))],
    )
    return pl.pallas_call(
        _combine_body,
        grid_spec=grid_spec,
        out_shape=jax.ShapeDtypeStruct((s, d), F32),
        compiler_params=_params("arbitrary"),
        name="moe_combine",
    )(offs, cnts, info, x, gp, gf, ys)


def _rot_half_cols(w):
    half = ROPE // 2
    return jnp.concatenate([-w[..., half:], w[..., :half]], axis=-1)


def _pad_cols(w, before, total):
    return jnp.pad(w, ((0, 0), (before, total - before - w.shape[1])))


def _mixer_weights(w_in, w_uq, w_ukv, w_branch_b):
    d = w_in.shape[0]
    o = 0
    parts = {}
    for name, n in (("u", GMLP_WIDTH), ("v", GMLP_WIDTH), ("cq", Q_RANK), ("ckv", KV_RANK),
                    ("kr", ROPE), ("ga", d), ("gb", d)):
        parts[name] = w_in[:, o:o + n]
        o += n
    kr_main = _pad_cols(parts["kr"], NOPE, HEAD_PAD)
    kr_swap = _pad_cols(_rot_half_cols(parts["kr"]), NOPE, HEAD_PAD)
    w_in_p = jnp.concatenate([parts["u"], parts["v"], parts["cq"], parts["ckv"], kr_main, kr_swap,
                              parts["ga"], parts["gb"]], axis=1).astype(BF16)

    wq = w_uq.reshape(Q_RANK, HEADS, NOPE + ROPE)
    zq = jnp.zeros((Q_RANK, HEADS, HEAD_PAD - NOPE - ROPE), w_uq.dtype)
    wq_main = jnp.concatenate([wq, zq], axis=-1).reshape(Q_RANK, HEADS * HEAD_PAD).astype(BF16)
    wq_swap = jnp.concatenate([jnp.zeros((Q_RANK, HEADS, NOPE), w_uq.dtype),
                               _rot_half_cols(wq[..., NOPE:]), zq], axis=-1)
    wq_swap = wq_swap.reshape(Q_RANK, HEADS * HEAD_PAD).astype(BF16)

    wkv = w_ukv.reshape(KV_RANK, HEADS, NOPE + VDIM)
    zk = jnp.zeros((KV_RANK, HEADS, HEAD_PAD - NOPE), w_ukv.dtype)
    wk = jnp.concatenate([wkv[..., :NOPE], zk], axis=-1).reshape(KV_RANK, HEADS * HEAD_PAD).astype(BF16)
    zv = jnp.zeros((KV_RANK, HEADS, VT_ROWS - VDIM), w_ukv.dtype)
    wv = jnp.concatenate([wkv[..., NOPE:], zv], axis=-1).reshape(KV_RANK, HEADS * VT_ROWS).T.astype(BF16)

    return w_in_p, wq_main, wq_swap, wk, wv, w_branch_b.astype(BF16)


def _expert_layout(cnts, nt, nblk):
    tb = EXPERT_ROWS
    c = cnts.reshape(nt, N_EXPERTS)
    total = jnp.sum(c, axis=0)
    per = jnp.where(total > 0, (total + REGION_SLACK + tb - 1) // tb, 0)
    stop = jnp.cumsum(per)
    start = stop - per
    offs = start[None, :] * tb + (jnp.cumsum(c, axis=0) - c)
    nval = stop[-1]
    blk = jnp.minimum(jnp.arange(nblk, dtype=jnp.int32), nval - 1)
    bexp = jnp.minimum(jnp.sum(blk[:, None] >= stop[None, :], axis=1), N_EXPERTS - 1)
    fill = jnp.concatenate([jnp.where(per > 0, stop - 1, -1), jnp.where(per > 1, stop - 2, -1)])
    i32 = lambda v: v.astype(jnp.int32)
    return i32(offs.reshape(-1)), i32(bexp), i32(nval.reshape(1)), i32(fill)


def kernel(x, c, positions, ada_w, ada_b, norm_mix_pre, norm_mix_post, norm_ffn_pre, norm_ffn_post, w_in, gmlp_ln_g, gmlp_ln_b, gmlp_ws, gmlp_bs, mla_q_norm, mla_w_uq, mla_kv_norm, mla_w_ukv, w_branch_a, w_branch_b, w_out, ffn_w_gate, ffn_w_up, ffn_w_down, moe_router, moe_router_bias, moe_w1, moe_w3, moe_w2):
    batch, s, d = x.shape
    assert batch == 1 and s % ROW_TILE == 0 and s % ATTN_TILE == 0
    depth = ada_w.shape[0]
    xs = x.reshape(s, d)
    mod = _ada_mod(c, ada_w, ada_b)
    cos_t, sin_t = _rope_tables(positions)
    vone = jnp.zeros((HEADS, VT_ROWS), F32).at[:, VDIM].set(1.0).reshape(HEADS * VT_ROWS, 1)
    row = lambda v: v.reshape(1, -1)

    for l in range(depth):
        sh_m, sc_m, g_m, sh_f, sc_f, g_f = [mod[l, :, k * d:(k + 1) * d] for k in range(6)]
        w_in_p, wq_main, wq_swap, wk, wv, wb = _mixer_weights(w_in[l], mla_w_uq[l], mla_w_ukv[l], w_branch_b[l])
        bias = jnp.broadcast_to(gmlp_bs[l].T[:, :, None], (CHUNK, GMLP_GROUPS, GMLP_GROUP_DIM))
        bias = bias.reshape(CHUNK, GMLP_WIDTH)
        a, gb, q, k, vt = _mixer_front(xs, row(norm_mix_pre[l]), sc_m, sh_m, w_in_p,
                                       row(gmlp_ln_g[l]), row(gmlp_ln_b[l]), gmlp_ws[l], bias,
                                       w_branch_a[l].astype(BF16), cos_t, sin_t,
                                       row(mla_q_norm[l]), row(mla_kv_norm[l]), wq_main, wq_swap, wk, wv, vone)
        o = _attention(q, k, vt)
        merge_args = (a, gb, o, xs, wb, w_out[l].astype(BF16), row(norm_mix_post[l]), g_m)

        j = l // 2
        if l % 2 == 0:
            ff = ffn_w_gate.shape[2]
            ff_pad = -(-ff // (2 * LANES)) * (2 * LANES)
            wg = jnp.pad(ffn_w_gate[j], ((0, 0), (0, ff_pad - ff))).astype(BF16)
            wu = jnp.pad(ffn_w_up[j], ((0, 0), (0, ff_pad - ff))).astype(BF16)
            wd = jnp.pad(ffn_w_down[j], ((0, ff_pad - ff), (0, 0))).astype(BF16)
            xs = _dense_ffn(merge_args, row(norm_ffn_pre[l]), sc_f, sh_f, wg, wu, wd, row(norm_ffn_post[l]), g_f)
        else:
            nt = s // ROW_TILE
            tb = EXPERT_ROWS
            max_rows = 2 * s + nt * N_EXPERTS * (BF16_ROWS - 1) + N_EXPERTS * (REGION_SLACK + tb - 1)
            nblk = -(-max_rows // tb) + 1
            wr = jnp.pad(moe_router[j], ((0, 0), (0, LANES - N_EXPERTS)))
            br = jnp.pad(moe_router_bias[j], (0, LANES - N_EXPERTS), constant_values=NEG).reshape(1, LANES)
            xs, hb, info, infot, cnts = _route(merge_args, row(norm_ffn_pre[l]), sc_f, sh_f, wr, br)
            offs, bexp, nval, fill = _expert_layout(cnts, nt, nblk)
            xsort = _dispatch(offs, cnts, fill, nval, hb, infot, nblk)
            ysort = _experts(xsort, moe_w1[j].astype(BF16), moe_w3[j].astype(BF16), moe_w2[j].astype(BF16),
                             bexp, nval, tf=moe_w1.shape[3] // 2)
            xs = _combine(offs, cnts, info, xs, row(norm_ffn_post[l]), g_f, ysort)
    return xs.reshape(batch, s, d)
```

```python
import functools

import jax
import jax.numpy as jnp
import numpy as np
from jax import lax
from jax.experimental import pallas as pl
from jax.experimental.pallas import tpu as pltpu

F32 = jnp.float32
BF16 = jnp.bfloat16

EPS = 1e-6
LANES = 128
BF16_ROWS = 16
GMLP_GROUPS = 8
GMLP_GROUP_DIM = 64
GMLP_WIDTH = GMLP_GROUPS * GMLP_GROUP_DIM
CHUNK = 128
HEADS = 8
NOPE = 64
ROPE = 32
VDIM = 64
HEAD_PAD = 128
VT_ROWS = 128
Q_RANK = 384
KV_RANK = 256
ROPE_THETA = 10000.0
N_EXPERTS = 8
NEG = -1e30

ROW_TILE = 512
ATTN_TILE = 512
ATTN_HEADS = 2
ATTN_UNROLL = 16
STAT_ROWS = 8
EXPERT_ROWS = 512
DISPATCH_ROWS = 128
COMBINE_ROWS = 256
FIRST_ROWS = 256
ROUTE_FIELDS = 8
REGION_SLACK = DISPATCH_ROWS - BF16_ROWS
VMEM_LIMIT = 58 * 1024 * 1024


def _params(*sem):
    return pltpu.CompilerParams(dimension_semantics=sem, vmem_limit_bytes=VMEM_LIMIT)


def _dot(a, b):
    return jnp.dot(a, b, preferred_element_type=F32)


def _rms(x, g):
    return x * lax.rsqrt(jnp.mean(x * x, axis=-1, keepdims=True) + EPS) * g


def _gelu(x):
    return 0.5 * x * (1.0 + lax.erf(x * np.float32(0.7071067811865476)))


def _full(shape):
    return pl.BlockSpec(shape, lambda *_: (0,) * len(shape), pipeline_mode=pl.Buffered(1))


def _mod_body(c_ref, w_ref, b_ref, o_ref):
    c = c_ref[...]
    ca = c * jax.nn.sigmoid(c)
    o_ref[0] = jnp.sum(ca * w_ref[0], axis=0, keepdims=True) + b_ref[0]


def _ada_mod(c, ada_w, ada_b):
    n_layers, d, n = ada_w.shape
    tn = n // 4
    return pl.pallas_call(
        _mod_body,
        grid=(n_layers, n // tn),
        in_specs=[pl.BlockSpec((d, 1), lambda l, j: (0, 0)),
                  pl.BlockSpec((1, d, tn), lambda l, j: (l, 0, j)),
                  pl.BlockSpec((1, 1, tn), lambda l, j: (l, 0, j))],
        out_specs=pl.BlockSpec((1, 1, tn), lambda l, j: (l, 0, j)),
        out_shape=jax.ShapeDtypeStruct((n_layers, 1, n), F32),
        compiler_params=_params("arbitrary", "arbitrary"),
        name="ada_mod",
    )(c.reshape(d, 1), ada_w, ada_b.reshape(n_layers, 1, n))


def _rope_body(pos_ref, invf_ref, cos_ref, sin_ref):
    ang = pos_ref[...].astype(F32) * invf_ref[...]
    cos_ref[...] = jnp.cos(ang)
    sin_ref[...] = jnp.sin(ang)


def _rope_tables(positions):
    s = positions.shape[-1]
    half = ROPE // 2
    per_row = LANES // half
    inv_freq = 1.0 / (ROPE_THETA ** (jnp.arange(0, ROPE, 2, dtype=F32) / ROPE))
    pos_dense = jnp.broadcast_to(positions.reshape(s, 1), (s, half)).reshape(s // per_row, LANES)
    invf = jnp.tile(inv_freq, per_row).reshape(1, LANES)
    rows = s // per_row
    tr = rows // 4
    cos_d, sin_d = pl.pallas_call(
        _rope_body,
        grid=(rows // tr,),
        in_specs=[pl.BlockSpec((tr, LANES), lambda i: (i, 0)), _full((1, LANES))],
        out_specs=[pl.BlockSpec((tr, LANES), lambda i: (i, 0))] * 2,
        out_shape=[jax.ShapeDtypeStruct((rows, LANES), F32)] * 2,
        compiler_params=_params("arbitrary"),
        name="rope_tables",
    )(pos_dense, invf)
    cos = cos_d.reshape(s, half)
    sin = sin_d.reshape(s, half)
    ones = jnp.ones((s, NOPE), F32)
    zeros = jnp.zeros((s, NOPE), F32)
    pad = jnp.zeros((s, HEAD_PAD - NOPE - ROPE), F32)
    cos_t = jnp.concatenate([ones, cos, cos, pad], axis=1)
    sin_t = jnp.concatenate([zeros, sin, sin, pad], axis=1)
    return cos_t, sin_t


def _gmlp_branch(u, v, ga, lng_ref, lnb_ref, ws_ref, bias_ref, wa_ref):
    tm = u.shape[0]
    gu = _gelu(u)
    gv = _gelu(v)
    mu = jnp.mean(gv, axis=-1, keepdims=True)
    xc = gv - mu
    vn = xc * lax.rsqrt(jnp.mean(xc * xc, axis=-1, keepdims=True) + EPS) * lng_ref[...] + lnb_ref[...]
    vb = vn.astype(BF16)
    t_idx = lax.broadcasted_iota(jnp.int32, (CHUNK, CHUNK), 0)
    s_idx = lax.broadcasted_iota(jnp.int32, (CHUNK, CHUNK), 1)
    causal = s_idx <= t_idx
    ws = [jnp.where(causal, ws_ref[g], 0.0).astype(BF16) for g in range(GMLP_GROUPS)]
    left = lax.broadcasted_iota(jnp.int32, (CHUNK, LANES), 1) < GMLP_GROUP_DIM
    bias = bias_ref[...]
    z_rows = []
    for c in range(tm // CHUNK):
        vc = vb[c * CHUNK:(c + 1) * CHUNK]
        z_cols = []
        for j in range(GMLP_WIDTH // LANES):
            vp = vc[:, j * LANES:(j + 1) * LANES]
            z_cols.append(jnp.where(left, _dot(ws[2 * j], vp), _dot(ws[2 * j + 1], vp)))
        z_rows.append(jnp.concatenate(z_cols, axis=1) + bias)
    z = jnp.concatenate(z_rows, axis=0)
    gated = (gu * z).astype(BF16)
    return jax.nn.sigmoid(ga) * _dot(gated, wa_ref[...])


def _mla_qkv(cq, ckv, kr, cos_ref, sin_ref, qn_ref, kvn_ref, wqm_ref, wqs_ref, wk_ref, wvt_ref, vone_ref,
             q_ref, k_ref, vt_ref, scale):
    cos = cos_ref[...]
    sin = sin_ref[...]
    cqn = _rms(cq, qn_ref[...]).astype(BF16)
    qm = _dot(cqn, wqm_ref[...])
    qs = _dot(cqn, wqs_ref[...])
    ckn = _rms(ckv, kvn_ref[...]).astype(BF16)
    km = _dot(ckn, wk_ref[...])
    kpe = kr[:, :HEAD_PAD] * cos + kr[:, HEAD_PAD:] * sin
    for h in range(HEADS):
        sl = slice(h * HEAD_PAD, (h + 1) * HEAD_PAD)
        q_ref[:, sl] = ((qm[:, sl] * cos + qs[:, sl] * sin) * scale).astype(q_ref.dtype)
        k_ref[:, sl] = (km[:, sl] + kpe).astype(k_ref.dtype)
    vt = lax.dot_general(wvt_ref[...], ckn, (((1,), (1,)), ((), ())), preferred_element_type=F32)
    vt_ref[...] = (vt + vone_ref[...]).astype(vt_ref.dtype)


def _front_body(x_ref, g_ref, sc_ref, sh_ref, w_ref, lng_ref, lnb_ref, ws_ref, bias_ref, wa_ref,
                cos_ref, sin_ref, qn_ref, kvn_ref, wqm_ref, wqs_ref, wk_ref, wvt_ref, vone_ref,
                a_ref, gb_ref, q_ref, k_ref, vt_ref, *, scale):
    h = _rms(x_ref[...], g_ref[...]) * (1.0 + sc_ref[...]) + sh_ref[...]
    proj = _dot(h.astype(BF16), w_ref[...])
    d = x_ref.shape[1]
    cuts = np.cumsum([0, GMLP_WIDTH, GMLP_WIDTH, Q_RANK, KV_RANK, 2 * HEAD_PAD, d, d])
    u, v, cq, ckv, kr, ga, gb = [proj[:, lo:hi] for lo, hi in zip(cuts[:-1], cuts[1:])]
    gb_ref[...] = gb.astype(gb_ref.dtype)
    a_ref[...] = _gmlp_branch(u, v, ga, lng_ref, lnb_ref, ws_ref, bias_ref, wa_ref).astype(a_ref.dtype)
    _mla_qkv(cq, ckv, kr, cos_ref, sin_ref, qn_ref, kvn_ref, wqm_ref, wqs_ref, wk_ref, wvt_ref, vone_ref,
             q_ref, k_ref, vt_ref, scale)


def _mixer_front(x, g, sc, sh, w, lng, lnb, ws, bias, wa, cos_t, sin_t, qn, kvn, wqm, wqs, wk, wvt, vone):
    s, d = x.shape
    tm = ROW_TILE
    width = HEADS * HEAD_PAD
    row = lambda n: pl.BlockSpec((tm, n), lambda i: (i, 0))
    scale = float((NOPE + ROPE) ** -0.5 * np.log2(np.e))
    consts = (g, sc, sh, w, lng, lnb, ws, bias, wa)
    mla_consts = (qn, kvn, wqm, wqs, wk, wvt, vone)
    return pl.pallas_call(
        functools.partial(_front_body, scale=scale),
        grid=(s // tm,),
        in_specs=[row(d)] + [_full(c.shape) for c in consts] + [row(HEAD_PAD), row(HEAD_PAD)]
                 + [_full(c.shape) for c in mla_consts],
        out_specs=[row(d), row(d), row(width), row(width), pl.BlockSpec((wvt.shape[0], tm), lambda i: (0, i))],
        out_shape=[jax.ShapeDtypeStruct((s, d), BF16), jax.ShapeDtypeStruct((s, d), BF16),
                   jax.ShapeDtypeStruct((s, width), BF16), jax.ShapeDtypeStruct((s, width), BF16),
                   jax.ShapeDtypeStruct((wvt.shape[0], s), BF16)],
        compiler_params=_params("arbitrary"),
        name="mixer_front",
    )(x, *consts, cos_t, sin_t, *mla_consts)


def _attn_body(q_ref, k_ref, vt_ref, o_ref, s_ref, mx_ref, m_ref, acc_ref):
    t = o_ref.shape[0]
    heads = q_ref.shape[1] // HEAD_PAD
    qi = pl.program_id(1)
    m_ref[...] = jnp.full(m_ref.shape, NEG, F32)
    acc_ref[...] = jnp.zeros(acc_ref.shape, F32)

    def scores(blk, slot, q_tile):
        off = pl.multiple_of(blk * t, t)
        q_off = pl.multiple_of(q_tile * t, t)
        for h in range(heads):
            sl = slice(h * HEAD_PAD, (h + 1) * HEAD_PAD)
            s = lax.dot_general(k_ref[pl.ds(off, t), sl], q_ref[pl.ds(q_off, t), sl],
                                (((1,), (1,)), ((), ())), preferred_element_type=F32)
            s_ref[slot, h] = s
            mx_ref[slot, h] = jnp.broadcast_to(jnp.max(s, axis=0, keepdims=True), (STAT_ROWS, t))

    def consume(blk, slot, mask):
        off = pl.multiple_of(blk * t, t)
        for h in range(heads):
            sl = slice(h * HEAD_PAD, (h + 1) * HEAD_PAD)
            s = s_ref[slot, h]
            if mask is None:
                mx = mx_ref[slot, h]
            else:
                s = jnp.where(mask, s, NEG)
                mx = jnp.max(s, axis=0, keepdims=True)
            m_prev = m_ref[h]
            m_new = jnp.maximum(m_prev, mx)
            alpha = jnp.exp2(m_prev - m_new)
            pv = alpha[0:1] * acc_ref[h]
            half = t // 2
            for part in range(2):
                p = jnp.exp2(s[part * half:(part + 1) * half] - m_new[0:1]).astype(BF16)
                keys = pl.ds(pl.multiple_of(off + part * half, half), half)
                pv = pv + _dot(vt_ref[h * VT_ROWS:(h + 1) * VT_ROWS, keys], p)
            acc_ref[h] = pv
            m_ref[h] = m_new

    @pl.when(qi == 0)
    def _():
        scores(0, 0, qi)

    def run(blk, n):
        for u in range(n):
            scores(blk + u + 1, (u + 1) % 2, qi)
            consume(blk + u, u % 2, None)

    def trip(j, carry):
        run(ATTN_UNROLL * j, ATTN_UNROLL)
        return carry

    lax.fori_loop(0, qi // ATTN_UNROLL, trip, 0)
    done = (qi // ATTN_UNROLL) * ATTN_UNROLL
    for n in [ATTN_UNROLL >> k for k in range(1, ATTN_UNROLL.bit_length() - 1)]:
        more = qi - done >= n

        @pl.when(more)
        def _():
            run(done, n)

        done = done + jnp.where(more, n, 0)

    row = lax.broadcasted_iota(jnp.int32, (t, t), 0)
    col = lax.broadcasted_iota(jnp.int32, (t, t), 1)
    causal = row <= col
    odd = lax.rem(qi, 2) == 1

    @pl.when(odd)
    def _():
        scores(qi, 1, qi)
        consume(qi - 1, 0, None)
        consume(qi, 1, causal)

    @pl.when(jnp.logical_not(odd))
    def _():
        consume(qi, 0, causal)

    def finalize():
        lower = lax.broadcasted_iota(jnp.int32, (t, HEAD_PAD), 1) < VDIM
        for j in range(heads // 2):
            outs = []
            for h in (2 * j, 2 * j + 1):
                acc = acc_ref[h]
                out_t = acc / acc[VDIM:VDIM + 1]
                if VT_ROWS < HEAD_PAD:
                    out_t = jnp.concatenate([out_t, jnp.zeros((HEAD_PAD - VT_ROWS, t), F32)], axis=0)
                outs.append(out_t.T)
            packed = jnp.where(lower, outs[0], pltpu.roll(outs[1], VDIM, 1))
            o_ref[:, j * HEAD_PAD:(j + 1) * HEAD_PAD] = packed.astype(o_ref.dtype)

    last = qi == pl.num_programs(1) - 1

    @pl.when(jnp.logical_not(last))
    def _():
        scores(0, 0, qi + 1)
        finalize()

    @pl.when(last)
    def _():
        finalize()


def _attention(q, k, vt):
    s, width = q.shape
    t = ATTN_TILE
    gw = ATTN_HEADS * HEAD_PAD
    return pl.pallas_call(
        _attn_body,
        grid=(width // gw, s // t),
        in_specs=[pl.BlockSpec((s, gw), lambda h, i: (0, h)),
                  pl.BlockSpec((s, gw), lambda h, i: (0, h)),
                  pl.BlockSpec((ATTN_HEADS * VT_ROWS, s), lambda h, i: (h, 0))],
        out_specs=pl.BlockSpec((t, ATTN_HEADS * VDIM), lambda h, i: (i, h)),
        out_shape=jax.ShapeDtypeStruct((s, HEADS * VDIM), BF16),
        scratch_shapes=[pltpu.VMEM((2, ATTN_HEADS, t, t), F32),
                        pltpu.VMEM((2, ATTN_HEADS, STAT_ROWS, t), F32),
                        pltpu.VMEM((ATTN_HEADS, STAT_ROWS, t), F32),
                        pltpu.VMEM((ATTN_HEADS, VT_ROWS, t), F32)],
        compiler_params=_params("arbitrary", "arbitrary"),
        name="mla_attention",
    )(q, k, vt)


MERGE_ARGS = 8


def _merge_math(a_ref, gb_ref, o_ref, x_ref, wb_ref, wo_ref, gp_ref, gm_ref):
    yb = _dot(o_ref[...], wb_ref[...])
    merged = a_ref[...].astype(F32) + jax.nn.sigmoid(gb_ref[...].astype(F32)) * yb
    y = _dot(merged.astype(BF16), wo_ref[...])
    return x_ref[...] + gm_ref[...] * _rms(y, gp_ref[...])


def _merge_specs(a, gb, o, x, wb, wo, gp, gm):
    tm = ROW_TILE
    d = x.shape[1]
    row = pl.BlockSpec((tm, d), lambda i: (i, 0))
    return [row, row, pl.BlockSpec((tm, o.shape[1]), lambda i: (i, 0)), row,
            _full(wb.shape), _full(wo.shape), _full((1, d)), _full((1, d))]


def _ffn_body(*refs):
    g_ref, sc_ref, sh_ref, wg_ref, wu_ref, wd_ref, gp_ref, gf_ref, o_ref = refs[MERGE_ARGS:]
    x = _merge_math(*refs[:MERGE_ARGS])
    hb = (_rms(x, g_ref[...]) * (1.0 + sc_ref[...]) + sh_ref[...]).astype(BF16)
    a = _dot(hb, wg_ref[...])
    b = _dot(hb, wu_ref[...])
    y = _dot((a * jax.nn.sigmoid(a) * b).astype(BF16), wd_ref[...])
    o_ref[...] = x + gf_ref[...] * _rms(y, gp_ref[...])


def _dense_ffn(merge_args, g, sc, sh, wg, wu, wd, gp, gf):
    s, d = merge_args[3].shape
    tm = ROW_TILE
    row = pl.BlockSpec((tm, d), lambda i: (i, 0))
    vec = _full((1, d))
    return pl.pallas_call(
        _ffn_body,
        grid=(s // tm,),
        in_specs=_merge_specs(*merge_args)
                 + [vec, vec, vec, _full(wg.shape), _full(wu.shape), _full(wd.shape), vec, vec],
        out_specs=row,
        out_shape=jax.ShapeDtypeStruct((s, d), F32),
        compiler_params=_params("arbitrary"),
        name="merge_dense_ffn",
    )(*merge_args, g, sc, sh, wg, wu, wd, gp, gf)


def _route_body(*refs):
    g_ref, sc_ref, sh_ref, wr_ref, br_ref, x_ref, hb_ref, info_ref, infot_ref, cnt_ref = refs[MERGE_ARGS:]
    i = pl.program_id(0)
    tm = x_ref.shape[0]
    x = _merge_math(*refs[:MERGE_ARGS])
    x_ref[...] = x
    h = _rms(x, g_ref[...]) * (1.0 + sc_ref[...]) + sh_ref[...]
    hb = h.astype(BF16)
    hb_ref[...] = hb
    h_lo = (h - hb.astype(F32)).astype(BF16)
    w = wr_ref[...]
    w_hi = w.astype(BF16)
    w_lo = (w - w_hi.astype(F32)).astype(BF16)
    logits = _dot(hb, w_hi) + (_dot(h_lo, w_hi) + _dot(hb, w_lo)) + br_ref[...]

    lane = lax.broadcasted_iota(jnp.int32, (tm, LANES), 1)
    m1 = jnp.max(logits, axis=-1, keepdims=True)
    i1 = jnp.min(jnp.where(logits == m1, lane, LANES), axis=-1, keepdims=True)
    oh1 = lane == i1
    rest = jnp.where(oh1, -3e38, logits)
    m2 = jnp.max(rest, axis=-1, keepdims=True)
    i2 = jnp.min(jnp.where(rest == m2, lane, LANES), axis=-1, keepdims=True)
    oh2 = lane == i2
    ex = jnp.exp(m2 - m1)
    g1 = 1.0 / (1.0 + ex)
    g2 = ex / (1.0 + ex)

    ohf = jnp.where(oh1 | oh2, 1.0, 0.0)
    r_idx = lax.broadcasted_iota(jnp.int32, (tm, tm), 0)
    c_idx = lax.broadcasted_iota(jnp.int32, (tm, tm), 1)
    earlier = jnp.where(c_idx < r_idx, 1.0, 0.0).astype(BF16)
    rank = _dot(earlier, ohf.astype(BF16))
    rank1 = jnp.sum(jnp.where(oh1, rank, 0.0), axis=-1, keepdims=True)
    rank2 = jnp.sum(jnp.where(oh2, rank, 0.0), axis=-1, keepdims=True)
    info = jnp.where(lane == 0, i1.astype(F32),
           jnp.where(lane == 1, i2.astype(F32),
           jnp.where(lane == 2, g1,
           jnp.where(lane == 3, g2,
           jnp.where(lane == 4, rank1,
           jnp.where(lane == 5, rank2, 0.0))))))
    info_ref[...] = info
    infot_ref[...] = info.T[:ROUTE_FIELDS]
    cnt = jnp.sum(ohf, axis=0, keepdims=True).astype(jnp.int32)
    cnt_al = ((cnt + (BF16_ROWS - 1)) // BF16_ROWS) * BF16_ROWS
    for e in range(N_EXPERTS):
        cnt_ref[i * N_EXPERTS + e] = cnt_al[0, e]


def _route(merge_args, g, sc, sh, wr, br):
    s, d = merge_args[3].shape
    tm = ROW_TILE
    nt = s // tm
    row = pl.BlockSpec((tm, d), lambda i: (i, 0))
    return pl.pallas_call(
        _route_body,
        grid=(nt,),
        in_specs=_merge_specs(*merge_args)
                 + [_full((1, d)), _full((1, d)), _full((1, d)), _full(wr.shape), _full(br.shape)],
        out_specs=[row, row,
                   pl.BlockSpec((tm, LANES), lambda i: (i, 0)),
                   pl.BlockSpec((ROUTE_FIELDS, tm), lambda i: (0, i)),
                   pl.BlockSpec(memory_space=pltpu.SMEM)],
        out_shape=[jax.ShapeDtypeStruct((s, d), F32),
                   jax.ShapeDtypeStruct((s, d), BF16),
                   jax.ShapeDtypeStruct((s, LANES), F32),
                   jax.ShapeDtypeStruct((ROUTE_FIELDS, s), F32),
                   jax.ShapeDtypeStruct((nt * N_EXPERTS,), jnp.int32)],
        compiler_params=_params("arbitrary"),
        name="merge_moe_route",
    )(*merge_args, g, sc, sh, wr, br)


def _routing(info):
    return [(info[:, k:k + 1], info[:, 2 + k:3 + k], info[:, 4 + k:5 + k]) for k in range(2)]


def _dispatch_body(offs_ref, cnts_ref, fill_ref, nval_ref, hb_ref, infot_ref, xs_ref, xbuf, zbuf, sem, zsem, *, nblk):
    i = pl.program_id(0)
    tm = hb_ref.shape[0]
    tb = zbuf.shape[0]

    def zero_block(blk):
        cp = pltpu.make_async_copy(zbuf, xs_ref.at[pl.ds(pl.multiple_of(blk * tb, tb), tb)], zsem)
        cp.start()
        cp.wait()

    @pl.when(i == 0)
    def _():
        zbuf[...] = jnp.zeros(zbuf.shape, zbuf.dtype)
        for n in range(fill_ref.shape[0]):
            @pl.when(fill_ref[n] >= 0)
            def _():
                zero_block(fill_ref[n])

        def unused(blk, carry):
            zero_block(blk)
            return carry

        lax.fori_loop(nval_ref[0], nblk, unused, 0)

    hb = hb_ref[...]
    fields = infot_ref[...]
    e1, e2, r1, r2 = fields[0:1], fields[1:2], fields[4:5], fields[5:6]
    chunks = tm // DISPATCH_ROWS
    buf = lax.rem(i, 2)

    def first_slot(e, r):
        return jnp.where(r < float(FIRST_ROWS), e * float(FIRST_ROWS) + r, -1.0)

    p1 = first_slot(e1, r1)
    p2 = first_slot(e2, r2)
    slot = lax.broadcasted_iota(jnp.int32, (N_EXPERTS * FIRST_ROWS, tm), 0).astype(F32)
    sel = jnp.where((p1 == slot) | (p2 == slot), 1.0, 0.0).astype(BF16)
    xbuf[buf, 0] = _dot(sel, hb).astype(BF16)

    first_chunks = FIRST_ROWS // DISPATCH_ROWS

    def buf_rows(e, c):
        return c // first_chunks, e * FIRST_ROWS + (c % first_chunks) * DISPATCH_ROWS

    for e in range(N_EXPERTS):
        for c in range(first_chunks, chunks):
            @pl.when(cnts_ref[i * N_EXPERTS + e] > c * DISPATCH_ROWS)
            def _():
                first = e1 == float(e)
                rank = jnp.where(first, r1, r2)
                late = (lax.broadcasted_iota(jnp.int32, (DISPATCH_ROWS, tm), 0) + c * DISPATCH_ROWS).astype(F32)
                hit = (rank == late) & (first | (e2 == float(e)))
                part, row0 = buf_rows(e, c)
                xbuf[buf, part, row0:row0 + DISPATCH_ROWS] = _dot(jnp.where(hit, 1.0, 0.0).astype(BF16),
                                                                  hb).astype(BF16)

    def copy(b, e, c, dst):
        part, row0 = buf_rows(e, c)
        return pltpu.make_async_copy(xbuf.at[b, part, pl.ds(row0, DISPATCH_ROWS)],
                                     xs_ref.at[pl.ds(dst, DISPATCH_ROWS)], sem.at[b, e, c])

    def for_each_copy(step, fn):
        for e in range(N_EXPERTS):
            for c in range(chunks):
                @pl.when(cnts_ref[step * N_EXPERTS + e] > c * DISPATCH_ROWS)
                def _():
                    fn(e, c)

    @pl.when(i > 0)
    def _():
        for_each_copy(i - 1, lambda e, c: copy(1 - buf, e, c, 0).wait())

    def start(e, c):
        off = offs_ref[i * N_EXPERTS + e]
        copy(buf, e, c, pl.multiple_of(off + c * DISPATCH_ROWS, BF16_ROWS)).start()

    for_each_copy(i, start)

    @pl.when(i == pl.num_programs(0) - 1)
    def _():
        for_each_copy(i, lambda e, c: copy(buf, e, c, 0).wait())


def _dispatch(offs, cnts, fill, nval, hb, infot, nblk):
    s, d = hb.shape
    tm = ROW_TILE
    tb = EXPERT_ROWS
    grid_spec = pltpu.PrefetchScalarGridSpec(
        num_scalar_prefetch=4,
        grid=(s // tm,),
        in_specs=[pl.BlockSpec((tm, d), lambda i, *_: (i, 0)),
                  pl.BlockSpec((ROUTE_FIELDS, tm), lambda i, *_: (0, i))],
        out_specs=pl.BlockSpec(memory_space=pl.ANY),
        scratch_shapes=[pltpu.VMEM((2, tm // FIRST_ROWS, N_EXPERTS * FIRST_ROWS, d), BF16),
                        pltpu.VMEM((tb, d), BF16),
                        pltpu.SemaphoreType.DMA((2, N_EXPERTS, tm // DISPATCH_ROWS)),
                        pltpu.SemaphoreType.DMA(())],
    )
    return pl.pallas_call(
        functools.partial(_dispatch_body, nblk=nblk),
        grid_spec=grid_spec,
        out_shape=jax.ShapeDtypeStruct((nblk * tb, d), BF16),
        compiler_params=_params("arbitrary"),
        name="moe_dispatch",
    )(offs, cnts, fill, nval, hb, infot)


def _expert_body(bexp_ref, nval_ref, x_ref, w1_ref, w3_ref, w2_ref, o_ref, acc_ref, *, nff):
    i = pl.program_id(0)
    f = pl.program_id(1)
    used = i < nval_ref[0]

    def ff_slice(first, final):
        xb = x_ref[...]
        a = _dot(xb, w1_ref[0])
        b = _dot(xb, w3_ref[0])
        acc = _dot((a * jax.nn.sigmoid(a) * b).astype(BF16), w2_ref[0])
        if not first:
            acc = acc_ref[...] + acc
        if final:
            o_ref[...] = acc.astype(o_ref.dtype)
        else:
            acc_ref[...] = acc

    for step in sorted({0, min(1, nff - 1), nff - 1}):
        pick = f == step if step in (0, nff - 1) else (f > 0) & (f < nff - 1)

        @pl.when(used & pick)
        def _():
            ff_slice(step == 0, step == nff - 1)

    @pl.when(jnp.logical_not(used) & (f == nff - 1))
    def _():
        o_ref[...] = jnp.zeros(o_ref.shape, o_ref.dtype)


def _experts(xs, w1, w3, w2, bexp, nval, tf):
    rows, d = xs.shape
    ff = w1.shape[2]
    tb = EXPERT_ROWS
    nff = ff // tf

    def xrow(i, f, be, nv):
        return (jnp.minimum(i, nv[0] - 1), 0)

    def fcol(i, f, nv):
        return jnp.where(i < nv[0], f, nff - 1)

    grid_spec = pltpu.PrefetchScalarGridSpec(
        num_scalar_prefetch=2,
        grid=(rows // tb, nff),
        in_specs=[pl.BlockSpec((tb, d), xrow),
                  pl.BlockSpec((1, d, tf), lambda i, f, be, nv: (be[i], 0, fcol(i, f, nv))),
                  pl.BlockSpec((1, d, tf), lambda i, f, be, nv: (be[i], 0, fcol(i, f, nv))),
                  pl.BlockSpec((1, tf, d), lambda i, f, be, nv: (be[i], fcol(i, f, nv), 0))],
        out_specs=pl.BlockSpec((tb, d), lambda i, f, be, nv: (i, 0)),
        scratch_shapes=[pltpu.VMEM((tb, d), F32)],
    )
    return pl.pallas_call(
        functools.partial(_expert_body, nff=nff),
        grid_spec=grid_spec,
        out_shape=jax.ShapeDtypeStruct((rows, d), BF16),
        compiler_params=_params("arbitrary", "arbitrary"),
        name="moe_experts",
    )(bexp, nval, xs, w1, w3, w2)


def _combine_body(offs_ref, cnts_ref, info_ref, x_ref, gp_ref, gf_ref, ys_ref, o_ref, ybuf, acc_ref, sem):
    i = pl.program_id(0)
    nt = pl.num_programs(0)
    tm = x_ref.shape[0]
    chunks = tm // COMBINE_ROWS
    buf = lax.rem(i, 2)

    def copy(b, e, c, src):
        dst = ybuf.at[b, pl.ds((c * N_EXPERTS + e) * COMBINE_ROWS, COMBINE_ROWS)]
        return pltpu.make_async_copy(ys_ref.at[pl.ds(src, COMBINE_ROWS)], dst, sem.at[b, e, c])

    def fetch(step, b):
        for e in range(N_EXPERTS):
            off = offs_ref[step * N_EXPERTS + e]
            copy(b, e, 0, pl.multiple_of(off, BF16_ROWS)).start()
            for c in range(1, chunks):
                @pl.when(cnts_ref[step * N_EXPERTS + e] > c * COMBINE_ROWS)
                def _():
                    copy(b, e, c, pl.multiple_of(off + c * COMBINE_ROWS, BF16_ROWS)).start()

    @pl.when(i == 0)
    def _():
        fetch(0, 0)

    @pl.when(i + 1 < nt)
    def _():
        fetch(i + 1, 1 - buf)

    (e1, g1, r1), (e2, g2, r2) = _routing(info_ref[...])
    for e in range(N_EXPERTS):
        copy(buf, e, 0, 0).wait()
    stack = N_EXPERTS * COMBINE_ROWS
    slot = lax.broadcasted_iota(jnp.int32, (tm, stack), 1).astype(F32)
    y_first = ybuf[buf, 0:stack]
    acc = None
    for ek, gk, rk in ((e1, g1, r1), (e2, g2, r2)):
        pos = jnp.where(rk < float(COMBINE_ROWS), ek * float(COMBINE_ROWS) + rk, -1.0)
        term = gk * _dot(jnp.where(pos == slot, 1.0, 0.0).astype(BF16), y_first)
        acc = term if acc is None else acc + term
    acc_ref[...] = acc

    late = lax.broadcasted_iota(jnp.int32, (tm, COMBINE_ROWS), 1).astype(F32)
    for e in range(N_EXPERTS):
        for c in range(1, chunks):
            @pl.when(cnts_ref[i * N_EXPERTS + e] > c * COMBINE_ROWS)
            def _():
                first = e1 == float(e)
                second = e2 == float(e)
                rank = jnp.where(first, r1, r2)
                gate = jnp.where(first, g1, jnp.where(second, g2, 0.0))
                copy(buf, e, c, 0).wait()
                hit = (rank == late + float(c * COMBINE_ROWS)) & (first | second)
                rows = ybuf[buf, pl.ds((c * N_EXPERTS + e) * COMBINE_ROWS, COMBINE_ROWS)]
                acc_ref[...] += gate * _dot(jnp.where(hit, 1.0, 0.0).astype(BF16), rows)

    o_ref[...] = x_ref[...] + gf_ref[...] * _rms(acc_ref[...], gp_ref[...])


def _combine(offs, cnts, info, x, gp, gf, ys):
    s, d = x.shape
    tm = ROW_TILE
    chunks = tm // COMBINE_ROWS
    grid_spec = pltpu.PrefetchScalarGridSpec(
        num_scalar_prefetch=2,
        grid=(s // tm,),
        in_specs=[pl.BlockSpec((tm, LANES), lambda i, *_: (i, 0)),
                  pl.BlockSpec((tm, d), lambda i, *_: (i, 0)),
                  pl.BlockSpec((1, d), lambda i, *_: (0, 0)),
                  pl.BlockSpec((1, d), lambda i, *_: (0, 0)),
                  pl.BlockSpec(memory_space=pl.ANY)],
        out_specs=pl.BlockSpec((tm, d), lambda i, *_: (i, 0)),
        scratch_shapes=[pltpu.VMEM((2, chunks * N_EXPERTS * COMBINE_ROWS, d), BF16),
                        pltpu.VMEM((tm, d), F32),
                        pltpu.SemaphoreType.DMA((2, N_EXPERTS, chunks))],
    )
    return pl.pallas_call(
        _combine_body,
        grid_spec=grid_spec,
        out_shape=jax.ShapeDtypeStruct((s, d), F32),
        compiler_params=_params("arbitrary"),
        name="moe_combine",
    )(offs, cnts, info, x, gp, gf, ys)


def _rot_half_cols(w):
    half = ROPE // 2
    return jnp.concatenate([-w[..., half:], w[..., :half]], axis=-1)


def _pad_cols(w, before, total):
    return jnp.pad(w, ((0, 0), (before, total - before - w.shape[1])))


def _mixer_weights(w_in, w_uq, w_ukv, w_branch_b):
    d = w_in.shape[0]
    o = 0
    parts = {}
    for name, n in (("u", GMLP_WIDTH), ("v", GMLP_WIDTH), ("cq", Q_RANK), ("ckv", KV_RANK),
                    ("kr", ROPE), ("ga", d), ("gb", d)):
        parts[name] = w_in[:, o:o + n]
        o += n
    kr_main = _pad_cols(parts["kr"], NOPE, HEAD_PAD)
    kr_swap = _pad_cols(_rot_half_cols(parts["kr"]), NOPE, HEAD_PAD)
    w_in_p = jnp.concatenate([parts["u"], parts["v"], parts["cq"], parts["ckv"], kr_main, kr_swap,
                              parts["ga"], parts["gb"]], axis=1).astype(BF16)

    wq = w_uq.reshape(Q_RANK, HEADS, NOPE + ROPE)
    zq = jnp.zeros((Q_RANK, HEADS, HEAD_PAD - NOPE - ROPE), w_uq.dtype)
    wq_main = jnp.concatenate([wq, zq], axis=-1).reshape(Q_RANK, HEADS * HEAD_PAD).astype(BF16)
    wq_swap = jnp.concatenate([jnp.zeros((Q_RANK, HEADS, NOPE), w_uq.dtype),
                               _rot_half_cols(wq[..., NOPE:]), zq], axis=-1)
    wq_swap = wq_swap.reshape(Q_RANK, HEADS * HEAD_PAD).astype(BF16)

    wkv = w_ukv.reshape(KV_RANK, HEADS, NOPE + VDIM)
    zk = jnp.zeros((KV_RANK, HEADS, HEAD_PAD - NOPE), w_ukv.dtype)
    wk = jnp.concatenate([wkv[..., :NOPE], zk], axis=-1).reshape(KV_RANK, HEADS * HEAD_PAD).astype(BF16)
    zv = jnp.zeros((KV_RANK, HEADS, VT_ROWS - VDIM), w_ukv.dtype)
    wv = jnp.concatenate([wkv[..., NOPE:], zv], axis=-1).reshape(KV_RANK, HEADS * VT_ROWS).T.astype(BF16)

    return w_in_p, wq_main, wq_swap, wk, wv, w_branch_b.astype(BF16)


def _expert_layout(cnts, nt, nblk):
    tb = EXPERT_ROWS
    c = cnts.reshape(nt, N_EXPERTS)
    total = jnp.sum(c, axis=0)
    per = jnp.where(total > 0, (total + REGION_SLACK + tb - 1) // tb, 0)
    stop = jnp.cumsum(per)
    start = stop - per
    offs = start[None, :] * tb + (jnp.cumsum(c, axis=0) - c)
    nval = stop[-1]
    blk = jnp.minimum(jnp.arange(nblk, dtype=jnp.int32), nval - 1)
    bexp = jnp.minimum(jnp.sum(blk[:, None] >= stop[None, :], axis=1), N_EXPERTS - 1)
    fill = jnp.concatenate([jnp.where(per > 0, stop - 1, -1), jnp.where(per > 1, stop - 2, -1)])
    i32 = lambda v: v.astype(jnp.int32)
    return i32(offs.reshape(-1)), i32(bexp), i32(nval.reshape(1)), i32(fill)


def kernel(x, c, positions, ada_w, ada_b, norm_mix_pre, norm_mix_post, norm_ffn_pre, norm_ffn_post, w_in, gmlp_ln_g, gmlp_ln_b, gmlp_ws, gmlp_bs, mla_q_norm, mla_w_uq, mla_kv_norm, mla_w_ukv, w_branch_a, w_branch_b, w_out, ffn_w_gate, ffn_w_up, ffn_w_down, moe_router, moe_router_bias, moe_w1, moe_w3, moe_w2):
    batch, s, d = x.shape
    assert batch == 1 and s % ROW_TILE == 0 and s % ATTN_TILE == 0
    depth = ada_w.shape[0]
    xs = x.reshape(s, d)
    mod = _ada_mod(c, ada_w, ada_b)
    cos_t, sin_t = _rope_tables(positions)
    vone = jnp.zeros((HEADS, VT_ROWS), F32).at[:, VDIM].set(1.0).reshape(HEADS * VT_ROWS, 1)
    row = lambda v: v.reshape(1, -1)

    for l in range(depth):
        sh_m, sc_m, g_m, sh_f, sc_f, g_f = [mod[l, :, k * d:(k + 1) * d] for k in range(6)]
        w_in_p, wq_main, wq_swap, wk, wv, wb = _mixer_weights(w_in[l], mla_w_uq[l], mla_w_ukv[l], w_branch_b[l])
        bias = jnp.broadcast_to(gmlp_bs[l].T[:, :, None], (CHUNK, GMLP_GROUPS, GMLP_GROUP_DIM))
        bias = bias.reshape(CHUNK, GMLP_WIDTH)
        a, gb, q, k, vt = _mixer_front(xs, row(norm_mix_pre[l]), sc_m, sh_m, w_in_p,
                                       row(gmlp_ln_g[l]), row(gmlp_ln_b[l]), gmlp_ws[l], bias,
                                       w_branch_a[l].astype(BF16), cos_t, sin_t,
                                       row(mla_q_norm[l]), row(mla_kv_norm[l]), wq_main, wq_swap, wk, wv, vone)
        o = _attention(q, k, vt)
        merge_args = (a, gb, o, xs, wb, w_out[l].astype(BF16), row(norm_mix_post[l]), g_m)

        j = l // 2
        if l % 2 == 0:
            ff = ffn_w_gate.shape[2]
            ff_pad = -(-ff // (2 * LANES)) * (2 * LANES)
            wg = jnp.pad(ffn_w_gate[j], ((0, 0), (0, ff_pad - ff))).astype(BF16)
            wu = jnp.pad(ffn_w_up[j], ((0, 0), (0, ff_pad - ff))).astype(BF16)
            wd = jnp.pad(ffn_w_down[j], ((0, ff_pad - ff), (0, 0))).astype(BF16)
            xs = _dense_ffn(merge_args, row(norm_ffn_pre[l]), sc_f, sh_f, wg, wu, wd, row(norm_ffn_post[l]), g_f)
        else:
            nt = s // ROW_TILE
            tb = EXPERT_ROWS
            max_rows = 2 * s + nt * N_EXPERTS * (BF16_ROWS - 1) + N_EXPERTS * (REGION_SLACK + tb - 1)
            nblk = -(-max_rows // tb) + 1
            wr = jnp.pad(moe_router[j], ((0, 0), (0, LANES - N_EXPERTS)))
            br = jnp.pad(moe_router_bias[j], (0, LANES - N_EXPERTS), constant_values=NEG).reshape(1, LANES)
            xs, hb, info, infot, cnts = _route(merge_args, row(norm_ffn_pre[l]), sc_f, sh_f, wr, br)
            offs, bexp, nval, fill = _expert_layout(cnts, nt, nblk)
            xsort = _dispatch(offs, cnts, fill, nval, hb, infot, nblk)
            ysort = _experts(xsort, moe_w1[j].astype(BF16), moe_w3[j].astype(BF16), moe_w2[j].astype(BF16),
                             bexp, nval, tf=moe_w1.shape[3] // 2)
            xs = _combine(offs, cnts, info, xs, row(norm_ffn_post[l]), g_f, ysort)
    return xs.reshape(batch, s, d)
```

```python
import functools

import jax
import jax.numpy as jnp
import numpy as np
from jax import lax
from jax.experimental import pallas as pl
from jax.experimental.pallas import tpu as pltpu

F32 = jnp.float32
BF16 = jnp.bfloat16

EPS = 1e-6
LANES = 128
BF16_ROWS = 16
GMLP_GROUPS = 8
GMLP_GROUP_DIM = 64
GMLP_WIDTH = GMLP_GROUPS * GMLP_GROUP_DIM
CHUNK = 128
HEADS = 8
NOPE = 64
ROPE = 32
VDIM = 64
HEAD_PAD = 128
VT_ROWS = 128
Q_RANK = 384
KV_RANK = 256
ROPE_THETA = 10000.0
N_EXPERTS = 8
NEG = -1e30

ROW_TILE = 512
ATTN_TILE = 512
ATTN_HEADS = 2
STAT_ROWS = 8
EXPERT_ROWS = 512
DISPATCH_ROWS = 128
COMBINE_ROWS = 256
FIRST_ROWS = 256
ROUTE_FIELDS = 8
REGION_SLACK = DISPATCH_ROWS - BF16_ROWS
VMEM_MIB = {"ada_mod": 24, "rope_tables": 16, "mixer_front": 48, "mla_attention": 58, "merge_dense_ffn": 56,
            "merge_moe_route": 32, "moe_dispatch": 32, "moe_experts": 48, "moe_combine": 40}


def _params(name, *sem):
    return pltpu.CompilerParams(dimension_semantics=sem, vmem_limit_bytes=VMEM_MIB[name] * 1024 * 1024)


def _dot(a, b):
    return jnp.dot(a, b, preferred_element_type=F32)


def _rms(x, g):
    return x * lax.rsqrt(jnp.mean(x * x, axis=-1, keepdims=True) + EPS) * g


def _gelu(x):
    return 0.5 * x * (1.0 + lax.erf(x * np.float32(0.7071067811865476)))


def _full(shape):
    return pl.BlockSpec(shape, lambda *_: (0,) * len(shape), pipeline_mode=pl.Buffered(1))


def _mod_body(c_ref, w_ref, b_ref, o_ref):
    c = c_ref[...]
    ca = c * jax.nn.sigmoid(c)
    o_ref[0] = jnp.sum(ca * w_ref[0], axis=0, keepdims=True) + b_ref[0]


def _ada_mod(c, ada_w, ada_b):
    n_layers, d, n = ada_w.shape
    tn = n // 4
    return pl.pallas_call(
        _mod_body,
        grid=(n_layers, n // tn),
        in_specs=[pl.BlockSpec((d, 1), lambda l, j: (0, 0)),
                  pl.BlockSpec((1, d, tn), lambda l, j: (l, 0, j)),
                  pl.BlockSpec((1, 1, tn), lambda l, j: (l, 0, j))],
        out_specs=pl.BlockSpec((1, 1, tn), lambda l, j: (l, 0, j)),
        out_shape=jax.ShapeDtypeStruct((n_layers, 1, n), F32),
        compiler_params=_params("ada_mod", "arbitrary", "arbitrary"),
        name="ada_mod",
    )(c.reshape(d, 1), ada_w, ada_b.reshape(n_layers, 1, n))


def _rope_body(pos_ref, invf_ref, cos_ref, sin_ref):
    ang = pos_ref[...].astype(F32) * invf_ref[...]
    cos_ref[...] = jnp.cos(ang)
    sin_ref[...] = jnp.sin(ang)


def _rope_tables(positions):
    s = positions.shape[-1]
    half = ROPE // 2
    per_row = LANES // half
    inv_freq = 1.0 / (ROPE_THETA ** (jnp.arange(0, ROPE, 2, dtype=F32) / ROPE))
    pos_dense = jnp.broadcast_to(positions.reshape(s, 1), (s, half)).reshape(s // per_row, LANES)
    invf = jnp.tile(inv_freq, per_row).reshape(1, LANES)
    rows = s // per_row
    tr = rows // 4
    cos_d, sin_d = pl.pallas_call(
        _rope_body,
        grid=(rows // tr,),
        in_specs=[pl.BlockSpec((tr, LANES), lambda i: (i, 0)), _full((1, LANES))],
        out_specs=[pl.BlockSpec((tr, LANES), lambda i: (i, 0))] * 2,
        out_shape=[jax.ShapeDtypeStruct((rows, LANES), F32)] * 2,
        compiler_params=_params("rope_tables", "arbitrary"),
        name="rope_tables",
    )(pos_dense, invf)
    cos = cos_d.reshape(s, half)
    sin = sin_d.reshape(s, half)
    ones = jnp.ones((s, NOPE), F32)
    zeros = jnp.zeros((s, NOPE), F32)
    pad = jnp.zeros((s, HEAD_PAD - NOPE - ROPE), F32)
    cos_t = jnp.concatenate([ones, cos, cos, pad], axis=1)
    sin_t = jnp.concatenate([zeros, sin, sin, pad], axis=1)
    return cos_t, sin_t


def _gmlp_branch(u, v, ga, lng_ref, lnb_ref, ws_ref, bias_ref, wa_ref):
    tm = u.shape[0]
    gu = _gelu(u)
    gv = _gelu(v)
    mu = jnp.mean(gv, axis=-1, keepdims=True)
    xc = gv - mu
    vn = xc * lax.rsqrt(jnp.mean(xc * xc, axis=-1, keepdims=True) + EPS) * lng_ref[...] + lnb_ref[...]
    vb = vn.astype(BF16)
    t_idx = lax.broadcasted_iota(jnp.int32, (CHUNK, CHUNK), 0)
    s_idx = lax.broadcasted_iota(jnp.int32, (CHUNK, CHUNK), 1)
    causal = s_idx <= t_idx
    ws = [jnp.where(causal, ws_ref[g], 0.0).astype(BF16) for g in range(GMLP_GROUPS)]
    left = lax.broadcasted_iota(jnp.int32, (CHUNK, LANES), 1) < GMLP_GROUP_DIM
    bias = bias_ref[...]
    z_rows = []
    for c in range(tm // CHUNK):
        vc = vb[c * CHUNK:(c + 1) * CHUNK]
        z_cols = []
        for j in range(GMLP_WIDTH // LANES):
            vp = vc[:, j * LANES:(j + 1) * LANES]
            z_cols.append(jnp.where(left, _dot(ws[2 * j], vp), _dot(ws[2 * j + 1], vp)))
        z_rows.append(jnp.concatenate(z_cols, axis=1) + bias)
    z = jnp.concatenate(z_rows, axis=0)
    gated = (gu * z).astype(BF16)
    return jax.nn.sigmoid(ga) * _dot(gated, wa_ref[...])


def _mla_qkv(cq, ckv, kr, cos_ref, sin_ref, qn_ref, kvn_ref, wqm_ref, wqs_ref, wk_ref, wvt_ref, vone_ref,
             q_ref, k_ref, vt_ref, scale):
    cos = cos_ref[...]
    sin = sin_ref[...]
    cqn = _rms(cq, qn_ref[...]).astype(BF16)
    qm = _dot(cqn, wqm_ref[...])
    qs = _dot(cqn, wqs_ref[...])
    ckn = _rms(ckv, kvn_ref[...]).astype(BF16)
    km = _dot(ckn, wk_ref[...])
    kpe = kr[:, :HEAD_PAD] * cos + kr[:, HEAD_PAD:] * sin
    for h in range(HEADS):
        sl = slice(h * HEAD_PAD, (h + 1) * HEAD_PAD)
        q_ref[:, sl] = ((qm[:, sl] * cos + qs[:, sl] * sin) * scale).astype(q_ref.dtype)
        k_ref[:, sl] = (km[:, sl] + kpe).astype(k_ref.dtype)
    vt = lax.dot_general(wvt_ref[...], ckn, (((1,), (1,)), ((), ())), preferred_element_type=F32)
    vt_ref[...] = (vt + vone_ref[...]).astype(vt_ref.dtype)


def _front_body(x_ref, g_ref, sc_ref, sh_ref, w_ref, lng_ref, lnb_ref, ws_ref, bias_ref, wa_ref,
                cos_ref, sin_ref, qn_ref, kvn_ref, wqm_ref, wqs_ref, wk_ref, wvt_ref, vone_ref,
                a_ref, gb_ref, q_ref, k_ref, vt_ref, *, scale):
    h = _rms(x_ref[...], g_ref[...]) * (1.0 + sc_ref[...]) + sh_ref[...]
    proj = _dot(h.astype(BF16), w_ref[...])
    d = x_ref.shape[1]
    cuts = np.cumsum([0, GMLP_WIDTH, GMLP_WIDTH, Q_RANK, KV_RANK, 2 * HEAD_PAD, d, d])
    u, v, cq, ckv, kr, ga, gb = [proj[:, lo:hi] for lo, hi in zip(cuts[:-1], cuts[1:])]
    gb_ref[...] = gb.astype(gb_ref.dtype)
    a_ref[...] = _gmlp_branch(u, v, ga, lng_ref, lnb_ref, ws_ref, bias_ref, wa_ref).astype(a_ref.dtype)
    _mla_qkv(cq, ckv, kr, cos_ref, sin_ref, qn_ref, kvn_ref, wqm_ref, wqs_ref, wk_ref, wvt_ref, vone_ref,
             q_ref, k_ref, vt_ref, scale)


def _mixer_front(x, g, sc, sh, w, lng, lnb, ws, bias, wa, cos_t, sin_t, qn, kvn, wqm, wqs, wk, wvt, vone):
    s, d = x.shape
    tm = ROW_TILE
    width = HEADS * HEAD_PAD
    row = lambda n: pl.BlockSpec((tm, n), lambda i: (i, 0))
    scale = float((NOPE + ROPE) ** -0.5 * np.log2(np.e))
    consts = (g, sc, sh, w, lng, lnb, ws, bias, wa)
    mla_consts = (qn, kvn, wqm, wqs, wk, wvt, vone)
    return pl.pallas_call(
        functools.partial(_front_body, scale=scale),
        grid=(s // tm,),
        in_specs=[row(d)] + [_full(c.shape) for c in consts] + [row(HEAD_PAD), row(HEAD_PAD)]
                 + [_full(c.shape) for c in mla_consts],
        out_specs=[row(d), row(d), row(width), row(width), pl.BlockSpec((wvt.shape[0], tm), lambda i: (0, i))],
        out_shape=[jax.ShapeDtypeStruct((s, d), BF16), jax.ShapeDtypeStruct((s, d), BF16),
                   jax.ShapeDtypeStruct((s, width), BF16), jax.ShapeDtypeStruct((s, width), BF16),
                   jax.ShapeDtypeStruct((wvt.shape[0], s), BF16)],
        compiler_params=_params("mixer_front", "arbitrary"),
        name="mixer_front",
    )(x, *consts, cos_t, sin_t, *mla_consts)


def _attn_body(q_ref, k_ref, vt_ref, o_ref, s_ref, mx_ref, m_ref, acc_ref):
    t = o_ref.shape[0]
    heads = q_ref.shape[1] // HEAD_PAD
    qi = pl.program_id(1)
    m_ref[...] = jnp.full(m_ref.shape, NEG, F32)
    acc_ref[...] = jnp.zeros(acc_ref.shape, F32)

    def scores(blk, slot, q_tile):
        off = pl.multiple_of(blk * t, t)
        q_off = pl.multiple_of(q_tile * t, t)
        for h in range(heads):
            sl = slice(h * HEAD_PAD, (h + 1) * HEAD_PAD)
            s = lax.dot_general(k_ref[pl.ds(off, t), sl], q_ref[pl.ds(q_off, t), sl],
                                (((1,), (1,)), ((), ())), preferred_element_type=F32)
            s_ref[slot, h] = s
            mx_ref[slot, h] = jnp.broadcast_to(jnp.max(s, axis=0, keepdims=True), (STAT_ROWS, t))

    def consume(blk, slot, mask):
        off = pl.multiple_of(blk * t, t)
        for h in range(heads):
            sl = slice(h * HEAD_PAD, (h + 1) * HEAD_PAD)
            s = s_ref[slot, h]
            if mask is None:
                mx = mx_ref[slot, h]
            else:
                s = jnp.where(mask, s, NEG)
                mx = jnp.max(s, axis=0, keepdims=True)
            m_prev = m_ref[h]
            m_new = jnp.maximum(m_prev, mx)
            alpha = jnp.exp2(m_prev - m_new)
            pv = alpha[0:1] * acc_ref[h]
            half = t // 2
            for part in range(2):
                p = jnp.exp2(s[part * half:(part + 1) * half] - m_new[0:1]).astype(BF16)
                keys = pl.ds(pl.multiple_of(off + part * half, half), half)
                pv = pv + _dot(vt_ref[h * VT_ROWS:(h + 1) * VT_ROWS, keys], p)
            acc_ref[h] = pv
            m_ref[h] = m_new

    @pl.when(qi == 0)
    def _():
        scores(0, 0, qi)

    def run(blk, n):
        for u in range(n):
            scores(blk + u + 1, (u + 1) % 2, qi)
            consume(blk + u, u % 2, None)

    def octet(j, carry):
        run(8 * j, 8)
        return carry

    lax.fori_loop(0, qi // 8, octet, 0)
    done = (qi // 8) * 8
    for n in (4, 2):
        more = qi - done >= n

        @pl.when(more)
        def _():
            run(done, n)

        done = done + jnp.where(more, n, 0)

    row = lax.broadcasted_iota(jnp.int32, (t, t), 0)
    col = lax.broadcasted_iota(jnp.int32, (t, t), 1)
    causal = row <= col
    odd = lax.rem(qi, 2) == 1

    @pl.when(odd)
    def _():
        scores(qi, 1, qi)
        consume(qi - 1, 0, None)
        consume(qi, 1, causal)

    @pl.when(jnp.logical_not(odd))
    def _():
        consume(qi, 0, causal)

    def finalize():
        lower = lax.broadcasted_iota(jnp.int32, (t, HEAD_PAD), 1) < VDIM
        for j in range(heads // 2):
            outs = []
            for h in (2 * j, 2 * j + 1):
                acc = acc_ref[h]
                out_t = acc / acc[VDIM:VDIM + 1]
                if VT_ROWS < HEAD_PAD:
                    out_t = jnp.concatenate([out_t, jnp.zeros((HEAD_PAD - VT_ROWS, t), F32)], axis=0)
                outs.append(out_t.T)
            packed = jnp.where(lower, outs[0], pltpu.roll(outs[1], VDIM, 1))
            o_ref[:, j * HEAD_PAD:(j + 1) * HEAD_PAD] = packed.astype(o_ref.dtype)

    last = qi == pl.num_programs(1) - 1

    @pl.when(jnp.logical_not(last))
    def _():
        scores(0, 0, qi + 1)
        finalize()

    @pl.when(last)
    def _():
        finalize()


def _attention(q, k, vt):
    s, width = q.shape
    t = ATTN_TILE
    gw = ATTN_HEADS * HEAD_PAD
    return pl.pallas_call(
        _attn_body,
        grid=(width // gw, s // t),
        in_specs=[pl.BlockSpec((s, gw), lambda h, i: (0, h)),
                  pl.BlockSpec((s, gw), lambda h, i: (0, h)),
                  pl.BlockSpec((ATTN_HEADS * VT_ROWS, s), lambda h, i: (h, 0))],
        out_specs=pl.BlockSpec((t, ATTN_HEADS * VDIM), lambda h, i: (i, h)),
        out_shape=jax.ShapeDtypeStruct((s, HEADS * VDIM), BF16),
        scratch_shapes=[pltpu.VMEM((2, ATTN_HEADS, t, t), F32),
                        pltpu.VMEM((2, ATTN_HEADS, STAT_ROWS, t), F32),
                        pltpu.VMEM((ATTN_HEADS, STAT_ROWS, t), F32),
                        pltpu.VMEM((ATTN_HEADS, VT_ROWS, t), F32)],
        compiler_params=_params("mla_attention", "arbitrary", "arbitrary"),
        name="mla_attention",
    )(q, k, vt)


MERGE_ARGS = 8


def _merge_math(a_ref, gb_ref, o_ref, x_ref, wb_ref, wo_ref, gp_ref, gm_ref):
    yb = _dot(o_ref[...], wb_ref[...])
    merged = a_ref[...].astype(F32) + jax.nn.sigmoid(gb_ref[...].astype(F32)) * yb
    y = _dot(merged.astype(BF16), wo_ref[...])
    return x_ref[...] + gm_ref[...] * _rms(y, gp_ref[...])


def _merge_specs(a, gb, o, x, wb, wo, gp, gm):
    tm = ROW_TILE
    d = x.shape[1]
    row = pl.BlockSpec((tm, d), lambda i: (i, 0))
    return [row, row, pl.BlockSpec((tm, o.shape[1]), lambda i: (i, 0)), row,
            _full(wb.shape), _full(wo.shape), _full((1, d)), _full((1, d))]


def _ffn_body(*refs):
    g_ref, sc_ref, sh_ref, wg_ref, wu_ref, wd_ref, gp_ref, gf_ref, o_ref = refs[MERGE_ARGS:]
    x = _merge_math(*refs[:MERGE_ARGS])
    hb = (_rms(x, g_ref[...]) * (1.0 + sc_ref[...]) + sh_ref[...]).astype(BF16)
    a = _dot(hb, wg_ref[...])
    b = _dot(hb, wu_ref[...])
    y = _dot((a * jax.nn.sigmoid(a) * b).astype(BF16), wd_ref[...])
    o_ref[...] = x + gf_ref[...] * _rms(y, gp_ref[...])


def _dense_ffn(merge_args, g, sc, sh, wg, wu, wd, gp, gf):
    s, d = merge_args[3].shape
    tm = ROW_TILE
    row = pl.BlockSpec((tm, d), lambda i: (i, 0))
    vec = _full((1, d))
    return pl.pallas_call(
        _ffn_body,
        grid=(s // tm,),
        in_specs=_merge_specs(*merge_args)
                 + [vec, vec, vec, _full(wg.shape), _full(wu.shape), _full(wd.shape), vec, vec],
        out_specs=row,
        out_shape=jax.ShapeDtypeStruct((s, d), F32),
        compiler_params=_params("merge_dense_ffn", "arbitrary"),
        name="merge_dense_ffn",
    )(*merge_args, g, sc, sh, wg, wu, wd, gp, gf)


def _route_body(*refs):
    g_ref, sc_ref, sh_ref, wr_ref, br_ref, x_ref, hb_ref, info_ref, infot_ref, cnt_ref = refs[MERGE_ARGS:]
    i = pl.program_id(0)
    tm = x_ref.shape[0]
    x = _merge_math(*refs[:MERGE_ARGS])
    x_ref[...] = x
    h = _rms(x, g_ref[...]) * (1.0 + sc_ref[...]) + sh_ref[...]
    hb = h.astype(BF16)
    hb_ref[...] = hb
    h_lo = (h - hb.astype(F32)).astype(BF16)
    w = wr_ref[...]
    w_hi = w.astype(BF16)
    w_lo = (w - w_hi.astype(F32)).astype(BF16)
    logits = _dot(hb, w_hi) + (_dot(h_lo, w_hi) + _dot(hb, w_lo)) + br_ref[...]

    lane = lax.broadcasted_iota(jnp.int32, (tm, LANES), 1)
    m1 = jnp.max(logits, axis=-1, keepdims=True)
    i1 = jnp.min(jnp.where(logits == m1, lane, LANES), axis=-1, keepdims=True)
    oh1 = lane == i1
    rest = jnp.where(oh1, -3e38, logits)
    m2 = jnp.max(rest, axis=-1, keepdims=True)
    i2 = jnp.min(jnp.where(rest == m2, lane, LANES), axis=-1, keepdims=True)
    oh2 = lane == i2
    ex = jnp.exp(m2 - m1)
    g1 = 1.0 / (1.0 + ex)
    g2 = ex / (1.0 + ex)

    ohf = jnp.where(oh1 | oh2, 1.0, 0.0)
    r_idx = lax.broadcasted_iota(jnp.int32, (tm, tm), 0)
    c_idx = lax.broadcasted_iota(jnp.int32, (tm, tm), 1)
    earlier = jnp.where(c_idx < r_idx, 1.0, 0.0).astype(BF16)
    rank = _dot(earlier, ohf.astype(BF16))
    rank1 = jnp.sum(jnp.where(oh1, rank, 0.0), axis=-1, keepdims=True)
    rank2 = jnp.sum(jnp.where(oh2, rank, 0.0), axis=-1, keepdims=True)
    info = jnp.where(lane == 0, i1.astype(F32),
           jnp.where(lane == 1, i2.astype(F32),
           jnp.where(lane == 2, g1,
           jnp.where(lane == 3, g2,
           jnp.where(lane == 4, rank1,
           jnp.where(lane == 5, rank2, 0.0))))))
    info_ref[...] = info
    infot_ref[...] = info.T[:ROUTE_FIELDS]
    cnt = jnp.sum(ohf, axis=0, keepdims=True).astype(jnp.int32)
    cnt_al = ((cnt + (BF16_ROWS - 1)) // BF16_ROWS) * BF16_ROWS
    for e in range(N_EXPERTS):
        cnt_ref[i * N_EXPERTS + e] = cnt_al[0, e]


def _route(merge_args, g, sc, sh, wr, br):
    s, d = merge_args[3].shape
    tm = ROW_TILE
    nt = s // tm
    row = pl.BlockSpec((tm, d), lambda i: (i, 0))
    return pl.pallas_call(
        _route_body,
        grid=(nt,),
        in_specs=_merge_specs(*merge_args)
                 + [_full((1, d)), _full((1, d)), _full((1, d)), _full(wr.shape), _full(br.shape)],
        out_specs=[row, row,
                   pl.BlockSpec((tm, LANES), lambda i: (i, 0)),
                   pl.BlockSpec((ROUTE_FIELDS, tm), lambda i: (0, i)),
                   pl.BlockSpec(memory_space=pltpu.SMEM)],
        out_shape=[jax.ShapeDtypeStruct((s, d), F32),
                   jax.ShapeDtypeStruct((s, d), BF16),
                   jax.ShapeDtypeStruct((s, LANES), F32),
                   jax.ShapeDtypeStruct((ROUTE_FIELDS, s), F32),
                   jax.ShapeDtypeStruct((nt * N_EXPERTS,), jnp.int32)],
        compiler_params=_params("merge_moe_route", "arbitrary"),
        name="merge_moe_route",
    )(*merge_args, g, sc, sh, wr, br)


def _routing(info):
    return [(info[:, k:k + 1], info[:, 2 + k:3 + k], info[:, 4 + k:5 + k]) for k in range(2)]


def _dispatch_body(offs_ref, cnts_ref, fill_ref, nval_ref, hb_ref, infot_ref, xs_ref, xbuf, zbuf, sem, zsem, *, nblk):
    i = pl.program_id(0)
    tm = hb_ref.shape[0]
    tb = zbuf.shape[0]

    def zero_block(blk):
        cp = pltpu.make_async_copy(zbuf, xs_ref.at[pl.ds(pl.multiple_of(blk * tb, tb), tb)], zsem)
        cp.start()
        cp.wait()

    @pl.when(i == 0)
    def _():
        zbuf[...] = jnp.zeros(zbuf.shape, zbuf.dtype)
        for n in range(fill_ref.shape[0]):
            @pl.when(fill_ref[n] >= 0)
            def _():
                zero_block(fill_ref[n])

        def unused(blk, carry):
            zero_block(blk)
            return carry

        lax.fori_loop(nval_ref[0], nblk, unused, 0)

    hb = hb_ref[...]
    fields = infot_ref[...]
    e1, e2, r1, r2 = fields[0:1], fields[1:2], fields[4:5], fields[5:6]
    chunks = tm // DISPATCH_ROWS
    buf = lax.rem(i, 2)

    def first_slot(e, r):
        return jnp.where(r < float(FIRST_ROWS), e * float(FIRST_ROWS) + r, -1.0)

    p1 = first_slot(e1, r1)
    p2 = first_slot(e2, r2)
    slot = lax.broadcasted_iota(jnp.int32, (N_EXPERTS * FIRST_ROWS, tm), 0).astype(F32)
    sel = jnp.where((p1 == slot) | (p2 == slot), 1.0, 0.0).astype(BF16)
    xbuf[buf, 0] = _dot(sel, hb).astype(BF16)

    first_chunks = FIRST_ROWS // DISPATCH_ROWS

    def buf_rows(e, c):
        return c // first_chunks, e * FIRST_ROWS + (c % first_chunks) * DISPATCH_ROWS

    for e in range(N_EXPERTS):
        for c in range(first_chunks, chunks):
            @pl.when(cnts_ref[i * N_EXPERTS + e] > c * DISPATCH_ROWS)
            def _():
                first = e1 == float(e)
                rank = jnp.where(first, r1, r2)
                late = (lax.broadcasted_iota(jnp.int32, (DISPATCH_ROWS, tm), 0) + c * DISPATCH_ROWS).astype(F32)
                hit = (rank == late) & (first | (e2 == float(e)))
                part, row0 = buf_rows(e, c)
                xbuf[buf, part, row0:row0 + DISPATCH_ROWS] = _dot(jnp.where(hit, 1.0, 0.0).astype(BF16),
                                                                  hb).astype(BF16)

    def copy(b, e, c, dst):
        part, row0 = buf_rows(e, c)
        return pltpu.make_async_copy(xbuf.at[b, part, pl.ds(row0, DISPATCH_ROWS)],
                                     xs_ref.at[pl.ds(dst, DISPATCH_ROWS)], sem.at[b, e, c])

    def for_each_copy(step, fn):
        for e in range(N_EXPERTS):
            for c in range(chunks):
                @pl.when(cnts_ref[step * N_EXPERTS + e] > c * DISPATCH_ROWS)
                def _():
                    fn(e, c)

    @pl.when(i > 0)
    def _():
        for_each_copy(i - 1, lambda e, c: copy(1 - buf, e, c, 0).wait())

    def start(e, c):
        off = offs_ref[i * N_EXPERTS + e]
        copy(buf, e, c, pl.multiple_of(off + c * DISPATCH_ROWS, BF16_ROWS)).start()

    for_each_copy(i, start)

    @pl.when(i == pl.num_programs(0) - 1)
    def _():
        for_each_copy(i, lambda e, c: copy(buf, e, c, 0).wait())


def _dispatch(offs, cnts, fill, nval, hb, infot, nblk):
    s, d = hb.shape
    tm = ROW_TILE
    tb = EXPERT_ROWS
    grid_spec = pltpu.PrefetchScalarGridSpec(
        num_scalar_prefetch=4,
        grid=(s // tm,),
        in_specs=[pl.BlockSpec((tm, d), lambda i, *_: (i, 0)),
                  pl.BlockSpec((ROUTE_FIELDS, tm), lambda i, *_: (0, i))],
        out_specs=pl.BlockSpec(memory_space=pl.ANY),
        scratch_shapes=[pltpu.VMEM((2, tm // FIRST_ROWS, N_EXPERTS * FIRST_ROWS, d), BF16),
                        pltpu.VMEM((tb, d), BF16),
                        pltpu.SemaphoreType.DMA((2, N_EXPERTS, tm // DISPATCH_ROWS)),
                        pltpu.SemaphoreType.DMA(())],
    )
    return pl.pallas_call(
        functools.partial(_dispatch_body, nblk=nblk),
        grid_spec=grid_spec,
        out_shape=jax.ShapeDtypeStruct((nblk * tb, d), BF16),
        compiler_params=_params("moe_dispatch", "arbitrary"),
        name="moe_dispatch",
    )(offs, cnts, fill, nval, hb, infot)


def _expert_body(bexp_ref, nval_ref, x_ref, w1_ref, w3_ref, w2_ref, o_ref, acc_ref, *, nff):
    i = pl.program_id(0)
    f = pl.program_id(1)
    used = i < nval_ref[0]

    def ff_slice(first, final):
        xb = x_ref[...]
        a = _dot(xb, w1_ref[0])
        b = _dot(xb, w3_ref[0])
        acc = _dot((a * jax.nn.sigmoid(a) * b).astype(BF16), w2_ref[0].astype(BF16))
        if not first:
            acc = acc_ref[...] + acc
        if final:
            o_ref[...] = acc.astype(o_ref.dtype)
        else:
            acc_ref[...] = acc

    variants = [(f == 0, True, nff == 1)]
    if nff > 2:
        variants.append(((f > 0) & (f < nff - 1), False, False))
    if nff > 1:
        variants.append((f == nff - 1, False, True))
    for pick, first, final in variants:
        @pl.when(used & pick)
        def _():
            ff_slice(first, final)

    @pl.when(jnp.logical_not(used) & (f == nff - 1))
    def _():
        o_ref[...] = jnp.zeros(o_ref.shape, o_ref.dtype)


def _experts(xs, w1, w3, w2, bexp, nval, tf):
    rows, d = xs.shape
    ff = w1.shape[2]
    tb = EXPERT_ROWS
    nff = ff // tf

    def xrow(i, f, be, nv):
        return (jnp.minimum(i, nv[0] - 1), 0)

    def fcol(i, f, nv):
        return jnp.where(i < nv[0], f, nff - 1)

    grid_spec = pltpu.PrefetchScalarGridSpec(
        num_scalar_prefetch=2,
        grid=(rows // tb, nff),
        in_specs=[pl.BlockSpec((tb, d), xrow),
                  pl.BlockSpec((1, d, tf), lambda i, f, be, nv: (be[i], 0, fcol(i, f, nv))),
                  pl.BlockSpec((1, d, tf), lambda i, f, be, nv: (be[i], 0, fcol(i, f, nv))),
                  pl.BlockSpec((1, tf, d), lambda i, f, be, nv: (be[i], fcol(i, f, nv), 0))],
        out_specs=pl.BlockSpec((tb, d), lambda i, f, be, nv: (i, 0)),
        scratch_shapes=[pltpu.VMEM((tb, d), F32)],
    )
    return pl.pallas_call(
        functools.partial(_expert_body, nff=nff),
        grid_spec=grid_spec,
        out_shape=jax.ShapeDtypeStruct((rows, d), BF16),
        compiler_params=_params("moe_experts", "arbitrary", "arbitrary"),
        name="moe_experts",
    )(bexp, nval, xs, w1, w3, w2)


def _combine_body(offs_ref, cnts_ref, info_ref, x_ref, gp_ref, gf_ref, ys_ref, o_ref, ybuf, acc_ref, sem):
    i = pl.program_id(0)
    nt = pl.num_programs(0)
    tm = x_ref.shape[0]
    chunks = tm // COMBINE_ROWS
    buf = lax.rem(i, 2)

    def copy(b, e, c, src):
        dst = ybuf.at[b, pl.ds((c * N_EXPERTS + e) * COMBINE_ROWS, COMBINE_ROWS)]
        return pltpu.make_async_copy(ys_ref.at[pl.ds(src, COMBINE_ROWS)], dst, sem.at[b, e, c])

    def fetch(step, b):
        for e in range(N_EXPERTS):
            off = offs_ref[step * N_EXPERTS + e]
            copy(b, e, 0, pl.multiple_of(off, BF16_ROWS)).start()
            for c in range(1, chunks):
                @pl.when(cnts_ref[step * N_EXPERTS + e] > c * COMBINE_ROWS)
                def _():
                    copy(b, e, c, pl.multiple_of(off + c * COMBINE_ROWS, BF16_ROWS)).start()

    @pl.when(i == 0)
    def _():
        fetch(0, 0)

    @pl.when(i + 1 < nt)
    def _():
        fetch(i + 1, 1 - buf)

    (e1, g1, r1), (e2, g2, r2) = _routing(info_ref[...])
    for e in range(N_EXPERTS):
        copy(buf, e, 0, 0).wait()
    stack = N_EXPERTS * COMBINE_ROWS
    slot = lax.broadcasted_iota(jnp.int32, (tm, stack), 1).astype(F32)
    y_first = ybuf[buf, 0:stack]
    acc = None
    for ek, gk, rk in ((e1, g1, r1), (e2, g2, r2)):
        pos = jnp.where(rk < float(COMBINE_ROWS), ek * float(COMBINE_ROWS) + rk, -1.0)
        term = gk * _dot(jnp.where(pos == slot, 1.0, 0.0).astype(BF16), y_first)
        acc = term if acc is None else acc + term
    acc_ref[...] = acc

    late = lax.broadcasted_iota(jnp.int32, (tm, COMBINE_ROWS), 1).astype(F32)
    for e in range(N_EXPERTS):
        for c in range(1, chunks):
            @pl.when(cnts_ref[i * N_EXPERTS + e] > c * COMBINE_ROWS)
            def _():
                first = e1 == float(e)
                second = e2 == float(e)
                rank = jnp.where(first, r1, r2)
                gate = jnp.where(first, g1, jnp.where(second, g2, 0.0))
                copy(buf, e, c, 0).wait()
                hit = (rank == late + float(c * COMBINE_ROWS)) & (first | second)
                rows = ybuf[buf, pl.ds((c * N_EXPERTS + e) * COMBINE_ROWS, COMBINE_ROWS)]
                acc_ref[...] += gate * _dot(jnp.where(hit, 1.0, 0.0).astype(BF16), rows)

    o_ref[...] = x_ref[...] + gf_ref[...] * _rms(acc_ref[...], gp_ref[...])


def _combine(offs, cnts, info, x, gp, gf, ys):
    s, d = x.shape
    tm = ROW_TILE
    chunks = tm // COMBINE_ROWS
    grid_spec = pltpu.PrefetchScalarGridSpec(
        num_scalar_prefetch=2,
        grid=(s // tm,),
        in_specs=[pl.BlockSpec((tm, LANES), lambda i, *_: (i, 0)),
                  pl.BlockSpec((tm, d), lambda i, *_: (i, 0)),
                  pl.BlockSpec((1, d), lambda i, *_: (0, 0)),
                  pl.BlockSpec((1, d), lambda i, *_: (0, 0)),
                  pl.BlockSpec(memory_space=pl.ANY)],
        out_specs=pl.BlockSpec((tm, d), lambda i, *_: (i, 0)),
        scratch_shapes=[pltpu.VMEM((2, chunks * N_EXPERTS * COMBINE_ROWS, d), BF16),
                        pltpu.VMEM((tm, d), F32),
                        pltpu.SemaphoreType.DMA((2, N_EXPERTS, chunks))],
    )
    return pl.pallas_call(
        _combine_body,
        grid_spec=grid_spec,
        out_shape=jax.ShapeDtypeStruct((s, d), F32),
        compiler_params=_params("moe_combine", "arbitrary"),
        name="moe_combine",
    )(offs, cnts, info, x, gp, gf, ys)


def _rot_half_cols(w):
    half = ROPE // 2
    return jnp.concatenate([-w[..., half:], w[..., :half]], axis=-1)


def _pad_cols(w, before, total):
    return jnp.pad(w, ((0, 0), (before, total - before - w.shape[1])))


def _mixer_weights(w_in, w_uq, w_ukv, w_branch_b):
    d = w_in.shape[0]
    o = 0
    parts = {}
    for name, n in (("u", GMLP_WIDTH), ("v", GMLP_WIDTH), ("cq", Q_RANK), ("ckv", KV_RANK),
                    ("kr", ROPE), ("ga", d), ("gb", d)):
        parts[name] = w_in[:, o:o + n]
        o += n
    kr_main = _pad_cols(parts["kr"], NOPE, HEAD_PAD)
    kr_swap = _pad_cols(_rot_half_cols(parts["kr"]), NOPE, HEAD_PAD)
    w_in_p = jnp.concatenate([parts["u"], parts["v"], parts["cq"], parts["ckv"], kr_main, kr_swap,
                              parts["ga"], parts["gb"]], axis=1).astype(BF16)

    wq = w_uq.reshape(Q_RANK, HEADS, NOPE + ROPE)
    zq = jnp.zeros((Q_RANK, HEADS, HEAD_PAD - NOPE - ROPE), w_uq.dtype)
    wq_main = jnp.concatenate([wq, zq], axis=-1).reshape(Q_RANK, HEADS * HEAD_PAD).astype(BF16)
    wq_swap = jnp.concatenate([jnp.zeros((Q_RANK, HEADS, NOPE), w_uq.dtype),
                               _rot_half_cols(wq[..., NOPE:]), zq], axis=-1)
    wq_swap = wq_swap.reshape(Q_RANK, HEADS * HEAD_PAD).astype(BF16)

    wkv = w_ukv.reshape(KV_RANK, HEADS, NOPE + VDIM)
    zk = jnp.zeros((KV_RANK, HEADS, HEAD_PAD - NOPE), w_ukv.dtype)
    wk = jnp.concatenate([wkv[..., :NOPE], zk], axis=-1).reshape(KV_RANK, HEADS * HEAD_PAD).astype(BF16)
    zv = jnp.zeros((KV_RANK, HEADS, VT_ROWS - VDIM), w_ukv.dtype)
    wv = jnp.concatenate([wkv[..., NOPE:], zv], axis=-1).reshape(KV_RANK, HEADS * VT_ROWS).T.astype(BF16)

    return w_in_p, wq_main, wq_swap, wk, wv, w_branch_b.astype(BF16)


def _expert_layout(cnts, nt, nblk):
    tb = EXPERT_ROWS
    c = cnts.reshape(nt, N_EXPERTS)
    total = jnp.sum(c, axis=0)
    per = jnp.where(total > 0, (total + REGION_SLACK + tb - 1) // tb, 0)
    stop = jnp.cumsum(per)
    start = stop - per
    offs = start[None, :] * tb + (jnp.cumsum(c, axis=0) - c)
    nval = stop[-1]
    blk = jnp.minimum(jnp.arange(nblk, dtype=jnp.int32), nval - 1)
    bexp = jnp.minimum(jnp.sum(blk[:, None] >= stop[None, :], axis=1), N_EXPERTS - 1)
    fill = jnp.concatenate([jnp.where(per > 0, stop - 1, -1), jnp.where(per > 1, stop - 2, -1)])
    i32 = lambda v: v.astype(jnp.int32)
    return i32(offs.reshape(-1)), i32(bexp), i32(nval.reshape(1)), i32(fill)


def kernel(x, c, positions, ada_w, ada_b, norm_mix_pre, norm_mix_post, norm_ffn_pre, norm_ffn_post, w_in, gmlp_ln_g, gmlp_ln_b, gmlp_ws, gmlp_bs, mla_q_norm, mla_w_uq, mla_kv_norm, mla_w_ukv, w_branch_a, w_branch_b, w_out, ffn_w_gate, ffn_w_up, ffn_w_down, moe_router, moe_router_bias, moe_w1, moe_w3, moe_w2):
    batch, s, d = x.shape
    assert batch == 1 and s % ROW_TILE == 0 and s % ATTN_TILE == 0
    depth = ada_w.shape[0]
    xs = x.reshape(s, d)
    mod = _ada_mod(c, ada_w, ada_b)
    cos_t, sin_t = _rope_tables(positions)
    vone = jnp.zeros((HEADS, VT_ROWS), F32).at[:, VDIM].set(1.0).reshape(HEADS * VT_ROWS, 1)
    row = lambda v: v.reshape(1, -1)

    for l in range(depth):
        sh_m, sc_m, g_m, sh_f, sc_f, g_f = [mod[l, :, k * d:(k + 1) * d] for k in range(6)]
        w_in_p, wq_main, wq_swap, wk, wv, wb = _mixer_weights(w_in[l], mla_w_uq[l], mla_w_ukv[l], w_branch_b[l])
        bias = jnp.broadcast_to(gmlp_bs[l].T[:, :, None], (CHUNK, GMLP_GROUPS, GMLP_GROUP_DIM))
        bias = bias.reshape(CHUNK, GMLP_WIDTH)
        a, gb, q, k, vt = _mixer_front(xs, row(norm_mix_pre[l]), sc_m, sh_m, w_in_p,
                                       row(gmlp_ln_g[l]), row(gmlp_ln_b[l]), gmlp_ws[l], bias,
                                       w_branch_a[l].astype(BF16), cos_t, sin_t,
                                       row(mla_q_norm[l]), row(mla_kv_norm[l]), wq_main, wq_swap, wk, wv, vone)
        o = _attention(q, k, vt)
        merge_args = (a, gb, o, xs, wb, w_out[l].astype(BF16), row(norm_mix_post[l]), g_m)

        j = l // 2
        if l % 2 == 0:
            ff = ffn_w_gate.shape[2]
            ff_pad = -(-ff // (2 * LANES)) * (2 * LANES)
            wg = jnp.pad(ffn_w_gate[j], ((0, 0), (0, ff_pad - ff))).astype(BF16)
            wu = jnp.pad(ffn_w_up[j], ((0, 0), (0, ff_pad - ff))).astype(BF16)
            wd = jnp.pad(ffn_w_down[j], ((0, ff_pad - ff), (0, 0))).astype(BF16)
            xs = _dense_ffn(merge_args, row(norm_ffn_pre[l]), sc_f, sh_f, wg, wu, wd, row(norm_ffn_post[l]), g_f)
        else:
            nt = s // ROW_TILE
            tb = EXPERT_ROWS
            max_rows = 2 * s + nt * N_EXPERTS * (BF16_ROWS - 1) + N_EXPERTS * (REGION_SLACK + tb - 1)
            nblk = -(-max_rows // tb) + 1
            wr = jnp.pad(moe_router[j], ((0, 0), (0, LANES - N_EXPERTS)))
            br = jnp.pad(moe_router_bias[j], (0, LANES - N_EXPERTS), constant_values=NEG).reshape(1, LANES)
            xs, hb, info, infot, cnts = _route(merge_args, row(norm_ffn_pre[l]), sc_f, sh_f, wr, br)
            offs, bexp, nval, fill = _expert_layout(cnts, nt, nblk)
            xsort = _dispatch(offs, cnts, fill, nval, hb, infot, nblk)
            ysort = _experts(xsort, moe_w1[j].astype(BF16), moe_w3[j].astype(BF16), moe_w2[j],
                             bexp, nval, tf=moe_w1.shape[3] // 2)
            xs = _combine(offs, cnts, info, xs, row(norm_ffn_post[l]), g_f, ysort)
    return xs.reshape(batch, s, d)
```

```python
import functools

import jax
import jax.numpy as jnp
import numpy as np
from jax import lax
from jax.experimental import pallas as pl
from jax.experimental.pallas import tpu as pltpu

F32 = jnp.float32
BF16 = jnp.bfloat16

EPS = 1e-6
LANES = 128
BF16_ROWS = 16
GMLP_GROUPS = 8
GMLP_GROUP_DIM = 64
GMLP_WIDTH = GMLP_GROUPS * GMLP_GROUP_DIM
CHUNK = 128
HEADS = 8
NOPE = 64
ROPE = 32
VDIM = 64
HEAD_PAD = 128
VT_ROWS = 128
Q_RANK = 384
KV_RANK = 256
ROPE_THETA = 10000.0
N_EXPERTS = 8
NEG = -1e30

ROW_TILE = 512
ATTN_TILE = 512
ATTN_HEADS = 2
STAT_ROWS = 8
EXPERT_ROWS = 512
DISPATCH_ROWS = 128
COMBINE_ROWS = 256
FIRST_ROWS = 256
ROUTE_FIELDS = 8
REGION_SLACK = DISPATCH_ROWS - BF16_ROWS
VMEM_MIB = {"ada_mod": 24, "rope_tables": 16, "mixer_front": 48, "mla_attention": 58, "merge_dense_ffn": 56,
            "merge_moe_route": 32, "moe_dispatch": 32, "moe_experts": 48, "moe_combine": 40}


def _params(name, *sem):
    return pltpu.CompilerParams(dimension_semantics=sem, vmem_limit_bytes=VMEM_MIB[name] * 1024 * 1024)


def _dot(a, b):
    return jnp.dot(a, b, preferred_element_type=F32)


def _rms(x, g):
    return x * lax.rsqrt(jnp.mean(x * x, axis=-1, keepdims=True) + EPS) * g


def _gelu(x):
    return 0.5 * x * (1.0 + lax.erf(x * np.float32(0.7071067811865476)))


def _full(shape):
    return pl.BlockSpec(shape, lambda *_: (0,) * len(shape), pipeline_mode=pl.Buffered(1))


def _mod_body(c_ref, w_ref, b_ref, o_ref):
    c = c_ref[...]
    ca = jnp.broadcast_to(c * jax.nn.sigmoid(c), (LANES, c.shape[1])).T[:, 0:1]
    o_ref[0] = jnp.sum(ca * w_ref[0], axis=0, keepdims=True) + b_ref[0]


def _ada_mod(c, ada_w, ada_b):
    n_layers, d, n = ada_w.shape
    tn = n // 4
    return pl.pallas_call(
        _mod_body,
        grid=(n_layers, n // tn),
        in_specs=[pl.BlockSpec((1, d), lambda l, j: (0, 0)),
                  pl.BlockSpec((1, d, tn), lambda l, j: (l, 0, j)),
                  pl.BlockSpec((1, 1, tn), lambda l, j: (l, 0, j))],
        out_specs=pl.BlockSpec((1, 1, tn), lambda l, j: (l, 0, j)),
        out_shape=jax.ShapeDtypeStruct((n_layers, 1, n), F32),
        compiler_params=_params("ada_mod", "arbitrary", "arbitrary"),
        name="ada_mod",
    )(c.reshape(1, d), ada_w, ada_b.reshape(n_layers, 1, n))


def _rope_body(pos_ref, invf_ref, cos_ref, sin_ref):
    ang = pos_ref[...].astype(F32) * invf_ref[...]
    cos_ref[...] = jnp.cos(ang)
    sin_ref[...] = jnp.sin(ang)


def _rope_tables(positions):
    s = positions.shape[-1]
    half = ROPE // 2
    per_row = LANES // half
    inv_freq = 1.0 / (ROPE_THETA ** (jnp.arange(0, ROPE, 2, dtype=F32) / ROPE))
    pos_dense = jnp.broadcast_to(positions.reshape(s, 1), (s, half)).reshape(s // per_row, LANES)
    invf = jnp.tile(inv_freq, per_row).reshape(1, LANES)
    rows = s // per_row
    tr = rows // 4
    cos_d, sin_d = pl.pallas_call(
        _rope_body,
        grid=(rows // tr,),
        in_specs=[pl.BlockSpec((tr, LANES), lambda i: (i, 0)), _full((1, LANES))],
        out_specs=[pl.BlockSpec((tr, LANES), lambda i: (i, 0))] * 2,
        out_shape=[jax.ShapeDtypeStruct((rows, LANES), F32)] * 2,
        compiler_params=_params("rope_tables", "arbitrary"),
        name="rope_tables",
    )(pos_dense, invf)
    cos = cos_d.reshape(s, half)
    sin = sin_d.reshape(s, half)
    ones = jnp.ones((s, NOPE), F32)
    zeros = jnp.zeros((s, NOPE), F32)
    pad = jnp.zeros((s, HEAD_PAD - NOPE - ROPE), F32)
    cos_t = jnp.concatenate([ones, cos, cos, pad], axis=1)
    sin_t = jnp.concatenate([zeros, sin, sin, pad], axis=1)
    return cos_t, sin_t


def _gmlp_branch(u, v, ga, lng_ref, lnb_ref, ws_ref, bias_ref, wa_ref):
    tm = u.shape[0]
    gu = _gelu(u)
    gv = _gelu(v)
    mu = jnp.mean(gv, axis=-1, keepdims=True)
    xc = gv - mu
    vn = xc * lax.rsqrt(jnp.mean(xc * xc, axis=-1, keepdims=True) + EPS) * lng_ref[...] + lnb_ref[...]
    vb = vn.astype(BF16)
    t_idx = lax.broadcasted_iota(jnp.int32, (CHUNK, CHUNK), 0)
    s_idx = lax.broadcasted_iota(jnp.int32, (CHUNK, CHUNK), 1)
    causal = s_idx <= t_idx
    ws = [jnp.where(causal, ws_ref[g], 0.0).astype(BF16) for g in range(GMLP_GROUPS)]
    left = lax.broadcasted_iota(jnp.int32, (CHUNK, LANES), 1) < GMLP_GROUP_DIM
    bias = bias_ref[...]
    z_rows = []
    for c in range(tm // CHUNK):
        vc = vb[c * CHUNK:(c + 1) * CHUNK]
        z_cols = []
        for j in range(GMLP_WIDTH // LANES):
            vp = vc[:, j * LANES:(j + 1) * LANES]
            z_cols.append(jnp.where(left, _dot(ws[2 * j], vp), _dot(ws[2 * j + 1], vp)))
        z_rows.append(jnp.concatenate(z_cols, axis=1) + bias)
    z = jnp.concatenate(z_rows, axis=0)
    gated = (gu * z).astype(BF16)
    return jax.nn.sigmoid(ga) * _dot(gated, wa_ref[...])


def _mla_qkv(cq, ckv, kr, cos_ref, sin_ref, qn_ref, kvn_ref, wqm_ref, wqs_ref, wk_ref, wvt_ref, vone_ref,
             q_ref, k_ref, vt_ref, scale):
    cos = cos_ref[...]
    sin = sin_ref[...]
    cqn = _rms(cq, qn_ref[...]).astype(BF16)
    qm = _dot(cqn, wqm_ref[...])
    qs = _dot(cqn, wqs_ref[...])
    ckn = _rms(ckv, kvn_ref[...]).astype(BF16)
    km = _dot(ckn, wk_ref[...])
    kpe = kr[:, :HEAD_PAD] * cos + kr[:, HEAD_PAD:] * sin
    for h in range(HEADS):
        sl = slice(h * HEAD_PAD, (h + 1) * HEAD_PAD)
        q_ref[:, sl] = ((qm[:, sl] * cos + qs[:, sl] * sin) * scale).astype(q_ref.dtype)
        k_ref[:, sl] = (km[:, sl] + kpe).astype(k_ref.dtype)
    vt = lax.dot_general(wvt_ref[...], ckn, (((1,), (1,)), ((), ())), preferred_element_type=F32)
    vt_ref[...] = (vt + jnp.tile(vone_ref[...], (1, vt.shape[1] // LANES))).astype(vt_ref.dtype)


def _front_body(x_ref, g_ref, sc_ref, sh_ref, w_ref, lng_ref, lnb_ref, ws_ref, bias_ref, wa_ref,
                cos_ref, sin_ref, qn_ref, kvn_ref, wqm_ref, wqs_ref, wk_ref, wvt_ref, vone_ref,
                a_ref, gb_ref, q_ref, k_ref, vt_ref, *, scale):
    h = _rms(x_ref[...], g_ref[...]) * (1.0 + sc_ref[...]) + sh_ref[...]
    proj = _dot(h.astype(BF16), w_ref[...])
    d = x_ref.shape[1]
    cuts = np.cumsum([0, GMLP_WIDTH, GMLP_WIDTH, Q_RANK, KV_RANK, 2 * HEAD_PAD, d, d])
    u, v, cq, ckv, kr, ga, gb = [proj[:, lo:hi] for lo, hi in zip(cuts[:-1], cuts[1:])]
    gb_ref[...] = gb.astype(gb_ref.dtype)
    a_ref[...] = _gmlp_branch(u, v, ga, lng_ref, lnb_ref, ws_ref, bias_ref, wa_ref).astype(a_ref.dtype)
    _mla_qkv(cq, ckv, kr, cos_ref, sin_ref, qn_ref, kvn_ref, wqm_ref, wqs_ref, wk_ref, wvt_ref, vone_ref,
             q_ref, k_ref, vt_ref, scale)


def _mixer_front(x, g, sc, sh, w, lng, lnb, ws, bias, wa, cos_t, sin_t, qn, kvn, wqm, wqs, wk, wvt, vone):
    s, d = x.shape
    tm = ROW_TILE
    width = HEADS * HEAD_PAD
    row = lambda n: pl.BlockSpec((tm, n), lambda i: (i, 0))
    scale = float((NOPE + ROPE) ** -0.5 * np.log2(np.e))
    consts = (g, sc, sh, w, lng, lnb, ws, bias, wa)
    mla_consts = (qn, kvn, wqm, wqs, wk, wvt, vone)
    return pl.pallas_call(
        functools.partial(_front_body, scale=scale),
        grid=(s // tm,),
        in_specs=[row(d)] + [_full(c.shape) for c in consts] + [row(HEAD_PAD), row(HEAD_PAD)]
                 + [_full(c.shape) for c in mla_consts],
        out_specs=[row(d), row(d), row(width), row(width), pl.BlockSpec((wvt.shape[0], tm), lambda i: (0, i))],
        out_shape=[jax.ShapeDtypeStruct((s, d), BF16), jax.ShapeDtypeStruct((s, d), BF16),
                   jax.ShapeDtypeStruct((s, width), BF16), jax.ShapeDtypeStruct((s, width), BF16),
                   jax.ShapeDtypeStruct((wvt.shape[0], s), BF16)],
        compiler_params=_params("mixer_front", "arbitrary"),
        name="mixer_front",
    )(x, *consts, cos_t, sin_t, *mla_consts)


def _attn_body(q_ref, k_ref, vt_ref, o_ref, s_ref, mx_ref, m_ref, acc_ref):
    t = o_ref.shape[0]
    heads = q_ref.shape[1] // HEAD_PAD
    qi = pl.program_id(1)
    m_ref[...] = jnp.full(m_ref.shape, NEG, F32)
    acc_ref[...] = jnp.zeros(acc_ref.shape, F32)

    def scores(blk, slot, q_tile):
        off = pl.multiple_of(blk * t, t)
        q_off = pl.multiple_of(q_tile * t, t)
        for h in range(heads):
            sl = slice(h * HEAD_PAD, (h + 1) * HEAD_PAD)
            s = lax.dot_general(k_ref[pl.ds(off, t), sl], q_ref[pl.ds(q_off, t), sl],
                                (((1,), (1,)), ((), ())), preferred_element_type=F32)
            s_ref[slot, h] = s
            mx_ref[slot, h] = jnp.broadcast_to(jnp.max(s, axis=0, keepdims=True), (STAT_ROWS, t))

    def consume(blk, slot, mask):
        off = pl.multiple_of(blk * t, t)
        for h in range(heads):
            sl = slice(h * HEAD_PAD, (h + 1) * HEAD_PAD)
            s = s_ref[slot, h]
            if mask is None:
                mx = mx_ref[slot, h]
            else:
                s = jnp.where(mask, s, NEG)
                mx = jnp.max(s, axis=0, keepdims=True)
            m_prev = m_ref[h]
            m_new = jnp.maximum(m_prev, mx)
            alpha = jnp.exp2(m_prev - m_new)
            pv = alpha[0:1] * acc_ref[h]
            half = t // 2
            for part in range(2):
                p = jnp.exp2(s[part * half:(part + 1) * half] - m_new[0:1]).astype(BF16)
                keys = pl.ds(pl.multiple_of(off + part * half, half), half)
                pv = pv + _dot(vt_ref[h * VT_ROWS:(h + 1) * VT_ROWS, keys], p)
            acc_ref[h] = pv
            m_ref[h] = m_new

    @pl.when(qi == 0)
    def _():
        scores(0, 0, qi)

    def run(blk, n):
        for u in range(n):
            scores(blk + u + 1, (u + 1) % 2, qi)
            consume(blk + u, u % 2, None)

    def octet(j, carry):
        run(8 * j, 8)
        return carry

    lax.fori_loop(0, qi // 8, octet, 0)
    done = (qi // 8) * 8
    for n in (4, 2):
        more = qi - done >= n

        @pl.when(more)
        def _():
            run(done, n)

        done = done + jnp.where(more, n, 0)

    row = lax.broadcasted_iota(jnp.int32, (t, t), 0)
    col = lax.broadcasted_iota(jnp.int32, (t, t), 1)
    causal = row <= col
    odd = lax.rem(qi, 2) == 1

    @pl.when(odd)
    def _():
        scores(qi, 1, qi)
        consume(qi - 1, 0, None)
        consume(qi, 1, causal)

    @pl.when(jnp.logical_not(odd))
    def _():
        consume(qi, 0, causal)

    def finalize():
        lower = lax.broadcasted_iota(jnp.int32, (t, HEAD_PAD), 1) < VDIM
        for j in range(heads // 2):
            outs = []
            for h in (2 * j, 2 * j + 1):
                acc = acc_ref[h]
                out_t = acc / acc[VDIM:VDIM + 1]
                if VT_ROWS < HEAD_PAD:
                    out_t = jnp.concatenate([out_t, jnp.zeros((HEAD_PAD - VT_ROWS, t), F32)], axis=0)
                outs.append(out_t.T)
            packed = jnp.where(lower, outs[0], pltpu.roll(outs[1], VDIM, 1))
            o_ref[:, j * HEAD_PAD:(j + 1) * HEAD_PAD] = packed.astype(o_ref.dtype)

    last = qi == pl.num_programs(1) - 1

    @pl.when(jnp.logical_not(last))
    def _():
        scores(0, 0, qi + 1)
        finalize()

    @pl.when(last)
    def _():
        finalize()


def _attention(q, k, vt):
    s, width = q.shape
    t = ATTN_TILE
    gw = ATTN_HEADS * HEAD_PAD
    return pl.pallas_call(
        _attn_body,
        grid=(width // gw, s // t),
        in_specs=[pl.BlockSpec((s, gw), lambda h, i: (0, h)),
                  pl.BlockSpec((s, gw), lambda h, i: (0, h)),
                  pl.BlockSpec((ATTN_HEADS * VT_ROWS, s), lambda h, i: (h, 0))],
        out_specs=pl.BlockSpec((t, ATTN_HEADS * VDIM), lambda h, i: (i, h)),
        out_shape=jax.ShapeDtypeStruct((s, HEADS * VDIM), BF16),
        scratch_shapes=[pltpu.VMEM((2, ATTN_HEADS, t, t), F32),
                        pltpu.VMEM((2, ATTN_HEADS, STAT_ROWS, t), F32),
                        pltpu.VMEM((ATTN_HEADS, STAT_ROWS, t), F32),
                        pltpu.VMEM((ATTN_HEADS, VT_ROWS, t), F32)],
        compiler_params=_params("mla_attention", "arbitrary", "arbitrary"),
        name="mla_attention",
    )(q, k, vt)


MERGE_ARGS = 8


def _merge_math(a_ref, gb_ref, o_ref, x_ref, wb_ref, wo_ref, gp_ref, gm_ref):
    yb = _dot(o_ref[...], wb_ref[...])
    merged = a_ref[...].astype(F32) + jax.nn.sigmoid(gb_ref[...].astype(F32)) * yb
    y = _dot(merged.astype(BF16), wo_ref[...])
    return x_ref[...] + gm_ref[...] * _rms(y, gp_ref[...])


def _merge_specs(a, gb, o, x, wb, wo, gp, gm):
    tm = ROW_TILE
    d = x.shape[1]
    row = pl.BlockSpec((tm, d), lambda i: (i, 0))
    return [row, row, pl.BlockSpec((tm, o.shape[1]), lambda i: (i, 0)), row,
            _full(wb.shape), _full(wo.shape), _full((1, d)), _full((1, d))]


def _ffn_body(*refs):
    g_ref, sc_ref, sh_ref, wg_ref, wu_ref, wd_ref, gp_ref, gf_ref, o_ref = refs[MERGE_ARGS:]
    x = _merge_math(*refs[:MERGE_ARGS])
    hb = (_rms(x, g_ref[...]) * (1.0 + sc_ref[...]) + sh_ref[...]).astype(BF16)
    a = _dot(hb, wg_ref[...])
    b = _dot(hb, wu_ref[...])
    y = _dot((a * jax.nn.sigmoid(a) * b).astype(BF16), wd_ref[...])
    o_ref[...] = x + gf_ref[...] * _rms(y, gp_ref[...])


def _dense_ffn(merge_args, g, sc, sh, wg, wu, wd, gp, gf):
    s, d = merge_args[3].shape
    tm = ROW_TILE
    row = pl.BlockSpec((tm, d), lambda i: (i, 0))
    vec = _full((1, d))
    return pl.pallas_call(
        _ffn_body,
        grid=(s // tm,),
        in_specs=_merge_specs(*merge_args)
                 + [vec, vec, vec, _full(wg.shape), _full(wu.shape), _full(wd.shape), vec, vec],
        out_specs=row,
        out_shape=jax.ShapeDtypeStruct((s, d), F32),
        compiler_params=_params("merge_dense_ffn", "arbitrary"),
        name="merge_dense_ffn",
    )(*merge_args, g, sc, sh, wg, wu, wd, gp, gf)


def _route_body(*refs):
    g_ref, sc_ref, sh_ref, wr_ref, br_ref, x_ref, hb_ref, info_ref, infot_ref, cnt_ref = refs[MERGE_ARGS:]
    i = pl.program_id(0)
    tm = x_ref.shape[0]
    x = _merge_math(*refs[:MERGE_ARGS])
    x_ref[...] = x
    h = _rms(x, g_ref[...]) * (1.0 + sc_ref[...]) + sh_ref[...]
    hb = h.astype(BF16)
    hb_ref[...] = hb
    h_lo = (h - hb.astype(F32)).astype(BF16)
    w = wr_ref[...]
    w_hi = w.astype(BF16)
    w_lo = (w - w_hi.astype(F32)).astype(BF16)
    logits = _dot(hb, w_hi) + (_dot(h_lo, w_hi) + _dot(hb, w_lo)) + br_ref[...]

    lane = lax.broadcasted_iota(jnp.int32, (tm, LANES), 1)
    m1 = jnp.max(logits, axis=-1, keepdims=True)
    i1 = jnp.min(jnp.where(logits == m1, lane, LANES), axis=-1, keepdims=True)
    oh1 = lane == i1
    rest = jnp.where(oh1, -3e38, logits)
    m2 = jnp.max(rest, axis=-1, keepdims=True)
    i2 = jnp.min(jnp.where(rest == m2, lane, LANES), axis=-1, keepdims=True)
    oh2 = lane == i2
    ex = jnp.exp(m2 - m1)
    g1 = 1.0 / (1.0 + ex)
    g2 = ex / (1.0 + ex)

    ohf = jnp.where(oh1 | oh2, 1.0, 0.0)
    r_idx = lax.broadcasted_iota(jnp.int32, (tm, tm), 0)
    c_idx = lax.broadcasted_iota(jnp.int32, (tm, tm), 1)
    earlier = jnp.where(c_idx < r_idx, 1.0, 0.0).astype(BF16)
    rank = _dot(earlier, ohf.astype(BF16))
    rank1 = jnp.sum(jnp.where(oh1, rank, 0.0), axis=-1, keepdims=True)
    rank2 = jnp.sum(jnp.where(oh2, rank, 0.0), axis=-1, keepdims=True)
    info = jnp.where(lane == 0, i1.astype(F32),
           jnp.where(lane == 1, i2.astype(F32),
           jnp.where(lane == 2, g1,
           jnp.where(lane == 3, g2,
           jnp.where(lane == 4, rank1,
           jnp.where(lane == 5, rank2, 0.0))))))
    info_ref[...] = info
    infot_ref[...] = info.T[:ROUTE_FIELDS]
    cnt = jnp.sum(ohf, axis=0, keepdims=True).astype(jnp.int32)
    cnt_al = ((cnt + (BF16_ROWS - 1)) // BF16_ROWS) * BF16_ROWS
    for e in range(N_EXPERTS):
        cnt_ref[i * N_EXPERTS + e] = cnt_al[0, e]


def _route(merge_args, g, sc, sh, wr, br):
    s, d = merge_args[3].shape
    tm = ROW_TILE
    nt = s // tm
    row = pl.BlockSpec((tm, d), lambda i: (i, 0))
    return pl.pallas_call(
        _route_body,
        grid=(nt,),
        in_specs=_merge_specs(*merge_args)
                 + [_full((1, d)), _full((1, d)), _full((1, d)), _full(wr.shape), _full(br.shape)],
        out_specs=[row, row,
                   pl.BlockSpec((tm, LANES), lambda i: (i, 0)),
                   pl.BlockSpec((ROUTE_FIELDS, tm), lambda i: (0, i)),
                   pl.BlockSpec(memory_space=pltpu.SMEM)],
        out_shape=[jax.ShapeDtypeStruct((s, d), F32),
                   jax.ShapeDtypeStruct((s, d), BF16),
                   jax.ShapeDtypeStruct((s, LANES), F32),
                   jax.ShapeDtypeStruct((ROUTE_FIELDS, s), F32),
                   jax.ShapeDtypeStruct((nt * N_EXPERTS,), jnp.int32)],
        compiler_params=_params("merge_moe_route", "arbitrary"),
        name="merge_moe_route",
    )(*merge_args, g, sc, sh, wr, br)


def _routing(info):
    return [(info[:, k:k + 1], info[:, 2 + k:3 + k], info[:, 4 + k:5 + k]) for k in range(2)]


def _dispatch_body(offs_ref, cnts_ref, fill_ref, nval_ref, hb_ref, infot_ref, xs_ref, xbuf, zbuf, sem, zsem, *, nblk):
    i = pl.program_id(0)
    tm = hb_ref.shape[0]
    tb = zbuf.shape[0]

    def zero_block(blk):
        cp = pltpu.make_async_copy(zbuf, xs_ref.at[pl.ds(pl.multiple_of(blk * tb, tb), tb)], zsem)
        cp.start()
        cp.wait()

    @pl.when(i == 0)
    def _():
        zbuf[...] = jnp.zeros(zbuf.shape, zbuf.dtype)
        for n in range(fill_ref.shape[0]):
            @pl.when(fill_ref[n] >= 0)
            def _():
                zero_block(fill_ref[n])

        def unused(blk, carry):
            zero_block(blk)
            return carry

        lax.fori_loop(nval_ref[0], nblk, unused, 0)

    hb = hb_ref[...]
    fields = infot_ref[...]
    e1, e2, r1, r2 = fields[0:1], fields[1:2], fields[4:5], fields[5:6]
    chunks = tm // DISPATCH_ROWS
    buf = lax.rem(i, 2)

    def first_slot(e, r):
        return jnp.where(r < float(FIRST_ROWS), e * float(FIRST_ROWS) + r, -1.0)

    p1 = first_slot(e1, r1)
    p2 = first_slot(e2, r2)
    slot = lax.broadcasted_iota(jnp.int32, (N_EXPERTS * FIRST_ROWS, tm), 0).astype(F32)
    sel = jnp.where((p1 == slot) | (p2 == slot), 1.0, 0.0).astype(BF16)
    xbuf[buf, 0] = _dot(sel, hb).astype(BF16)

    first_chunks = FIRST_ROWS // DISPATCH_ROWS

    def buf_rows(e, c):
        return c // first_chunks, e * FIRST_ROWS + (c % first_chunks) * DISPATCH_ROWS

    for e in range(N_EXPERTS):
        for c in range(first_chunks, chunks):
            @pl.when(cnts_ref[i * N_EXPERTS + e] > c * DISPATCH_ROWS)
            def _():
                first = e1 == float(e)
                rank = jnp.where(first, r1, r2)
                late = (lax.broadcasted_iota(jnp.int32, (DISPATCH_ROWS, tm), 0) + c * DISPATCH_ROWS).astype(F32)
                hit = (rank == late) & (first | (e2 == float(e)))
                part, row0 = buf_rows(e, c)
                xbuf[buf, part, row0:row0 + DISPATCH_ROWS] = _dot(jnp.where(hit, 1.0, 0.0).astype(BF16),
                                                                  hb).astype(BF16)

    def copy(b, e, c, dst):
        part, row0 = buf_rows(e, c)
        return pltpu.make_async_copy(xbuf.at[b, part, pl.ds(row0, DISPATCH_ROWS)],
                                     xs_ref.at[pl.ds(dst, DISPATCH_ROWS)], sem.at[b, e, c])

    def for_each_copy(step, fn):
        for e in range(N_EXPERTS):
            for c in range(chunks):
                @pl.when(cnts_ref[step * N_EXPERTS + e] > c * DISPATCH_ROWS)
                def _():
                    fn(e, c)

    @pl.when(i > 0)
    def _():
        for_each_copy(i - 1, lambda e, c: copy(1 - buf, e, c, 0).wait())

    def start(e, c):
        off = offs_ref[i * N_EXPERTS + e]
        copy(buf, e, c, pl.multiple_of(off + c * DISPATCH_ROWS, BF16_ROWS)).start()

    for_each_copy(i, start)

    @pl.when(i == pl.num_programs(0) - 1)
    def _():
        for_each_copy(i, lambda e, c: copy(buf, e, c, 0).wait())


def _dispatch(offs, cnts, fill, nval, hb, infot, nblk):
    s, d = hb.shape
    tm = ROW_TILE
    tb = EXPERT_ROWS
    grid_spec = pltpu.PrefetchScalarGridSpec(
        num_scalar_prefetch=4,
        grid=(s // tm,),
        in_specs=[pl.BlockSpec((tm, d), lambda i, *_: (i, 0)),
                  pl.BlockSpec((ROUTE_FIELDS, tm), lambda i, *_: (0, i))],
        out_specs=pl.BlockSpec(memory_space=pl.ANY),
        scratch_shapes=[pltpu.VMEM((2, tm // FIRST_ROWS, N_EXPERTS * FIRST_ROWS, d), BF16),
                        pltpu.VMEM((tb, d), BF16),
                        pltpu.SemaphoreType.DMA((2, N_EXPERTS, tm // DISPATCH_ROWS)),
                        pltpu.SemaphoreType.DMA(())],
    )
    return pl.pallas_call(
        functools.partial(_dispatch_body, nblk=nblk),
        grid_spec=grid_spec,
        out_shape=jax.ShapeDtypeStruct((nblk * tb, d), BF16),
        compiler_params=_params("moe_dispatch", "arbitrary"),
        name="moe_dispatch",
    )(offs, cnts, fill, nval, hb, infot)


def _expert_body(bexp_ref, nval_ref, x_ref, w1_ref, w3_ref, w2_ref, o_ref, acc_ref, *, nff):
    i = pl.program_id(0)
    f = pl.program_id(1)
    used = i < nval_ref[0]

    def ff_slice(first, final):
        xb = x_ref[...]
        a = _dot(xb, w1_ref[0])
        b = _dot(xb, w3_ref[0])
        acc = _dot((a * jax.nn.sigmoid(a) * b).astype(BF16), w2_ref[0].astype(BF16))
        if not first:
            acc = acc_ref[...] + acc
        if final:
            o_ref[...] = acc.astype(o_ref.dtype)
        else:
            acc_ref[...] = acc

    variants = [(f == 0, True, nff == 1)]
    if nff > 2:
        variants.append(((f > 0) & (f < nff - 1), False, False))
    if nff > 1:
        variants.append((f == nff - 1, False, True))
    for pick, first, final in variants:
        @pl.when(used & pick)
        def _():
            ff_slice(first, final)

    @pl.when(jnp.logical_not(used) & (f == nff - 1))
    def _():
        o_ref[...] = jnp.zeros(o_ref.shape, o_ref.dtype)


def _experts(xs, w1, w3, w2, bexp, nval, tf):
    rows, d = xs.shape
    ff = w1.shape[2]
    tb = EXPERT_ROWS
    nff = ff // tf

    def xrow(i, f, be, nv):
        return (jnp.minimum(i, nv[0] - 1), 0)

    def fcol(i, f, nv):
        return jnp.where(i < nv[0], f, nff - 1)

    grid_spec = pltpu.PrefetchScalarGridSpec(
        num_scalar_prefetch=2,
        grid=(rows // tb, nff),
        in_specs=[pl.BlockSpec((tb, d), xrow),
                  pl.BlockSpec((1, d, tf), lambda i, f, be, nv: (be[i], 0, fcol(i, f, nv))),
                  pl.BlockSpec((1, d, tf), lambda i, f, be, nv: (be[i], 0, fcol(i, f, nv))),
                  pl.BlockSpec((1, tf, d), lambda i, f, be, nv: (be[i], fcol(i, f, nv), 0))],
        out_specs=pl.BlockSpec((tb, d), lambda i, f, be, nv: (i, 0)),
        scratch_shapes=[pltpu.VMEM((tb, d), F32)],
    )
    return pl.pallas_call(
        functools.partial(_expert_body, nff=nff),
        grid_spec=grid_spec,
        out_shape=jax.ShapeDtypeStruct((rows, d), BF16),
        compiler_params=_params("moe_experts", "arbitrary", "arbitrary"),
        name="moe_experts",
    )(bexp, nval, xs, w1, w3, w2)


def _combine_body(offs_ref, cnts_ref, info_ref, x_ref, gp_ref, gf_ref, ys_ref, o_ref, ybuf, acc_ref, sem):
    i = pl.program_id(0)
    nt = pl.num_programs(0)
    tm = x_ref.shape[0]
    chunks = tm // COMBINE_ROWS
    buf = lax.rem(i, 2)

    def copy(b, e, c, src):
        dst = ybuf.at[b, pl.ds((c * N_EXPERTS + e) * COMBINE_ROWS, COMBINE_ROWS)]
        return pltpu.make_async_copy(ys_ref.at[pl.ds(src, COMBINE_ROWS)], dst, sem.at[b, e, c])

    def fetch(step, b):
        for e in range(N_EXPERTS):
            off = offs_ref[step * N_EXPERTS + e]
            copy(b, e, 0, pl.multiple_of(off, BF16_ROWS)).start()
            for c in range(1, chunks):
                @pl.when(cnts_ref[step * N_EXPERTS + e] > c * COMBINE_ROWS)
                def _():
                    copy(b, e, c, pl.multiple_of(off + c * COMBINE_ROWS, BF16_ROWS)).start()

    @pl.when(i == 0)
    def _():
        fetch(0, 0)

    @pl.when(i + 1 < nt)
    def _():
        fetch(i + 1, 1 - buf)

    (e1, g1, r1), (e2, g2, r2) = _routing(info_ref[...])
    for e in range(N_EXPERTS):
        copy(buf, e, 0, 0).wait()
    stack = N_EXPERTS * COMBINE_ROWS
    slot = lax.broadcasted_iota(jnp.int32, (tm, stack), 1).astype(F32)
    y_first = ybuf[buf, 0:stack]
    acc = None
    for ek, gk, rk in ((e1, g1, r1), (e2, g2, r2)):
        pos = jnp.where(rk < float(COMBINE_ROWS), ek * float(COMBINE_ROWS) + rk, -1.0)
        term = gk * _dot(jnp.where(pos == slot, 1.0, 0.0).astype(BF16), y_first)
        acc = term if acc is None else acc + term
    acc_ref[...] = acc

    late = lax.broadcasted_iota(jnp.int32, (tm, COMBINE_ROWS), 1).astype(F32)
    for e in range(N_EXPERTS):
        for c in range(1, chunks):
            @pl.when(cnts_ref[i * N_EXPERTS + e] > c * COMBINE_ROWS)
            def _():
                first = e1 == float(e)
                second = e2 == float(e)
                rank = jnp.where(first, r1, r2)
                gate = jnp.where(first, g1, jnp.where(second, g2, 0.0))
                copy(buf, e, c, 0).wait()
                hit = (rank == late + float(c * COMBINE_ROWS)) & (first | second)
                rows = ybuf[buf, pl.ds((c * N_EXPERTS + e) * COMBINE_ROWS, COMBINE_ROWS)]
                acc_ref[...] += gate * _dot(jnp.where(hit, 1.0, 0.0).astype(BF16), rows)

    o_ref[...] = x_ref[...] + gf_ref[...] * _rms(acc_ref[...], gp_ref[...])


def _combine(offs, cnts, info, x, gp, gf, ys):
    s, d = x.shape
    tm = ROW_TILE
    chunks = tm // COMBINE_ROWS
    grid_spec = pltpu.PrefetchScalarGridSpec(
        num_scalar_prefetch=2,
        grid=(s // tm,),
        in_specs=[pl.BlockSpec((tm, LANES), lambda i, *_: (i, 0)),
                  pl.BlockSpec((tm, d), lambda i, *_: (i, 0)),
                  pl.BlockSpec((1, d), lambda i, *_: (0, 0)),
                  pl.BlockSpec((1, d), lambda i, *_: (0, 0)),
                  pl.BlockSpec(memory_space=pl.ANY)],
        out_specs=pl.BlockSpec((tm, d), lambda i, *_: (i, 0)),
        scratch_shapes=[pltpu.VMEM((2, chunks * N_EXPERTS * COMBINE_ROWS, d), BF16),
                        pltpu.VMEM((tm, d), F32),
                        pltpu.SemaphoreType.DMA((2, N_EXPERTS, chunks))],
    )
    return pl.pallas_call(
        _combine_body,
        grid_spec=grid_spec,
        out_shape=jax.ShapeDtypeStruct((s, d), F32),
        compiler_params=_params("moe_combine", "arbitrary"),
        name="moe_combine",
    )(offs, cnts, info, x, gp, gf, ys)


def _rot_half_cols(w):
    half = ROPE // 2
    return jnp.concatenate([-w[..., half:], w[..., :half]], axis=-1)


def _pad_cols(w, before, total):
    return jnp.pad(w, ((0, 0), (before, total - before - w.shape[1])))


def _mixer_weights(w_in, w_uq, w_ukv, w_branch_b):
    d = w_in.shape[0]
    o = 0
    parts = {}
    for name, n in (("u", GMLP_WIDTH), ("v", GMLP_WIDTH), ("cq", Q_RANK), ("ckv", KV_RANK),
                    ("kr", ROPE), ("ga", d), ("gb", d)):
        parts[name] = w_in[:, o:o + n]
        o += n
    kr_main = _pad_cols(parts["kr"], NOPE, HEAD_PAD)
    kr_swap = _pad_cols(_rot_half_cols(parts["kr"]), NOPE, HEAD_PAD)
    w_in_p = jnp.concatenate([parts["u"], parts["v"], parts["cq"], parts["ckv"], kr_main, kr_swap,
                              parts["ga"], parts["gb"]], axis=1).astype(BF16)

    wq = w_uq.reshape(Q_RANK, HEADS, NOPE + ROPE)
    zq = jnp.zeros((Q_RANK, HEADS, HEAD_PAD - NOPE - ROPE), w_uq.dtype)
    wq_main = jnp.concatenate([wq, zq], axis=-1).reshape(Q_RANK, HEADS * HEAD_PAD).astype(BF16)
    wq_swap = jnp.concatenate([jnp.zeros((Q_RANK, HEADS, NOPE), w_uq.dtype),
                               _rot_half_cols(wq[..., NOPE:]), zq], axis=-1)
    wq_swap = wq_swap.reshape(Q_RANK, HEADS * HEAD_PAD).astype(BF16)

    wkv = w_ukv.reshape(KV_RANK, HEADS, NOPE + VDIM)
    zk = jnp.zeros((KV_RANK, HEADS, HEAD_PAD - NOPE), w_ukv.dtype)
    wk = jnp.concatenate([wkv[..., :NOPE], zk], axis=-1).reshape(KV_RANK, HEADS * HEAD_PAD).astype(BF16)
    zv = jnp.zeros((KV_RANK, HEADS, VT_ROWS - VDIM), w_ukv.dtype)
    wv = jnp.concatenate([wkv[..., NOPE:], zv], axis=-1).reshape(KV_RANK, HEADS * VT_ROWS).T.astype(BF16)

    return w_in_p, wq_main, wq_swap, wk, wv, w_branch_b.astype(BF16)


def _expert_layout(cnts, nt, nblk):
    tb = EXPERT_ROWS
    c = cnts.reshape(nt, N_EXPERTS)
    total = jnp.sum(c, axis=0)
    per = jnp.where(total > 0, (total + REGION_SLACK + tb - 1) // tb, 0)
    stop = jnp.cumsum(per)
    start = stop - per
    offs = start[None, :] * tb + (jnp.cumsum(c, axis=0) - c)
    nval = stop[-1]
    blk = jnp.minimum(jnp.arange(nblk, dtype=jnp.int32), nval - 1)
    bexp = jnp.minimum(jnp.sum(blk[:, None] >= stop[None, :], axis=1), N_EXPERTS - 1)
    fill = jnp.concatenate([jnp.where(per > 0, stop - 1, -1), jnp.where(per > 1, stop - 2, -1)])
    i32 = lambda v: v.astype(jnp.int32)
    return i32(offs.reshape(-1)), i32(bexp), i32(nval.reshape(1)), i32(fill)


def kernel(x, c, positions, ada_w, ada_b, norm_mix_pre, norm_mix_post, norm_ffn_pre, norm_ffn_post, w_in, gmlp_ln_g, gmlp_ln_b, gmlp_ws, gmlp_bs, mla_q_norm, mla_w_uq, mla_kv_norm, mla_w_ukv, w_branch_a, w_branch_b, w_out, ffn_w_gate, ffn_w_up, ffn_w_down, moe_router, moe_router_bias, moe_w1, moe_w3, moe_w2):
    batch, s, d = x.shape
    assert batch == 1 and s % ROW_TILE == 0 and s % ATTN_TILE == 0
    depth = ada_w.shape[0]
    xs = x.reshape(s, d)
    mod = _ada_mod(c, ada_w, ada_b)
    cos_t, sin_t = _rope_tables(positions)
    vone = np.zeros((HEADS, VT_ROWS, LANES), np.float32)
    vone[:, VDIM] = 1.0
    vone = jnp.asarray(vone.reshape(HEADS * VT_ROWS, LANES))
    row = lambda v: v.reshape(1, -1)

    for l in range(depth):
        sh_m, sc_m, g_m, sh_f, sc_f, g_f = [mod[l, :, k * d:(k + 1) * d] for k in range(6)]
        w_in_p, wq_main, wq_swap, wk, wv, wb = _mixer_weights(w_in[l], mla_w_uq[l], mla_w_ukv[l], w_branch_b[l])
        bias = jnp.broadcast_to(gmlp_bs[l].T[:, :, None], (CHUNK, GMLP_GROUPS, GMLP_GROUP_DIM))
        bias = bias.reshape(CHUNK, GMLP_WIDTH)
        a, gb, q, k, vt = _mixer_front(xs, row(norm_mix_pre[l]), sc_m, sh_m, w_in_p,
                                       row(gmlp_ln_g[l]), row(gmlp_ln_b[l]), gmlp_ws[l], bias,
                                       w_branch_a[l].astype(BF16), cos_t, sin_t,
                                       row(mla_q_norm[l]), row(mla_kv_norm[l]), wq_main, wq_swap, wk, wv, vone)
        o = _attention(q, k, vt)
        merge_args = (a, gb, o, xs, wb, w_out[l].astype(BF16), row(norm_mix_post[l]), g_m)

        j = l // 2
        if l % 2 == 0:
            ff = ffn_w_gate.shape[2]
            ff_pad = -(-ff // (2 * LANES)) * (2 * LANES)
            wg = jnp.pad(ffn_w_gate[j], ((0, 0), (0, ff_pad - ff))).astype(BF16)
            wu = jnp.pad(ffn_w_up[j], ((0, 0), (0, ff_pad - ff))).astype(BF16)
            wd = jnp.pad(ffn_w_down[j], ((0, ff_pad - ff), (0, 0))).astype(BF16)
            xs = _dense_ffn(merge_args, row(norm_ffn_pre[l]), sc_f, sh_f, wg, wu, wd, row(norm_ffn_post[l]), g_f)
        else:
            nt = s // ROW_TILE
            tb = EXPERT_ROWS
            max_rows = 2 * s + nt * N_EXPERTS * (BF16_ROWS - 1) + N_EXPERTS * (REGION_SLACK + tb - 1)
            nblk = -(-max_rows // tb) + 1
            wr = jnp.pad(moe_router[j], ((0, 0), (0, LANES - N_EXPERTS)))
            br = jnp.pad(moe_router_bias[j], (0, LANES - N_EXPERTS), constant_values=NEG).reshape(1, LANES)
            xs, hb, info, infot, cnts = _route(merge_args, row(norm_ffn_pre[l]), sc_f, sh_f, wr, br)
            offs, bexp, nval, fill = _expert_layout(cnts, nt, nblk)
            xsort = _dispatch(offs, cnts, fill, nval, hb, infot, nblk)
            ysort = _experts(xsort, moe_w1[j].astype(BF16), moe_w3[j].astype(BF16), moe_w2[j],
                             bexp, nval, tf=moe_w1.shape[3] // 2)
            xs = _combine(offs, cnts, info, xs, row(norm_ffn_post[l]), g_f, ysort)
    return xs.reshape(batch, s, d)
```

```python
import functools

import jax
import jax.numpy as jnp
import numpy as np
from jax import lax
from jax.experimental import pallas as pl
from jax.experimental.pallas import tpu as pltpu

F32 = jnp.float32
BF16 = jnp.bfloat16

EPS = 1e-6
LANES = 128
BF16_ROWS = 16
GMLP_GROUPS = 8
GMLP_GROUP_DIM = 64
GMLP_WIDTH = GMLP_GROUPS * GMLP_GROUP_DIM
CHUNK = 128
HEADS = 8
NOPE = 64
ROPE = 32
VDIM = 64
HEAD_PAD = 128
VT_ROWS = 128
Q_RANK = 384
KV_RANK = 256
ROPE_THETA = 10000.0
N_EXPERTS = 8
NEG = -1e30

ROW_TILE = 512
ATTN_TILE = 512
ATTN_HEADS = 2
STAT_ROWS = 8
EXPERT_ROWS = 512
DISPATCH_ROWS = 128
COMBINE_ROWS = 256
FIRST_ROWS = 256
ROUTE_FIELDS = 8
REGION_SLACK = DISPATCH_ROWS - BF16_ROWS
VMEM_MIB = {"ada_mod": 24, "rope_tables": 16, "w_in_layout": 32, "mixer_front": 48, "mla_attention": 58, "merge_dense_ffn": 56,
            "merge_moe_route": 32, "moe_dispatch": 32, "moe_experts": 48, "moe_combine": 40}


def _params(name, *sem):
    return pltpu.CompilerParams(dimension_semantics=sem, vmem_limit_bytes=VMEM_MIB[name] * 1024 * 1024)


def _dot(a, b):
    return jnp.dot(a, b, preferred_element_type=F32)


def _rms(x, g):
    return x * lax.rsqrt(jnp.mean(x * x, axis=-1, keepdims=True) + EPS) * g


def _gelu(x):
    return 0.5 * x * (1.0 + lax.erf(x * np.float32(0.7071067811865476)))


def _full(shape):
    return pl.BlockSpec(shape, lambda *_: (0,) * len(shape), pipeline_mode=pl.Buffered(1))


def _mod_body(c_ref, w_ref, b_ref, o_ref):
    c = c_ref[...]
    ca = jnp.broadcast_to(c * jax.nn.sigmoid(c), (LANES, c.shape[1])).T[:, 0:1]
    o_ref[0] = jnp.sum(ca * w_ref[0], axis=0, keepdims=True) + b_ref[0]


def _ada_mod(c, ada_w, ada_b):
    n_layers, d, n = ada_w.shape
    tn = n // 4
    return pl.pallas_call(
        _mod_body,
        grid=(n_layers, n // tn),
        in_specs=[pl.BlockSpec((1, d), lambda l, j: (0, 0)),
                  pl.BlockSpec((1, d, tn), lambda l, j: (l, 0, j)),
                  pl.BlockSpec((1, 1, tn), lambda l, j: (l, 0, j))],
        out_specs=pl.BlockSpec((1, 1, tn), lambda l, j: (l, 0, j)),
        out_shape=jax.ShapeDtypeStruct((n_layers, 1, n), F32),
        compiler_params=_params("ada_mod", "arbitrary", "arbitrary"),
        name="ada_mod",
    )(c.reshape(1, d), ada_w, ada_b.reshape(n_layers, 1, n))


def _rope_body(pos_ref, invf_ref, cos_ref, sin_ref):
    ang = pos_ref[...].astype(F32) * invf_ref[...]
    cos_ref[...] = jnp.cos(ang)
    sin_ref[...] = jnp.sin(ang)


def _rope_tables(positions):
    s = positions.shape[-1]
    half = ROPE // 2
    per_row = LANES // half
    inv_freq = 1.0 / (ROPE_THETA ** (jnp.arange(0, ROPE, 2, dtype=F32) / ROPE))
    pos_dense = jnp.broadcast_to(positions.reshape(s, 1), (s, half)).reshape(s // per_row, LANES)
    invf = jnp.tile(inv_freq, per_row).reshape(1, LANES)
    rows = s // per_row
    tr = rows // 4
    cos_d, sin_d = pl.pallas_call(
        _rope_body,
        grid=(rows // tr,),
        in_specs=[pl.BlockSpec((tr, LANES), lambda i: (i, 0)), _full((1, LANES))],
        out_specs=[pl.BlockSpec((tr, LANES), lambda i: (i, 0))] * 2,
        out_shape=[jax.ShapeDtypeStruct((rows, LANES), F32)] * 2,
        compiler_params=_params("rope_tables", "arbitrary"),
        name="rope_tables",
    )(pos_dense, invf)
    cos = cos_d.reshape(s, half)
    sin = sin_d.reshape(s, half)
    ones = jnp.ones((s, NOPE), F32)
    zeros = jnp.zeros((s, NOPE), F32)
    pad = jnp.zeros((s, HEAD_PAD - NOPE - ROPE), F32)
    cos_t = jnp.concatenate([ones, cos, cos, pad], axis=1)
    sin_t = jnp.concatenate([zeros, sin, sin, pad], axis=1)
    return cos_t, sin_t


def _gmlp_branch(u, v, ga, lng_ref, lnb_ref, ws_ref, bias_ref, wa_ref):
    tm = u.shape[0]
    gu = _gelu(u)
    gv = _gelu(v)
    mu = jnp.mean(gv, axis=-1, keepdims=True)
    xc = gv - mu
    vn = xc * lax.rsqrt(jnp.mean(xc * xc, axis=-1, keepdims=True) + EPS) * lng_ref[...] + lnb_ref[...]
    vb = vn.astype(BF16)
    t_idx = lax.broadcasted_iota(jnp.int32, (CHUNK, CHUNK), 0)
    s_idx = lax.broadcasted_iota(jnp.int32, (CHUNK, CHUNK), 1)
    causal = s_idx <= t_idx
    ws = [jnp.where(causal, ws_ref[g], 0.0).astype(BF16) for g in range(GMLP_GROUPS)]
    left = lax.broadcasted_iota(jnp.int32, (CHUNK, LANES), 1) < GMLP_GROUP_DIM
    bias = bias_ref[...]
    z_rows = []
    for c in range(tm // CHUNK):
        vc = vb[c * CHUNK:(c + 1) * CHUNK]
        z_cols = []
        for j in range(GMLP_WIDTH // LANES):
            vp = vc[:, j * LANES:(j + 1) * LANES]
            z_cols.append(jnp.where(left, _dot(ws[2 * j], vp), _dot(ws[2 * j + 1], vp)))
        z_rows.append(jnp.concatenate(z_cols, axis=1) + bias)
    z = jnp.concatenate(z_rows, axis=0)
    gated = (gu * z).astype(BF16)
    return jax.nn.sigmoid(ga) * _dot(gated, wa_ref[...])


def _mla_qkv(cq, ckv, kr, cos_ref, sin_ref, qn_ref, kvn_ref, wqm_ref, wqs_ref, wk_ref, wvt_ref, vone_ref,
             q_ref, k_ref, vt_ref, scale):
    cos = cos_ref[...]
    sin = sin_ref[...]
    cqn = _rms(cq, qn_ref[...]).astype(BF16)
    qm = _dot(cqn, wqm_ref[...])
    qs = _dot(cqn, wqs_ref[...])
    ckn = _rms(ckv, kvn_ref[...]).astype(BF16)
    km = _dot(ckn, wk_ref[...])
    kpe = kr[:, :HEAD_PAD] * cos + kr[:, HEAD_PAD:] * sin
    for h in range(HEADS):
        sl = slice(h * HEAD_PAD, (h + 1) * HEAD_PAD)
        q_ref[:, sl] = ((qm[:, sl] * cos + qs[:, sl] * sin) * scale).astype(q_ref.dtype)
        k_ref[:, sl] = (km[:, sl] + kpe).astype(k_ref.dtype)
    vt = lax.dot_general(wvt_ref[...], ckn, (((1,), (1,)), ((), ())), preferred_element_type=F32)
    vt_ref[...] = (vt + jnp.tile(vone_ref[...], (1, vt.shape[1] // LANES))).astype(vt_ref.dtype)


def _front_body(x_ref, g_ref, sc_ref, sh_ref, w_ref, lng_ref, lnb_ref, ws_ref, bias_ref, wa_ref,
                cos_ref, sin_ref, qn_ref, kvn_ref, wqm_ref, wqs_ref, wk_ref, wvt_ref, vone_ref,
                a_ref, gb_ref, q_ref, k_ref, vt_ref, *, scale):
    h = _rms(x_ref[...], g_ref[...]) * (1.0 + sc_ref[...]) + sh_ref[...]
    proj = _dot(h.astype(BF16), w_ref[...])
    d = x_ref.shape[1]
    cuts = np.cumsum([0, GMLP_WIDTH, GMLP_WIDTH, Q_RANK, KV_RANK, 2 * HEAD_PAD, d, d])
    u, v, cq, ckv, kr, ga, gb = [proj[:, lo:hi] for lo, hi in zip(cuts[:-1], cuts[1:])]
    gb_ref[...] = gb.astype(gb_ref.dtype)
    a_ref[...] = _gmlp_branch(u, v, ga, lng_ref, lnb_ref, ws_ref, bias_ref, wa_ref).astype(a_ref.dtype)
    _mla_qkv(cq, ckv, kr, cos_ref, sin_ref, qn_ref, kvn_ref, wqm_ref, wqs_ref, wk_ref, wvt_ref, vone_ref,
             q_ref, k_ref, vt_ref, scale)


def _mixer_front(x, g, sc, sh, w, lng, lnb, ws, bias, wa, cos_t, sin_t, qn, kvn, wqm, wqs, wk, wvt, vone):
    s, d = x.shape
    tm = ROW_TILE
    width = HEADS * HEAD_PAD
    row = lambda n: pl.BlockSpec((tm, n), lambda i: (i, 0))
    scale = float((NOPE + ROPE) ** -0.5 * np.log2(np.e))
    consts = (g, sc, sh, w, lng, lnb, ws, bias, wa)
    mla_consts = (qn, kvn, wqm, wqs, wk, wvt, vone)
    return pl.pallas_call(
        functools.partial(_front_body, scale=scale),
        grid=(s // tm,),
        in_specs=[row(d)] + [_full(c.shape) for c in consts] + [row(HEAD_PAD), row(HEAD_PAD)]
                 + [_full(c.shape) for c in mla_consts],
        out_specs=[row(d), row(d), row(width), row(width), pl.BlockSpec((wvt.shape[0], tm), lambda i: (0, i))],
        out_shape=[jax.ShapeDtypeStruct((s, d), BF16), jax.ShapeDtypeStruct((s, d), BF16),
                   jax.ShapeDtypeStruct((s, width), BF16), jax.ShapeDtypeStruct((s, width), BF16),
                   jax.ShapeDtypeStruct((wvt.shape[0], s), BF16)],
        compiler_params=_params("mixer_front", "arbitrary"),
        name="mixer_front",
    )(x, *consts, cos_t, sin_t, *mla_consts)


def _attn_body(q_ref, k_ref, vt_ref, o_ref, s_ref, mx_ref, m_ref, acc_ref):
    t = o_ref.shape[0]
    heads = q_ref.shape[1] // HEAD_PAD
    qi = pl.program_id(1)
    m_ref[...] = jnp.full(m_ref.shape, NEG, F32)
    acc_ref[...] = jnp.zeros(acc_ref.shape, F32)

    def scores(blk, slot, q_tile):
        off = pl.multiple_of(blk * t, t)
        q_off = pl.multiple_of(q_tile * t, t)
        for h in range(heads):
            sl = slice(h * HEAD_PAD, (h + 1) * HEAD_PAD)
            s = lax.dot_general(k_ref[pl.ds(off, t), sl], q_ref[pl.ds(q_off, t), sl],
                                (((1,), (1,)), ((), ())), preferred_element_type=F32)
            s_ref[slot, h] = s
            mx_ref[slot, h] = jnp.broadcast_to(jnp.max(s, axis=0, keepdims=True), (STAT_ROWS, t))

    def consume(blk, slot, mask):
        off = pl.multiple_of(blk * t, t)
        for h in range(heads):
            sl = slice(h * HEAD_PAD, (h + 1) * HEAD_PAD)
            s = s_ref[slot, h]
            if mask is None:
                mx = mx_ref[slot, h]
            else:
                s = jnp.where(mask, s, NEG)
                mx = jnp.max(s, axis=0, keepdims=True)
            m_prev = m_ref[h]
            m_new = jnp.maximum(m_prev, mx)
            alpha = jnp.exp2(m_prev - m_new)
            pv = alpha[0:1] * acc_ref[h]
            half = t // 2
            for part in range(2):
                p = jnp.exp2(s[part * half:(part + 1) * half] - m_new[0:1]).astype(BF16)
                keys = pl.ds(pl.multiple_of(off + part * half, half), half)
                pv = pv + _dot(vt_ref[h * VT_ROWS:(h + 1) * VT_ROWS, keys], p)
            acc_ref[h] = pv
            m_ref[h] = m_new

    @pl.when(qi == 0)
    def _():
        scores(0, 0, qi)

    def run(blk, n):
        for u in range(n):
            scores(blk + u + 1, (u + 1) % 2, qi)
            consume(blk + u, u % 2, None)

    def octet(j, carry):
        run(8 * j, 8)
        return carry

    lax.fori_loop(0, qi // 8, octet, 0)
    done = (qi // 8) * 8
    for n in (4, 2):
        more = qi - done >= n

        @pl.when(more)
        def _():
            run(done, n)

        done = done + jnp.where(more, n, 0)

    row = lax.broadcasted_iota(jnp.int32, (t, t), 0)
    col = lax.broadcasted_iota(jnp.int32, (t, t), 1)
    causal = row <= col
    odd = lax.rem(qi, 2) == 1

    @pl.when(odd)
    def _():
        scores(qi, 1, qi)
        consume(qi - 1, 0, None)
        consume(qi, 1, causal)

    @pl.when(jnp.logical_not(odd))
    def _():
        consume(qi, 0, causal)

    def finalize():
        lower = lax.broadcasted_iota(jnp.int32, (t, HEAD_PAD), 1) < VDIM
        for j in range(heads // 2):
            outs = []
            for h in (2 * j, 2 * j + 1):
                acc = acc_ref[h]
                out_t = acc / acc[VDIM:VDIM + 1]
                if VT_ROWS < HEAD_PAD:
                    out_t = jnp.concatenate([out_t, jnp.zeros((HEAD_PAD - VT_ROWS, t), F32)], axis=0)
                outs.append(out_t.T)
            packed = jnp.where(lower, outs[0], pltpu.roll(outs[1], VDIM, 1))
            o_ref[:, j * HEAD_PAD:(j + 1) * HEAD_PAD] = packed.astype(o_ref.dtype)

    last = qi == pl.num_programs(1) - 1

    @pl.when(jnp.logical_not(last))
    def _():
        scores(0, 0, qi + 1)
        finalize()

    @pl.when(last)
    def _():
        finalize()


def _attention(q, k, vt):
    s, width = q.shape
    t = ATTN_TILE
    gw = ATTN_HEADS * HEAD_PAD
    return pl.pallas_call(
        _attn_body,
        grid=(width // gw, s // t),
        in_specs=[pl.BlockSpec((s, gw), lambda h, i: (0, h)),
                  pl.BlockSpec((s, gw), lambda h, i: (0, h)),
                  pl.BlockSpec((ATTN_HEADS * VT_ROWS, s), lambda h, i: (h, 0))],
        out_specs=pl.BlockSpec((t, ATTN_HEADS * VDIM), lambda h, i: (i, h)),
        out_shape=jax.ShapeDtypeStruct((s, HEADS * VDIM), BF16),
        scratch_shapes=[pltpu.VMEM((2, ATTN_HEADS, t, t), F32),
                        pltpu.VMEM((2, ATTN_HEADS, STAT_ROWS, t), F32),
                        pltpu.VMEM((ATTN_HEADS, STAT_ROWS, t), F32),
                        pltpu.VMEM((ATTN_HEADS, VT_ROWS, t), F32)],
        compiler_params=_params("mla_attention", "arbitrary", "arbitrary"),
        name="mla_attention",
    )(q, k, vt)


MERGE_ARGS = 8


def _merge_math(a_ref, gb_ref, o_ref, x_ref, wb_ref, wo_ref, gp_ref, gm_ref):
    yb = _dot(o_ref[...], wb_ref[...])
    merged = a_ref[...].astype(F32) + jax.nn.sigmoid(gb_ref[...].astype(F32)) * yb
    y = _dot(merged.astype(BF16), wo_ref[...])
    return x_ref[...] + gm_ref[...] * _rms(y, gp_ref[...])


def _merge_specs(a, gb, o, x, wb, wo, gp, gm):
    tm = ROW_TILE
    d = x.shape[1]
    row = pl.BlockSpec((tm, d), lambda i: (i, 0))
    return [row, row, pl.BlockSpec((tm, o.shape[1]), lambda i: (i, 0)), row,
            _full(wb.shape), _full(wo.shape), _full((1, d)), _full((1, d))]


def _ffn_body(*refs):
    g_ref, sc_ref, sh_ref, wg_ref, wu_ref, wd_ref, gp_ref, gf_ref, o_ref = refs[MERGE_ARGS:]
    x = _merge_math(*refs[:MERGE_ARGS])
    hb = (_rms(x, g_ref[...]) * (1.0 + sc_ref[...]) + sh_ref[...]).astype(BF16)
    a = _dot(hb, wg_ref[...])
    b = _dot(hb, wu_ref[...])
    y = _dot((a * jax.nn.sigmoid(a) * b).astype(BF16), wd_ref[...])
    o_ref[...] = x + gf_ref[...] * _rms(y, gp_ref[...])


def _dense_ffn(merge_args, g, sc, sh, wg, wu, wd, gp, gf):
    s, d = merge_args[3].shape
    tm = ROW_TILE
    row = pl.BlockSpec((tm, d), lambda i: (i, 0))
    vec = _full((1, d))
    return pl.pallas_call(
        _ffn_body,
        grid=(s // tm,),
        in_specs=_merge_specs(*merge_args)
                 + [vec, vec, vec, _full(wg.shape), _full(wu.shape), _full(wd.shape), vec, vec],
        out_specs=row,
        out_shape=jax.ShapeDtypeStruct((s, d), F32),
        compiler_params=_params("merge_dense_ffn", "arbitrary"),
        name="merge_dense_ffn",
    )(*merge_args, g, sc, sh, wg, wu, wd, gp, gf)


def _route_body(*refs):
    g_ref, sc_ref, sh_ref, wr_ref, br_ref, x_ref, hb_ref, info_ref, infot_ref, cnt_ref = refs[MERGE_ARGS:]
    i = pl.program_id(0)
    tm = x_ref.shape[0]
    x = _merge_math(*refs[:MERGE_ARGS])
    x_ref[...] = x
    h = _rms(x, g_ref[...]) * (1.0 + sc_ref[...]) + sh_ref[...]
    hb = h.astype(BF16)
    hb_ref[...] = hb
    h_lo = (h - hb.astype(F32)).astype(BF16)
    w = wr_ref[...]
    w_hi = w.astype(BF16)
    w_lo = (w - w_hi.astype(F32)).astype(BF16)
    logits = _dot(hb, w_hi) + (_dot(h_lo, w_hi) + _dot(hb, w_lo)) + br_ref[...]

    lane = lax.broadcasted_iota(jnp.int32, (tm, LANES), 1)
    m1 = jnp.max(logits, axis=-1, keepdims=True)
    i1 = jnp.min(jnp.where(logits == m1, lane, LANES), axis=-1, keepdims=True)
    oh1 = lane == i1
    rest = jnp.where(oh1, -3e38, logits)
    m2 = jnp.max(rest, axis=-1, keepdims=True)
    i2 = jnp.min(jnp.where(rest == m2, lane, LANES), axis=-1, keepdims=True)
    oh2 = lane == i2
    ex = jnp.exp(m2 - m1)
    g1 = 1.0 / (1.0 + ex)
    g2 = ex / (1.0 + ex)

    ohf = jnp.where(oh1 | oh2, 1.0, 0.0)
    r_idx = lax.broadcasted_iota(jnp.int32, (tm, tm), 0)
    c_idx = lax.broadcasted_iota(jnp.int32, (tm, tm), 1)
    earlier = jnp.where(c_idx < r_idx, 1.0, 0.0).astype(BF16)
    rank = _dot(earlier, ohf.astype(BF16))
    rank1 = jnp.sum(jnp.where(oh1, rank, 0.0), axis=-1, keepdims=True)
    rank2 = jnp.sum(jnp.where(oh2, rank, 0.0), axis=-1, keepdims=True)
    info = jnp.where(lane == 0, i1.astype(F32),
           jnp.where(lane == 1, i2.astype(F32),
           jnp.where(lane == 2, g1,
           jnp.where(lane == 3, g2,
           jnp.where(lane == 4, rank1,
           jnp.where(lane == 5, rank2, 0.0))))))
    info_ref[...] = info
    infot_ref[...] = info.T[:ROUTE_FIELDS]
    cnt = jnp.sum(ohf, axis=0, keepdims=True).astype(jnp.int32)
    cnt_al = ((cnt + (BF16_ROWS - 1)) // BF16_ROWS) * BF16_ROWS
    for e in range(N_EXPERTS):
        cnt_ref[i * N_EXPERTS + e] = cnt_al[0, e]


def _route(merge_args, g, sc, sh, wr, br):
    s, d = merge_args[3].shape
    tm = ROW_TILE
    nt = s // tm
    row = pl.BlockSpec((tm, d), lambda i: (i, 0))
    return pl.pallas_call(
        _route_body,
        grid=(nt,),
        in_specs=_merge_specs(*merge_args)
                 + [_full((1, d)), _full((1, d)), _full((1, d)), _full(wr.shape), _full(br.shape)],
        out_specs=[row, row,
                   pl.BlockSpec((tm, LANES), lambda i: (i, 0)),
                   pl.BlockSpec((ROUTE_FIELDS, tm), lambda i: (0, i)),
                   pl.BlockSpec(memory_space=pltpu.SMEM)],
        out_shape=[jax.ShapeDtypeStruct((s, d), F32),
                   jax.ShapeDtypeStruct((s, d), BF16),
                   jax.ShapeDtypeStruct((s, LANES), F32),
                   jax.ShapeDtypeStruct((ROUTE_FIELDS, s), F32),
                   jax.ShapeDtypeStruct((nt * N_EXPERTS,), jnp.int32)],
        compiler_params=_params("merge_moe_route", "arbitrary"),
        name="merge_moe_route",
    )(*merge_args, g, sc, sh, wr, br)


def _routing(info):
    return [(info[:, k:k + 1], info[:, 2 + k:3 + k], info[:, 4 + k:5 + k]) for k in range(2)]


def _dispatch_body(offs_ref, cnts_ref, fill_ref, nval_ref, hb_ref, infot_ref, xs_ref, xbuf, zbuf, sem, zsem, *, nblk):
    i = pl.program_id(0)
    tm = hb_ref.shape[0]
    tb = zbuf.shape[0]

    def zero_block(blk):
        cp = pltpu.make_async_copy(zbuf, xs_ref.at[pl.ds(pl.multiple_of(blk * tb, tb), tb)], zsem)
        cp.start()
        cp.wait()

    @pl.when(i == 0)
    def _():
        zbuf[...] = jnp.zeros(zbuf.shape, zbuf.dtype)
        for n in range(fill_ref.shape[0]):
            @pl.when(fill_ref[n] >= 0)
            def _():
                zero_block(fill_ref[n])

        def unused(blk, carry):
            zero_block(blk)
            return carry

        lax.fori_loop(nval_ref[0], nblk, unused, 0)

    hb = hb_ref[...]
    fields = infot_ref[...]
    e1, e2, r1, r2 = fields[0:1], fields[1:2], fields[4:5], fields[5:6]
    chunks = tm // DISPATCH_ROWS
    buf = lax.rem(i, 2)

    def first_slot(e, r):
        return jnp.where(r < float(FIRST_ROWS), e * float(FIRST_ROWS) + r, -1.0)

    p1 = first_slot(e1, r1)
    p2 = first_slot(e2, r2)
    slot = lax.broadcasted_iota(jnp.int32, (N_EXPERTS * FIRST_ROWS, tm), 0).astype(F32)
    sel = jnp.where((p1 == slot) | (p2 == slot), 1.0, 0.0).astype(BF16)
    xbuf[buf, 0] = _dot(sel, hb).astype(BF16)

    first_chunks = FIRST_ROWS // DISPATCH_ROWS

    def buf_rows(e, c):
        return c // first_chunks, e * FIRST_ROWS + (c % first_chunks) * DISPATCH_ROWS

    for e in range(N_EXPERTS):
        for c in range(first_chunks, chunks):
            @pl.when(cnts_ref[i * N_EXPERTS + e] > c * DISPATCH_ROWS)
            def _():
                first = e1 == float(e)
                rank = jnp.where(first, r1, r2)
                late = (lax.broadcasted_iota(jnp.int32, (DISPATCH_ROWS, tm), 0) + c * DISPATCH_ROWS).astype(F32)
                hit = (rank == late) & (first | (e2 == float(e)))
                part, row0 = buf_rows(e, c)
                xbuf[buf, part, row0:row0 + DISPATCH_ROWS] = _dot(jnp.where(hit, 1.0, 0.0).astype(BF16),
                                                                  hb).astype(BF16)

    def copy(b, e, c, dst):
        part, row0 = buf_rows(e, c)
        return pltpu.make_async_copy(xbuf.at[b, part, pl.ds(row0, DISPATCH_ROWS)],
                                     xs_ref.at[pl.ds(dst, DISPATCH_ROWS)], sem.at[b, e, c])

    def for_each_copy(step, fn):
        for e in range(N_EXPERTS):
            for c in range(chunks):
                @pl.when(cnts_ref[step * N_EXPERTS + e] > c * DISPATCH_ROWS)
                def _():
                    fn(e, c)

    @pl.when(i > 0)
    def _():
        for_each_copy(i - 1, lambda e, c: copy(1 - buf, e, c, 0).wait())

    def start(e, c):
        off = offs_ref[i * N_EXPERTS + e]
        copy(buf, e, c, pl.multiple_of(off + c * DISPATCH_ROWS, BF16_ROWS)).start()

    for_each_copy(i, start)

    @pl.when(i == pl.num_programs(0) - 1)
    def _():
        for_each_copy(i, lambda e, c: copy(buf, e, c, 0).wait())


def _dispatch(offs, cnts, fill, nval, hb, infot, nblk):
    s, d = hb.shape
    tm = ROW_TILE
    tb = EXPERT_ROWS
    grid_spec = pltpu.PrefetchScalarGridSpec(
        num_scalar_prefetch=4,
        grid=(s // tm,),
        in_specs=[pl.BlockSpec((tm, d), lambda i, *_: (i, 0)),
                  pl.BlockSpec((ROUTE_FIELDS, tm), lambda i, *_: (0, i))],
        out_specs=pl.BlockSpec(memory_space=pl.ANY),
        scratch_shapes=[pltpu.VMEM((2, tm // FIRST_ROWS, N_EXPERTS * FIRST_ROWS, d), BF16),
                        pltpu.VMEM((tb, d), BF16),
                        pltpu.SemaphoreType.DMA((2, N_EXPERTS, tm // DISPATCH_ROWS)),
                        pltpu.SemaphoreType.DMA(())],
    )
    return pl.pallas_call(
        functools.partial(_dispatch_body, nblk=nblk),
        grid_spec=grid_spec,
        out_shape=jax.ShapeDtypeStruct((nblk * tb, d), BF16),
        compiler_params=_params("moe_dispatch", "arbitrary"),
        name="moe_dispatch",
    )(offs, cnts, fill, nval, hb, infot)


def _expert_body(bexp_ref, nval_ref, x_ref, w1_ref, w3_ref, w2_ref, o_ref, acc_ref, *, nff):
    i = pl.program_id(0)
    f = pl.program_id(1)
    used = i < nval_ref[0]

    def ff_slice(first, final):
        xb = x_ref[...]
        a = _dot(xb, w1_ref[0])
        b = _dot(xb, w3_ref[0])
        acc = _dot((a * jax.nn.sigmoid(a) * b).astype(BF16), w2_ref[0].astype(BF16))
        if not first:
            acc = acc_ref[...] + acc
        if final:
            o_ref[...] = acc.astype(o_ref.dtype)
        else:
            acc_ref[...] = acc

    variants = [(f == 0, True, nff == 1)]
    if nff > 2:
        variants.append(((f > 0) & (f < nff - 1), False, False))
    if nff > 1:
        variants.append((f == nff - 1, False, True))
    for pick, first, final in variants:
        @pl.when(used & pick)
        def _():
            ff_slice(first, final)

    @pl.when(jnp.logical_not(used) & (f == nff - 1))
    def _():
        o_ref[...] = jnp.zeros(o_ref.shape, o_ref.dtype)


def _experts(xs, w1, w3, w2, bexp, nval, tf):
    rows, d = xs.shape
    ff = w1.shape[2]
    tb = EXPERT_ROWS
    nff = ff // tf

    def xrow(i, f, be, nv):
        return (jnp.minimum(i, nv[0] - 1), 0)

    def fcol(i, f, nv):
        return jnp.where(i < nv[0], f, nff - 1)

    grid_spec = pltpu.PrefetchScalarGridSpec(
        num_scalar_prefetch=2,
        grid=(rows // tb, nff),
        in_specs=[pl.BlockSpec((tb, d), xrow),
                  pl.BlockSpec((1, d, tf), lambda i, f, be, nv: (be[i], 0, fcol(i, f, nv))),
                  pl.BlockSpec((1, d, tf), lambda i, f, be, nv: (be[i], 0, fcol(i, f, nv))),
                  pl.BlockSpec((1, tf, d), lambda i, f, be, nv: (be[i], fcol(i, f, nv), 0))],
        out_specs=pl.BlockSpec((tb, d), lambda i, f, be, nv: (i, 0)),
        scratch_shapes=[pltpu.VMEM((tb, d), F32)],
    )
    return pl.pallas_call(
        functools.partial(_expert_body, nff=nff),
        grid_spec=grid_spec,
        out_shape=jax.ShapeDtypeStruct((rows, d), BF16),
        compiler_params=_params("moe_experts", "arbitrary", "arbitrary"),
        name="moe_experts",
    )(bexp, nval, xs, w1, w3, w2)


def _combine_body(offs_ref, cnts_ref, info_ref, x_ref, gp_ref, gf_ref, ys_ref, o_ref, ybuf, acc_ref, sem):
    i = pl.program_id(0)
    nt = pl.num_programs(0)
    tm = x_ref.shape[0]
    chunks = tm // COMBINE_ROWS
    buf = lax.rem(i, 2)

    def copy(b, e, c, src):
        dst = ybuf.at[b, pl.ds((c * N_EXPERTS + e) * COMBINE_ROWS, COMBINE_ROWS)]
        return pltpu.make_async_copy(ys_ref.at[pl.ds(src, COMBINE_ROWS)], dst, sem.at[b, e, c])

    def fetch(step, b):
        for e in range(N_EXPERTS):
            off = offs_ref[step * N_EXPERTS + e]
            copy(b, e, 0, pl.multiple_of(off, BF16_ROWS)).start()
            for c in range(1, chunks):
                @pl.when(cnts_ref[step * N_EXPERTS + e] > c * COMBINE_ROWS)
                def _():
                    copy(b, e, c, pl.multiple_of(off + c * COMBINE_ROWS, BF16_ROWS)).start()

    @pl.when(i == 0)
    def _():
        fetch(0, 0)

    @pl.when(i + 1 < nt)
    def _():
        fetch(i + 1, 1 - buf)

    (e1, g1, r1), (e2, g2, r2) = _routing(info_ref[...])
    for e in range(N_EXPERTS):
        copy(buf, e, 0, 0).wait()
    stack = N_EXPERTS * COMBINE_ROWS
    slot = lax.broadcasted_iota(jnp.int32, (tm, stack), 1).astype(F32)
    y_first = ybuf[buf, 0:stack]
    acc = None
    for ek, gk, rk in ((e1, g1, r1), (e2, g2, r2)):
        pos = jnp.where(rk < float(COMBINE_ROWS), ek * float(COMBINE_ROWS) + rk, -1.0)
        term = gk * _dot(jnp.where(pos == slot, 1.0, 0.0).astype(BF16), y_first)
        acc = term if acc is None else acc + term
    acc_ref[...] = acc

    late = lax.broadcasted_iota(jnp.int32, (tm, COMBINE_ROWS), 1).astype(F32)
    for e in range(N_EXPERTS):
        for c in range(1, chunks):
            @pl.when(cnts_ref[i * N_EXPERTS + e] > c * COMBINE_ROWS)
            def _():
                first = e1 == float(e)
                second = e2 == float(e)
                rank = jnp.where(first, r1, r2)
                gate = jnp.where(first, g1, jnp.where(second, g2, 0.0))
                copy(buf, e, c, 0).wait()
                hit = (rank == late + float(c * COMBINE_ROWS)) & (first | second)
                rows = ybuf[buf, pl.ds((c * N_EXPERTS + e) * COMBINE_ROWS, COMBINE_ROWS)]
                acc_ref[...] += gate * _dot(jnp.where(hit, 1.0, 0.0).astype(BF16), rows)

    o_ref[...] = x_ref[...] + gf_ref[...] * _rms(acc_ref[...], gp_ref[...])


def _combine(offs, cnts, info, x, gp, gf, ys):
    s, d = x.shape
    tm = ROW_TILE
    chunks = tm // COMBINE_ROWS
    grid_spec = pltpu.PrefetchScalarGridSpec(
        num_scalar_prefetch=2,
        grid=(s // tm,),
        in_specs=[pl.BlockSpec((tm, LANES), lambda i, *_: (i, 0)),
                  pl.BlockSpec((tm, d), lambda i, *_: (i, 0)),
                  pl.BlockSpec((1, d), lambda i, *_: (0, 0)),
                  pl.BlockSpec((1, d), lambda i, *_: (0, 0)),
                  pl.BlockSpec(memory_space=pl.ANY)],
        out_specs=pl.BlockSpec((tm, d), lambda i, *_: (i, 0)),
        scratch_shapes=[pltpu.VMEM((2, chunks * N_EXPERTS * COMBINE_ROWS, d), BF16),
                        pltpu.VMEM((tm, d), F32),
                        pltpu.SemaphoreType.DMA((2, N_EXPERTS, chunks))],
    )
    return pl.pallas_call(
        _combine_body,
        grid_spec=grid_spec,
        out_shape=jax.ShapeDtypeStruct((s, d), F32),
        compiler_params=_params("moe_combine", "arbitrary"),
        name="moe_combine",
    )(offs, cnts, info, x, gp, gf, ys)


def _rot_half_cols(w):
    half = ROPE // 2
    return jnp.concatenate([-w[..., half:], w[..., :half]], axis=-1)


def _w_in_body(w_ref, o_ref):
    w = w_ref[0]
    rows = w.shape[0]
    head = 2 * GMLP_WIDTH + Q_RANK + KV_RANK
    kr = w[:, head:head + ROPE]
    half = ROPE // 2
    before = jnp.zeros((rows, NOPE), F32)
    after = jnp.zeros((rows, HEAD_PAD - NOPE - ROPE), F32)
    wide = jnp.concatenate([w[:, :head], before, kr, after, before, -kr[:, half:], kr[:, :half], after,
                            w[:, head + ROPE:]], axis=1)
    o_ref[0] = wide.astype(o_ref.dtype)


def _w_in_layout(w_in):
    n_layers, d, n = w_in.shape
    tk = d // 4
    wide = n - ROPE + 2 * HEAD_PAD
    return pl.pallas_call(
        _w_in_body,
        grid=(n_layers, d // tk),
        in_specs=[pl.BlockSpec((1, tk, n), lambda l, k: (l, k, 0))],
        out_specs=pl.BlockSpec((1, tk, wide), lambda l, k: (l, k, 0)),
        out_shape=jax.ShapeDtypeStruct((n_layers, d, wide), BF16),
        compiler_params=_params("w_in_layout", "arbitrary", "arbitrary"),
        name="w_in_layout",
    )(w_in)


def _mixer_weights(w_uq, w_ukv, w_branch_b):
    d = w_branch_b.shape[1]
    wq = w_uq.reshape(Q_RANK, HEADS, NOPE + ROPE)
    zq = jnp.zeros((Q_RANK, HEADS, HEAD_PAD - NOPE - ROPE), w_uq.dtype)
    wq_main = jnp.concatenate([wq, zq], axis=-1).reshape(Q_RANK, HEADS * HEAD_PAD).astype(BF16)
    wq_swap = jnp.concatenate([jnp.zeros((Q_RANK, HEADS, NOPE), w_uq.dtype),
                               _rot_half_cols(wq[..., NOPE:]), zq], axis=-1)
    wq_swap = wq_swap.reshape(Q_RANK, HEADS * HEAD_PAD).astype(BF16)

    wkv = w_ukv.reshape(KV_RANK, HEADS, NOPE + VDIM)
    zk = jnp.zeros((KV_RANK, HEADS, HEAD_PAD - NOPE), w_ukv.dtype)
    wk = jnp.concatenate([wkv[..., :NOPE], zk], axis=-1).reshape(KV_RANK, HEADS * HEAD_PAD).astype(BF16)
    zv = jnp.zeros((KV_RANK, HEADS, VT_ROWS - VDIM), w_ukv.dtype)
    wv = jnp.concatenate([wkv[..., NOPE:], zv], axis=-1).reshape(KV_RANK, HEADS * VT_ROWS).T.astype(BF16)

    return wq_main, wq_swap, wk, wv, w_branch_b.astype(BF16)


def _expert_layout(cnts, nt, nblk):
    tb = EXPERT_ROWS
    c = cnts.reshape(nt, N_EXPERTS)
    total = jnp.sum(c, axis=0)
    per = jnp.where(total > 0, (total + REGION_SLACK + tb - 1) // tb, 0)
    stop = jnp.cumsum(per)
    start = stop - per
    offs = start[None, :] * tb + (jnp.cumsum(c, axis=0) - c)
    nval = stop[-1]
    blk = jnp.minimum(jnp.arange(nblk, dtype=jnp.int32), nval - 1)
    bexp = jnp.minimum(jnp.sum(blk[:, None] >= stop[None, :], axis=1), N_EXPERTS - 1)
    fill = jnp.concatenate([jnp.where(per > 0, stop - 1, -1), jnp.where(per > 1, stop - 2, -1)])
    i32 = lambda v: v.astype(jnp.int32)
    return i32(offs.reshape(-1)), i32(bexp), i32(nval.reshape(1)), i32(fill)


def kernel(x, c, positions, ada_w, ada_b, norm_mix_pre, norm_mix_post, norm_ffn_pre, norm_ffn_post, w_in, gmlp_ln_g, gmlp_ln_b, gmlp_ws, gmlp_bs, mla_q_norm, mla_w_uq, mla_kv_norm, mla_w_ukv, w_branch_a, w_branch_b, w_out, ffn_w_gate, ffn_w_up, ffn_w_down, moe_router, moe_router_bias, moe_w1, moe_w3, moe_w2):
    batch, s, d = x.shape
    assert batch == 1 and s % ROW_TILE == 0 and s % ATTN_TILE == 0
    depth = ada_w.shape[0]
    xs = x.reshape(s, d)
    mod = _ada_mod(c, ada_w, ada_b)
    w_in_wide = _w_in_layout(w_in)
    cos_t, sin_t = _rope_tables(positions)
    vone = np.zeros((HEADS, VT_ROWS, LANES), np.float32)
    vone[:, VDIM] = 1.0
    vone = jnp.asarray(vone.reshape(HEADS * VT_ROWS, LANES))
    row = lambda v: v.reshape(1, -1)

    for l in range(depth):
        sh_m, sc_m, g_m, sh_f, sc_f, g_f = [mod[l, :, k * d:(k + 1) * d] for k in range(6)]
        wq_main, wq_swap, wk, wv, wb = _mixer_weights(mla_w_uq[l], mla_w_ukv[l], w_branch_b[l])
        bias = jnp.broadcast_to(gmlp_bs[l].T[:, :, None], (CHUNK, GMLP_GROUPS, GMLP_GROUP_DIM))
        bias = bias.reshape(CHUNK, GMLP_WIDTH)
        a, gb, q, k, vt = _mixer_front(xs, row(norm_mix_pre[l]), sc_m, sh_m, w_in_wide[l],
                                       row(gmlp_ln_g[l]), row(gmlp_ln_b[l]), gmlp_ws[l], bias,
                                       w_branch_a[l].astype(BF16), cos_t, sin_t,
                                       row(mla_q_norm[l]), row(mla_kv_norm[l]), wq_main, wq_swap, wk, wv, vone)
        o = _attention(q, k, vt)
        merge_args = (a, gb, o, xs, wb, w_out[l].astype(BF16), row(norm_mix_post[l]), g_m)

        j = l // 2
        if l % 2 == 0:
            wg, wu, wd = ffn_w_gate[j].astype(BF16), ffn_w_up[j].astype(BF16), ffn_w_down[j].astype(BF16)
            xs = _dense_ffn(merge_args, row(norm_ffn_pre[l]), sc_f, sh_f, wg, wu, wd, row(norm_ffn_post[l]), g_f)
        else:
            nt = s // ROW_TILE
            tb = EXPERT_ROWS
            max_rows = 2 * s + nt * N_EXPERTS * (BF16_ROWS - 1) + N_EXPERTS * (REGION_SLACK + tb - 1)
            nblk = -(-max_rows // tb) + 1
            wr = jnp.pad(moe_router[j], ((0, 0), (0, LANES - N_EXPERTS)))
            br = jnp.pad(moe_router_bias[j], (0, LANES - N_EXPERTS), constant_values=NEG).reshape(1, LANES)
            xs, hb, info, infot, cnts = _route(merge_args, row(norm_ffn_pre[l]), sc_f, sh_f, wr, br)
            offs, bexp, nval, fill = _expert_layout(cnts, nt, nblk)
            xsort = _dispatch(offs, cnts, fill, nval, hb, infot, nblk)
            ysort = _experts(xsort, moe_w1[j].astype(BF16), moe_w3[j].astype(BF16), moe_w2[j],
                             bexp, nval, tf=moe_w1.shape[3] // 2)
            xs = _combine(offs, cnts, info, xs, row(norm_ffn_post[l]), g_f, ysort)
    return xs.reshape(batch, s, d)
```

```python
import functools

import jax
import jax.numpy as jnp
import numpy as np
from jax import lax
from jax.experimental import pallas as pl
from jax.experimental.pallas import tpu as pltpu

F32 = jnp.float32
BF16 = jnp.bfloat16

EPS = 1e-6
LANES = 128
BF16_ROWS = 16
GMLP_GROUPS = 8
GMLP_GROUP_DIM = 64
GMLP_WIDTH = GMLP_GROUPS * GMLP_GROUP_DIM
CHUNK = 128
HEADS = 8
NOPE = 64
ROPE = 32
VDIM = 64
HEAD_PAD = 128
VT_ROWS = 128
Q_RANK = 384
KV_RANK = 256
ROPE_THETA = 10000.0
N_EXPERTS = 8
NEG = -1e30

ROW_TILE = 512
ATTN_TILE = 512
ATTN_HEADS = 2
STAT_ROWS = 8
EXPERT_ROWS = 512
DISPATCH_ROWS = 128
COMBINE_ROWS = 256
FIRST_ROWS = 256
ROUTE_FIELDS = 8
REGION_SLACK = DISPATCH_ROWS - BF16_ROWS
VMEM_MIB = {"ada_mod": 24, "rope_tables": 16, "w_in_layout": 32, "mixer_front": 48, "mla_attention": 58, "merge_dense_ffn": 56,
            "merge_moe_route": 32, "moe_dispatch": 32, "moe_experts": 48, "moe_combine": 40}


def _params(name, *sem):
    return pltpu.CompilerParams(dimension_semantics=sem, vmem_limit_bytes=VMEM_MIB[name] * 1024 * 1024)


def _dot(a, b):
    return jnp.dot(a, b, preferred_element_type=F32)


def _rms(x, g):
    return x * lax.rsqrt(jnp.mean(x * x, axis=-1, keepdims=True) + EPS) * g


def _gelu(x):
    return 0.5 * x * (1.0 + lax.erf(x * np.float32(0.7071067811865476)))


def _full(shape):
    return pl.BlockSpec(shape, lambda *_: (0,) * len(shape), pipeline_mode=pl.Buffered(1))


def _mod_body(c_ref, w_ref, b_ref, o_ref):
    c = c_ref[...]
    ca = jnp.broadcast_to(c * jax.nn.sigmoid(c), (LANES, c.shape[1])).T[:, 0:1]
    o_ref[0] = jnp.sum(ca * w_ref[0], axis=0, keepdims=True) + b_ref[0]


def _ada_mod(c, ada_w, ada_b):
    n_layers, d, n = ada_w.shape
    tn = n // 4
    return pl.pallas_call(
        _mod_body,
        grid=(n_layers, n // tn),
        in_specs=[pl.BlockSpec((1, d), lambda l, j: (0, 0)),
                  pl.BlockSpec((1, d, tn), lambda l, j: (l, 0, j)),
                  pl.BlockSpec((1, 1, tn), lambda l, j: (l, 0, j))],
        out_specs=pl.BlockSpec((1, 1, tn), lambda l, j: (l, 0, j)),
        out_shape=jax.ShapeDtypeStruct((n_layers, 1, n), F32),
        compiler_params=_params("ada_mod", "arbitrary", "arbitrary"),
        name="ada_mod",
    )(c.reshape(1, d), ada_w, ada_b.reshape(n_layers, 1, n))


def _rope_body(pos_ref, invf_ref, cos_ref, sin_ref):
    ang = pos_ref[...].astype(F32) * invf_ref[...]
    cos_ref[...] = jnp.cos(ang)
    sin_ref[...] = jnp.sin(ang)


def _rope_tables(positions):
    s = positions.shape[-1]
    half = ROPE // 2
    per_row = LANES // half
    inv_freq = 1.0 / (ROPE_THETA ** (jnp.arange(0, ROPE, 2, dtype=F32) / ROPE))
    pos_dense = jnp.broadcast_to(positions.reshape(s, 1), (s, half)).reshape(s // per_row, LANES)
    invf = jnp.tile(inv_freq, per_row).reshape(1, LANES)
    rows = s // per_row
    tr = rows // 4
    cos_d, sin_d = pl.pallas_call(
        _rope_body,
        grid=(rows // tr,),
        in_specs=[pl.BlockSpec((tr, LANES), lambda i: (i, 0)), _full((1, LANES))],
        out_specs=[pl.BlockSpec((tr, LANES), lambda i: (i, 0))] * 2,
        out_shape=[jax.ShapeDtypeStruct((rows, LANES), F32)] * 2,
        compiler_params=_params("rope_tables", "arbitrary"),
        name="rope_tables",
    )(pos_dense, invf)
    cos = cos_d.reshape(s, half)
    sin = sin_d.reshape(s, half)
    ones = jnp.ones((s, NOPE), F32)
    zeros = jnp.zeros((s, NOPE), F32)
    pad = jnp.zeros((s, HEAD_PAD - NOPE - ROPE), F32)
    cos_t = jnp.concatenate([ones, cos, cos, pad], axis=1)
    sin_t = jnp.concatenate([zeros, sin, sin, pad], axis=1)
    return cos_t, sin_t


def _gmlp_branch(u, v, ga, lng_ref, lnb_ref, ws_ref, bias_ref, wa_ref):
    tm = u.shape[0]
    gu = _gelu(u)
    gv = _gelu(v)
    mu = jnp.mean(gv, axis=-1, keepdims=True)
    xc = gv - mu
    vn = xc * lax.rsqrt(jnp.mean(xc * xc, axis=-1, keepdims=True) + EPS) * lng_ref[...] + lnb_ref[...]
    vb = vn.astype(BF16)
    t_idx = lax.broadcasted_iota(jnp.int32, (CHUNK, CHUNK), 0)
    s_idx = lax.broadcasted_iota(jnp.int32, (CHUNK, CHUNK), 1)
    causal = s_idx <= t_idx
    ws = [jnp.where(causal, ws_ref[g], 0.0).astype(BF16) for g in range(GMLP_GROUPS)]
    left = lax.broadcasted_iota(jnp.int32, (CHUNK, LANES), 1) < GMLP_GROUP_DIM
    bias = bias_ref[...]
    z_rows = []
    for c in range(tm // CHUNK):
        vc = vb[c * CHUNK:(c + 1) * CHUNK]
        z_cols = []
        for j in range(GMLP_WIDTH // LANES):
            vp = vc[:, j * LANES:(j + 1) * LANES]
            z_cols.append(jnp.where(left, _dot(ws[2 * j], vp), _dot(ws[2 * j + 1], vp)))
        z_rows.append(jnp.concatenate(z_cols, axis=1) + bias)
    z = jnp.concatenate(z_rows, axis=0)
    gated = (gu * z).astype(BF16)
    return jax.nn.sigmoid(ga) * _dot(gated, wa_ref[...])


def _mla_qkv(cq, ckv, kr, cos_ref, sin_ref, qn_ref, kvn_ref, wqm_ref, wk_ref, wvt_ref, vone_ref,
             q_ref, k_ref, vt_ref, scale):
    cos = cos_ref[...]
    sin = sin_ref[...]
    half = ROPE // 2
    lane = lax.broadcasted_iota(jnp.int32, sin.shape, 1)
    sin_x1 = jnp.where(lane < NOPE + half, -sin, 0.0)
    sin_x2 = jnp.where(lane >= NOPE + half, sin, 0.0)

    def rope(x):
        return x * cos + pltpu.roll(x, HEAD_PAD - half, 1) * sin_x1 + pltpu.roll(x, half, 1) * sin_x2

    cqn = _rms(cq, qn_ref[...]).astype(BF16)
    qm = _dot(cqn, wqm_ref[...])
    ckn = _rms(ckv, kvn_ref[...]).astype(BF16)
    km = _dot(ckn, wk_ref[...])
    kpe = rope(kr)
    for h in range(HEADS):
        sl = slice(h * HEAD_PAD, (h + 1) * HEAD_PAD)
        q_ref[:, sl] = (rope(qm[:, sl]) * scale).astype(q_ref.dtype)
        k_ref[:, sl] = (km[:, sl] + kpe).astype(k_ref.dtype)
    vt = lax.dot_general(wvt_ref[...], ckn, (((1,), (1,)), ((), ())), preferred_element_type=F32)
    vt_ref[...] = (vt + jnp.tile(vone_ref[...], (1, vt.shape[1] // LANES))).astype(vt_ref.dtype)


def _front_body(x_ref, g_ref, sc_ref, sh_ref, w_ref, lng_ref, lnb_ref, ws_ref, bias_ref, wa_ref,
                cos_ref, sin_ref, qn_ref, kvn_ref, wqm_ref, wk_ref, wvt_ref, vone_ref,
                a_ref, gb_ref, q_ref, k_ref, vt_ref, *, scale):
    h = _rms(x_ref[...], g_ref[...]) * (1.0 + sc_ref[...]) + sh_ref[...]
    proj = _dot(h.astype(BF16), w_ref[...])
    d = x_ref.shape[1]
    cuts = np.cumsum([0, GMLP_WIDTH, GMLP_WIDTH, Q_RANK, KV_RANK, HEAD_PAD, d, d])
    u, v, cq, ckv, kr, ga, gb = [proj[:, lo:hi] for lo, hi in zip(cuts[:-1], cuts[1:])]
    gb_ref[...] = gb.astype(gb_ref.dtype)
    a_ref[...] = _gmlp_branch(u, v, ga, lng_ref, lnb_ref, ws_ref, bias_ref, wa_ref).astype(a_ref.dtype)
    _mla_qkv(cq, ckv, kr, cos_ref, sin_ref, qn_ref, kvn_ref, wqm_ref, wk_ref, wvt_ref, vone_ref,
             q_ref, k_ref, vt_ref, scale)


def _mixer_front(x, g, sc, sh, w, lng, lnb, ws, bias, wa, cos_t, sin_t, qn, kvn, wqm, wk, wvt, vone):
    s, d = x.shape
    tm = ROW_TILE
    width = HEADS * HEAD_PAD
    row = lambda n: pl.BlockSpec((tm, n), lambda i: (i, 0))
    scale = float((NOPE + ROPE) ** -0.5 * np.log2(np.e))
    consts = (g, sc, sh, w, lng, lnb, ws, bias, wa)
    mla_consts = (qn, kvn, wqm, wk, wvt, vone)
    return pl.pallas_call(
        functools.partial(_front_body, scale=scale),
        grid=(s // tm,),
        in_specs=[row(d)] + [_full(c.shape) for c in consts] + [row(HEAD_PAD), row(HEAD_PAD)]
                 + [_full(c.shape) for c in mla_consts],
        out_specs=[row(d), row(d), row(width), row(width), pl.BlockSpec((wvt.shape[0], tm), lambda i: (0, i))],
        out_shape=[jax.ShapeDtypeStruct((s, d), BF16), jax.ShapeDtypeStruct((s, d), BF16),
                   jax.ShapeDtypeStruct((s, width), BF16), jax.ShapeDtypeStruct((s, width), BF16),
                   jax.ShapeDtypeStruct((wvt.shape[0], s), BF16)],
        compiler_params=_params("mixer_front", "arbitrary"),
        name="mixer_front",
    )(x, *consts, cos_t, sin_t, *mla_consts)


def _attn_body(q_ref, k_ref, vt_ref, o_ref, s_ref, mx_ref, m_ref, acc_ref):
    t = o_ref.shape[0]
    heads = q_ref.shape[1] // HEAD_PAD
    qi = pl.program_id(1)
    m_ref[...] = jnp.full(m_ref.shape, NEG, F32)
    acc_ref[...] = jnp.zeros(acc_ref.shape, F32)

    def scores(blk, slot, q_tile):
        off = pl.multiple_of(blk * t, t)
        q_off = pl.multiple_of(q_tile * t, t)
        for h in range(heads):
            sl = slice(h * HEAD_PAD, (h + 1) * HEAD_PAD)
            s = lax.dot_general(k_ref[pl.ds(off, t), sl], q_ref[pl.ds(q_off, t), sl],
                                (((1,), (1,)), ((), ())), preferred_element_type=F32)
            s_ref[slot, h] = s
            mx_ref[slot, h] = jnp.broadcast_to(jnp.max(s, axis=0, keepdims=True), (STAT_ROWS, t))

    def consume(blk, slot, mask):
        off = pl.multiple_of(blk * t, t)
        for h in range(heads):
            sl = slice(h * HEAD_PAD, (h + 1) * HEAD_PAD)
            s = s_ref[slot, h]
            if mask is None:
                mx = mx_ref[slot, h]
            else:
                s = jnp.where(mask, s, NEG)
                mx = jnp.max(s, axis=0, keepdims=True)
            m_prev = m_ref[h]
            m_new = jnp.maximum(m_prev, mx)
            alpha = jnp.exp2(m_prev - m_new)
            pv = alpha[0:1] * acc_ref[h]
            half = t // 2
            for part in range(2):
                p = jnp.exp2(s[part * half:(part + 1) * half] - m_new[0:1]).astype(BF16)
                keys = pl.ds(pl.multiple_of(off + part * half, half), half)
                pv = pv + _dot(vt_ref[h * VT_ROWS:(h + 1) * VT_ROWS, keys], p)
            acc_ref[h] = pv
            m_ref[h] = m_new

    @pl.when(qi == 0)
    def _():
        scores(0, 0, qi)

    def run(blk, n):
        for u in range(n):
            scores(blk + u + 1, (u + 1) % 2, qi)
            consume(blk + u, u % 2, None)

    def octet(j, carry):
        run(8 * j, 8)
        return carry

    lax.fori_loop(0, qi // 8, octet, 0)
    done = (qi // 8) * 8
    for n in (4, 2):
        more = qi - done >= n

        @pl.when(more)
        def _():
            run(done, n)

        done = done + jnp.where(more, n, 0)

    row = lax.broadcasted_iota(jnp.int32, (t, t), 0)
    col = lax.broadcasted_iota(jnp.int32, (t, t), 1)
    causal = row <= col
    odd = lax.rem(qi, 2) == 1

    @pl.when(odd)
    def _():
        scores(qi, 1, qi)
        consume(qi - 1, 0, None)
        consume(qi, 1, causal)

    @pl.when(jnp.logical_not(odd))
    def _():
        consume(qi, 0, causal)

    def finalize():
        lower = lax.broadcasted_iota(jnp.int32, (t, HEAD_PAD), 1) < VDIM
        for j in range(heads // 2):
            outs = []
            for h in (2 * j, 2 * j + 1):
                acc = acc_ref[h]
                out_t = acc / acc[VDIM:VDIM + 1]
                if VT_ROWS < HEAD_PAD:
                    out_t = jnp.concatenate([out_t, jnp.zeros((HEAD_PAD - VT_ROWS, t), F32)], axis=0)
                outs.append(out_t.T)
            packed = jnp.where(lower, outs[0], pltpu.roll(outs[1], VDIM, 1))
            o_ref[:, j * HEAD_PAD:(j + 1) * HEAD_PAD] = packed.astype(o_ref.dtype)

    last = qi == pl.num_programs(1) - 1

    @pl.when(jnp.logical_not(last))
    def _():
        scores(0, 0, qi + 1)
        finalize()

    @pl.when(last)
    def _():
        finalize()


def _attention(q, k, vt):
    s, width = q.shape
    t = ATTN_TILE
    gw = ATTN_HEADS * HEAD_PAD
    return pl.pallas_call(
        _attn_body,
        grid=(width // gw, s // t),
        in_specs=[pl.BlockSpec((s, gw), lambda h, i: (0, h)),
                  pl.BlockSpec((s, gw), lambda h, i: (0, h)),
                  pl.BlockSpec((ATTN_HEADS * VT_ROWS, s), lambda h, i: (h, 0))],
        out_specs=pl.BlockSpec((t, ATTN_HEADS * VDIM), lambda h, i: (i, h)),
        out_shape=jax.ShapeDtypeStruct((s, HEADS * VDIM), BF16),
        scratch_shapes=[pltpu.VMEM((2, ATTN_HEADS, t, t), F32),
                        pltpu.VMEM((2, ATTN_HEADS, STAT_ROWS, t), F32),
                        pltpu.VMEM((ATTN_HEADS, STAT_ROWS, t), F32),
                        pltpu.VMEM((ATTN_HEADS, VT_ROWS, t), F32)],
        compiler_params=_params("mla_attention", "arbitrary", "arbitrary"),
        name="mla_attention",
    )(q, k, vt)


MERGE_ARGS = 8


def _merge_math(a_ref, gb_ref, o_ref, x_ref, wb_ref, wo_ref, gp_ref, gm_ref):
    yb = _dot(o_ref[...], wb_ref[...])
    merged = a_ref[...].astype(F32) + jax.nn.sigmoid(gb_ref[...].astype(F32)) * yb
    y = _dot(merged.astype(BF16), wo_ref[...])
    return x_ref[...] + gm_ref[...] * _rms(y, gp_ref[...])


def _merge_specs(a, gb, o, x, wb, wo, gp, gm):
    tm = ROW_TILE
    d = x.shape[1]
    row = pl.BlockSpec((tm, d), lambda i: (i, 0))
    return [row, row, pl.BlockSpec((tm, o.shape[1]), lambda i: (i, 0)), row,
            _full(wb.shape), _full(wo.shape), _full((1, d)), _full((1, d))]


def _ffn_body(*refs):
    g_ref, sc_ref, sh_ref, wg_ref, wu_ref, wd_ref, gp_ref, gf_ref, o_ref = refs[MERGE_ARGS:]
    x = _merge_math(*refs[:MERGE_ARGS])
    hb = (_rms(x, g_ref[...]) * (1.0 + sc_ref[...]) + sh_ref[...]).astype(BF16)
    a = _dot(hb, wg_ref[...])
    b = _dot(hb, wu_ref[...])
    y = _dot((a * jax.nn.sigmoid(a) * b).astype(BF16), wd_ref[...])
    o_ref[...] = x + gf_ref[...] * _rms(y, gp_ref[...])


def _dense_ffn(merge_args, g, sc, sh, wg, wu, wd, gp, gf):
    s, d = merge_args[3].shape
    tm = ROW_TILE
    row = pl.BlockSpec((tm, d), lambda i: (i, 0))
    vec = _full((1, d))
    return pl.pallas_call(
        _ffn_body,
        grid=(s // tm,),
        in_specs=_merge_specs(*merge_args)
                 + [vec, vec, vec, _full(wg.shape), _full(wu.shape), _full(wd.shape), vec, vec],
        out_specs=row,
        out_shape=jax.ShapeDtypeStruct((s, d), F32),
        compiler_params=_params("merge_dense_ffn", "arbitrary"),
        name="merge_dense_ffn",
    )(*merge_args, g, sc, sh, wg, wu, wd, gp, gf)


def _route_body(*refs):
    g_ref, sc_ref, sh_ref, wr_ref, br_ref, x_ref, hb_ref, info_ref, infot_ref, cnt_ref = refs[MERGE_ARGS:]
    i = pl.program_id(0)
    tm = x_ref.shape[0]
    x = _merge_math(*refs[:MERGE_ARGS])
    x_ref[...] = x
    h = _rms(x, g_ref[...]) * (1.0 + sc_ref[...]) + sh_ref[...]
    hb = h.astype(BF16)
    hb_ref[...] = hb
    h_lo = (h - hb.astype(F32)).astype(BF16)
    w = wr_ref[...]
    w_hi = w.astype(BF16)
    w_lo = (w - w_hi.astype(F32)).astype(BF16)
    logits = _dot(hb, w_hi) + (_dot(h_lo, w_hi) + _dot(hb, w_lo)) + br_ref[...]

    lane = lax.broadcasted_iota(jnp.int32, (tm, LANES), 1)
    m1 = jnp.max(logits, axis=-1, keepdims=True)
    i1 = jnp.min(jnp.where(logits == m1, lane, LANES), axis=-1, keepdims=True)
    oh1 = lane == i1
    rest = jnp.where(oh1, -3e38, logits)
    m2 = jnp.max(rest, axis=-1, keepdims=True)
    i2 = jnp.min(jnp.where(rest == m2, lane, LANES), axis=-1, keepdims=True)
    oh2 = lane == i2
    ex = jnp.exp(m2 - m1)
    g1 = 1.0 / (1.0 + ex)
    g2 = ex / (1.0 + ex)

    ohf = jnp.where(oh1 | oh2, 1.0, 0.0)
    r_idx = lax.broadcasted_iota(jnp.int32, (tm, tm), 0)
    c_idx = lax.broadcasted_iota(jnp.int32, (tm, tm), 1)
    earlier = jnp.where(c_idx < r_idx, 1.0, 0.0).astype(BF16)
    rank = _dot(earlier, ohf.astype(BF16))
    rank1 = jnp.sum(jnp.where(oh1, rank, 0.0), axis=-1, keepdims=True)
    rank2 = jnp.sum(jnp.where(oh2, rank, 0.0), axis=-1, keepdims=True)
    info = jnp.where(lane == 0, i1.astype(F32),
           jnp.where(lane == 1, i2.astype(F32),
           jnp.where(lane == 2, g1,
           jnp.where(lane == 3, g2,
           jnp.where(lane == 4, rank1,
           jnp.where(lane == 5, rank2, 0.0))))))
    info_ref[...] = info
    infot_ref[...] = info.T[:ROUTE_FIELDS]
    cnt = jnp.sum(ohf, axis=0, keepdims=True).astype(jnp.int32)
    cnt_al = ((cnt + (BF16_ROWS - 1)) // BF16_ROWS) * BF16_ROWS
    for e in range(N_EXPERTS):
        cnt_ref[i * N_EXPERTS + e] = cnt_al[0, e]


def _route(merge_args, g, sc, sh, wr, br):
    s, d = merge_args[3].shape
    tm = ROW_TILE
    nt = s // tm
    row = pl.BlockSpec((tm, d), lambda i: (i, 0))
    return pl.pallas_call(
        _route_body,
        grid=(nt,),
        in_specs=_merge_specs(*merge_args)
                 + [_full((1, d)), _full((1, d)), _full((1, d)), _full(wr.shape), _full(br.shape)],
        out_specs=[row, row,
                   pl.BlockSpec((tm, LANES), lambda i: (i, 0)),
                   pl.BlockSpec((ROUTE_FIELDS, tm), lambda i: (0, i)),
                   pl.BlockSpec(memory_space=pltpu.SMEM)],
        out_shape=[jax.ShapeDtypeStruct((s, d), F32),
                   jax.ShapeDtypeStruct((s, d), BF16),
                   jax.ShapeDtypeStruct((s, LANES), F32),
                   jax.ShapeDtypeStruct((ROUTE_FIELDS, s), F32),
                   jax.ShapeDtypeStruct((nt * N_EXPERTS,), jnp.int32)],
        compiler_params=_params("merge_moe_route", "arbitrary"),
        name="merge_moe_route",
    )(*merge_args, g, sc, sh, wr, br)


def _routing(info):
    return [(info[:, k:k + 1], info[:, 2 + k:3 + k], info[:, 4 + k:5 + k]) for k in range(2)]


def _dispatch_body(offs_ref, cnts_ref, fill_ref, nval_ref, hb_ref, infot_ref, xs_ref, xbuf, zbuf, sem, zsem, *, nblk):
    i = pl.program_id(0)
    tm = hb_ref.shape[0]
    tb = zbuf.shape[0]

    def zero_block(blk):
        cp = pltpu.make_async_copy(zbuf, xs_ref.at[pl.ds(pl.multiple_of(blk * tb, tb), tb)], zsem)
        cp.start()
        cp.wait()

    @pl.when(i == 0)
    def _():
        zbuf[...] = jnp.zeros(zbuf.shape, zbuf.dtype)
        for n in range(fill_ref.shape[0]):
            @pl.when(fill_ref[n] >= 0)
            def _():
                zero_block(fill_ref[n])

        def unused(blk, carry):
            zero_block(blk)
            return carry

        lax.fori_loop(nval_ref[0], nblk, unused, 0)

    hb = hb_ref[...]
    fields = infot_ref[...]
    e1, e2, r1, r2 = fields[0:1], fields[1:2], fields[4:5], fields[5:6]
    chunks = tm // DISPATCH_ROWS
    buf = lax.rem(i, 2)

    def first_slot(e, r):
        return jnp.where(r < float(FIRST_ROWS), e * float(FIRST_ROWS) + r, -1.0)

    p1 = first_slot(e1, r1)
    p2 = first_slot(e2, r2)
    slot = lax.broadcasted_iota(jnp.int32, (N_EXPERTS * FIRST_ROWS, tm), 0).astype(F32)
    sel = jnp.where((p1 == slot) | (p2 == slot), 1.0, 0.0).astype(BF16)
    xbuf[buf, 0] = _dot(sel, hb).astype(BF16)

    first_chunks = FIRST_ROWS // DISPATCH_ROWS

    def buf_rows(e, c):
        return c // first_chunks, e * FIRST_ROWS + (c % first_chunks) * DISPATCH_ROWS

    for e in range(N_EXPERTS):
        for c in range(first_chunks, chunks):
            @pl.when(cnts_ref[i * N_EXPERTS + e] > c * DISPATCH_ROWS)
            def _():
                first = e1 == float(e)
                rank = jnp.where(first, r1, r2)
                late = (lax.broadcasted_iota(jnp.int32, (DISPATCH_ROWS, tm), 0) + c * DISPATCH_ROWS).astype(F32)
                hit = (rank == late) & (first | (e2 == float(e)))
                part, row0 = buf_rows(e, c)
                xbuf[buf, part, row0:row0 + DISPATCH_ROWS] = _dot(jnp.where(hit, 1.0, 0.0).astype(BF16),
                                                                  hb).astype(BF16)

    def copy(b, e, c, dst):
        part, row0 = buf_rows(e, c)
        return pltpu.make_async_copy(xbuf.at[b, part, pl.ds(row0, DISPATCH_ROWS)],
                                     xs_ref.at[pl.ds(dst, DISPATCH_ROWS)], sem.at[b, e, c])

    def for_each_copy(step, fn):
        for e in range(N_EXPERTS):
            for c in range(chunks):
                @pl.when(cnts_ref[step * N_EXPERTS + e] > c * DISPATCH_ROWS)
                def _():
                    fn(e, c)

    @pl.when(i > 0)
    def _():
        for_each_copy(i - 1, lambda e, c: copy(1 - buf, e, c, 0).wait())

    def start(e, c):
        off = offs_ref[i * N_EXPERTS + e]
        copy(buf, e, c, pl.multiple_of(off + c * DISPATCH_ROWS, BF16_ROWS)).start()

    for_each_copy(i, start)

    @pl.when(i == pl.num_programs(0) - 1)
    def _():
        for_each_copy(i, lambda e, c: copy(buf, e, c, 0).wait())


def _dispatch(offs, cnts, fill, nval, hb, infot, nblk):
    s, d = hb.shape
    tm = ROW_TILE
    tb = EXPERT_ROWS
    grid_spec = pltpu.PrefetchScalarGridSpec(
        num_scalar_prefetch=4,
        grid=(s // tm,),
        in_specs=[pl.BlockSpec((tm, d), lambda i, *_: (i, 0)),
                  pl.BlockSpec((ROUTE_FIELDS, tm), lambda i, *_: (0, i))],
        out_specs=pl.BlockSpec(memory_space=pl.ANY),
        scratch_shapes=[pltpu.VMEM((2, tm // FIRST_ROWS, N_EXPERTS * FIRST_ROWS, d), BF16),
                        pltpu.VMEM((tb, d), BF16),
                        pltpu.SemaphoreType.DMA((2, N_EXPERTS, tm // DISPATCH_ROWS)),
                        pltpu.SemaphoreType.DMA(())],
    )
    return pl.pallas_call(
        functools.partial(_dispatch_body, nblk=nblk),
        grid_spec=grid_spec,
        out_shape=jax.ShapeDtypeStruct((nblk * tb, d), BF16),
        compiler_params=_params("moe_dispatch", "arbitrary"),
        name="moe_dispatch",
    )(offs, cnts, fill, nval, hb, infot)


def _expert_body(bexp_ref, nval_ref, x_ref, w1_ref, w3_ref, w2_ref, o_ref, acc_ref, *, nff):
    i = pl.program_id(0)
    f = pl.program_id(1)
    used = i < nval_ref[0]

    def ff_slice(first, final):
        xb = x_ref[...]
        a = _dot(xb, w1_ref[0])
        b = _dot(xb, w3_ref[0])
        acc = _dot((a * jax.nn.sigmoid(a) * b).astype(BF16), w2_ref[0].astype(BF16))
        if not first:
            acc = acc_ref[...] + acc
        if final:
            o_ref[...] = acc.astype(o_ref.dtype)
        else:
            acc_ref[...] = acc

    variants = [(f == 0, True, nff == 1)]
    if nff > 2:
        variants.append(((f > 0) & (f < nff - 1), False, False))
    if nff > 1:
        variants.append((f == nff - 1, False, True))
    for pick, first, final in variants:
        @pl.when(used & pick)
        def _():
            ff_slice(first, final)

    @pl.when(jnp.logical_not(used) & (f == nff - 1))
    def _():
        o_ref[...] = jnp.zeros(o_ref.shape, o_ref.dtype)


def _experts(xs, w1, w3, w2, bexp, nval, tf):
    rows, d = xs.shape
    ff = w1.shape[2]
    tb = EXPERT_ROWS
    nff = ff // tf

    def xrow(i, f, be, nv):
        return (jnp.minimum(i, nv[0] - 1), 0)

    def fcol(i, f, nv):
        return jnp.where(i < nv[0], f, nff - 1)

    grid_spec = pltpu.PrefetchScalarGridSpec(
        num_scalar_prefetch=2,
        grid=(rows // tb, nff),
        in_specs=[pl.BlockSpec((tb, d), xrow),
                  pl.BlockSpec((1, d, tf), lambda i, f, be, nv: (be[i], 0, fcol(i, f, nv))),
                  pl.BlockSpec((1, d, tf), lambda i, f, be, nv: (be[i], 0, fcol(i, f, nv))),
                  pl.BlockSpec((1, tf, d), lambda i, f, be, nv: (be[i], fcol(i, f, nv), 0))],
        out_specs=pl.BlockSpec((tb, d), lambda i, f, be, nv: (i, 0)),
        scratch_shapes=[pltpu.VMEM((tb, d), F32)],
    )
    return pl.pallas_call(
        functools.partial(_expert_body, nff=nff),
        grid_spec=grid_spec,
        out_shape=jax.ShapeDtypeStruct((rows, d), BF16),
        compiler_params=_params("moe_experts", "arbitrary", "arbitrary"),
        name="moe_experts",
    )(bexp, nval, xs, w1, w3, w2)


def _combine_body(offs_ref, cnts_ref, info_ref, x_ref, gp_ref, gf_ref, ys_ref, o_ref, ybuf, acc_ref, sem):
    i = pl.program_id(0)
    nt = pl.num_programs(0)
    tm = x_ref.shape[0]
    chunks = tm // COMBINE_ROWS
    buf = lax.rem(i, 2)

    def copy(b, e, c, src):
        dst = ybuf.at[b, pl.ds((c * N_EXPERTS + e) * COMBINE_ROWS, COMBINE_ROWS)]
        return pltpu.make_async_copy(ys_ref.at[pl.ds(src, COMBINE_ROWS)], dst, sem.at[b, e, c])

    def fetch(step, b):
        for e in range(N_EXPERTS):
            off = offs_ref[step * N_EXPERTS + e]
            copy(b, e, 0, pl.multiple_of(off, BF16_ROWS)).start()
            for c in range(1, chunks):
                @pl.when(cnts_ref[step * N_EXPERTS + e] > c * COMBINE_ROWS)
                def _():
                    copy(b, e, c, pl.multiple_of(off + c * COMBINE_ROWS, BF16_ROWS)).start()

    @pl.when(i == 0)
    def _():
        fetch(0, 0)

    @pl.when(i + 1 < nt)
    def _():
        fetch(i + 1, 1 - buf)

    (e1, g1, r1), (e2, g2, r2) = _routing(info_ref[...])
    for e in range(N_EXPERTS):
        copy(buf, e, 0, 0).wait()
    stack = N_EXPERTS * COMBINE_ROWS
    slot = lax.broadcasted_iota(jnp.int32, (tm, stack), 1).astype(F32)
    y_first = ybuf[buf, 0:stack]
    acc = None
    for ek, gk, rk in ((e1, g1, r1), (e2, g2, r2)):
        pos = jnp.where(rk < float(COMBINE_ROWS), ek * float(COMBINE_ROWS) + rk, -1.0)
        term = gk * _dot(jnp.where(pos == slot, 1.0, 0.0).astype(BF16), y_first)
        acc = term if acc is None else acc + term
    acc_ref[...] = acc

    late = lax.broadcasted_iota(jnp.int32, (tm, COMBINE_ROWS), 1).astype(F32)
    for e in range(N_EXPERTS):
        for c in range(1, chunks):
            @pl.when(cnts_ref[i * N_EXPERTS + e] > c * COMBINE_ROWS)
            def _():
                first = e1 == float(e)
                second = e2 == float(e)
                rank = jnp.where(first, r1, r2)
                gate = jnp.where(first, g1, jnp.where(second, g2, 0.0))
                copy(buf, e, c, 0).wait()
                hit = (rank == late + float(c * COMBINE_ROWS)) & (first | second)
                rows = ybuf[buf, pl.ds((c * N_EXPERTS + e) * COMBINE_ROWS, COMBINE_ROWS)]
                acc_ref[...] += gate * _dot(jnp.where(hit, 1.0, 0.0).astype(BF16), rows)

    o_ref[...] = x_ref[...] + gf_ref[...] * _rms(acc_ref[...], gp_ref[...])


def _combine(offs, cnts, info, x, gp, gf, ys):
    s, d = x.shape
    tm = ROW_TILE
    chunks = tm // COMBINE_ROWS
    grid_spec = pltpu.PrefetchScalarGridSpec(
        num_scalar_prefetch=2,
        grid=(s // tm,),
        in_specs=[pl.BlockSpec((tm, LANES), lambda i, *_: (i, 0)),
                  pl.BlockSpec((tm, d), lambda i, *_: (i, 0)),
                  pl.BlockSpec((1, d), lambda i, *_: (0, 0)),
                  pl.BlockSpec((1, d), lambda i, *_: (0, 0)),
                  pl.BlockSpec(memory_space=pl.ANY)],
        out_specs=pl.BlockSpec((tm, d), lambda i, *_: (i, 0)),
        scratch_shapes=[pltpu.VMEM((2, chunks * N_EXPERTS * COMBINE_ROWS, d), BF16),
                        pltpu.VMEM((tm, d), F32),
                        pltpu.SemaphoreType.DMA((2, N_EXPERTS, chunks))],
    )
    return pl.pallas_call(
        _combine_body,
        grid_spec=grid_spec,
        out_shape=jax.ShapeDtypeStruct((s, d), F32),
        compiler_params=_params("moe_combine", "arbitrary"),
        name="moe_combine",
    )(offs, cnts, info, x, gp, gf, ys)


def _w_in_body(w_ref, o_ref):
    w = w_ref[0]
    rows = w.shape[0]
    head = 2 * GMLP_WIDTH + Q_RANK + KV_RANK
    kr = w[:, head:head + ROPE]
    before = jnp.zeros((rows, NOPE), F32)
    after = jnp.zeros((rows, HEAD_PAD - NOPE - ROPE), F32)
    wide = jnp.concatenate([w[:, :head], before, kr, after, w[:, head + ROPE:]], axis=1)
    o_ref[0] = wide.astype(o_ref.dtype)


def _w_in_layout(w_in):
    n_layers, d, n = w_in.shape
    tk = d // 4
    wide = n - ROPE + HEAD_PAD
    return pl.pallas_call(
        _w_in_body,
        grid=(n_layers, d // tk),
        in_specs=[pl.BlockSpec((1, tk, n), lambda l, k: (l, k, 0))],
        out_specs=pl.BlockSpec((1, tk, wide), lambda l, k: (l, k, 0)),
        out_shape=jax.ShapeDtypeStruct((n_layers, d, wide), BF16),
        compiler_params=_params("w_in_layout", "arbitrary", "arbitrary"),
        name="w_in_layout",
    )(w_in)


def _mixer_weights(w_uq, w_ukv, w_branch_b):
    wq = w_uq.reshape(Q_RANK, HEADS, NOPE + ROPE)
    zq = jnp.zeros((Q_RANK, HEADS, HEAD_PAD - NOPE - ROPE), w_uq.dtype)
    wq_main = jnp.concatenate([wq, zq], axis=-1).reshape(Q_RANK, HEADS * HEAD_PAD).astype(BF16)

    wkv = w_ukv.reshape(KV_RANK, HEADS, NOPE + VDIM)
    zk = jnp.zeros((KV_RANK, HEADS, HEAD_PAD - NOPE), w_ukv.dtype)
    wk = jnp.concatenate([wkv[..., :NOPE], zk], axis=-1).reshape(KV_RANK, HEADS * HEAD_PAD).astype(BF16)
    zv = jnp.zeros((KV_RANK, HEADS, VT_ROWS - VDIM), w_ukv.dtype)
    wv = jnp.concatenate([wkv[..., NOPE:], zv], axis=-1).reshape(KV_RANK, HEADS * VT_ROWS).T.astype(BF16)

    return wq_main, wk, wv, w_branch_b.astype(BF16)


def _expert_layout(cnts, nt, nblk):
    tb = EXPERT_ROWS
    c = cnts.reshape(nt, N_EXPERTS)
    total = jnp.sum(c, axis=0)
    per = jnp.where(total > 0, (total + REGION_SLACK + tb - 1) // tb, 0)
    stop = jnp.cumsum(per)
    start = stop - per
    offs = start[None, :] * tb + (jnp.cumsum(c, axis=0) - c)
    nval = stop[-1]
    blk = jnp.minimum(jnp.arange(nblk, dtype=jnp.int32), nval - 1)
    bexp = jnp.minimum(jnp.sum(blk[:, None] >= stop[None, :], axis=1), N_EXPERTS - 1)
    fill = jnp.concatenate([jnp.where(per > 0, stop - 1, -1), jnp.where(per > 1, stop - 2, -1)])
    i32 = lambda v: v.astype(jnp.int32)
    return i32(offs.reshape(-1)), i32(bexp), i32(nval.reshape(1)), i32(fill)


def kernel(x, c, positions, ada_w, ada_b, norm_mix_pre, norm_mix_post, norm_ffn_pre, norm_ffn_post, w_in, gmlp_ln_g, gmlp_ln_b, gmlp_ws, gmlp_bs, mla_q_norm, mla_w_uq, mla_kv_norm, mla_w_ukv, w_branch_a, w_branch_b, w_out, ffn_w_gate, ffn_w_up, ffn_w_down, moe_router, moe_router_bias, moe_w1, moe_w3, moe_w2):
    batch, s, d = x.shape
    assert batch == 1 and s % ROW_TILE == 0 and s % ATTN_TILE == 0
    depth = ada_w.shape[0]
    xs = x.reshape(s, d)
    mod = _ada_mod(c, ada_w, ada_b)
    w_in_wide = _w_in_layout(w_in)
    cos_t, sin_t = _rope_tables(positions)
    vone = np.zeros((HEADS, VT_ROWS, LANES), np.float32)
    vone[:, VDIM] = 1.0
    vone = jnp.asarray(vone.reshape(HEADS * VT_ROWS, LANES))
    row = lambda v: v.reshape(1, -1)

    for l in range(depth):
        sh_m, sc_m, g_m, sh_f, sc_f, g_f = [mod[l, :, k * d:(k + 1) * d] for k in range(6)]
        wq_main, wk, wv, wb = _mixer_weights(mla_w_uq[l], mla_w_ukv[l], w_branch_b[l])
        bias = jnp.broadcast_to(gmlp_bs[l].T[:, :, None], (CHUNK, GMLP_GROUPS, GMLP_GROUP_DIM))
        bias = bias.reshape(CHUNK, GMLP_WIDTH)
        a, gb, q, k, vt = _mixer_front(xs, row(norm_mix_pre[l]), sc_m, sh_m, w_in_wide[l],
                                       row(gmlp_ln_g[l]), row(gmlp_ln_b[l]), gmlp_ws[l], bias,
                                       w_branch_a[l].astype(BF16), cos_t, sin_t,
                                       row(mla_q_norm[l]), row(mla_kv_norm[l]), wq_main, wk, wv, vone)
        o = _attention(q, k, vt)
        merge_args = (a, gb, o, xs, wb, w_out[l].astype(BF16), row(norm_mix_post[l]), g_m)

        j = l // 2
        if l % 2 == 0:
            wg, wu, wd = ffn_w_gate[j].astype(BF16), ffn_w_up[j].astype(BF16), ffn_w_down[j].astype(BF16)
            xs = _dense_ffn(merge_args, row(norm_ffn_pre[l]), sc_f, sh_f, wg, wu, wd, row(norm_ffn_post[l]), g_f)
        else:
            nt = s // ROW_TILE
            tb = EXPERT_ROWS
            max_rows = 2 * s + nt * N_EXPERTS * (BF16_ROWS - 1) + N_EXPERTS * (REGION_SLACK + tb - 1)
            nblk = -(-max_rows // tb) + 1
            wr = jnp.pad(moe_router[j], ((0, 0), (0, LANES - N_EXPERTS)))
            br = jnp.pad(moe_router_bias[j], (0, LANES - N_EXPERTS), constant_values=NEG).reshape(1, LANES)
            xs, hb, info, infot, cnts = _route(merge_args, row(norm_ffn_pre[l]), sc_f, sh_f, wr, br)
            offs, bexp, nval, fill = _expert_layout(cnts, nt, nblk)
            xsort = _dispatch(offs, cnts, fill, nval, hb, infot, nblk)
            ysort = _experts(xsort, moe_w1[j].astype(BF16), moe_w3[j].astype(BF16), moe_w2[j],
                             bexp, nval, tf=moe_w1.shape[3] // 2)
            xs = _combine(offs, cnts, info, xs, row(norm_ffn_post[l]), g_f, ysort)
    return xs.reshape(batch, s, d)
```

```python
import functools

import jax
import jax.numpy as jnp
import numpy as np
from jax import lax
from jax.experimental import pallas as pl
from jax.experimental.pallas import tpu as pltpu

F32 = jnp.float32
BF16 = jnp.bfloat16

EPS = 1e-6
LANES = 128
BF16_ROWS = 16
GMLP_GROUPS = 8
GMLP_GROUP_DIM = 64
GMLP_WIDTH = GMLP_GROUPS * GMLP_GROUP_DIM
CHUNK = 128
HEADS = 8
NOPE = 64
ROPE = 32
VDIM = 64
HEAD_PAD = 128
VT_ROWS = 128
Q_RANK = 384
KV_RANK = 256
ROPE_THETA = 10000.0
N_EXPERTS = 8
NEG = -1e30

ROW_TILE = 512
ATTN_TILE = 512
ATTN_HEADS = 2
STAT_ROWS = 8
EXPERT_ROWS = 512
DISPATCH_ROWS = 128
COMBINE_ROWS = 256
FIRST_ROWS = 256
ROUTE_FIELDS = 8
REGION_SLACK = DISPATCH_ROWS - BF16_ROWS
VMEM_MIB = {"ada_mod": 24, "rope_tables": 16, "w_in_layout": 32, "mixer_front": 48, "mla_attention": 58, "merge_dense_ffn": 56,
            "merge_moe_route": 32, "moe_dispatch": 32, "moe_experts": 56, "moe_combine": 40}


def _params(name, *sem):
    return pltpu.CompilerParams(dimension_semantics=sem, vmem_limit_bytes=VMEM_MIB[name] * 1024 * 1024)


def _dot(a, b):
    return jnp.dot(a, b, preferred_element_type=F32)


def _rms(x, g):
    return x * lax.rsqrt(jnp.mean(x * x, axis=-1, keepdims=True) + EPS) * g


def _gelu(x):
    return 0.5 * x * (1.0 + lax.erf(x * np.float32(0.7071067811865476)))


def _full(shape):
    return pl.BlockSpec(shape, lambda *_: (0,) * len(shape), pipeline_mode=pl.Buffered(1))


def _mod_body(c_ref, w_ref, b_ref, o_ref):
    c = c_ref[...]
    ca = jnp.broadcast_to(c * jax.nn.sigmoid(c), (LANES, c.shape[1])).T[:, 0:1]
    o_ref[0] = jnp.sum(ca * w_ref[0], axis=0, keepdims=True) + b_ref[0]


def _ada_mod(c, ada_w, ada_b):
    n_layers, d, n = ada_w.shape
    tn = n // 4
    return pl.pallas_call(
        _mod_body,
        grid=(n_layers, n // tn),
        in_specs=[pl.BlockSpec((1, d), lambda l, j: (0, 0)),
                  pl.BlockSpec((1, d, tn), lambda l, j: (l, 0, j)),
                  pl.BlockSpec((1, 1, tn), lambda l, j: (l, 0, j))],
        out_specs=pl.BlockSpec((1, 1, tn), lambda l, j: (l, 0, j)),
        out_shape=jax.ShapeDtypeStruct((n_layers, 1, n), F32),
        compiler_params=_params("ada_mod", "arbitrary", "arbitrary"),
        name="ada_mod",
    )(c.reshape(1, d), ada_w, ada_b.reshape(n_layers, 1, n))


def _rope_body(pos_ref, invf_ref, cos_ref, sin_ref):
    ang = pos_ref[...].astype(F32) * invf_ref[...]
    cos_ref[...] = jnp.cos(ang)
    sin_ref[...] = jnp.sin(ang)


def _rope_tables(positions):
    s = positions.shape[-1]
    half = ROPE // 2
    per_row = LANES // half
    inv_freq = 1.0 / (ROPE_THETA ** (jnp.arange(0, ROPE, 2, dtype=F32) / ROPE))
    pos_dense = jnp.broadcast_to(positions.reshape(s, 1), (s, half)).reshape(s // per_row, LANES)
    invf = jnp.tile(inv_freq, per_row).reshape(1, LANES)
    rows = s // per_row
    tr = rows // 4
    cos_d, sin_d = pl.pallas_call(
        _rope_body,
        grid=(rows // tr,),
        in_specs=[pl.BlockSpec((tr, LANES), lambda i: (i, 0)), _full((1, LANES))],
        out_specs=[pl.BlockSpec((tr, LANES), lambda i: (i, 0))] * 2,
        out_shape=[jax.ShapeDtypeStruct((rows, LANES), F32)] * 2,
        compiler_params=_params("rope_tables", "arbitrary"),
        name="rope_tables",
    )(pos_dense, invf)
    cos = cos_d.reshape(s, half)
    sin = sin_d.reshape(s, half)
    ones = jnp.ones((s, NOPE), F32)
    zeros = jnp.zeros((s, NOPE), F32)
    pad = jnp.zeros((s, HEAD_PAD - NOPE - ROPE), F32)
    cos_t = jnp.concatenate([ones, cos, cos, pad], axis=1)
    sin_t = jnp.concatenate([zeros, sin, sin, pad], axis=1)
    return cos_t, sin_t


def _gmlp_branch(u, v, ga, lng_ref, lnb_ref, ws_ref, bias_ref, wa_ref):
    tm = u.shape[0]
    gu = _gelu(u)
    gv = _gelu(v)
    mu = jnp.mean(gv, axis=-1, keepdims=True)
    xc = gv - mu
    vn = xc * lax.rsqrt(jnp.mean(xc * xc, axis=-1, keepdims=True) + EPS) * lng_ref[...] + lnb_ref[...]
    vb = vn.astype(BF16)
    t_idx = lax.broadcasted_iota(jnp.int32, (CHUNK, CHUNK), 0)
    s_idx = lax.broadcasted_iota(jnp.int32, (CHUNK, CHUNK), 1)
    causal = s_idx <= t_idx
    ws = [jnp.where(causal, ws_ref[g], 0.0).astype(BF16) for g in range(GMLP_GROUPS)]
    left = lax.broadcasted_iota(jnp.int32, (CHUNK, LANES), 1) < GMLP_GROUP_DIM
    bias = bias_ref[...]
    z_rows = []
    for c in range(tm // CHUNK):
        vc = vb[c * CHUNK:(c + 1) * CHUNK]
        z_cols = []
        for j in range(GMLP_WIDTH // LANES):
            vp = vc[:, j * LANES:(j + 1) * LANES]
            z_cols.append(jnp.where(left, _dot(ws[2 * j], vp), _dot(ws[2 * j + 1], vp)))
        z_rows.append(jnp.concatenate(z_cols, axis=1) + bias)
    z = jnp.concatenate(z_rows, axis=0)
    gated = (gu * z).astype(BF16)
    return jax.nn.sigmoid(ga) * _dot(gated, wa_ref[...])


def _mla_qkv(cq, ckv, kr, cos_ref, sin_ref, qn_ref, kvn_ref, wqm_ref, wk_ref, wvt_ref, vone_ref,
             q_ref, k_ref, vt_ref, scale):
    cos = cos_ref[...]
    sin = sin_ref[...]
    half = ROPE // 2
    lane = lax.broadcasted_iota(jnp.int32, sin.shape, 1)
    sin_x1 = jnp.where(lane < NOPE + half, -sin, 0.0)
    sin_x2 = jnp.where(lane >= NOPE + half, sin, 0.0)

    def rope(x):
        return x * cos + pltpu.roll(x, HEAD_PAD - half, 1) * sin_x1 + pltpu.roll(x, half, 1) * sin_x2

    cqn = _rms(cq, qn_ref[...]).astype(BF16)
    qm = _dot(cqn, wqm_ref[...])
    ckn = _rms(ckv, kvn_ref[...]).astype(BF16)
    km = _dot(ckn, wk_ref[...])
    kpe = rope(kr)
    for h in range(HEADS):
        sl = slice(h * HEAD_PAD, (h + 1) * HEAD_PAD)
        q_ref[:, sl] = (rope(qm[:, sl]) * scale).astype(q_ref.dtype)
        k_ref[:, sl] = (km[:, sl] + kpe).astype(k_ref.dtype)
    vt = lax.dot_general(wvt_ref[...], ckn, (((1,), (1,)), ((), ())), preferred_element_type=F32)
    vt_ref[...] = (vt + jnp.tile(vone_ref[...], (1, vt.shape[1] // LANES))).astype(vt_ref.dtype)


def _front_body(x_ref, g_ref, sc_ref, sh_ref, w_ref, lng_ref, lnb_ref, ws_ref, bias_ref, wa_ref,
                cos_ref, sin_ref, qn_ref, kvn_ref, wqm_ref, wk_ref, wvt_ref, vone_ref,
                a_ref, gb_ref, q_ref, k_ref, vt_ref, *, scale):
    h = _rms(x_ref[...], g_ref[...]) * (1.0 + sc_ref[...]) + sh_ref[...]
    proj = _dot(h.astype(BF16), w_ref[...])
    d = x_ref.shape[1]
    cuts = np.cumsum([0, GMLP_WIDTH, GMLP_WIDTH, Q_RANK, KV_RANK, HEAD_PAD, d, d])
    u, v, cq, ckv, kr, ga, gb = [proj[:, lo:hi] for lo, hi in zip(cuts[:-1], cuts[1:])]
    gb_ref[...] = gb.astype(gb_ref.dtype)
    a_ref[...] = _gmlp_branch(u, v, ga, lng_ref, lnb_ref, ws_ref, bias_ref, wa_ref).astype(a_ref.dtype)
    _mla_qkv(cq, ckv, kr, cos_ref, sin_ref, qn_ref, kvn_ref, wqm_ref, wk_ref, wvt_ref, vone_ref,
             q_ref, k_ref, vt_ref, scale)


def _mixer_front(x, g, sc, sh, w, lng, lnb, ws, bias, wa, cos_t, sin_t, qn, kvn, wqm, wk, wvt, vone):
    s, d = x.shape
    tm = ROW_TILE
    width = HEADS * HEAD_PAD
    row = lambda n: pl.BlockSpec((tm, n), lambda i: (i, 0))
    scale = float((NOPE + ROPE) ** -0.5 * np.log2(np.e))
    consts = (g, sc, sh, w, lng, lnb, ws, bias, wa)
    mla_consts = (qn, kvn, wqm, wk, wvt, vone)
    return pl.pallas_call(
        functools.partial(_front_body, scale=scale),
        grid=(s // tm,),
        in_specs=[row(d)] + [_full(c.shape) for c in consts] + [row(HEAD_PAD), row(HEAD_PAD)]
                 + [_full(c.shape) for c in mla_consts],
        out_specs=[row(d), row(d), row(width), row(width), pl.BlockSpec((wvt.shape[0], tm), lambda i: (0, i))],
        out_shape=[jax.ShapeDtypeStruct((s, d), BF16), jax.ShapeDtypeStruct((s, d), BF16),
                   jax.ShapeDtypeStruct((s, width), BF16), jax.ShapeDtypeStruct((s, width), BF16),
                   jax.ShapeDtypeStruct((wvt.shape[0], s), BF16)],
        compiler_params=_params("mixer_front", "arbitrary"),
        name="mixer_front",
    )(x, *consts, cos_t, sin_t, *mla_consts)


def _attn_body(q_ref, k_ref, vt_ref, o_ref, s_ref, mx_ref, m_ref, acc_ref):
    t = o_ref.shape[0]
    heads = q_ref.shape[1] // HEAD_PAD
    qi = pl.program_id(1)
    m_ref[...] = jnp.full(m_ref.shape, NEG, F32)
    acc_ref[...] = jnp.zeros(acc_ref.shape, F32)

    def scores(blk, slot, q_tile):
        off = pl.multiple_of(blk * t, t)
        q_off = pl.multiple_of(q_tile * t, t)
        for h in range(heads):
            sl = slice(h * HEAD_PAD, (h + 1) * HEAD_PAD)
            s = lax.dot_general(k_ref[pl.ds(off, t), sl], q_ref[pl.ds(q_off, t), sl],
                                (((1,), (1,)), ((), ())), preferred_element_type=F32)
            s_ref[slot, h] = s
            mx_ref[slot, h] = jnp.broadcast_to(jnp.max(s, axis=0, keepdims=True), (STAT_ROWS, t))

    def consume(blk, slot, mask):
        off = pl.multiple_of(blk * t, t)
        for h in range(heads):
            sl = slice(h * HEAD_PAD, (h + 1) * HEAD_PAD)
            s = s_ref[slot, h]
            if mask is None:
                mx = mx_ref[slot, h]
            else:
                s = jnp.where(mask, s, NEG)
                mx = jnp.max(s, axis=0, keepdims=True)
            m_prev = m_ref[h]
            m_new = jnp.maximum(m_prev, mx)
            alpha = jnp.exp2(m_prev - m_new)
            pv = alpha[0:1] * acc_ref[h]
            half = t // 2
            for part in range(2):
                p = jnp.exp2(s[part * half:(part + 1) * half] - m_new[0:1]).astype(BF16)
                keys = pl.ds(pl.multiple_of(off + part * half, half), half)
                pv = pv + _dot(vt_ref[h * VT_ROWS:(h + 1) * VT_ROWS, keys], p)
            acc_ref[h] = pv
            m_ref[h] = m_new

    @pl.when(qi == 0)
    def _():
        scores(0, 0, qi)

    def run(blk, n):
        for u in range(n):
            scores(blk + u + 1, (u + 1) % 2, qi)
            consume(blk + u, u % 2, None)

    def octet(j, carry):
        run(8 * j, 8)
        return carry

    lax.fori_loop(0, qi // 8, octet, 0)
    done = (qi // 8) * 8
    for n in (4, 2):
        more = qi - done >= n

        @pl.when(more)
        def _():
            run(done, n)

        done = done + jnp.where(more, n, 0)

    row = lax.broadcasted_iota(jnp.int32, (t, t), 0)
    col = lax.broadcasted_iota(jnp.int32, (t, t), 1)
    causal = row <= col
    odd = lax.rem(qi, 2) == 1

    @pl.when(odd)
    def _():
        scores(qi, 1, qi)
        consume(qi - 1, 0, None)
        consume(qi, 1, causal)

    @pl.when(jnp.logical_not(odd))
    def _():
        consume(qi, 0, causal)

    def finalize():
        lower = lax.broadcasted_iota(jnp.int32, (t, HEAD_PAD), 1) < VDIM
        for j in range(heads // 2):
            outs = []
            for h in (2 * j, 2 * j + 1):
                acc = acc_ref[h]
                out_t = acc / acc[VDIM:VDIM + 1]
                if VT_ROWS < HEAD_PAD:
                    out_t = jnp.concatenate([out_t, jnp.zeros((HEAD_PAD - VT_ROWS, t), F32)], axis=0)
                outs.append(out_t.T)
            packed = jnp.where(lower, outs[0], pltpu.roll(outs[1], VDIM, 1))
            o_ref[:, j * HEAD_PAD:(j + 1) * HEAD_PAD] = packed.astype(o_ref.dtype)

    last = qi == pl.num_programs(1) - 1

    @pl.when(jnp.logical_not(last))
    def _():
        scores(0, 0, qi + 1)
        finalize()

    @pl.when(last)
    def _():
        finalize()


def _attention(q, k, vt):
    s, width = q.shape
    t = ATTN_TILE
    gw = ATTN_HEADS * HEAD_PAD
    return pl.pallas_call(
        _attn_body,
        grid=(width // gw, s // t),
        in_specs=[pl.BlockSpec((s, gw), lambda h, i: (0, h)),
                  pl.BlockSpec((s, gw), lambda h, i: (0, h)),
                  pl.BlockSpec((ATTN_HEADS * VT_ROWS, s), lambda h, i: (h, 0))],
        out_specs=pl.BlockSpec((t, ATTN_HEADS * VDIM), lambda h, i: (i, h)),
        out_shape=jax.ShapeDtypeStruct((s, HEADS * VDIM), BF16),
        scratch_shapes=[pltpu.VMEM((2, ATTN_HEADS, t, t), F32),
                        pltpu.VMEM((2, ATTN_HEADS, STAT_ROWS, t), F32),
                        pltpu.VMEM((ATTN_HEADS, STAT_ROWS, t), F32),
                        pltpu.VMEM((ATTN_HEADS, VT_ROWS, t), F32)],
        compiler_params=_params("mla_attention", "arbitrary", "arbitrary"),
        name="mla_attention",
    )(q, k, vt)


MERGE_ARGS = 8


def _merge_math(a_ref, gb_ref, o_ref, x_ref, wb_ref, wo_ref, gp_ref, gm_ref):
    yb = _dot(o_ref[...], wb_ref[...])
    merged = a_ref[...].astype(F32) + jax.nn.sigmoid(gb_ref[...].astype(F32)) * yb
    y = _dot(merged.astype(BF16), wo_ref[...])
    return x_ref[...] + gm_ref[...] * _rms(y, gp_ref[...])


def _merge_specs(a, gb, o, x, wb, wo, gp, gm):
    tm = ROW_TILE
    d = x.shape[1]
    row = pl.BlockSpec((tm, d), lambda i: (i, 0))
    return [row, row, pl.BlockSpec((tm, o.shape[1]), lambda i: (i, 0)), row,
            _full(wb.shape), _full(wo.shape), _full((1, d)), _full((1, d))]


def _ffn_body(*refs):
    g_ref, sc_ref, sh_ref, wg_ref, wu_ref, wd_ref, gp_ref, gf_ref, o_ref = refs[MERGE_ARGS:]
    x = _merge_math(*refs[:MERGE_ARGS])
    hb = (_rms(x, g_ref[...]) * (1.0 + sc_ref[...]) + sh_ref[...]).astype(BF16)
    a = _dot(hb, wg_ref[...])
    b = _dot(hb, wu_ref[...])
    y = _dot((a * jax.nn.sigmoid(a) * b).astype(BF16), wd_ref[...])
    o_ref[...] = x + gf_ref[...] * _rms(y, gp_ref[...])


def _dense_ffn(merge_args, g, sc, sh, wg, wu, wd, gp, gf):
    s, d = merge_args[3].shape
    tm = ROW_TILE
    row = pl.BlockSpec((tm, d), lambda i: (i, 0))
    vec = _full((1, d))
    return pl.pallas_call(
        _ffn_body,
        grid=(s // tm,),
        in_specs=_merge_specs(*merge_args)
                 + [vec, vec, vec, _full(wg.shape), _full(wu.shape), _full(wd.shape), vec, vec],
        out_specs=row,
        out_shape=jax.ShapeDtypeStruct((s, d), F32),
        compiler_params=_params("merge_dense_ffn", "arbitrary"),
        name="merge_dense_ffn",
    )(*merge_args, g, sc, sh, wg, wu, wd, gp, gf)


def _route_body(*refs):
    g_ref, sc_ref, sh_ref, wr_ref, br_ref, x_ref, hb_ref, info_ref, infot_ref, cnt_ref = refs[MERGE_ARGS:]
    i = pl.program_id(0)
    tm = x_ref.shape[0]
    x = _merge_math(*refs[:MERGE_ARGS])
    x_ref[...] = x
    h = _rms(x, g_ref[...]) * (1.0 + sc_ref[...]) + sh_ref[...]
    hb = h.astype(BF16)
    hb_ref[...] = hb
    h_lo = (h - hb.astype(F32)).astype(BF16)
    w = wr_ref[...]
    w_hi = w.astype(BF16)
    w_lo = (w - w_hi.astype(F32)).astype(BF16)
    logits = _dot(hb, w_hi) + (_dot(h_lo, w_hi) + _dot(hb, w_lo)) + br_ref[...]

    lane = lax.broadcasted_iota(jnp.int32, (tm, LANES), 1)
    m1 = jnp.max(logits, axis=-1, keepdims=True)
    i1 = jnp.min(jnp.where(logits == m1, lane, LANES), axis=-1, keepdims=True)
    oh1 = lane == i1
    rest = jnp.where(oh1, -3e38, logits)
    m2 = jnp.max(rest, axis=-1, keepdims=True)
    i2 = jnp.min(jnp.where(rest == m2, lane, LANES), axis=-1, keepdims=True)
    oh2 = lane == i2
    ex = jnp.exp(m2 - m1)
    g1 = 1.0 / (1.0 + ex)
    g2 = ex / (1.0 + ex)

    ohf = jnp.where(oh1 | oh2, 1.0, 0.0)
    r_idx = lax.broadcasted_iota(jnp.int32, (tm, tm), 0)
    c_idx = lax.broadcasted_iota(jnp.int32, (tm, tm), 1)
    earlier = jnp.where(c_idx < r_idx, 1.0, 0.0).astype(BF16)
    rank = _dot(earlier, ohf.astype(BF16))
    rank1 = jnp.sum(jnp.where(oh1, rank, 0.0), axis=-1, keepdims=True)
    rank2 = jnp.sum(jnp.where(oh2, rank, 0.0), axis=-1, keepdims=True)
    info = jnp.where(lane == 0, i1.astype(F32),
           jnp.where(lane == 1, i2.astype(F32),
           jnp.where(lane == 2, g1,
           jnp.where(lane == 3, g2,
           jnp.where(lane == 4, rank1,
           jnp.where(lane == 5, rank2, 0.0))))))
    info_ref[...] = info
    infot_ref[...] = info.T[:ROUTE_FIELDS]
    cnt = jnp.sum(ohf, axis=0, keepdims=True).astype(jnp.int32)
    cnt_al = ((cnt + (BF16_ROWS - 1)) // BF16_ROWS) * BF16_ROWS
    for e in range(N_EXPERTS):
        cnt_ref[i * N_EXPERTS + e] = cnt_al[0, e]


def _route(merge_args, g, sc, sh, wr, br):
    s, d = merge_args[3].shape
    tm = ROW_TILE
    nt = s // tm
    row = pl.BlockSpec((tm, d), lambda i: (i, 0))
    return pl.pallas_call(
        _route_body,
        grid=(nt,),
        in_specs=_merge_specs(*merge_args)
                 + [_full((1, d)), _full((1, d)), _full((1, d)), _full(wr.shape), _full(br.shape)],
        out_specs=[row, row,
                   pl.BlockSpec((tm, LANES), lambda i: (i, 0)),
                   pl.BlockSpec((ROUTE_FIELDS, tm), lambda i: (0, i)),
                   pl.BlockSpec(memory_space=pltpu.SMEM)],
        out_shape=[jax.ShapeDtypeStruct((s, d), F32),
                   jax.ShapeDtypeStruct((s, d), BF16),
                   jax.ShapeDtypeStruct((s, LANES), F32),
                   jax.ShapeDtypeStruct((ROUTE_FIELDS, s), F32),
                   jax.ShapeDtypeStruct((nt * N_EXPERTS,), jnp.int32)],
        compiler_params=_params("merge_moe_route", "arbitrary"),
        name="merge_moe_route",
    )(*merge_args, g, sc, sh, wr, br)


def _routing(info):
    return [(info[:, k:k + 1], info[:, 2 + k:3 + k], info[:, 4 + k:5 + k]) for k in range(2)]


def _dispatch_body(offs_ref, cnts_ref, fill_ref, nval_ref, hb_ref, infot_ref, xs_ref, xbuf, zbuf, sem, zsem, *, nblk):
    i = pl.program_id(0)
    tm = hb_ref.shape[0]
    tb = zbuf.shape[0]

    def zero_block(blk):
        cp = pltpu.make_async_copy(zbuf, xs_ref.at[pl.ds(pl.multiple_of(blk * tb, tb), tb)], zsem)
        cp.start()
        cp.wait()

    @pl.when(i == 0)
    def _():
        zbuf[...] = jnp.zeros(zbuf.shape, zbuf.dtype)
        for n in range(fill_ref.shape[0]):
            @pl.when(fill_ref[n] >= 0)
            def _():
                zero_block(fill_ref[n])

        def unused(blk, carry):
            zero_block(blk)
            return carry

        lax.fori_loop(nval_ref[0], nblk, unused, 0)

    hb = hb_ref[...]
    fields = infot_ref[...]
    e1, e2, r1, r2 = fields[0:1], fields[1:2], fields[4:5], fields[5:6]
    chunks = tm // DISPATCH_ROWS
    buf = lax.rem(i, 2)

    def first_slot(e, r):
        return jnp.where(r < float(FIRST_ROWS), e * float(FIRST_ROWS) + r, -1.0)

    p1 = first_slot(e1, r1)
    p2 = first_slot(e2, r2)
    slot = lax.broadcasted_iota(jnp.int32, (N_EXPERTS * FIRST_ROWS, tm), 0).astype(F32)
    sel = jnp.where((p1 == slot) | (p2 == slot), 1.0, 0.0).astype(BF16)
    xbuf[buf, 0] = _dot(sel, hb).astype(BF16)

    first_chunks = FIRST_ROWS // DISPATCH_ROWS

    def buf_rows(e, c):
        return c // first_chunks, e * FIRST_ROWS + (c % first_chunks) * DISPATCH_ROWS

    for e in range(N_EXPERTS):
        for c in range(first_chunks, chunks):
            @pl.when(cnts_ref[i * N_EXPERTS + e] > c * DISPATCH_ROWS)
            def _():
                first = e1 == float(e)
                rank = jnp.where(first, r1, r2)
                late = (lax.broadcasted_iota(jnp.int32, (DISPATCH_ROWS, tm), 0) + c * DISPATCH_ROWS).astype(F32)
                hit = (rank == late) & (first | (e2 == float(e)))
                part, row0 = buf_rows(e, c)
                xbuf[buf, part, row0:row0 + DISPATCH_ROWS] = _dot(jnp.where(hit, 1.0, 0.0).astype(BF16),
                                                                  hb).astype(BF16)

    def copy(b, e, c, dst):
        part, row0 = buf_rows(e, c)
        return pltpu.make_async_copy(xbuf.at[b, part, pl.ds(row0, DISPATCH_ROWS)],
                                     xs_ref.at[pl.ds(dst, DISPATCH_ROWS)], sem.at[b, e, c])

    def for_each_copy(step, fn):
        for e in range(N_EXPERTS):
            for c in range(chunks):
                @pl.when(cnts_ref[step * N_EXPERTS + e] > c * DISPATCH_ROWS)
                def _():
                    fn(e, c)

    @pl.when(i > 0)
    def _():
        for_each_copy(i - 1, lambda e, c: copy(1 - buf, e, c, 0).wait())

    def start(e, c):
        off = offs_ref[i * N_EXPERTS + e]
        copy(buf, e, c, pl.multiple_of(off + c * DISPATCH_ROWS, BF16_ROWS)).start()

    for_each_copy(i, start)

    @pl.when(i == pl.num_programs(0) - 1)
    def _():
        for_each_copy(i, lambda e, c: copy(buf, e, c, 0).wait())


def _dispatch(offs, cnts, fill, nval, hb, infot, nblk):
    s, d = hb.shape
    tm = ROW_TILE
    tb = EXPERT_ROWS
    grid_spec = pltpu.PrefetchScalarGridSpec(
        num_scalar_prefetch=4,
        grid=(s // tm,),
        in_specs=[pl.BlockSpec((tm, d), lambda i, *_: (i, 0)),
                  pl.BlockSpec((ROUTE_FIELDS, tm), lambda i, *_: (0, i))],
        out_specs=pl.BlockSpec(memory_space=pl.ANY),
        scratch_shapes=[pltpu.VMEM((2, tm // FIRST_ROWS, N_EXPERTS * FIRST_ROWS, d), BF16),
                        pltpu.VMEM((tb, d), BF16),
                        pltpu.SemaphoreType.DMA((2, N_EXPERTS, tm // DISPATCH_ROWS)),
                        pltpu.SemaphoreType.DMA(())],
    )
    return pl.pallas_call(
        functools.partial(_dispatch_body, nblk=nblk),
        grid_spec=grid_spec,
        out_shape=jax.ShapeDtypeStruct((nblk * tb, d), BF16),
        compiler_params=_params("moe_dispatch", "arbitrary"),
        name="moe_dispatch",
    )(offs, cnts, fill, nval, hb, infot)


def _expert_body(bexp_ref, nval_ref, x_ref, w1_ref, w3_ref, w2_ref, o_ref, acc_ref, *, nff):
    i = pl.program_id(0)
    f = pl.program_id(1)
    used = i < nval_ref[0]

    def ff_slice(first, final):
        xb = x_ref[...]
        a = _dot(xb, w1_ref[0].astype(BF16))
        b = _dot(xb, w3_ref[0])
        acc = _dot((a * jax.nn.sigmoid(a) * b).astype(BF16), w2_ref[0].astype(BF16))
        if not first:
            acc = acc_ref[...] + acc
        if final:
            o_ref[...] = acc.astype(o_ref.dtype)
        else:
            acc_ref[...] = acc

    variants = [(f == 0, True, nff == 1)]
    if nff > 2:
        variants.append(((f > 0) & (f < nff - 1), False, False))
    if nff > 1:
        variants.append((f == nff - 1, False, True))
    for pick, first, final in variants:
        @pl.when(used & pick)
        def _():
            ff_slice(first, final)

    @pl.when(jnp.logical_not(used) & (f == nff - 1))
    def _():
        o_ref[...] = jnp.zeros(o_ref.shape, o_ref.dtype)


def _experts(xs, w1, w3, w2, bexp, nval, tf):
    rows, d = xs.shape
    ff = w1.shape[2]
    tb = EXPERT_ROWS
    nff = ff // tf

    def xrow(i, f, be, nv):
        return (jnp.minimum(i, nv[0] - 1), 0)

    def fcol(i, f, nv):
        return jnp.where(i < nv[0], f, nff - 1)

    grid_spec = pltpu.PrefetchScalarGridSpec(
        num_scalar_prefetch=2,
        grid=(rows // tb, nff),
        in_specs=[pl.BlockSpec((tb, d), xrow),
                  pl.BlockSpec((1, d, tf), lambda i, f, be, nv: (be[i], 0, fcol(i, f, nv))),
                  pl.BlockSpec((1, d, tf), lambda i, f, be, nv: (be[i], 0, fcol(i, f, nv))),
                  pl.BlockSpec((1, tf, d), lambda i, f, be, nv: (be[i], fcol(i, f, nv), 0))],
        out_specs=pl.BlockSpec((tb, d), lambda i, f, be, nv: (i, 0)),
        scratch_shapes=[pltpu.VMEM((tb, d), F32)],
    )
    return pl.pallas_call(
        functools.partial(_expert_body, nff=nff),
        grid_spec=grid_spec,
        out_shape=jax.ShapeDtypeStruct((rows, d), BF16),
        compiler_params=_params("moe_experts", "arbitrary", "arbitrary"),
        name="moe_experts",
    )(bexp, nval, xs, w1, w3, w2)


def _combine_body(offs_ref, cnts_ref, info_ref, x_ref, gp_ref, gf_ref, ys_ref, o_ref, ybuf, acc_ref, sem):
    i = pl.program_id(0)
    nt = pl.num_programs(0)
    tm = x_ref.shape[0]
    chunks = tm // COMBINE_ROWS
    buf = lax.rem(i, 2)

    def copy(b, e, c, src):
        dst = ybuf.at[b, pl.ds((c * N_EXPERTS + e) * COMBINE_ROWS, COMBINE_ROWS)]
        return pltpu.make_async_copy(ys_ref.at[pl.ds(src, COMBINE_ROWS)], dst, sem.at[b, e, c])

    def fetch(step, b):
        for e in range(N_EXPERTS):
            off = offs_ref[step * N_EXPERTS + e]
            copy(b, e, 0, pl.multiple_of(off, BF16_ROWS)).start()
            for c in range(1, chunks):
                @pl.when(cnts_ref[step * N_EXPERTS + e] > c * COMBINE_ROWS)
                def _():
                    copy(b, e, c, pl.multiple_of(off + c * COMBINE_ROWS, BF16_ROWS)).start()

    @pl.when(i == 0)
    def _():
        fetch(0, 0)

    @pl.when(i + 1 < nt)
    def _():
        fetch(i + 1, 1 - buf)

    (e1, g1, r1), (e2, g2, r2) = _routing(info_ref[...])
    for e in range(N_EXPERTS):
        copy(buf, e, 0, 0).wait()
    stack = N_EXPERTS * COMBINE_ROWS
    slot = lax.broadcasted_iota(jnp.int32, (tm, stack), 1).astype(F32)
    y_first = ybuf[buf, 0:stack]
    acc = None
    for ek, gk, rk in ((e1, g1, r1), (e2, g2, r2)):
        pos = jnp.where(rk < float(COMBINE_ROWS), ek * float(COMBINE_ROWS) + rk, -1.0)
        term = gk * _dot(jnp.where(pos == slot, 1.0, 0.0).astype(BF16), y_first)
        acc = term if acc is None else acc + term
    acc_ref[...] = acc

    late = lax.broadcasted_iota(jnp.int32, (tm, COMBINE_ROWS), 1).astype(F32)
    for e in range(N_EXPERTS):
        for c in range(1, chunks):
            @pl.when(cnts_ref[i * N_EXPERTS + e] > c * COMBINE_ROWS)
            def _():
                first = e1 == float(e)
                second = e2 == float(e)
                rank = jnp.where(first, r1, r2)
                gate = jnp.where(first, g1, jnp.where(second, g2, 0.0))
                copy(buf, e, c, 0).wait()
                hit = (rank == late + float(c * COMBINE_ROWS)) & (first | second)
                rows = ybuf[buf, pl.ds((c * N_EXPERTS + e) * COMBINE_ROWS, COMBINE_ROWS)]
                acc_ref[...] += gate * _dot(jnp.where(hit, 1.0, 0.0).astype(BF16), rows)

    o_ref[...] = x_ref[...] + gf_ref[...] * _rms(acc_ref[...], gp_ref[...])


def _combine(offs, cnts, info, x, gp, gf, ys):
    s, d = x.shape
    tm = ROW_TILE
    chunks = tm // COMBINE_ROWS
    grid_spec = pltpu.PrefetchScalarGridSpec(
        num_scalar_prefetch=2,
        grid=(s // tm,),
        in_specs=[pl.BlockSpec((tm, LANES), lambda i, *_: (i, 0)),
                  pl.BlockSpec((tm, d), lambda i, *_: (i, 0)),
                  pl.BlockSpec((1, d), lambda i, *_: (0, 0)),
                  pl.BlockSpec((1, d), lambda i, *_: (0, 0)),
                  pl.BlockSpec(memory_space=pl.ANY)],
        out_specs=pl.BlockSpec((tm, d), lambda i, *_: (i, 0)),
        scratch_shapes=[pltpu.VMEM((2, chunks * N_EXPERTS * COMBINE_ROWS, d), BF16),
                        pltpu.VMEM((tm, d), F32),
                        pltpu.SemaphoreType.DMA((2, N_EXPERTS, chunks))],
    )
    return pl.pallas_call(
        _combine_body,
        grid_spec=grid_spec,
        out_shape=jax.ShapeDtypeStruct((s, d), F32),
        compiler_params=_params("moe_combine", "arbitrary"),
        name="moe_combine",
    )(offs, cnts, info, x, gp, gf, ys)


def _w_in_body(w_ref, o_ref):
    w = w_ref[0]
    rows = w.shape[0]
    head = 2 * GMLP_WIDTH + Q_RANK + KV_RANK
    kr = w[:, head:head + ROPE]
    before = jnp.zeros((rows, NOPE), F32)
    after = jnp.zeros((rows, HEAD_PAD - NOPE - ROPE), F32)
    wide = jnp.concatenate([w[:, :head], before, kr, after, w[:, head + ROPE:]], axis=1)
    o_ref[0] = wide.astype(o_ref.dtype)


def _w_in_layout(w_in):
    n_layers, d, n = w_in.shape
    tk = d // 4
    wide = n - ROPE + HEAD_PAD
    return pl.pallas_call(
        _w_in_body,
        grid=(n_layers, d // tk),
        in_specs=[pl.BlockSpec((1, tk, n), lambda l, k: (l, k, 0))],
        out_specs=pl.BlockSpec((1, tk, wide), lambda l, k: (l, k, 0)),
        out_shape=jax.ShapeDtypeStruct((n_layers, d, wide), BF16),
        compiler_params=_params("w_in_layout", "arbitrary", "arbitrary"),
        name="w_in_layout",
    )(w_in)


def _mixer_weights(w_uq, w_ukv, w_branch_b):
    wq = w_uq.reshape(Q_RANK, HEADS, NOPE + ROPE)
    zq = jnp.zeros((Q_RANK, HEADS, HEAD_PAD - NOPE - ROPE), w_uq.dtype)
    wq_main = jnp.concatenate([wq, zq], axis=-1).reshape(Q_RANK, HEADS * HEAD_PAD).astype(BF16)

    wkv = w_ukv.reshape(KV_RANK, HEADS, NOPE + VDIM)
    zk = jnp.zeros((KV_RANK, HEADS, HEAD_PAD - NOPE), w_ukv.dtype)
    wk = jnp.concatenate([wkv[..., :NOPE], zk], axis=-1).reshape(KV_RANK, HEADS * HEAD_PAD).astype(BF16)
    zv = jnp.zeros((KV_RANK, HEADS, VT_ROWS - VDIM), w_ukv.dtype)
    wv = jnp.concatenate([wkv[..., NOPE:], zv], axis=-1).reshape(KV_RANK, HEADS * VT_ROWS).T.astype(BF16)

    return wq_main, wk, wv, w_branch_b.astype(BF16)


def _expert_layout(cnts, nt, nblk):
    tb = EXPERT_ROWS
    c = cnts.reshape(nt, N_EXPERTS)
    total = jnp.sum(c, axis=0)
    per = jnp.where(total > 0, (total + REGION_SLACK + tb - 1) // tb, 0)
    stop = jnp.cumsum(per)
    start = stop - per
    offs = start[None, :] * tb + (jnp.cumsum(c, axis=0) - c)
    nval = stop[-1]
    blk = jnp.minimum(jnp.arange(nblk, dtype=jnp.int32), nval - 1)
    bexp = jnp.minimum(jnp.sum(blk[:, None] >= stop[None, :], axis=1), N_EXPERTS - 1)
    fill = jnp.concatenate([jnp.where(per > 0, stop - 1, -1), jnp.where(per > 1, stop - 2, -1)])
    i32 = lambda v: v.astype(jnp.int32)
    return i32(offs.reshape(-1)), i32(bexp), i32(nval.reshape(1)), i32(fill)


def kernel(x, c, positions, ada_w, ada_b, norm_mix_pre, norm_mix_post, norm_ffn_pre, norm_ffn_post, w_in, gmlp_ln_g, gmlp_ln_b, gmlp_ws, gmlp_bs, mla_q_norm, mla_w_uq, mla_kv_norm, mla_w_ukv, w_branch_a, w_branch_b, w_out, ffn_w_gate, ffn_w_up, ffn_w_down, moe_router, moe_router_bias, moe_w1, moe_w3, moe_w2):
    batch, s, d = x.shape
    assert batch == 1 and s % ROW_TILE == 0 and s % ATTN_TILE == 0
    depth = ada_w.shape[0]
    xs = x.reshape(s, d)
    mod = _ada_mod(c, ada_w, ada_b)
    w_in_wide = _w_in_layout(w_in)
    cos_t, sin_t = _rope_tables(positions)
    vone = np.zeros((HEADS, VT_ROWS, LANES), np.float32)
    vone[:, VDIM] = 1.0
    vone = jnp.asarray(vone.reshape(HEADS * VT_ROWS, LANES))
    row = lambda v: v.reshape(1, -1)

    for l in range(depth):
        sh_m, sc_m, g_m, sh_f, sc_f, g_f = [mod[l, :, k * d:(k + 1) * d] for k in range(6)]
        wq_main, wk, wv, wb = _mixer_weights(mla_w_uq[l], mla_w_ukv[l], w_branch_b[l])
        bias = jnp.broadcast_to(gmlp_bs[l].T[:, :, None], (CHUNK, GMLP_GROUPS, GMLP_GROUP_DIM))
        bias = bias.reshape(CHUNK, GMLP_WIDTH)
        a, gb, q, k, vt = _mixer_front(xs, row(norm_mix_pre[l]), sc_m, sh_m, w_in_wide[l],
                                       row(gmlp_ln_g[l]), row(gmlp_ln_b[l]), gmlp_ws[l], bias,
                                       w_branch_a[l].astype(BF16), cos_t, sin_t,
                                       row(mla_q_norm[l]), row(mla_kv_norm[l]), wq_main, wk, wv, vone)
        o = _attention(q, k, vt)
        merge_args = (a, gb, o, xs, wb, w_out[l].astype(BF16), row(norm_mix_post[l]), g_m)

        j = l // 2
        if l % 2 == 0:
            wg, wu, wd = ffn_w_gate[j].astype(BF16), ffn_w_up[j].astype(BF16), ffn_w_down[j].astype(BF16)
            xs = _dense_ffn(merge_args, row(norm_ffn_pre[l]), sc_f, sh_f, wg, wu, wd, row(norm_ffn_post[l]), g_f)
        else:
            nt = s // ROW_TILE
            tb = EXPERT_ROWS
            max_rows = 2 * s + nt * N_EXPERTS * (BF16_ROWS - 1) + N_EXPERTS * (REGION_SLACK + tb - 1)
            nblk = -(-max_rows // tb) + 1
            wr = jnp.pad(moe_router[j], ((0, 0), (0, LANES - N_EXPERTS)))
            br = jnp.pad(moe_router_bias[j], (0, LANES - N_EXPERTS), constant_values=NEG).reshape(1, LANES)
            xs, hb, info, infot, cnts = _route(merge_args, row(norm_ffn_pre[l]), sc_f, sh_f, wr, br)
            offs, bexp, nval, fill = _expert_layout(cnts, nt, nblk)
            xsort = _dispatch(offs, cnts, fill, nval, hb, infot, nblk)
            ysort = _experts(xsort, moe_w1[j], moe_w3[j].astype(BF16), moe_w2[j],
                             bexp, nval, tf=moe_w1.shape[3] // 2)
            xs = _combine(offs, cnts, info, xs, row(norm_ffn_post[l]), g_f, ysort)
    return xs.reshape(batch, s, d)
```

```python
import functools

import jax
import jax.numpy as jnp
import numpy as np
from jax import lax
from jax.experimental import pallas as pl
from jax.experimental.pallas import tpu as pltpu

F32 = jnp.float32
BF16 = jnp.bfloat16

EPS = 1e-6
LANES = 128
BF16_ROWS = 16
GMLP_GROUPS = 8
GMLP_GROUP_DIM = 64
GMLP_WIDTH = GMLP_GROUPS * GMLP_GROUP_DIM
CHUNK = 128
HEADS = 8
NOPE = 64
ROPE = 32
VDIM = 64
HEAD_PAD = 128
VT_ROWS = 128
Q_RANK = 384
KV_RANK = 256
ROPE_THETA = 10000.0
N_EXPERTS = 8
NEG = -1e30

ROW_TILE = 512
ATTN_TILE = 512
ATTN_HEADS = 2
STAT_ROWS = 8
EXPERT_ROWS = 512
DISPATCH_ROWS = 128
COMBINE_ROWS = 256
FIRST_ROWS = 256
ROUTE_FIELDS = 8
REGION_SLACK = DISPATCH_ROWS - BF16_ROWS
VMEM_MIB = {"ada_mod": 24, "rope_tables": 16, "w_in_layout": 32, "mixer_front": 48, "mla_attention": 58, "merge_dense_ffn": 56,
            "merge_moe_route": 32, "moe_dispatch": 32, "moe_experts": 48, "moe_combine": 40}


def _params(name, *sem):
    return pltpu.CompilerParams(dimension_semantics=sem, vmem_limit_bytes=VMEM_MIB[name] * 1024 * 1024)


def _dot(a, b):
    return jnp.dot(a, b, preferred_element_type=F32)


def _rms(x, g):
    return x * lax.rsqrt(jnp.mean(x * x, axis=-1, keepdims=True) + EPS) * g


def _gelu(x):
    return 0.5 * x * (1.0 + lax.erf(x * np.float32(0.7071067811865476)))


def _full(shape):
    return pl.BlockSpec(shape, lambda *_: (0,) * len(shape), pipeline_mode=pl.Buffered(1))


def _mod_body(c_ref, w_ref, b_ref, o_ref):
    c = c_ref[...]
    ca = jnp.broadcast_to(c * jax.nn.sigmoid(c), (LANES, c.shape[1])).T[:, 0:1]
    o_ref[0] = jnp.sum(ca * w_ref[0], axis=0, keepdims=True) + b_ref[0]


def _ada_mod(c, ada_w, ada_b):
    n_layers, d, n = ada_w.shape
    tn = n // 4
    return pl.pallas_call(
        _mod_body,
        grid=(n_layers, n // tn),
        in_specs=[pl.BlockSpec((1, d), lambda l, j: (0, 0)),
                  pl.BlockSpec((1, d, tn), lambda l, j: (l, 0, j)),
                  pl.BlockSpec((1, 1, tn), lambda l, j: (l, 0, j))],
        out_specs=pl.BlockSpec((1, 1, tn), lambda l, j: (l, 0, j)),
        out_shape=jax.ShapeDtypeStruct((n_layers, 1, n), F32),
        compiler_params=_params("ada_mod", "arbitrary", "arbitrary"),
        name="ada_mod",
    )(c.reshape(1, d), ada_w, ada_b.reshape(n_layers, 1, n))


def _rope_body(pos_ref, invf_ref, cos_ref, sin_ref):
    ang = pos_ref[...].astype(F32) * invf_ref[...]
    cos_ref[...] = jnp.cos(ang)
    sin_ref[...] = jnp.sin(ang)


def _rope_tables(positions):
    s = positions.shape[-1]
    half = ROPE // 2
    per_row = LANES // half
    inv_freq = 1.0 / (ROPE_THETA ** (jnp.arange(0, ROPE, 2, dtype=F32) / ROPE))
    pos_dense = jnp.broadcast_to(positions.reshape(s, 1), (s, half)).reshape(s // per_row, LANES)
    invf = jnp.tile(inv_freq, per_row).reshape(1, LANES)
    rows = s // per_row
    tr = rows // 4
    cos_d, sin_d = pl.pallas_call(
        _rope_body,
        grid=(rows // tr,),
        in_specs=[pl.BlockSpec((tr, LANES), lambda i: (i, 0)), _full((1, LANES))],
        out_specs=[pl.BlockSpec((tr, LANES), lambda i: (i, 0))] * 2,
        out_shape=[jax.ShapeDtypeStruct((rows, LANES), F32)] * 2,
        compiler_params=_params("rope_tables", "arbitrary"),
        name="rope_tables",
    )(pos_dense, invf)
    cos = cos_d.reshape(s, half)
    sin = sin_d.reshape(s, half)
    ones = jnp.ones((s, NOPE), F32)
    zeros = jnp.zeros((s, NOPE), F32)
    pad = jnp.zeros((s, HEAD_PAD - NOPE - ROPE), F32)
    cos_t = jnp.concatenate([ones, cos, cos, pad], axis=1)
    sin_t = jnp.concatenate([zeros, sin, sin, pad], axis=1)
    return cos_t, sin_t


def _gmlp_branch(u, v, ga, lng_ref, lnb_ref, ws_ref, bias_ref, wa_ref):
    tm = u.shape[0]
    gu = _gelu(u)
    gv = _gelu(v)
    mu = jnp.mean(gv, axis=-1, keepdims=True)
    xc = gv - mu
    vn = xc * lax.rsqrt(jnp.mean(xc * xc, axis=-1, keepdims=True) + EPS) * lng_ref[...] + lnb_ref[...]
    vb = vn.astype(BF16)
    t_idx = lax.broadcasted_iota(jnp.int32, (CHUNK, CHUNK), 0)
    s_idx = lax.broadcasted_iota(jnp.int32, (CHUNK, CHUNK), 1)
    causal = s_idx <= t_idx
    ws = [jnp.where(causal, ws_ref[g], 0.0).astype(BF16) for g in range(GMLP_GROUPS)]
    left = lax.broadcasted_iota(jnp.int32, (CHUNK, LANES), 1) < GMLP_GROUP_DIM
    bias = bias_ref[...]
    z_rows = []
    for c in range(tm // CHUNK):
        vc = vb[c * CHUNK:(c + 1) * CHUNK]
        z_cols = []
        for j in range(GMLP_WIDTH // LANES):
            vp = vc[:, j * LANES:(j + 1) * LANES]
            z_cols.append(jnp.where(left, _dot(ws[2 * j], vp), _dot(ws[2 * j + 1], vp)))
        z_rows.append(jnp.concatenate(z_cols, axis=1) + bias)
    z = jnp.concatenate(z_rows, axis=0)
    gated = (gu * z).astype(BF16)
    return jax.nn.sigmoid(ga) * _dot(gated, wa_ref[...])


def _mla_qkv(cq, ckv, kr, cos_ref, sin_ref, qn_ref, kvn_ref, wqm_ref, wk_ref, wvt_ref, vone_ref,
             q_ref, k_ref, vt_ref, scale):
    cos = cos_ref[...]
    sin = sin_ref[...]
    half = ROPE // 2
    lane = lax.broadcasted_iota(jnp.int32, sin.shape, 1)
    sin_x1 = jnp.where(lane < NOPE + half, -sin, 0.0)
    sin_x2 = jnp.where(lane >= NOPE + half, sin, 0.0)

    def rope(x):
        return x * cos + pltpu.roll(x, HEAD_PAD - half, 1) * sin_x1 + pltpu.roll(x, half, 1) * sin_x2

    cqn = _rms(cq, qn_ref[...]).astype(BF16)
    qm = _dot(cqn, wqm_ref[...])
    ckn = _rms(ckv, kvn_ref[...]).astype(BF16)
    km = _dot(ckn, wk_ref[...])
    kpe = rope(kr)
    for h in range(HEADS):
        sl = slice(h * HEAD_PAD, (h + 1) * HEAD_PAD)
        q_ref[:, sl] = (rope(qm[:, sl]) * scale).astype(q_ref.dtype)
        k_ref[:, sl] = (km[:, sl] + kpe).astype(k_ref.dtype)
    vt = lax.dot_general(wvt_ref[...], ckn, (((1,), (1,)), ((), ())), preferred_element_type=F32)
    vt_ref[...] = (vt + jnp.tile(vone_ref[...], (1, vt.shape[1] // LANES))).astype(vt_ref.dtype)


def _front_body(x_ref, g_ref, sc_ref, sh_ref, w_ref, lng_ref, lnb_ref, ws_ref, bias_ref, wa_ref,
                cos_ref, sin_ref, qn_ref, kvn_ref, wqm_ref, wk_ref, wvt_ref, vone_ref,
                a_ref, gb_ref, q_ref, k_ref, vt_ref, *, scale):
    h = _rms(x_ref[...], g_ref[...]) * (1.0 + sc_ref[...]) + sh_ref[...]
    proj = _dot(h.astype(BF16), w_ref[...])
    d = x_ref.shape[1]
    cuts = np.cumsum([0, GMLP_WIDTH, GMLP_WIDTH, Q_RANK, KV_RANK, HEAD_PAD, d, d])
    u, v, cq, ckv, kr, ga, gb = [proj[:, lo:hi] for lo, hi in zip(cuts[:-1], cuts[1:])]
    gb_ref[...] = gb.astype(gb_ref.dtype)
    a_ref[...] = _gmlp_branch(u, v, ga, lng_ref, lnb_ref, ws_ref, bias_ref, wa_ref).astype(a_ref.dtype)
    _mla_qkv(cq, ckv, kr, cos_ref, sin_ref, qn_ref, kvn_ref, wqm_ref, wk_ref, wvt_ref, vone_ref,
             q_ref, k_ref, vt_ref, scale)


def _mixer_front(x, g, sc, sh, w, lng, lnb, ws, bias, wa, cos_t, sin_t, qn, kvn, wqm, wk, wvt, vone):
    s, d = x.shape
    tm = ROW_TILE
    width = HEADS * HEAD_PAD
    row = lambda n: pl.BlockSpec((tm, n), lambda i: (i, 0))
    scale = float((NOPE + ROPE) ** -0.5 * np.log2(np.e))
    consts = (g, sc, sh, w, lng, lnb, ws, bias, wa)
    mla_consts = (qn, kvn, wqm, wk, wvt, vone)
    return pl.pallas_call(
        functools.partial(_front_body, scale=scale),
        grid=(s // tm,),
        in_specs=[row(d)] + [_full(c.shape) for c in consts] + [row(HEAD_PAD), row(HEAD_PAD)]
                 + [_full(c.shape) for c in mla_consts],
        out_specs=[row(d), row(d), row(width), row(width), pl.BlockSpec((wvt.shape[0], tm), lambda i: (0, i))],
        out_shape=[jax.ShapeDtypeStruct((s, d), BF16), jax.ShapeDtypeStruct((s, d), BF16),
                   jax.ShapeDtypeStruct((s, width), BF16), jax.ShapeDtypeStruct((s, width), BF16),
                   jax.ShapeDtypeStruct((wvt.shape[0], s), BF16)],
        compiler_params=_params("mixer_front", "arbitrary"),
        name="mixer_front",
    )(x, *consts, cos_t, sin_t, *mla_consts)


def _attn_body(q_ref, k_ref, vt_ref, o_ref, s_ref, mx_ref, m_ref, acc_ref):
    t = o_ref.shape[0]
    heads = q_ref.shape[1] // HEAD_PAD
    qi = pl.program_id(1)
    m_ref[...] = jnp.full(m_ref.shape, NEG, F32)
    acc_ref[...] = jnp.zeros(acc_ref.shape, F32)

    def scores(blk, slot, q_tile, hs=None):
        off = pl.multiple_of(blk * t, t)
        q_off = pl.multiple_of(q_tile * t, t)
        for h in (range(heads) if hs is None else hs):
            sl = slice(h * HEAD_PAD, (h + 1) * HEAD_PAD)
            s = lax.dot_general(k_ref[pl.ds(off, t), sl], q_ref[pl.ds(q_off, t), sl],
                                (((1,), (1,)), ((), ())), preferred_element_type=F32)
            s_ref[slot, h] = s
            mx_ref[slot, h] = jnp.broadcast_to(jnp.max(s, axis=0, keepdims=True), (STAT_ROWS, t))

    def consume(blk, slot, mask, hs=None):
        off = pl.multiple_of(blk * t, t)
        for h in (range(heads) if hs is None else hs):
            sl = slice(h * HEAD_PAD, (h + 1) * HEAD_PAD)
            s = s_ref[slot, h]
            if mask is None:
                mx = mx_ref[slot, h]
            else:
                s = jnp.where(mask, s, NEG)
                mx = jnp.max(s, axis=0, keepdims=True)
            m_prev = m_ref[h]
            m_new = jnp.maximum(m_prev, mx)
            alpha = jnp.exp2(m_prev - m_new)
            pv = alpha[0:1] * acc_ref[h]
            half = t // 2
            for part in range(2):
                p = jnp.exp2(s[part * half:(part + 1) * half] - m_new[0:1]).astype(BF16)
                keys = pl.ds(pl.multiple_of(off + part * half, half), half)
                pv = pv + _dot(vt_ref[h * VT_ROWS:(h + 1) * VT_ROWS, keys], p)
            acc_ref[h] = pv
            m_ref[h] = m_new

    @pl.when(qi == 0)
    def _():
        scores(0, 0, qi)

    def run(blk, n):
        for u in range(n):
            for h in range(heads):
                scores(blk + u + 1, (u + 1) % 2, qi, [h])
                consume(blk + u, u % 2, None, [h])

    def octet(j, carry):
        run(8 * j, 8)
        return carry

    lax.fori_loop(0, qi // 8, octet, 0)
    done = (qi // 8) * 8
    for n in (4, 2):
        more = qi - done >= n

        @pl.when(more)
        def _():
            run(done, n)

        done = done + jnp.where(more, n, 0)

    row = lax.broadcasted_iota(jnp.int32, (t, t), 0)
    col = lax.broadcasted_iota(jnp.int32, (t, t), 1)
    causal = row <= col
    odd = lax.rem(qi, 2) == 1

    @pl.when(odd)
    def _():
        scores(qi, 1, qi)
        consume(qi - 1, 0, None)
        consume(qi, 1, causal)

    @pl.when(jnp.logical_not(odd))
    def _():
        consume(qi, 0, causal)

    def finalize():
        lower = lax.broadcasted_iota(jnp.int32, (t, HEAD_PAD), 1) < VDIM
        for j in range(heads // 2):
            outs = []
            for h in (2 * j, 2 * j + 1):
                acc = acc_ref[h]
                out_t = acc / acc[VDIM:VDIM + 1]
                if VT_ROWS < HEAD_PAD:
                    out_t = jnp.concatenate([out_t, jnp.zeros((HEAD_PAD - VT_ROWS, t), F32)], axis=0)
                outs.append(out_t.T)
            packed = jnp.where(lower, outs[0], pltpu.roll(outs[1], VDIM, 1))
            o_ref[:, j * HEAD_PAD:(j + 1) * HEAD_PAD] = packed.astype(o_ref.dtype)

    last = qi == pl.num_programs(1) - 1

    @pl.when(jnp.logical_not(last))
    def _():
        scores(0, 0, qi + 1)
        finalize()

    @pl.when(last)
    def _():
        finalize()


def _attention(q, k, vt):
    s, width = q.shape
    t = ATTN_TILE
    gw = ATTN_HEADS * HEAD_PAD
    return pl.pallas_call(
        _attn_body,
        grid=(width // gw, s // t),
        in_specs=[pl.BlockSpec((s, gw), lambda h, i: (0, h)),
                  pl.BlockSpec((s, gw), lambda h, i: (0, h)),
                  pl.BlockSpec((ATTN_HEADS * VT_ROWS, s), lambda h, i: (h, 0))],
        out_specs=pl.BlockSpec((t, ATTN_HEADS * VDIM), lambda h, i: (i, h)),
        out_shape=jax.ShapeDtypeStruct((s, HEADS * VDIM), BF16),
        scratch_shapes=[pltpu.VMEM((2, ATTN_HEADS, t, t), F32),
                        pltpu.VMEM((2, ATTN_HEADS, STAT_ROWS, t), F32),
                        pltpu.VMEM((ATTN_HEADS, STAT_ROWS, t), F32),
                        pltpu.VMEM((ATTN_HEADS, VT_ROWS, t), F32)],
        compiler_params=_params("mla_attention", "arbitrary", "arbitrary"),
        name="mla_attention",
    )(q, k, vt)


MERGE_ARGS = 8


def _merge_math(a_ref, gb_ref, o_ref, x_ref, wb_ref, wo_ref, gp_ref, gm_ref):
    yb = _dot(o_ref[...], wb_ref[...])
    merged = a_ref[...].astype(F32) + jax.nn.sigmoid(gb_ref[...].astype(F32)) * yb
    y = _dot(merged.astype(BF16), wo_ref[...])
    return x_ref[...] + gm_ref[...] * _rms(y, gp_ref[...])


def _merge_specs(a, gb, o, x, wb, wo, gp, gm):
    tm = ROW_TILE
    d = x.shape[1]
    row = pl.BlockSpec((tm, d), lambda i: (i, 0))
    return [row, row, pl.BlockSpec((tm, o.shape[1]), lambda i: (i, 0)), row,
            _full(wb.shape), _full(wo.shape), _full((1, d)), _full((1, d))]


def _ffn_body(*refs):
    g_ref, sc_ref, sh_ref, wg_ref, wu_ref, wd_ref, gp_ref, gf_ref, o_ref = refs[MERGE_ARGS:]
    x = _merge_math(*refs[:MERGE_ARGS])
    hb = (_rms(x, g_ref[...]) * (1.0 + sc_ref[...]) + sh_ref[...]).astype(BF16)
    a = _dot(hb, wg_ref[...])
    b = _dot(hb, wu_ref[...])
    y = _dot((a * jax.nn.sigmoid(a) * b).astype(BF16), wd_ref[...])
    o_ref[...] = x + gf_ref[...] * _rms(y, gp_ref[...])


def _dense_ffn(merge_args, g, sc, sh, wg, wu, wd, gp, gf):
    s, d = merge_args[3].shape
    tm = ROW_TILE
    row = pl.BlockSpec((tm, d), lambda i: (i, 0))
    vec = _full((1, d))
    return pl.pallas_call(
        _ffn_body,
        grid=(s // tm,),
        in_specs=_merge_specs(*merge_args)
                 + [vec, vec, vec, _full(wg.shape), _full(wu.shape), _full(wd.shape), vec, vec],
        out_specs=row,
        out_shape=jax.ShapeDtypeStruct((s, d), F32),
        compiler_params=_params("merge_dense_ffn", "arbitrary"),
        name="merge_dense_ffn",
    )(*merge_args, g, sc, sh, wg, wu, wd, gp, gf)


def _route_body(*refs):
    g_ref, sc_ref, sh_ref, wr_ref, br_ref, x_ref, hb_ref, info_ref, infot_ref, cnt_ref = refs[MERGE_ARGS:]
    i = pl.program_id(0)
    tm = x_ref.shape[0]
    x = _merge_math(*refs[:MERGE_ARGS])
    x_ref[...] = x
    h = _rms(x, g_ref[...]) * (1.0 + sc_ref[...]) + sh_ref[...]
    hb = h.astype(BF16)
    hb_ref[...] = hb
    h_lo = (h - hb.astype(F32)).astype(BF16)
    w = wr_ref[...]
    w_hi = w.astype(BF16)
    w_lo = (w - w_hi.astype(F32)).astype(BF16)
    logits = _dot(hb, w_hi) + (_dot(h_lo, w_hi) + _dot(hb, w_lo)) + br_ref[...]

    lane = lax.broadcasted_iota(jnp.int32, (tm, LANES), 1)
    m1 = jnp.max(logits, axis=-1, keepdims=True)
    i1 = jnp.min(jnp.where(logits == m1, lane, LANES), axis=-1, keepdims=True)
    oh1 = lane == i1
    rest = jnp.where(oh1, -3e38, logits)
    m2 = jnp.max(rest, axis=-1, keepdims=True)
    i2 = jnp.min(jnp.where(rest == m2, lane, LANES), axis=-1, keepdims=True)
    oh2 = lane == i2
    ex = jnp.exp(m2 - m1)
    g1 = 1.0 / (1.0 + ex)
    g2 = ex / (1.0 + ex)

    ohf = jnp.where(oh1 | oh2, 1.0, 0.0)
    r_idx = lax.broadcasted_iota(jnp.int32, (tm, tm), 0)
    c_idx = lax.broadcasted_iota(jnp.int32, (tm, tm), 1)
    earlier = jnp.where(c_idx < r_idx, 1.0, 0.0).astype(BF16)
    rank = _dot(earlier, ohf.astype(BF16))
    rank1 = jnp.sum(jnp.where(oh1, rank, 0.0), axis=-1, keepdims=True)
    rank2 = jnp.sum(jnp.where(oh2, rank, 0.0), axis=-1, keepdims=True)
    info = jnp.where(lane == 0, i1.astype(F32),
           jnp.where(lane == 1, i2.astype(F32),
           jnp.where(lane == 2, g1,
           jnp.where(lane == 3, g2,
           jnp.where(lane == 4, rank1,
           jnp.where(lane == 5, rank2, 0.0))))))
    info_ref[...] = info
    infot_ref[...] = info.T[:ROUTE_FIELDS]
    cnt = jnp.sum(ohf, axis=0, keepdims=True).astype(jnp.int32)
    cnt_al = ((cnt + (BF16_ROWS - 1)) // BF16_ROWS) * BF16_ROWS
    for e in range(N_EXPERTS):
        cnt_ref[i * N_EXPERTS + e] = cnt_al[0, e]


def _route(merge_args, g, sc, sh, wr, br):
    s, d = merge_args[3].shape
    tm = ROW_TILE
    nt = s // tm
    row = pl.BlockSpec((tm, d), lambda i: (i, 0))
    return pl.pallas_call(
        _route_body,
        grid=(nt,),
        in_specs=_merge_specs(*merge_args)
                 + [_full((1, d)), _full((1, d)), _full((1, d)), _full(wr.shape), _full(br.shape)],
        out_specs=[row, row,
                   pl.BlockSpec((tm, LANES), lambda i: (i, 0)),
                   pl.BlockSpec((ROUTE_FIELDS, tm), lambda i: (0, i)),
                   pl.BlockSpec(memory_space=pltpu.SMEM)],
        out_shape=[jax.ShapeDtypeStruct((s, d), F32),
                   jax.ShapeDtypeStruct((s, d), BF16),
                   jax.ShapeDtypeStruct((s, LANES), F32),
                   jax.ShapeDtypeStruct((ROUTE_FIELDS, s), F32),
                   jax.ShapeDtypeStruct((nt * N_EXPERTS,), jnp.int32)],
        compiler_params=_params("merge_moe_route", "arbitrary"),
        name="merge_moe_route",
    )(*merge_args, g, sc, sh, wr, br)


def _routing(info):
    return [(info[:, k:k + 1], info[:, 2 + k:3 + k], info[:, 4 + k:5 + k]) for k in range(2)]


def _dispatch_body(offs_ref, cnts_ref, fill_ref, nval_ref, hb_ref, infot_ref, xs_ref, xbuf, zbuf, sem, zsem, *, nblk):
    i = pl.program_id(0)
    tm = hb_ref.shape[0]
    tb = zbuf.shape[0]

    def zero_block(blk):
        cp = pltpu.make_async_copy(zbuf, xs_ref.at[pl.ds(pl.multiple_of(blk * tb, tb), tb)], zsem)
        cp.start()
        cp.wait()

    @pl.when(i == 0)
    def _():
        zbuf[...] = jnp.zeros(zbuf.shape, zbuf.dtype)
        for n in range(fill_ref.shape[0]):
            @pl.when(fill_ref[n] >= 0)
            def _():
                zero_block(fill_ref[n])

        def unused(blk, carry):
            zero_block(blk)
            return carry

        lax.fori_loop(nval_ref[0], nblk, unused, 0)

    hb = hb_ref[...]
    fields = infot_ref[...]
    e1, e2, r1, r2 = fields[0:1], fields[1:2], fields[4:5], fields[5:6]
    chunks = tm // DISPATCH_ROWS
    buf = lax.rem(i, 2)

    def first_slot(e, r):
        return jnp.where(r < float(FIRST_ROWS), e * float(FIRST_ROWS) + r, -1.0)

    p1 = first_slot(e1, r1)
    p2 = first_slot(e2, r2)
    slot = lax.broadcasted_iota(jnp.int32, (N_EXPERTS * FIRST_ROWS, tm), 0).astype(F32)
    sel = jnp.where((p1 == slot) | (p2 == slot), 1.0, 0.0).astype(BF16)
    xbuf[buf, 0] = _dot(sel, hb).astype(BF16)

    first_chunks = FIRST_ROWS // DISPATCH_ROWS

    def buf_rows(e, c):
        return c // first_chunks, e * FIRST_ROWS + (c % first_chunks) * DISPATCH_ROWS

    for e in range(N_EXPERTS):
        for c in range(first_chunks, chunks):
            @pl.when(cnts_ref[i * N_EXPERTS + e] > c * DISPATCH_ROWS)
            def _():
                first = e1 == float(e)
                rank = jnp.where(first, r1, r2)
                late = (lax.broadcasted_iota(jnp.int32, (DISPATCH_ROWS, tm), 0) + c * DISPATCH_ROWS).astype(F32)
                hit = (rank == late) & (first | (e2 == float(e)))
                part, row0 = buf_rows(e, c)
                xbuf[buf, part, row0:row0 + DISPATCH_ROWS] = _dot(jnp.where(hit, 1.0, 0.0).astype(BF16),
                                                                  hb).astype(BF16)

    def copy(b, e, c, dst):
        part, row0 = buf_rows(e, c)
        return pltpu.make_async_copy(xbuf.at[b, part, pl.ds(row0, DISPATCH_ROWS)],
                                     xs_ref.at[pl.ds(dst, DISPATCH_ROWS)], sem.at[b, e, c])

    def for_each_copy(step, fn):
        for e in range(N_EXPERTS):
            for c in range(chunks):
                @pl.when(cnts_ref[step * N_EXPERTS + e] > c * DISPATCH_ROWS)
                def _():
                    fn(e, c)

    @pl.when(i > 0)
    def _():
        for_each_copy(i - 1, lambda e, c: copy(1 - buf, e, c, 0).wait())

    def start(e, c):
        off = offs_ref[i * N_EXPERTS + e]
        copy(buf, e, c, pl.multiple_of(off + c * DISPATCH_ROWS, BF16_ROWS)).start()

    for_each_copy(i, start)

    @pl.when(i == pl.num_programs(0) - 1)
    def _():
        for_each_copy(i, lambda e, c: copy(buf, e, c, 0).wait())


def _dispatch(offs, cnts, fill, nval, hb, infot, nblk):
    s, d = hb.shape
    tm = ROW_TILE
    tb = EXPERT_ROWS
    grid_spec = pltpu.PrefetchScalarGridSpec(
        num_scalar_prefetch=4,
        grid=(s // tm,),
        in_specs=[pl.BlockSpec((tm, d), lambda i, *_: (i, 0)),
                  pl.BlockSpec((ROUTE_FIELDS, tm), lambda i, *_: (0, i))],
        out_specs=pl.BlockSpec(memory_space=pl.ANY),
        scratch_shapes=[pltpu.VMEM((2, tm // FIRST_ROWS, N_EXPERTS * FIRST_ROWS, d), BF16),
                        pltpu.VMEM((tb, d), BF16),
                        pltpu.SemaphoreType.DMA((2, N_EXPERTS, tm // DISPATCH_ROWS)),
                        pltpu.SemaphoreType.DMA(())],
    )
    return pl.pallas_call(
        functools.partial(_dispatch_body, nblk=nblk),
        grid_spec=grid_spec,
        out_shape=jax.ShapeDtypeStruct((nblk * tb, d), BF16),
        compiler_params=_params("moe_dispatch", "arbitrary"),
        name="moe_dispatch",
    )(offs, cnts, fill, nval, hb, infot)


def _expert_body(bexp_ref, nval_ref, x_ref, w1_ref, w3_ref, w2_ref, o_ref, acc_ref, *, nff):
    i = pl.program_id(0)
    f = pl.program_id(1)
    used = i < nval_ref[0]

    def ff_slice(first, final):
        xb = x_ref[...]
        a = _dot(xb, w1_ref[0])
        b = _dot(xb, w3_ref[0])
        acc = _dot((a * jax.nn.sigmoid(a) * b).astype(BF16), w2_ref[0].astype(BF16))
        if not first:
            acc = acc_ref[...] + acc
        if final:
            o_ref[...] = acc.astype(o_ref.dtype)
        else:
            acc_ref[...] = acc

    variants = [(f == 0, True, nff == 1)]
    if nff > 2:
        variants.append(((f > 0) & (f < nff - 1), False, False))
    if nff > 1:
        variants.append((f == nff - 1, False, True))
    for pick, first, final in variants:
        @pl.when(used & pick)
        def _():
            ff_slice(first, final)

    @pl.when(jnp.logical_not(used) & (f == nff - 1))
    def _():
        o_ref[...] = jnp.zeros(o_ref.shape, o_ref.dtype)


def _experts(xs, w1, w3, w2, bexp, nval, tf):
    rows, d = xs.shape
    ff = w1.shape[2]
    tb = EXPERT_ROWS
    nff = ff // tf

    def xrow(i, f, be, nv):
        return (jnp.minimum(i, nv[0] - 1), 0)

    def fcol(i, f, nv):
        return jnp.where(i < nv[0], f, nff - 1)

    grid_spec = pltpu.PrefetchScalarGridSpec(
        num_scalar_prefetch=2,
        grid=(rows // tb, nff),
        in_specs=[pl.BlockSpec((tb, d), xrow),
                  pl.BlockSpec((1, d, tf), lambda i, f, be, nv: (be[i], 0, fcol(i, f, nv))),
                  pl.BlockSpec((1, d, tf), lambda i, f, be, nv: (be[i], 0, fcol(i, f, nv))),
                  pl.BlockSpec((1, tf, d), lambda i, f, be, nv: (be[i], fcol(i, f, nv), 0))],
        out_specs=pl.BlockSpec((tb, d), lambda i, f, be, nv: (i, 0)),
        scratch_shapes=[pltpu.VMEM((tb, d), F32)],
    )
    return pl.pallas_call(
        functools.partial(_expert_body, nff=nff),
        grid_spec=grid_spec,
        out_shape=jax.ShapeDtypeStruct((rows, d), BF16),
        compiler_params=_params("moe_experts", "arbitrary", "arbitrary"),
        name="moe_experts",
    )(bexp, nval, xs, w1, w3, w2)


def _combine_body(offs_ref, cnts_ref, info_ref, x_ref, gp_ref, gf_ref, ys_ref, o_ref, ybuf, acc_ref, sem):
    i = pl.program_id(0)
    nt = pl.num_programs(0)
    tm = x_ref.shape[0]
    chunks = tm // COMBINE_ROWS
    buf = lax.rem(i, 2)

    def copy(b, e, c, src):
        dst = ybuf.at[b, pl.ds((c * N_EXPERTS + e) * COMBINE_ROWS, COMBINE_ROWS)]
        return pltpu.make_async_copy(ys_ref.at[pl.ds(src, COMBINE_ROWS)], dst, sem.at[b, e, c])

    def fetch(step, b):
        for e in range(N_EXPERTS):
            off = offs_ref[step * N_EXPERTS + e]
            copy(b, e, 0, pl.multiple_of(off, BF16_ROWS)).start()
            for c in range(1, chunks):
                @pl.when(cnts_ref[step * N_EXPERTS + e] > c * COMBINE_ROWS)
                def _():
                    copy(b, e, c, pl.multiple_of(off + c * COMBINE_ROWS, BF16_ROWS)).start()

    @pl.when(i == 0)
    def _():
        fetch(0, 0)

    @pl.when(i + 1 < nt)
    def _():
        fetch(i + 1, 1 - buf)

    (e1, g1, r1), (e2, g2, r2) = _routing(info_ref[...])
    for e in range(N_EXPERTS):
        copy(buf, e, 0, 0).wait()
    stack = N_EXPERTS * COMBINE_ROWS
    slot = lax.broadcasted_iota(jnp.int32, (tm, stack), 1).astype(F32)
    y_first = ybuf[buf, 0:stack]
    acc = None
    for ek, gk, rk in ((e1, g1, r1), (e2, g2, r2)):
        pos = jnp.where(rk < float(COMBINE_ROWS), ek * float(COMBINE_ROWS) + rk, -1.0)
        term = gk * _dot(jnp.where(pos == slot, 1.0, 0.0).astype(BF16), y_first)
        acc = term if acc is None else acc + term
    acc_ref[...] = acc

    late = lax.broadcasted_iota(jnp.int32, (tm, COMBINE_ROWS), 1).astype(F32)
    for e in range(N_EXPERTS):
        for c in range(1, chunks):
            @pl.when(cnts_ref[i * N_EXPERTS + e] > c * COMBINE_ROWS)
            def _():
                first = e1 == float(e)
                second = e2 == float(e)
                rank = jnp.where(first, r1, r2)
                gate = jnp.where(first, g1, jnp.where(second, g2, 0.0))
                copy(buf, e, c, 0).wait()
                hit = (rank == late + float(c * COMBINE_ROWS)) & (first | second)
                rows = ybuf[buf, pl.ds((c * N_EXPERTS + e) * COMBINE_ROWS, COMBINE_ROWS)]
                acc_ref[...] += gate * _dot(jnp.where(hit, 1.0, 0.0).astype(BF16), rows)

    o_ref[...] = x_ref[...] + gf_ref[...] * _rms(acc_ref[...], gp_ref[...])


def _combine(offs, cnts, info, x, gp, gf, ys):
    s, d = x.shape
    tm = ROW_TILE
    chunks = tm // COMBINE_ROWS
    grid_spec = pltpu.PrefetchScalarGridSpec(
        num_scalar_prefetch=2,
        grid=(s // tm,),
        in_specs=[pl.BlockSpec((tm, LANES), lambda i, *_: (i, 0)),
                  pl.BlockSpec((tm, d), lambda i, *_: (i, 0)),
                  pl.BlockSpec((1, d), lambda i, *_: (0, 0)),
                  pl.BlockSpec((1, d), lambda i, *_: (0, 0)),
                  pl.BlockSpec(memory_space=pl.ANY)],
        out_specs=pl.BlockSpec((tm, d), lambda i, *_: (i, 0)),
        scratch_shapes=[pltpu.VMEM((2, chunks * N_EXPERTS * COMBINE_ROWS, d), BF16),
                        pltpu.VMEM((tm, d), F32),
                        pltpu.SemaphoreType.DMA((2, N_EXPERTS, chunks))],
    )
    return pl.pallas_call(
        _combine_body,
        grid_spec=grid_spec,
        out_shape=jax.ShapeDtypeStruct((s, d), F32),
        compiler_params=_params("moe_combine", "arbitrary"),
        name="moe_combine",
    )(offs, cnts, info, x, gp, gf, ys)


def _w_in_body(w_ref, o_ref):
    w = w_ref[0]
    rows = w.shape[0]
    head = 2 * GMLP_WIDTH + Q_RANK + KV_RANK
    kr = w[:, head:head + ROPE]
    before = jnp.zeros((rows, NOPE), F32)
    after = jnp.zeros((rows, HEAD_PAD - NOPE - ROPE), F32)
    wide = jnp.concatenate([w[:, :head], before, kr, after, w[:, head + ROPE:]], axis=1)
    o_ref[0] = wide.astype(o_ref.dtype)


def _w_in_layout(w_in):
    n_layers, d, n = w_in.shape
    tk = d // 4
    wide = n - ROPE + HEAD_PAD
    return pl.pallas_call(
        _w_in_body,
        grid=(n_layers, d // tk),
        in_specs=[pl.BlockSpec((1, tk, n), lambda l, k: (l, k, 0))],
        out_specs=pl.BlockSpec((1, tk, wide), lambda l, k: (l, k, 0)),
        out_shape=jax.ShapeDtypeStruct((n_layers, d, wide), BF16),
        compiler_params=_params("w_in_layout", "arbitrary", "arbitrary"),
        name="w_in_layout",
    )(w_in)


def _mixer_weights(w_uq, w_ukv, w_branch_b):
    wq = w_uq.reshape(Q_RANK, HEADS, NOPE + ROPE)
    zq = jnp.zeros((Q_RANK, HEADS, HEAD_PAD - NOPE - ROPE), w_uq.dtype)
    wq_main = jnp.concatenate([wq, zq], axis=-1).reshape(Q_RANK, HEADS * HEAD_PAD).astype(BF16)

    wkv = w_ukv.reshape(KV_RANK, HEADS, NOPE + VDIM)
    zk = jnp.zeros((KV_RANK, HEADS, HEAD_PAD - NOPE), w_ukv.dtype)
    wk = jnp.concatenate([wkv[..., :NOPE], zk], axis=-1).reshape(KV_RANK, HEADS * HEAD_PAD).astype(BF16)
    zv = jnp.zeros((KV_RANK, HEADS, VT_ROWS - VDIM), w_ukv.dtype)
    wv = jnp.concatenate([wkv[..., NOPE:], zv], axis=-1).reshape(KV_RANK, HEADS * VT_ROWS).T.astype(BF16)

    return wq_main, wk, wv, w_branch_b.astype(BF16)


def _expert_layout(cnts, nt, nblk):
    tb = EXPERT_ROWS
    c = cnts.reshape(nt, N_EXPERTS)
    total = jnp.sum(c, axis=0)
    per = jnp.where(total > 0, (total + REGION_SLACK + tb - 1) // tb, 0)
    stop = jnp.cumsum(per)
    start = stop - per
    offs = start[None, :] * tb + (jnp.cumsum(c, axis=0) - c)
    nval = stop[-1]
    blk = jnp.minimum(jnp.arange(nblk, dtype=jnp.int32), nval - 1)
    bexp = jnp.minimum(jnp.sum(blk[:, None] >= stop[None, :], axis=1), N_EXPERTS - 1)
    fill = jnp.concatenate([jnp.where(per > 0, stop - 1, -1), jnp.where(per > 1, stop - 2, -1)])
    i32 = lambda v: v.astype(jnp.int32)
    return i32(offs.reshape(-1)), i32(bexp), i32(nval.reshape(1)), i32(fill)


def kernel(x, c, positions, ada_w, ada_b, norm_mix_pre, norm_mix_post, norm_ffn_pre, norm_ffn_post, w_in, gmlp_ln_g, gmlp_ln_b, gmlp_ws, gmlp_bs, mla_q_norm, mla_w_uq, mla_kv_norm, mla_w_ukv, w_branch_a, w_branch_b, w_out, ffn_w_gate, ffn_w_up, ffn_w_down, moe_router, moe_router_bias, moe_w1, moe_w3, moe_w2):
    batch, s, d = x.shape
    assert batch == 1 and s % ROW_TILE == 0 and s % ATTN_TILE == 0
    depth = ada_w.shape[0]
    xs = x.reshape(s, d)
    mod = _ada_mod(c, ada_w, ada_b)
    w_in_wide = _w_in_layout(w_in)
    cos_t, sin_t = _rope_tables(positions)
    vone = np.zeros((HEADS, VT_ROWS, LANES), np.float32)
    vone[:, VDIM] = 1.0
    vone = jnp.asarray(vone.reshape(HEADS * VT_ROWS, LANES))
    row = lambda v: v.reshape(1, -1)

    for l in range(depth):
        sh_m, sc_m, g_m, sh_f, sc_f, g_f = [mod[l, :, k * d:(k + 1) * d] for k in range(6)]
        wq_main, wk, wv, wb = _mixer_weights(mla_w_uq[l], mla_w_ukv[l], w_branch_b[l])
        bias = jnp.broadcast_to(gmlp_bs[l].T[:, :, None], (CHUNK, GMLP_GROUPS, GMLP_GROUP_DIM))
        bias = bias.reshape(CHUNK, GMLP_WIDTH)
        a, gb, q, k, vt = _mixer_front(xs, row(norm_mix_pre[l]), sc_m, sh_m, w_in_wide[l],
                                       row(gmlp_ln_g[l]), row(gmlp_ln_b[l]), gmlp_ws[l], bias,
                                       w_branch_a[l].astype(BF16), cos_t, sin_t,
                                       row(mla_q_norm[l]), row(mla_kv_norm[l]), wq_main, wk, wv, vone)
        o = _attention(q, k, vt)
        merge_args = (a, gb, o, xs, wb, w_out[l].astype(BF16), row(norm_mix_post[l]), g_m)

        j = l // 2
        if l % 2 == 0:
            wg, wu, wd = ffn_w_gate[j].astype(BF16), ffn_w_up[j].astype(BF16), ffn_w_down[j].astype(BF16)
            xs = _dense_ffn(merge_args, row(norm_ffn_pre[l]), sc_f, sh_f, wg, wu, wd, row(norm_ffn_post[l]), g_f)
        else:
            nt = s // ROW_TILE
            tb = EXPERT_ROWS
            max_rows = 2 * s + nt * N_EXPERTS * (BF16_ROWS - 1) + N_EXPERTS * (REGION_SLACK + tb - 1)
            nblk = -(-max_rows // tb) + 1
            wr = jnp.pad(moe_router[j], ((0, 0), (0, LANES - N_EXPERTS)))
            br = jnp.pad(moe_router_bias[j], (0, LANES - N_EXPERTS), constant_values=NEG).reshape(1, LANES)
            xs, hb, info, infot, cnts = _route(merge_args, row(norm_ffn_pre[l]), sc_f, sh_f, wr, br)
            offs, bexp, nval, fill = _expert_layout(cnts, nt, nblk)
            xsort = _dispatch(offs, cnts, fill, nval, hb, infot, nblk)
            ysort = _experts(xsort, moe_w1[j].astype(BF16), moe_w3[j].astype(BF16), moe_w2[j],
                             bexp, nval, tf=moe_w1.shape[3] // 2)
            xs = _combine(offs, cnts, info, xs, row(norm_ffn_post[l]), g_f, ysort)
    return xs.reshape(batch, s, d)
```

```python
import functools

import jax
import jax.numpy as jnp
import numpy as np
from jax import lax
from jax.experimental import pallas as pl
from jax.experimental.pallas import tpu as pltpu

F32 = jnp.float32
BF16 = jnp.bfloat16

EPS = 1e-6
LANES = 128
BF16_ROWS = 16
GMLP_GROUPS = 8
GMLP_GROUP_DIM = 64
GMLP_WIDTH = GMLP_GROUPS * GMLP_GROUP_DIM
CHUNK = 128
HEADS = 8
NOPE = 64
ROPE = 32
VDIM = 64
HEAD_PAD = 128
VT_ROWS = 128
Q_RANK = 384
KV_RANK = 256
ROPE_THETA = 10000.0
N_EXPERTS = 8
NEG = -1e30

ROW_TILE = 512
ATTN_TILE = 512
ATTN_HEADS = 2
STAT_ROWS = 8
EXPERT_ROWS = 512
DISPATCH_ROWS = 128
COMBINE_ROWS = 256
FIRST_ROWS = 256
ROUTE_FIELDS = 8
REGION_SLACK = DISPATCH_ROWS - BF16_ROWS
VMEM_MIB = {"ada_mod": 24, "rope_tables": 16, "w_in_layout": 32, "mixer_front": 48, "mla_attention": 58, "merge_dense_ffn": 56,
            "merge_moe_route": 32, "moe_dispatch": 32, "moe_experts": 48, "moe_combine": 40}


def _params(name, *sem):
    return pltpu.CompilerParams(dimension_semantics=sem, vmem_limit_bytes=VMEM_MIB[name] * 1024 * 1024)


def _dot(a, b):
    return jnp.dot(a, b, preferred_element_type=F32)


def _rms(x, g):
    return x * lax.rsqrt(jnp.mean(x * x, axis=-1, keepdims=True) + EPS) * g


def _gelu(x):
    return 0.5 * x * (1.0 + lax.erf(x * np.float32(0.7071067811865476)))


def _full(shape):
    return pl.BlockSpec(shape, lambda *_: (0,) * len(shape), pipeline_mode=pl.Buffered(1))


def _mod_body(c_ref, w_ref, b_ref, o_ref):
    c = c_ref[...]
    ca = jnp.broadcast_to(c * jax.nn.sigmoid(c), (LANES, c.shape[1])).T[:, 0:1]
    o_ref[0] = jnp.sum(ca * w_ref[0], axis=0, keepdims=True) + b_ref[0]


def _ada_mod(c, ada_w, ada_b):
    n_layers, d, n = ada_w.shape
    tn = n // 4
    return pl.pallas_call(
        _mod_body,
        grid=(n_layers, n // tn),
        in_specs=[pl.BlockSpec((1, d), lambda l, j: (0, 0)),
                  pl.BlockSpec((1, d, tn), lambda l, j: (l, 0, j)),
                  pl.BlockSpec((1, 1, tn), lambda l, j: (l, 0, j))],
        out_specs=pl.BlockSpec((1, 1, tn), lambda l, j: (l, 0, j)),
        out_shape=jax.ShapeDtypeStruct((n_layers, 1, n), F32),
        compiler_params=_params("ada_mod", "arbitrary", "arbitrary"),
        name="ada_mod",
    )(c.reshape(1, d), ada_w, ada_b.reshape(n_layers, 1, n))


def _rope_body(pos_ref, invf_ref, cos_ref, sin_ref):
    ang = pos_ref[...].astype(F32) * invf_ref[...]
    cos_ref[...] = jnp.cos(ang)
    sin_ref[...] = jnp.sin(ang)


def _rope_tables(positions):
    s = positions.shape[-1]
    half = ROPE // 2
    per_row = LANES // half
    inv_freq = 1.0 / (ROPE_THETA ** (jnp.arange(0, ROPE, 2, dtype=F32) / ROPE))
    pos_dense = jnp.broadcast_to(positions.reshape(s, 1), (s, half)).reshape(s // per_row, LANES)
    invf = jnp.tile(inv_freq, per_row).reshape(1, LANES)
    rows = s // per_row
    tr = rows // 4
    cos_d, sin_d = pl.pallas_call(
        _rope_body,
        grid=(rows // tr,),
        in_specs=[pl.BlockSpec((tr, LANES), lambda i: (i, 0)), _full((1, LANES))],
        out_specs=[pl.BlockSpec((tr, LANES), lambda i: (i, 0))] * 2,
        out_shape=[jax.ShapeDtypeStruct((rows, LANES), F32)] * 2,
        compiler_params=_params("rope_tables", "arbitrary"),
        name="rope_tables",
    )(pos_dense, invf)
    cos = cos_d.reshape(s, half)
    sin = sin_d.reshape(s, half)
    ones = jnp.ones((s, NOPE), F32)
    zeros = jnp.zeros((s, NOPE), F32)
    pad = jnp.zeros((s, HEAD_PAD - NOPE - ROPE), F32)
    cos_t = jnp.concatenate([ones, cos, cos, pad], axis=1)
    sin_t = jnp.concatenate([zeros, sin, sin, pad], axis=1)
    return cos_t, sin_t


def _gmlp_branch(u, v, ga, lng_ref, lnb_ref, ws_ref, bias_ref, wa_ref):
    tm = u.shape[0]
    gu = _gelu(u)
    gv = _gelu(v)
    mu = jnp.mean(gv, axis=-1, keepdims=True)
    xc = gv - mu
    vn = xc * lax.rsqrt(jnp.mean(xc * xc, axis=-1, keepdims=True) + EPS) * lng_ref[...] + lnb_ref[...]
    vb = vn.astype(BF16)
    t_idx = lax.broadcasted_iota(jnp.int32, (CHUNK, CHUNK), 0)
    s_idx = lax.broadcasted_iota(jnp.int32, (CHUNK, CHUNK), 1)
    causal = s_idx <= t_idx
    ws = [jnp.where(causal, ws_ref[g], 0.0).astype(BF16) for g in range(GMLP_GROUPS)]
    left = lax.broadcasted_iota(jnp.int32, (CHUNK, LANES), 1) < GMLP_GROUP_DIM
    bias = bias_ref[...]
    z_rows = []
    for c in range(tm // CHUNK):
        vc = vb[c * CHUNK:(c + 1) * CHUNK]
        z_cols = []
        for j in range(GMLP_WIDTH // LANES):
            vp = vc[:, j * LANES:(j + 1) * LANES]
            z_cols.append(jnp.where(left, _dot(ws[2 * j], vp), _dot(ws[2 * j + 1], vp)))
        z_rows.append(jnp.concatenate(z_cols, axis=1) + bias)
    z = jnp.concatenate(z_rows, axis=0)
    gated = (gu * z).astype(BF16)
    return jax.nn.sigmoid(ga) * _dot(gated, wa_ref[...])


def _mla_qkv(cq, ckv, kr, cos_ref, sin_ref, qn_ref, kvn_ref, wqm_ref, wk_ref, wvt_ref, vone_ref,
             q_ref, k_ref, vt_ref, scale):
    cos = cos_ref[...]
    sin = sin_ref[...]
    half = ROPE // 2
    lane = lax.broadcasted_iota(jnp.int32, sin.shape, 1)
    sin_x1 = jnp.where(lane < NOPE + half, -sin, 0.0)
    sin_x2 = jnp.where(lane >= NOPE + half, sin, 0.0)

    def rope(x):
        return x * cos + pltpu.roll(x, HEAD_PAD - half, 1) * sin_x1 + pltpu.roll(x, half, 1) * sin_x2

    cqn = _rms(cq, qn_ref[...]).astype(BF16)
    qm = _dot(cqn, wqm_ref[...])
    ckn = _rms(ckv, kvn_ref[...]).astype(BF16)
    km = _dot(ckn, wk_ref[...])
    kpe = rope(kr)
    for h in range(HEADS):
        sl = slice(h * HEAD_PAD, (h + 1) * HEAD_PAD)
        q_ref[:, sl] = (rope(qm[:, sl]) * scale).astype(q_ref.dtype)
        k_ref[:, sl] = (km[:, sl] + kpe).astype(k_ref.dtype)
    vt = lax.dot_general(wvt_ref[...], ckn, (((1,), (1,)), ((), ())), preferred_element_type=F32)
    vt_ref[...] = (vt + jnp.tile(vone_ref[...], (1, vt.shape[1] // LANES))).astype(vt_ref.dtype)


def _front_body(x_ref, g_ref, sc_ref, sh_ref, w_ref, lng_ref, lnb_ref, ws_ref, bias_ref, wa_ref,
                cos_ref, sin_ref, qn_ref, kvn_ref, wqm_ref, wk_ref, wvt_ref, vone_ref,
                a_ref, gb_ref, q_ref, k_ref, vt_ref, *, scale):
    h = _rms(x_ref[...], g_ref[...]) * (1.0 + sc_ref[...]) + sh_ref[...]
    proj = _dot(h.astype(BF16), w_ref[...])
    d = x_ref.shape[1]
    cuts = np.cumsum([0, GMLP_WIDTH, GMLP_WIDTH, Q_RANK, KV_RANK, HEAD_PAD, d, d])
    u, v, cq, ckv, kr, ga, gb = [proj[:, lo:hi] for lo, hi in zip(cuts[:-1], cuts[1:])]
    gb_ref[...] = gb.astype(gb_ref.dtype)
    a_ref[...] = _gmlp_branch(u, v, ga, lng_ref, lnb_ref, ws_ref, bias_ref, wa_ref).astype(a_ref.dtype)
    _mla_qkv(cq, ckv, kr, cos_ref, sin_ref, qn_ref, kvn_ref, wqm_ref, wk_ref, wvt_ref, vone_ref,
             q_ref, k_ref, vt_ref, scale)


def _mixer_front(x, g, sc, sh, w, lng, lnb, ws, bias, wa, cos_t, sin_t, qn, kvn, wqm, wk, wvt, vone):
    s, d = x.shape
    tm = ROW_TILE
    width = HEADS * HEAD_PAD
    row = lambda n: pl.BlockSpec((tm, n), lambda i: (i, 0))
    scale = float((NOPE + ROPE) ** -0.5 * np.log2(np.e))
    consts = (g, sc, sh, w, lng, lnb, ws, bias, wa)
    mla_consts = (qn, kvn, wqm, wk, wvt, vone)
    return pl.pallas_call(
        functools.partial(_front_body, scale=scale),
        grid=(s // tm,),
        in_specs=[row(d)] + [_full(c.shape) for c in consts] + [row(HEAD_PAD), row(HEAD_PAD)]
                 + [_full(c.shape) for c in mla_consts],
        out_specs=[row(d), row(d), row(width), row(width), pl.BlockSpec((wvt.shape[0], tm), lambda i: (0, i))],
        out_shape=[jax.ShapeDtypeStruct((s, d), BF16), jax.ShapeDtypeStruct((s, d), BF16),
                   jax.ShapeDtypeStruct((s, width), BF16), jax.ShapeDtypeStruct((s, width), BF16),
                   jax.ShapeDtypeStruct((wvt.shape[0], s), BF16)],
        compiler_params=_params("mixer_front", "arbitrary"),
        name="mixer_front",
    )(x, *consts, cos_t, sin_t, *mla_consts)


def _attn_body(q_ref, k_ref, vt_ref, o_ref, s_ref, mx_ref, m_ref, acc_ref):
    t = o_ref.shape[0]
    heads = q_ref.shape[1] // HEAD_PAD
    qi = pl.program_id(1)
    m_ref[...] = jnp.full(m_ref.shape, NEG, F32)
    acc_ref[...] = jnp.zeros(acc_ref.shape, F32)

    qs = t // 2
    chains = [(h, c) for h in range(heads) for c in range(2)]

    def scores(blk, slot, q_tile, which=chains):
        off = pl.multiple_of(blk * t, t)
        for h, c in which:
            sl = slice(h * HEAD_PAD, (h + 1) * HEAD_PAD)
            ql = slice(c * qs, (c + 1) * qs)
            q_off = pl.multiple_of(q_tile * t + c * qs, qs)
            s = lax.dot_general(k_ref[pl.ds(off, t), sl], q_ref[pl.ds(q_off, qs), sl],
                                (((1,), (1,)), ((), ())), preferred_element_type=F32)
            s_ref[slot, h, :, ql] = s
            mx_ref[slot, h, :, ql] = jnp.broadcast_to(jnp.max(s, axis=0, keepdims=True), (STAT_ROWS, qs))

    def consume(blk, slot, mask, which=chains):
        off = pl.multiple_of(blk * t, t)
        for h, c in which:
            ql = slice(c * qs, (c + 1) * qs)
            s = s_ref[slot, h, :, ql]
            if mask is None:
                mx = mx_ref[slot, h, :, ql]
            else:
                s = jnp.where(mask[:, ql], s, NEG)
                mx = jnp.max(s, axis=0, keepdims=True)
            m_prev = m_ref[h, :, ql]
            m_new = jnp.maximum(m_prev, mx)
            alpha = jnp.exp2(m_prev - m_new)
            pv = alpha[0:1] * acc_ref[h, :, ql]
            half = t // 2
            for part in range(2):
                p = jnp.exp2(s[part * half:(part + 1) * half] - m_new[0:1]).astype(BF16)
                keys = pl.ds(pl.multiple_of(off + part * half, half), half)
                pv = pv + _dot(vt_ref[h * VT_ROWS:(h + 1) * VT_ROWS, keys], p)
            acc_ref[h, :, ql] = pv
            m_ref[h, :, ql] = m_new

    @pl.when(qi == 0)
    def _():
        scores(0, 0, qi)

    def run(blk, n):
        for u in range(n):
            for chain in chains:
                scores(blk + u + 1, (u + 1) % 2, qi, [chain])
                consume(blk + u, u % 2, None, [chain])

    def octet(j, carry):
        run(8 * j, 8)
        return carry

    lax.fori_loop(0, qi // 8, octet, 0)
    done = (qi // 8) * 8
    for n in (4, 2):
        more = qi - done >= n

        @pl.when(more)
        def _():
            run(done, n)

        done = done + jnp.where(more, n, 0)

    row = lax.broadcasted_iota(jnp.int32, (t, t), 0)
    col = lax.broadcasted_iota(jnp.int32, (t, t), 1)
    causal = row <= col
    odd = lax.rem(qi, 2) == 1

    @pl.when(odd)
    def _():
        scores(qi, 1, qi)
        consume(qi - 1, 0, None)
        consume(qi, 1, causal)

    @pl.when(jnp.logical_not(odd))
    def _():
        consume(qi, 0, causal)

    def finalize():
        lower = lax.broadcasted_iota(jnp.int32, (t, HEAD_PAD), 1) < VDIM
        for j in range(heads // 2):
            outs = []
            for h in (2 * j, 2 * j + 1):
                acc = acc_ref[h]
                out_t = acc / acc[VDIM:VDIM + 1]
                if VT_ROWS < HEAD_PAD:
                    out_t = jnp.concatenate([out_t, jnp.zeros((HEAD_PAD - VT_ROWS, t), F32)], axis=0)
                outs.append(out_t.T)
            packed = jnp.where(lower, outs[0], pltpu.roll(outs[1], VDIM, 1))
            o_ref[:, j * HEAD_PAD:(j + 1) * HEAD_PAD] = packed.astype(o_ref.dtype)

    last = qi == pl.num_programs(1) - 1

    @pl.when(jnp.logical_not(last))
    def _():
        scores(0, 0, qi + 1)
        finalize()

    @pl.when(last)
    def _():
        finalize()


def _attention(q, k, vt):
    s, width = q.shape
    t = ATTN_TILE
    gw = ATTN_HEADS * HEAD_PAD
    return pl.pallas_call(
        _attn_body,
        grid=(width // gw, s // t),
        in_specs=[pl.BlockSpec((s, gw), lambda h, i: (0, h)),
                  pl.BlockSpec((s, gw), lambda h, i: (0, h)),
                  pl.BlockSpec((ATTN_HEADS * VT_ROWS, s), lambda h, i: (h, 0))],
        out_specs=pl.BlockSpec((t, ATTN_HEADS * VDIM), lambda h, i: (i, h)),
        out_shape=jax.ShapeDtypeStruct((s, HEADS * VDIM), BF16),
        scratch_shapes=[pltpu.VMEM((2, ATTN_HEADS, t, t), F32),
                        pltpu.VMEM((2, ATTN_HEADS, STAT_ROWS, t), F32),
                        pltpu.VMEM((ATTN_HEADS, STAT_ROWS, t), F32),
                        pltpu.VMEM((ATTN_HEADS, VT_ROWS, t), F32)],
        compiler_params=_params("mla_attention", "arbitrary", "arbitrary"),
        name="mla_attention",
    )(q, k, vt)


MERGE_ARGS = 8


def _merge_math(a_ref, gb_ref, o_ref, x_ref, wb_ref, wo_ref, gp_ref, gm_ref):
    yb = _dot(o_ref[...], wb_ref[...])
    merged = a_ref[...].astype(F32) + jax.nn.sigmoid(gb_ref[...].astype(F32)) * yb
    y = _dot(merged.astype(BF16), wo_ref[...])
    return x_ref[...] + gm_ref[...] * _rms(y, gp_ref[...])


def _merge_specs(a, gb, o, x, wb, wo, gp, gm):
    tm = ROW_TILE
    d = x.shape[1]
    row = pl.BlockSpec((tm, d), lambda i: (i, 0))
    return [row, row, pl.BlockSpec((tm, o.shape[1]), lambda i: (i, 0)), row,
            _full(wb.shape), _full(wo.shape), _full((1, d)), _full((1, d))]


def _ffn_body(*refs):
    g_ref, sc_ref, sh_ref, wg_ref, wu_ref, wd_ref, gp_ref, gf_ref, o_ref = refs[MERGE_ARGS:]
    x = _merge_math(*refs[:MERGE_ARGS])
    hb = (_rms(x, g_ref[...]) * (1.0 + sc_ref[...]) + sh_ref[...]).astype(BF16)
    a = _dot(hb, wg_ref[...])
    b = _dot(hb, wu_ref[...])
    y = _dot((a * jax.nn.sigmoid(a) * b).astype(BF16), wd_ref[...])
    o_ref[...] = x + gf_ref[...] * _rms(y, gp_ref[...])


def _dense_ffn(merge_args, g, sc, sh, wg, wu, wd, gp, gf):
    s, d = merge_args[3].shape
    tm = ROW_TILE
    row = pl.BlockSpec((tm, d), lambda i: (i, 0))
    vec = _full((1, d))
    return pl.pallas_call(
        _ffn_body,
        grid=(s // tm,),
        in_specs=_merge_specs(*merge_args)
                 + [vec, vec, vec, _full(wg.shape), _full(wu.shape), _full(wd.shape), vec, vec],
        out_specs=row,
        out_shape=jax.ShapeDtypeStruct((s, d), F32),
        compiler_params=_params("merge_dense_ffn", "arbitrary"),
        name="merge_dense_ffn",
    )(*merge_args, g, sc, sh, wg, wu, wd, gp, gf)


def _route_body(*refs):
    g_ref, sc_ref, sh_ref, wr_ref, br_ref, x_ref, hb_ref, info_ref, infot_ref, cnt_ref = refs[MERGE_ARGS:]
    i = pl.program_id(0)
    tm = x_ref.shape[0]
    x = _merge_math(*refs[:MERGE_ARGS])
    x_ref[...] = x
    h = _rms(x, g_ref[...]) * (1.0 + sc_ref[...]) + sh_ref[...]
    hb = h.astype(BF16)
    hb_ref[...] = hb
    h_lo = (h - hb.astype(F32)).astype(BF16)
    w = wr_ref[...]
    w_hi = w.astype(BF16)
    w_lo = (w - w_hi.astype(F32)).astype(BF16)
    logits = _dot(hb, w_hi) + (_dot(h_lo, w_hi) + _dot(hb, w_lo)) + br_ref[...]

    lane = lax.broadcasted_iota(jnp.int32, (tm, LANES), 1)
    m1 = jnp.max(logits, axis=-1, keepdims=True)
    i1 = jnp.min(jnp.where(logits == m1, lane, LANES), axis=-1, keepdims=True)
    oh1 = lane == i1
    rest = jnp.where(oh1, -3e38, logits)
    m2 = jnp.max(rest, axis=-1, keepdims=True)
    i2 = jnp.min(jnp.where(rest == m2, lane, LANES), axis=-1, keepdims=True)
    oh2 = lane == i2
    ex = jnp.exp(m2 - m1)
    g1 = 1.0 / (1.0 + ex)
    g2 = ex / (1.0 + ex)

    ohf = jnp.where(oh1 | oh2, 1.0, 0.0)
    r_idx = lax.broadcasted_iota(jnp.int32, (tm, tm), 0)
    c_idx = lax.broadcasted_iota(jnp.int32, (tm, tm), 1)
    earlier = jnp.where(c_idx < r_idx, 1.0, 0.0).astype(BF16)
    rank = _dot(earlier, ohf.astype(BF16))
    rank1 = jnp.sum(jnp.where(oh1, rank, 0.0), axis=-1, keepdims=True)
    rank2 = jnp.sum(jnp.where(oh2, rank, 0.0), axis=-1, keepdims=True)
    info = jnp.where(lane == 0, i1.astype(F32),
           jnp.where(lane == 1, i2.astype(F32),
           jnp.where(lane == 2, g1,
           jnp.where(lane == 3, g2,
           jnp.where(lane == 4, rank1,
           jnp.where(lane == 5, rank2, 0.0))))))
    info_ref[...] = info
    infot_ref[...] = info.T[:ROUTE_FIELDS]
    cnt = jnp.sum(ohf, axis=0, keepdims=True).astype(jnp.int32)
    cnt_al = ((cnt + (BF16_ROWS - 1)) // BF16_ROWS) * BF16_ROWS
    for e in range(N_EXPERTS):
        cnt_ref[i * N_EXPERTS + e] = cnt_al[0, e]


def _route(merge_args, g, sc, sh, wr, br):
    s, d = merge_args[3].shape
    tm = ROW_TILE
    nt = s // tm
    row = pl.BlockSpec((tm, d), lambda i: (i, 0))
    return pl.pallas_call(
        _route_body,
        grid=(nt,),
        in_specs=_merge_specs(*merge_args)
                 + [_full((1, d)), _full((1, d)), _full((1, d)), _full(wr.shape), _full(br.shape)],
        out_specs=[row, row,
                   pl.BlockSpec((tm, LANES), lambda i: (i, 0)),
                   pl.BlockSpec((ROUTE_FIELDS, tm), lambda i: (0, i)),
                   pl.BlockSpec(memory_space=pltpu.SMEM)],
        out_shape=[jax.ShapeDtypeStruct((s, d), F32),
                   jax.ShapeDtypeStruct((s, d), BF16),
                   jax.ShapeDtypeStruct((s, LANES), F32),
                   jax.ShapeDtypeStruct((ROUTE_FIELDS, s), F32),
                   jax.ShapeDtypeStruct((nt * N_EXPERTS,), jnp.int32)],
        compiler_params=_params("merge_moe_route", "arbitrary"),
        name="merge_moe_route",
    )(*merge_args, g, sc, sh, wr, br)


def _routing(info):
    return [(info[:, k:k + 1], info[:, 2 + k:3 + k], info[:, 4 + k:5 + k]) for k in range(2)]


def _dispatch_body(offs_ref, cnts_ref, fill_ref, nval_ref, hb_ref, infot_ref, xs_ref, xbuf, zbuf, sem, zsem, *, nblk):
    i = pl.program_id(0)
    tm = hb_ref.shape[0]
    tb = zbuf.shape[0]

    def zero_block(blk):
        cp = pltpu.make_async_copy(zbuf, xs_ref.at[pl.ds(pl.multiple_of(blk * tb, tb), tb)], zsem)
        cp.start()
        cp.wait()

    @pl.when(i == 0)
    def _():
        zbuf[...] = jnp.zeros(zbuf.shape, zbuf.dtype)
        for n in range(fill_ref.shape[0]):
            @pl.when(fill_ref[n] >= 0)
            def _():
                zero_block(fill_ref[n])

        def unused(blk, carry):
            zero_block(blk)
            return carry

        lax.fori_loop(nval_ref[0], nblk, unused, 0)

    hb = hb_ref[...]
    fields = infot_ref[...]
    e1, e2, r1, r2 = fields[0:1], fields[1:2], fields[4:5], fields[5:6]
    chunks = tm // DISPATCH_ROWS
    buf = lax.rem(i, 2)

    def first_slot(e, r):
        return jnp.where(r < float(FIRST_ROWS), e * float(FIRST_ROWS) + r, -1.0)

    p1 = first_slot(e1, r1)
    p2 = first_slot(e2, r2)
    slot = lax.broadcasted_iota(jnp.int32, (N_EXPERTS * FIRST_ROWS, tm), 0).astype(F32)
    sel = jnp.where((p1 == slot) | (p2 == slot), 1.0, 0.0).astype(BF16)
    xbuf[buf, 0] = _dot(sel, hb).astype(BF16)

    first_chunks = FIRST_ROWS // DISPATCH_ROWS

    def buf_rows(e, c):
        return c // first_chunks, e * FIRST_ROWS + (c % first_chunks) * DISPATCH_ROWS

    for e in range(N_EXPERTS):
        for c in range(first_chunks, chunks):
            @pl.when(cnts_ref[i * N_EXPERTS + e] > c * DISPATCH_ROWS)
            def _():
                first = e1 == float(e)
                rank = jnp.where(first, r1, r2)
                late = (lax.broadcasted_iota(jnp.int32, (DISPATCH_ROWS, tm), 0) + c * DISPATCH_ROWS).astype(F32)
                hit = (rank == late) & (first | (e2 == float(e)))
                part, row0 = buf_rows(e, c)
                xbuf[buf, part, row0:row0 + DISPATCH_ROWS] = _dot(jnp.where(hit, 1.0, 0.0).astype(BF16),
                                                                  hb).astype(BF16)

    def copy(b, e, c, dst):
        part, row0 = buf_rows(e, c)
        return pltpu.make_async_copy(xbuf.at[b, part, pl.ds(row0, DISPATCH_ROWS)],
                                     xs_ref.at[pl.ds(dst, DISPATCH_ROWS)], sem.at[b, e, c])

    def for_each_copy(step, fn):
        for e in range(N_EXPERTS):
            for c in range(chunks):
                @pl.when(cnts_ref[step * N_EXPERTS + e] > c * DISPATCH_ROWS)
                def _():
                    fn(e, c)

    @pl.when(i > 0)
    def _():
        for_each_copy(i - 1, lambda e, c: copy(1 - buf, e, c, 0).wait())

    def start(e, c):
        off = offs_ref[i * N_EXPERTS + e]
        copy(buf, e, c, pl.multiple_of(off + c * DISPATCH_ROWS, BF16_ROWS)).start()

    for_each_copy(i, start)

    @pl.when(i == pl.num_programs(0) - 1)
    def _():
        for_each_copy(i, lambda e, c: copy(buf, e, c, 0).wait())


def _dispatch(offs, cnts, fill, nval, hb, infot, nblk):
    s, d = hb.shape
    tm = ROW_TILE
    tb = EXPERT_ROWS
    grid_spec = pltpu.PrefetchScalarGridSpec(
        num_scalar_prefetch=4,
        grid=(s // tm,),
        in_specs=[pl.BlockSpec((tm, d), lambda i, *_: (i, 0)),
                  pl.BlockSpec((ROUTE_FIELDS, tm), lambda i, *_: (0, i))],
        out_specs=pl.BlockSpec(memory_space=pl.ANY),
        scratch_shapes=[pltpu.VMEM((2, tm // FIRST_ROWS, N_EXPERTS * FIRST_ROWS, d), BF16),
                        pltpu.VMEM((tb, d), BF16),
                        pltpu.SemaphoreType.DMA((2, N_EXPERTS, tm // DISPATCH_ROWS)),
                        pltpu.SemaphoreType.DMA(())],
    )
    return pl.pallas_call(
        functools.partial(_dispatch_body, nblk=nblk),
        grid_spec=grid_spec,
        out_shape=jax.ShapeDtypeStruct((nblk * tb, d), BF16),
        compiler_params=_params("moe_dispatch", "arbitrary"),
        name="moe_dispatch",
    )(offs, cnts, fill, nval, hb, infot)


def _expert_body(bexp_ref, nval_ref, x_ref, w1_ref, w3_ref, w2_ref, o_ref, acc_ref, *, nff):
    i = pl.program_id(0)
    f = pl.program_id(1)
    used = i < nval_ref[0]

    def ff_slice(first, final):
        xb = x_ref[...]
        a = _dot(xb, w1_ref[0])
        b = _dot(xb, w3_ref[0])
        acc = _dot((a * jax.nn.sigmoid(a) * b).astype(BF16), w2_ref[0].astype(BF16))
        if not first:
            acc = acc_ref[...] + acc
        if final:
            o_ref[...] = acc.astype(o_ref.dtype)
        else:
            acc_ref[...] = acc

    variants = [(f == 0, True, nff == 1)]
    if nff > 2:
        variants.append(((f > 0) & (f < nff - 1), False, False))
    if nff > 1:
        variants.append((f == nff - 1, False, True))
    for pick, first, final in variants:
        @pl.when(used & pick)
        def _():
            ff_slice(first, final)

    @pl.when(jnp.logical_not(used) & (f == nff - 1))
    def _():
        o_ref[...] = jnp.zeros(o_ref.shape, o_ref.dtype)


def _experts(xs, w1, w3, w2, bexp, nval, tf):
    rows, d = xs.shape
    ff = w1.shape[2]
    tb = EXPERT_ROWS
    nff = ff // tf

    def xrow(i, f, be, nv):
        return (jnp.minimum(i, nv[0] - 1), 0)

    def fcol(i, f, nv):
        return jnp.where(i < nv[0], f, nff - 1)

    grid_spec = pltpu.PrefetchScalarGridSpec(
        num_scalar_prefetch=2,
        grid=(rows // tb, nff),
        in_specs=[pl.BlockSpec((tb, d), xrow),
                  pl.BlockSpec((1, d, tf), lambda i, f, be, nv: (be[i], 0, fcol(i, f, nv))),
                  pl.BlockSpec((1, d, tf), lambda i, f, be, nv: (be[i], 0, fcol(i, f, nv))),
                  pl.BlockSpec((1, tf, d), lambda i, f, be, nv: (be[i], fcol(i, f, nv), 0))],
        out_specs=pl.BlockSpec((tb, d), lambda i, f, be, nv: (i, 0)),
        scratch_shapes=[pltpu.VMEM((tb, d), F32)],
    )
    return pl.pallas_call(
        functools.partial(_expert_body, nff=nff),
        grid_spec=grid_spec,
        out_shape=jax.ShapeDtypeStruct((rows, d), BF16),
        compiler_params=_params("moe_experts", "arbitrary", "arbitrary"),
        name="moe_experts",
    )(bexp, nval, xs, w1, w3, w2)


def _combine_body(offs_ref, cnts_ref, info_ref, x_ref, gp_ref, gf_ref, ys_ref, o_ref, ybuf, acc_ref, sem):
    i = pl.program_id(0)
    nt = pl.num_programs(0)
    tm = x_ref.shape[0]
    chunks = tm // COMBINE_ROWS
    buf = lax.rem(i, 2)

    def copy(b, e, c, src):
        dst = ybuf.at[b, pl.ds((c * N_EXPERTS + e) * COMBINE_ROWS, COMBINE_ROWS)]
        return pltpu.make_async_copy(ys_ref.at[pl.ds(src, COMBINE_ROWS)], dst, sem.at[b, e, c])

    def fetch(step, b):
        for e in range(N_EXPERTS):
            off = offs_ref[step * N_EXPERTS + e]
            copy(b, e, 0, pl.multiple_of(off, BF16_ROWS)).start()
            for c in range(1, chunks):
                @pl.when(cnts_ref[step * N_EXPERTS + e] > c * COMBINE_ROWS)
                def _():
                    copy(b, e, c, pl.multiple_of(off + c * COMBINE_ROWS, BF16_ROWS)).start()

    @pl.when(i == 0)
    def _():
        fetch(0, 0)

    @pl.when(i + 1 < nt)
    def _():
        fetch(i + 1, 1 - buf)

    (e1, g1, r1), (e2, g2, r2) = _routing(info_ref[...])
    for e in range(N_EXPERTS):
        copy(buf, e, 0, 0).wait()
    stack = N_EXPERTS * COMBINE_ROWS
    slot = lax.broadcasted_iota(jnp.int32, (tm, stack), 1).astype(F32)
    y_first = ybuf[buf, 0:stack]
    acc = None
    for ek, gk, rk in ((e1, g1, r1), (e2, g2, r2)):
        pos = jnp.where(rk < float(COMBINE_ROWS), ek * float(COMBINE_ROWS) + rk, -1.0)
        term = gk * _dot(jnp.where(pos == slot, 1.0, 0.0).astype(BF16), y_first)
        acc = term if acc is None else acc + term
    acc_ref[...] = acc

    late = lax.broadcasted_iota(jnp.int32, (tm, COMBINE_ROWS), 1).astype(F32)
    for e in range(N_EXPERTS):
        for c in range(1, chunks):
            @pl.when(cnts_ref[i * N_EXPERTS + e] > c * COMBINE_ROWS)
            def _():
                first = e1 == float(e)
                second = e2 == float(e)
                rank = jnp.where(first, r1, r2)
                gate = jnp.where(first, g1, jnp.where(second, g2, 0.0))
                copy(buf, e, c, 0).wait()
                hit = (rank == late + float(c * COMBINE_ROWS)) & (first | second)
                rows = ybuf[buf, pl.ds((c * N_EXPERTS + e) * COMBINE_ROWS, COMBINE_ROWS)]
                acc_ref[...] += gate * _dot(jnp.where(hit, 1.0, 0.0).astype(BF16), rows)

    o_ref[...] = x_ref[...] + gf_ref[...] * _rms(acc_ref[...], gp_ref[...])


def _combine(offs, cnts, info, x, gp, gf, ys):
    s, d = x.shape
    tm = ROW_TILE
    chunks = tm // COMBINE_ROWS
    grid_spec = pltpu.PrefetchScalarGridSpec(
        num_scalar_prefetch=2,
        grid=(s // tm,),
        in_specs=[pl.BlockSpec((tm, LANES), lambda i, *_: (i, 0)),
                  pl.BlockSpec((tm, d), lambda i, *_: (i, 0)),
                  pl.BlockSpec((1, d), lambda i, *_: (0, 0)),
                  pl.BlockSpec((1, d), lambda i, *_: (0, 0)),
                  pl.BlockSpec(memory_space=pl.ANY)],
        out_specs=pl.BlockSpec((tm, d), lambda i, *_: (i, 0)),
        scratch_shapes=[pltpu.VMEM((2, chunks * N_EXPERTS * COMBINE_ROWS, d), BF16),
                        pltpu.VMEM((tm, d), F32),
                        pltpu.SemaphoreType.DMA((2, N_EXPERTS, chunks))],
    )
    return pl.pallas_call(
        _combine_body,
        grid_spec=grid_spec,
        out_shape=jax.ShapeDtypeStruct((s, d), F32),
        compiler_params=_params("moe_combine", "arbitrary"),
        name="moe_combine",
    )(offs, cnts, info, x, gp, gf, ys)


def _w_in_body(w_ref, o_ref):
    w = w_ref[0]
    rows = w.shape[0]
    head = 2 * GMLP_WIDTH + Q_RANK + KV_RANK
    kr = w[:, head:head + ROPE]
    before = jnp.zeros((rows, NOPE), F32)
    after = jnp.zeros((rows, HEAD_PAD - NOPE - ROPE), F32)
    wide = jnp.concatenate([w[:, :head], before, kr, after, w[:, head + ROPE:]], axis=1)
    o_ref[0] = wide.astype(o_ref.dtype)


def _w_in_layout(w_in):
    n_layers, d, n = w_in.shape
    tk = d // 4
    wide = n - ROPE + HEAD_PAD
    return pl.pallas_call(
        _w_in_body,
        grid=(n_layers, d // tk),
        in_specs=[pl.BlockSpec((1, tk, n), lambda l, k: (l, k, 0))],
        out_specs=pl.BlockSpec((1, tk, wide), lambda l, k: (l, k, 0)),
        out_shape=jax.ShapeDtypeStruct((n_layers, d, wide), BF16),
        compiler_params=_params("w_in_layout", "arbitrary", "arbitrary"),
        name="w_in_layout",
    )(w_in)


def _mixer_weights(w_uq, w_ukv, w_branch_b):
    wq = w_uq.reshape(Q_RANK, HEADS, NOPE + ROPE)
    zq = jnp.zeros((Q_RANK, HEADS, HEAD_PAD - NOPE - ROPE), w_uq.dtype)
    wq_main = jnp.concatenate([wq, zq], axis=-1).reshape(Q_RANK, HEADS * HEAD_PAD).astype(BF16)

    wkv = w_ukv.reshape(KV_RANK, HEADS, NOPE + VDIM)
    zk = jnp.zeros((KV_RANK, HEADS, HEAD_PAD - NOPE), w_ukv.dtype)
    wk = jnp.concatenate([wkv[..., :NOPE], zk], axis=-1).reshape(KV_RANK, HEADS * HEAD_PAD).astype(BF16)
    zv = jnp.zeros((KV_RANK, HEADS, VT_ROWS - VDIM), w_ukv.dtype)
    wv = jnp.concatenate([wkv[..., NOPE:], zv], axis=-1).reshape(KV_RANK, HEADS * VT_ROWS).T.astype(BF16)

    return wq_main, wk, wv, w_branch_b.astype(BF16)


def _expert_layout(cnts, nt, nblk):
    tb = EXPERT_ROWS
    c = cnts.reshape(nt, N_EXPERTS)
    total = jnp.sum(c, axis=0)
    per = jnp.where(total > 0, (total + REGION_SLACK + tb - 1) // tb, 0)
    stop = jnp.cumsum(per)
    start = stop - per
    offs = start[None, :] * tb + (jnp.cumsum(c, axis=0) - c)
    nval = stop[-1]
    blk = jnp.minimum(jnp.arange(nblk, dtype=jnp.int32), nval - 1)
    bexp = jnp.minimum(jnp.sum(blk[:, None] >= stop[None, :], axis=1), N_EXPERTS - 1)
    fill = jnp.concatenate([jnp.where(per > 0, stop - 1, -1), jnp.where(per > 1, stop - 2, -1)])
    i32 = lambda v: v.astype(jnp.int32)
    return i32(offs.reshape(-1)), i32(bexp), i32(nval.reshape(1)), i32(fill)


def kernel(x, c, positions, ada_w, ada_b, norm_mix_pre, norm_mix_post, norm_ffn_pre, norm_ffn_post, w_in, gmlp_ln_g, gmlp_ln_b, gmlp_ws, gmlp_bs, mla_q_norm, mla_w_uq, mla_kv_norm, mla_w_ukv, w_branch_a, w_branch_b, w_out, ffn_w_gate, ffn_w_up, ffn_w_down, moe_router, moe_router_bias, moe_w1, moe_w3, moe_w2):
    batch, s, d = x.shape
    assert batch == 1 and s % ROW_TILE == 0 and s % ATTN_TILE == 0
    depth = ada_w.shape[0]
    xs = x.reshape(s, d)
    mod = _ada_mod(c, ada_w, ada_b)
    w_in_wide = _w_in_layout(w_in)
    cos_t, sin_t = _rope_tables(positions)
    vone = np.zeros((HEADS, VT_ROWS, LANES), np.float32)
    vone[:, VDIM] = 1.0
    vone = jnp.asarray(vone.reshape(HEADS * VT_ROWS, LANES))
    row = lambda v: v.reshape(1, -1)

    for l in range(depth):
        sh_m, sc_m, g_m, sh_f, sc_f, g_f = [mod[l, :, k * d:(k + 1) * d] for k in range(6)]
        wq_main, wk, wv, wb = _mixer_weights(mla_w_uq[l], mla_w_ukv[l], w_branch_b[l])
        bias = jnp.broadcast_to(gmlp_bs[l].T[:, :, None], (CHUNK, GMLP_GROUPS, GMLP_GROUP_DIM))
        bias = bias.reshape(CHUNK, GMLP_WIDTH)
        a, gb, q, k, vt = _mixer_front(xs, row(norm_mix_pre[l]), sc_m, sh_m, w_in_wide[l],
                                       row(gmlp_ln_g[l]), row(gmlp_ln_b[l]), gmlp_ws[l], bias,
                                       w_branch_a[l].astype(BF16), cos_t, sin_t,
                                       row(mla_q_norm[l]), row(mla_kv_norm[l]), wq_main, wk, wv, vone)
        o = _attention(q, k, vt)
        merge_args = (a, gb, o, xs, wb, w_out[l].astype(BF16), row(norm_mix_post[l]), g_m)

        j = l // 2
        if l % 2 == 0:
            wg, wu, wd = ffn_w_gate[j].astype(BF16), ffn_w_up[j].astype(BF16), ffn_w_down[j].astype(BF16)
            xs = _dense_ffn(merge_args, row(norm_ffn_pre[l]), sc_f, sh_f, wg, wu, wd, row(norm_ffn_post[l]), g_f)
        else:
            nt = s // ROW_TILE
            tb = EXPERT_ROWS
            max_rows = 2 * s + nt * N_EXPERTS * (BF16_ROWS - 1) + N_EXPERTS * (REGION_SLACK + tb - 1)
            nblk = -(-max_rows // tb) + 1
            wr = jnp.pad(moe_router[j], ((0, 0), (0, LANES - N_EXPERTS)))
            br = jnp.pad(moe_router_bias[j], (0, LANES - N_EXPERTS), constant_values=NEG).reshape(1, LANES)
            xs, hb, info, infot, cnts = _route(merge_args, row(norm_ffn_pre[l]), sc_f, sh_f, wr, br)
            offs, bexp, nval, fill = _expert_layout(cnts, nt, nblk)
            xsort = _dispatch(offs, cnts, fill, nval, hb, infot, nblk)
            ysort = _experts(xsort, moe_w1[j].astype(BF16), moe_w3[j].astype(BF16), moe_w2[j],
                             bexp, nval, tf=moe_w1.shape[3] // 2)
            xs = _combine(offs, cnts, info, xs, row(norm_ffn_post[l]), g_f, ysort)
    return xs.reshape(batch, s, d)
```

```python
import functools

import jax
import jax.numpy as jnp
import numpy as np
from jax import lax
from jax.experimental import pallas as pl
from jax.experimental.pallas import tpu as pltpu

F32 = jnp.float32
BF16 = jnp.bfloat16

EPS = 1e-6
LANES = 128
BF16_ROWS = 16
GMLP_GROUPS = 8
GMLP_GROUP_DIM = 64
GMLP_WIDTH = GMLP_GROUPS * GMLP_GROUP_DIM
CHUNK = 128
HEADS = 8
NOPE = 64
ROPE = 32
VDIM = 64
HEAD_PAD = 128
VT_ROWS = 128
Q_RANK = 384
KV_RANK = 256
ROPE_THETA = 10000.0
N_EXPERTS = 8
NEG = -1e30

ROW_TILE = 512
ATTN_TILE = 512
ATTN_HEADS = 2
STAT_ROWS = 8
EXPERT_ROWS = 512
DISPATCH_ROWS = 128
COMBINE_ROWS = 256
FIRST_ROWS = 256
ROUTE_FIELDS = 8
REGION_SLACK = DISPATCH_ROWS - BF16_ROWS
VMEM_MIB = {"ada_mod": 24, "rope_tables": 16, "w_in_layout": 32, "mixer_front": 48, "mla_attention": 58, "merge_dense_ffn": 56,
            "merge_moe_route": 32, "moe_dispatch": 32, "moe_experts": 48, "moe_combine": 40}


def _params(name, *sem):
    return pltpu.CompilerParams(dimension_semantics=sem, vmem_limit_bytes=VMEM_MIB[name] * 1024 * 1024)


def _dot(a, b):
    return jnp.dot(a, b, preferred_element_type=F32)


def _rms(x, g):
    return x * lax.rsqrt(jnp.mean(x * x, axis=-1, keepdims=True) + EPS) * g


def _gelu(x):
    return 0.5 * x * (1.0 + lax.erf(x * np.float32(0.7071067811865476)))


def _full(shape):
    return pl.BlockSpec(shape, lambda *_: (0,) * len(shape), pipeline_mode=pl.Buffered(1))


def _mod_body(c_ref, w_ref, b_ref, o_ref):
    c = c_ref[...]
    ca = jnp.broadcast_to(c * jax.nn.sigmoid(c), (LANES, c.shape[1])).T[:, 0:1]
    o_ref[0] = jnp.sum(ca * w_ref[0], axis=0, keepdims=True) + b_ref[0]


def _ada_mod(c, ada_w, ada_b):
    n_layers, d, n = ada_w.shape
    tn = n // 4
    return pl.pallas_call(
        _mod_body,
        grid=(n_layers, n // tn),
        in_specs=[pl.BlockSpec((1, d), lambda l, j: (0, 0)),
                  pl.BlockSpec((1, d, tn), lambda l, j: (l, 0, j)),
                  pl.BlockSpec((1, 1, tn), lambda l, j: (l, 0, j))],
        out_specs=pl.BlockSpec((1, 1, tn), lambda l, j: (l, 0, j)),
        out_shape=jax.ShapeDtypeStruct((n_layers, 1, n), F32),
        compiler_params=_params("ada_mod", "arbitrary", "arbitrary"),
        name="ada_mod",
    )(c.reshape(1, d), ada_w, ada_b.reshape(n_layers, 1, n))


def _rope_body(pos_ref, invf_ref, cos_ref, sin_ref):
    ang = pos_ref[...].astype(F32) * invf_ref[...]
    cos_ref[...] = jnp.cos(ang)
    sin_ref[...] = jnp.sin(ang)


def _rope_tables(positions):
    s = positions.shape[-1]
    half = ROPE // 2
    per_row = LANES // half
    inv_freq = 1.0 / (ROPE_THETA ** (jnp.arange(0, ROPE, 2, dtype=F32) / ROPE))
    pos_dense = jnp.broadcast_to(positions.reshape(s, 1), (s, half)).reshape(s // per_row, LANES)
    invf = jnp.tile(inv_freq, per_row).reshape(1, LANES)
    rows = s // per_row
    tr = rows // 4
    cos_d, sin_d = pl.pallas_call(
        _rope_body,
        grid=(rows // tr,),
        in_specs=[pl.BlockSpec((tr, LANES), lambda i: (i, 0)), _full((1, LANES))],
        out_specs=[pl.BlockSpec((tr, LANES), lambda i: (i, 0))] * 2,
        out_shape=[jax.ShapeDtypeStruct((rows, LANES), F32)] * 2,
        compiler_params=_params("rope_tables", "arbitrary"),
        name="rope_tables",
    )(pos_dense, invf)
    cos = cos_d.reshape(s, half)
    sin = sin_d.reshape(s, half)
    ones = jnp.ones((s, NOPE), F32)
    zeros = jnp.zeros((s, NOPE), F32)
    pad = jnp.zeros((s, HEAD_PAD - NOPE - ROPE), F32)
    cos_t = jnp.concatenate([ones, cos, cos, pad], axis=1)
    sin_t = jnp.concatenate([zeros, sin, sin, pad], axis=1)
    return cos_t, sin_t


def _gmlp_branch(u, v, ga, lng_ref, lnb_ref, ws_ref, bias_ref, wa_ref):
    tm = u.shape[0]
    gu = _gelu(u)
    gv = _gelu(v)
    mu = jnp.mean(gv, axis=-1, keepdims=True)
    xc = gv - mu
    vn = xc * lax.rsqrt(jnp.mean(xc * xc, axis=-1, keepdims=True) + EPS) * lng_ref[...] + lnb_ref[...]
    vb = vn.astype(BF16)
    t_idx = lax.broadcasted_iota(jnp.int32, (CHUNK, CHUNK), 0)
    s_idx = lax.broadcasted_iota(jnp.int32, (CHUNK, CHUNK), 1)
    causal = s_idx <= t_idx
    ws = [jnp.where(causal, ws_ref[g], 0.0).astype(BF16) for g in range(GMLP_GROUPS)]
    left = lax.broadcasted_iota(jnp.int32, (CHUNK, LANES), 1) < GMLP_GROUP_DIM
    bias = bias_ref[...]
    z_rows = []
    for c in range(tm // CHUNK):
        vc = vb[c * CHUNK:(c + 1) * CHUNK]
        z_cols = []
        for j in range(GMLP_WIDTH // LANES):
            vp = vc[:, j * LANES:(j + 1) * LANES]
            z_cols.append(jnp.where(left, _dot(ws[2 * j], vp), _dot(ws[2 * j + 1], vp)))
        z_rows.append(jnp.concatenate(z_cols, axis=1) + bias)
    z = jnp.concatenate(z_rows, axis=0)
    gated = (gu * z).astype(BF16)
    return jax.nn.sigmoid(ga) * _dot(gated, wa_ref[...])


def _mla_qkv(cq, ckv, kr, cos_ref, sin_ref, qn_ref, kvn_ref, wqm_ref, wk_ref, wvt_ref, vone_ref,
             q_ref, k_ref, vt_ref, scale):
    cos = cos_ref[...]
    sin = sin_ref[...]
    half = ROPE // 2
    lane = lax.broadcasted_iota(jnp.int32, sin.shape, 1)
    sin_x1 = jnp.where(lane < NOPE + half, -sin, 0.0)
    sin_x2 = jnp.where(lane >= NOPE + half, sin, 0.0)

    def rope(x):
        return x * cos + pltpu.roll(x, HEAD_PAD - half, 1) * sin_x1 + pltpu.roll(x, half, 1) * sin_x2

    cqn = _rms(cq, qn_ref[...]).astype(BF16)
    qm = _dot(cqn, wqm_ref[...])
    ckn = _rms(ckv, kvn_ref[...]).astype(BF16)
    km = _dot(ckn, wk_ref[...])
    kpe = rope(kr)
    for h in range(HEADS):
        sl = slice(h * HEAD_PAD, (h + 1) * HEAD_PAD)
        q_ref[:, sl] = (rope(qm[:, sl]) * scale).astype(q_ref.dtype)
        k_ref[:, sl] = (km[:, sl] + kpe).astype(k_ref.dtype)
    vt = lax.dot_general(wvt_ref[...], ckn, (((1,), (1,)), ((), ())), preferred_element_type=F32)
    vt_ref[...] = (vt + jnp.tile(vone_ref[...], (1, vt.shape[1] // LANES))).astype(vt_ref.dtype)


def _front_body(x_ref, g_ref, sc_ref, sh_ref, w_ref, lng_ref, lnb_ref, ws_ref, bias_ref, wa_ref,
                cos_ref, sin_ref, qn_ref, kvn_ref, wqm_ref, wk_ref, wvt_ref, vone_ref,
                a_ref, gb_ref, q_ref, k_ref, vt_ref, *, scale):
    h = _rms(x_ref[...], g_ref[...]) * (1.0 + sc_ref[...]) + sh_ref[...]
    proj = _dot(h.astype(BF16), w_ref[...])
    d = x_ref.shape[1]
    cuts = np.cumsum([0, GMLP_WIDTH, GMLP_WIDTH, Q_RANK, KV_RANK, HEAD_PAD, d, d])
    u, v, cq, ckv, kr, ga, gb = [proj[:, lo:hi] for lo, hi in zip(cuts[:-1], cuts[1:])]
    gb_ref[...] = gb.astype(gb_ref.dtype)
    _mla_qkv(cq, ckv, kr, cos_ref, sin_ref, qn_ref, kvn_ref, wqm_ref, wk_ref, wvt_ref, vone_ref,
             q_ref, k_ref, vt_ref, scale)
    a_ref[...] = _gmlp_branch(u, v, ga, lng_ref, lnb_ref, ws_ref, bias_ref, wa_ref).astype(a_ref.dtype)


def _mixer_front(x, g, sc, sh, w, lng, lnb, ws, bias, wa, cos_t, sin_t, qn, kvn, wqm, wk, wvt, vone):
    s, d = x.shape
    tm = ROW_TILE
    width = HEADS * HEAD_PAD
    row = lambda n: pl.BlockSpec((tm, n), lambda i: (i, 0))
    scale = float((NOPE + ROPE) ** -0.5 * np.log2(np.e))
    consts = (g, sc, sh, w, lng, lnb, ws, bias, wa)
    mla_consts = (qn, kvn, wqm, wk, wvt, vone)
    return pl.pallas_call(
        functools.partial(_front_body, scale=scale),
        grid=(s // tm,),
        in_specs=[row(d)] + [_full(c.shape) for c in consts] + [row(HEAD_PAD), row(HEAD_PAD)]
                 + [_full(c.shape) for c in mla_consts],
        out_specs=[row(d), row(d), row(width), row(width), pl.BlockSpec((wvt.shape[0], tm), lambda i: (0, i))],
        out_shape=[jax.ShapeDtypeStruct((s, d), BF16), jax.ShapeDtypeStruct((s, d), BF16),
                   jax.ShapeDtypeStruct((s, width), BF16), jax.ShapeDtypeStruct((s, width), BF16),
                   jax.ShapeDtypeStruct((wvt.shape[0], s), BF16)],
        compiler_params=_params("mixer_front", "arbitrary"),
        name="mixer_front",
    )(x, *consts, cos_t, sin_t, *mla_consts)


def _attn_body(q_ref, k_ref, vt_ref, o_ref, s_ref, mx_ref, m_ref, acc_ref):
    t = o_ref.shape[0]
    heads = q_ref.shape[1] // HEAD_PAD
    qi = pl.program_id(1)
    m_ref[...] = jnp.full(m_ref.shape, NEG, F32)
    acc_ref[...] = jnp.zeros(acc_ref.shape, F32)

    qs = t // 2
    chains = [(h, c) for h in range(heads) for c in range(2)]

    def scores(blk, slot, q_tile, which=chains):
        off = pl.multiple_of(blk * t, t)
        for h, c in which:
            sl = slice(h * HEAD_PAD, (h + 1) * HEAD_PAD)
            ql = slice(c * qs, (c + 1) * qs)
            q_off = pl.multiple_of(q_tile * t + c * qs, qs)
            s = lax.dot_general(k_ref[pl.ds(off, t), sl], q_ref[pl.ds(q_off, qs), sl],
                                (((1,), (1,)), ((), ())), preferred_element_type=F32)
            s_ref[slot, h, :, ql] = s
            mx_ref[slot, h, :, ql] = jnp.broadcast_to(jnp.max(s, axis=0, keepdims=True), (STAT_ROWS, qs))

    def consume(blk, slot, mask, which=chains):
        off = pl.multiple_of(blk * t, t)
        for h, c in which:
            ql = slice(c * qs, (c + 1) * qs)
            s = s_ref[slot, h, :, ql]
            if mask is None:
                mx = mx_ref[slot, h, :, ql]
            else:
                s = jnp.where(mask[:, ql], s, NEG)
                mx = jnp.max(s, axis=0, keepdims=True)
            m_prev = m_ref[h, :, ql]
            m_new = jnp.maximum(m_prev, mx)
            alpha = jnp.exp2(m_prev - m_new)
            pv = alpha[0:1] * acc_ref[h, :, ql]
            half = t // 2
            for part in range(2):
                p = jnp.exp2(s[part * half:(part + 1) * half] - m_new[0:1]).astype(BF16)
                keys = pl.ds(pl.multiple_of(off + part * half, half), half)
                pv = pv + _dot(vt_ref[h * VT_ROWS:(h + 1) * VT_ROWS, keys], p)
            acc_ref[h, :, ql] = pv
            m_ref[h, :, ql] = m_new

    @pl.when(qi == 0)
    def _():
        scores(0, 0, qi)

    def run(blk, n):
        for u in range(n):
            for chain in chains:
                scores(blk + u + 1, (u + 1) % 2, qi, [chain])
                consume(blk + u, u % 2, None, [chain])

    def octet(j, carry):
        run(8 * j, 8)
        return carry

    lax.fori_loop(0, qi // 8, octet, 0)
    done = (qi // 8) * 8
    for n in (4, 2):
        more = qi - done >= n

        @pl.when(more)
        def _():
            run(done, n)

        done = done + jnp.where(more, n, 0)

    row = lax.broadcasted_iota(jnp.int32, (t, t), 0)
    col = lax.broadcasted_iota(jnp.int32, (t, t), 1)
    causal = row <= col
    odd = lax.rem(qi, 2) == 1

    @pl.when(odd)
    def _():
        for chain in chains:
            scores(qi, 1, qi, [chain])
            consume(qi - 1, 0, None, [chain])
        consume(qi, 1, causal)

    @pl.when(jnp.logical_not(odd))
    def _():
        consume(qi, 0, causal)

    def finalize():
        lower = lax.broadcasted_iota(jnp.int32, (t, HEAD_PAD), 1) < VDIM
        for j in range(heads // 2):
            outs = []
            for h in (2 * j, 2 * j + 1):
                acc = acc_ref[h]
                out_t = acc / acc[VDIM:VDIM + 1]
                if VT_ROWS < HEAD_PAD:
                    out_t = jnp.concatenate([out_t, jnp.zeros((HEAD_PAD - VT_ROWS, t), F32)], axis=0)
                outs.append(out_t.T)
            packed = jnp.where(lower, outs[0], pltpu.roll(outs[1], VDIM, 1))
            o_ref[:, j * HEAD_PAD:(j + 1) * HEAD_PAD] = packed.astype(o_ref.dtype)

    last = qi == pl.num_programs(1) - 1

    @pl.when(jnp.logical_not(last))
    def _():
        scores(0, 0, qi + 1)
        finalize()

    @pl.when(last)
    def _():
        finalize()


def _attention(q, k, vt):
    s, width = q.shape
    t = ATTN_TILE
    gw = ATTN_HEADS * HEAD_PAD
    return pl.pallas_call(
        _attn_body,
        grid=(width // gw, s // t),
        in_specs=[pl.BlockSpec((s, gw), lambda h, i: (0, h)),
                  pl.BlockSpec((s, gw), lambda h, i: (0, h)),
                  pl.BlockSpec((ATTN_HEADS * VT_ROWS, s), lambda h, i: (h, 0))],
        out_specs=pl.BlockSpec((t, ATTN_HEADS * VDIM), lambda h, i: (i, h)),
        out_shape=jax.ShapeDtypeStruct((s, HEADS * VDIM), BF16),
        scratch_shapes=[pltpu.VMEM((2, ATTN_HEADS, t, t), F32),
                        pltpu.VMEM((2, ATTN_HEADS, STAT_ROWS, t), F32),
                        pltpu.VMEM((ATTN_HEADS, STAT_ROWS, t), F32),
                        pltpu.VMEM((ATTN_HEADS, VT_ROWS, t), F32)],
        compiler_params=_params("mla_attention", "arbitrary", "arbitrary"),
        name="mla_attention",
    )(q, k, vt)


MERGE_ARGS = 8


def _merge_math(a_ref, gb_ref, o_ref, x_ref, wb_ref, wo_ref, gp_ref, gm_ref):
    yb = _dot(o_ref[...], wb_ref[...])
    merged = a_ref[...].astype(F32) + jax.nn.sigmoid(gb_ref[...].astype(F32)) * yb
    y = _dot(merged.astype(BF16), wo_ref[...])
    return x_ref[...] + gm_ref[...] * _rms(y, gp_ref[...])


def _merge_specs(a, gb, o, x, wb, wo, gp, gm):
    tm = ROW_TILE
    d = x.shape[1]
    row = pl.BlockSpec((tm, d), lambda i: (i, 0))
    return [row, row, pl.BlockSpec((tm, o.shape[1]), lambda i: (i, 0)), row,
            _full(wb.shape), _full(wo.shape), _full((1, d)), _full((1, d))]


def _ffn_body(*refs):
    g_ref, sc_ref, sh_ref, wg_ref, wu_ref, wd_ref, gp_ref, gf_ref, o_ref = refs[MERGE_ARGS:]
    x = _merge_math(*refs[:MERGE_ARGS])
    hb = (_rms(x, g_ref[...]) * (1.0 + sc_ref[...]) + sh_ref[...]).astype(BF16)
    a = _dot(hb, wg_ref[...])
    b = _dot(hb, wu_ref[...])
    y = _dot((a * jax.nn.sigmoid(a) * b).astype(BF16), wd_ref[...])
    o_ref[...] = x + gf_ref[...] * _rms(y, gp_ref[...])


def _dense_ffn(merge_args, g, sc, sh, wg, wu, wd, gp, gf):
    s, d = merge_args[3].shape
    tm = ROW_TILE
    row = pl.BlockSpec((tm, d), lambda i: (i, 0))
    vec = _full((1, d))
    return pl.pallas_call(
        _ffn_body,
        grid=(s // tm,),
        in_specs=_merge_specs(*merge_args)
                 + [vec, vec, vec, _full(wg.shape), _full(wu.shape), _full(wd.shape), vec, vec],
        out_specs=row,
        out_shape=jax.ShapeDtypeStruct((s, d), F32),
        compiler_params=_params("merge_dense_ffn", "arbitrary"),
        name="merge_dense_ffn",
    )(*merge_args, g, sc, sh, wg, wu, wd, gp, gf)


def _route_body(*refs):
    g_ref, sc_ref, sh_ref, wr_ref, br_ref, x_ref, hb_ref, info_ref, infot_ref, cnt_ref = refs[MERGE_ARGS:]
    i = pl.program_id(0)
    tm = x_ref.shape[0]
    x = _merge_math(*refs[:MERGE_ARGS])
    x_ref[...] = x
    h = _rms(x, g_ref[...]) * (1.0 + sc_ref[...]) + sh_ref[...]
    hb = h.astype(BF16)
    hb_ref[...] = hb
    h_lo = (h - hb.astype(F32)).astype(BF16)
    w = wr_ref[...]
    w_hi = w.astype(BF16)
    w_lo = (w - w_hi.astype(F32)).astype(BF16)
    logits = _dot(hb, w_hi) + (_dot(h_lo, w_hi) + _dot(hb, w_lo)) + br_ref[...]

    lane = lax.broadcasted_iota(jnp.int32, (tm, LANES), 1)
    m1 = jnp.max(logits, axis=-1, keepdims=True)
    i1 = jnp.min(jnp.where(logits == m1, lane, LANES), axis=-1, keepdims=True)
    oh1 = lane == i1
    rest = jnp.where(oh1, -3e38, logits)
    m2 = jnp.max(rest, axis=-1, keepdims=True)
    i2 = jnp.min(jnp.where(rest == m2, lane, LANES), axis=-1, keepdims=True)
    oh2 = lane == i2
    ex = jnp.exp(m2 - m1)
    g1 = 1.0 / (1.0 + ex)
    g2 = ex / (1.0 + ex)

    ohf = jnp.where(oh1 | oh2, 1.0, 0.0)
    r_idx = lax.broadcasted_iota(jnp.int32, (tm, tm), 0)
    c_idx = lax.broadcasted_iota(jnp.int32, (tm, tm), 1)
    earlier = jnp.where(c_idx < r_idx, 1.0, 0.0).astype(BF16)
    rank = _dot(earlier, ohf.astype(BF16))
    rank1 = jnp.sum(jnp.where(oh1, rank, 0.0), axis=-1, keepdims=True)
    rank2 = jnp.sum(jnp.where(oh2, rank, 0.0), axis=-1, keepdims=True)
    info = jnp.where(lane == 0, i1.astype(F32),
           jnp.where(lane == 1, i2.astype(F32),
           jnp.where(lane == 2, g1,
           jnp.where(lane == 3, g2,
           jnp.where(lane == 4, rank1,
           jnp.where(lane == 5, rank2, 0.0))))))
    info_ref[...] = info
    infot_ref[...] = info.T[:ROUTE_FIELDS]
    cnt = jnp.sum(ohf, axis=0, keepdims=True).astype(jnp.int32)
    cnt_al = ((cnt + (BF16_ROWS - 1)) // BF16_ROWS) * BF16_ROWS
    for e in range(N_EXPERTS):
        cnt_ref[i * N_EXPERTS + e] = cnt_al[0, e]


def _route(merge_args, g, sc, sh, wr, br):
    s, d = merge_args[3].shape
    tm = ROW_TILE
    nt = s // tm
    row = pl.BlockSpec((tm, d), lambda i: (i, 0))
    return pl.pallas_call(
        _route_body,
        grid=(nt,),
        in_specs=_merge_specs(*merge_args)
                 + [_full((1, d)), _full((1, d)), _full((1, d)), _full(wr.shape), _full(br.shape)],
        out_specs=[row, row,
                   pl.BlockSpec((tm, LANES), lambda i: (i, 0)),
                   pl.BlockSpec((ROUTE_FIELDS, tm), lambda i: (0, i)),
                   pl.BlockSpec(memory_space=pltpu.SMEM)],
        out_shape=[jax.ShapeDtypeStruct((s, d), F32),
                   jax.ShapeDtypeStruct((s, d), BF16),
                   jax.ShapeDtypeStruct((s, LANES), F32),
                   jax.ShapeDtypeStruct((ROUTE_FIELDS, s), F32),
                   jax.ShapeDtypeStruct((nt * N_EXPERTS,), jnp.int32)],
        compiler_params=_params("merge_moe_route", "arbitrary"),
        name="merge_moe_route",
    )(*merge_args, g, sc, sh, wr, br)


def _routing(info):
    return [(info[:, k:k + 1], info[:, 2 + k:3 + k], info[:, 4 + k:5 + k]) for k in range(2)]


def _dispatch_body(offs_ref, cnts_ref, fill_ref, nval_ref, hb_ref, infot_ref, xs_ref, xbuf, zbuf, sem, zsem, *, nblk):
    i = pl.program_id(0)
    tm = hb_ref.shape[0]
    tb = zbuf.shape[0]

    def zero_block(blk):
        cp = pltpu.make_async_copy(zbuf, xs_ref.at[pl.ds(pl.multiple_of(blk * tb, tb), tb)], zsem)
        cp.start()
        cp.wait()

    @pl.when(i == 0)
    def _():
        zbuf[...] = jnp.zeros(zbuf.shape, zbuf.dtype)
        for n in range(fill_ref.shape[0]):
            @pl.when(fill_ref[n] >= 0)
            def _():
                zero_block(fill_ref[n])

        def unused(blk, carry):
            zero_block(blk)
            return carry

        lax.fori_loop(nval_ref[0], nblk, unused, 0)

    hb = hb_ref[...]
    fields = infot_ref[...]
    e1, e2, r1, r2 = fields[0:1], fields[1:2], fields[4:5], fields[5:6]
    chunks = tm // DISPATCH_ROWS
    buf = lax.rem(i, 2)

    def first_slot(e, r):
        return jnp.where(r < float(FIRST_ROWS), e * float(FIRST_ROWS) + r, -1.0)

    p1 = first_slot(e1, r1)
    p2 = first_slot(e2, r2)
    slot = lax.broadcasted_iota(jnp.int32, (N_EXPERTS * FIRST_ROWS, tm), 0).astype(F32)
    sel = jnp.where((p1 == slot) | (p2 == slot), 1.0, 0.0).astype(BF16)
    xbuf[buf, 0] = _dot(sel, hb).astype(BF16)

    first_chunks = FIRST_ROWS // DISPATCH_ROWS

    def buf_rows(e, c):
        return c // first_chunks, e * FIRST_ROWS + (c % first_chunks) * DISPATCH_ROWS

    for e in range(N_EXPERTS):
        for c in range(first_chunks, chunks):
            @pl.when(cnts_ref[i * N_EXPERTS + e] > c * DISPATCH_ROWS)
            def _():
                first = e1 == float(e)
                rank = jnp.where(first, r1, r2)
                late = (lax.broadcasted_iota(jnp.int32, (DISPATCH_ROWS, tm), 0) + c * DISPATCH_ROWS).astype(F32)
                hit = (rank == late) & (first | (e2 == float(e)))
                part, row0 = buf_rows(e, c)
                xbuf[buf, part, row0:row0 + DISPATCH_ROWS] = _dot(jnp.where(hit, 1.0, 0.0).astype(BF16),
                                                                  hb).astype(BF16)

    def copy(b, e, c, dst):
        part, row0 = buf_rows(e, c)
        return pltpu.make_async_copy(xbuf.at[b, part, pl.ds(row0, DISPATCH_ROWS)],
                                     xs_ref.at[pl.ds(dst, DISPATCH_ROWS)], sem.at[b, e, c])

    def for_each_copy(step, fn):
        for e in range(N_EXPERTS):
            for c in range(chunks):
                @pl.when(cnts_ref[step * N_EXPERTS + e] > c * DISPATCH_ROWS)
                def _():
                    fn(e, c)

    @pl.when(i > 0)
    def _():
        for_each_copy(i - 1, lambda e, c: copy(1 - buf, e, c, 0).wait())

    def start(e, c):
        off = offs_ref[i * N_EXPERTS + e]
        copy(buf, e, c, pl.multiple_of(off + c * DISPATCH_ROWS, BF16_ROWS)).start()

    for_each_copy(i, start)

    @pl.when(i == pl.num_programs(0) - 1)
    def _():
        for_each_copy(i, lambda e, c: copy(buf, e, c, 0).wait())


def _dispatch(offs, cnts, fill, nval, hb, infot, nblk):
    s, d = hb.shape
    tm = ROW_TILE
    tb = EXPERT_ROWS
    grid_spec = pltpu.PrefetchScalarGridSpec(
        num_scalar_prefetch=4,
        grid=(s // tm,),
        in_specs=[pl.BlockSpec((tm, d), lambda i, *_: (i, 0)),
                  pl.BlockSpec((ROUTE_FIELDS, tm), lambda i, *_: (0, i))],
        out_specs=pl.BlockSpec(memory_space=pl.ANY),
        scratch_shapes=[pltpu.VMEM((2, tm // FIRST_ROWS, N_EXPERTS * FIRST_ROWS, d), BF16),
                        pltpu.VMEM((tb, d), BF16),
                        pltpu.SemaphoreType.DMA((2, N_EXPERTS, tm // DISPATCH_ROWS)),
                        pltpu.SemaphoreType.DMA(())],
    )
    return pl.pallas_call(
        functools.partial(_dispatch_body, nblk=nblk),
        grid_spec=grid_spec,
        out_shape=jax.ShapeDtypeStruct((nblk * tb, d), BF16),
        compiler_params=_params("moe_dispatch", "arbitrary"),
        name="moe_dispatch",
    )(offs, cnts, fill, nval, hb, infot)


def _expert_body(bexp_ref, nval_ref, x_ref, w1_ref, w3_ref, w2_ref, o_ref, acc_ref, *, nff):
    i = pl.program_id(0)
    f = pl.program_id(1)
    used = i < nval_ref[0]

    def ff_slice(first, final):
        xb = x_ref[...]
        a = _dot(xb, w1_ref[0])
        b = _dot(xb, w3_ref[0])
        acc = _dot((a * jax.nn.sigmoid(a) * b).astype(BF16), w2_ref[0].astype(BF16))
        if not first:
            acc = acc_ref[...] + acc
        if final:
            o_ref[...] = acc.astype(o_ref.dtype)
        else:
            acc_ref[...] = acc

    variants = [(f == 0, True, nff == 1)]
    if nff > 2:
        variants.append(((f > 0) & (f < nff - 1), False, False))
    if nff > 1:
        variants.append((f == nff - 1, False, True))
    for pick, first, final in variants:
        @pl.when(used & pick)
        def _():
            ff_slice(first, final)

    @pl.when(jnp.logical_not(used) & (f == nff - 1))
    def _():
        o_ref[...] = jnp.zeros(o_ref.shape, o_ref.dtype)


def _experts(xs, w1, w3, w2, bexp, nval, tf):
    rows, d = xs.shape
    ff = w1.shape[2]
    tb = EXPERT_ROWS
    nff = ff // tf

    def xrow(i, f, be, nv):
        return (jnp.minimum(i, nv[0] - 1), 0)

    def fcol(i, f, nv):
        return jnp.where(i < nv[0], f, nff - 1)

    grid_spec = pltpu.PrefetchScalarGridSpec(
        num_scalar_prefetch=2,
        grid=(rows // tb, nff),
        in_specs=[pl.BlockSpec((tb, d), xrow),
                  pl.BlockSpec((1, d, tf), lambda i, f, be, nv: (be[i], 0, fcol(i, f, nv))),
                  pl.BlockSpec((1, d, tf), lambda i, f, be, nv: (be[i], 0, fcol(i, f, nv))),
                  pl.BlockSpec((1, tf, d), lambda i, f, be, nv: (be[i], fcol(i, f, nv), 0))],
        out_specs=pl.BlockSpec((tb, d), lambda i, f, be, nv: (i, 0)),
        scratch_shapes=[pltpu.VMEM((tb, d), F32)],
    )
    return pl.pallas_call(
        functools.partial(_expert_body, nff=nff),
        grid_spec=grid_spec,
        out_shape=jax.ShapeDtypeStruct((rows, d), BF16),
        compiler_params=_params("moe_experts", "arbitrary", "arbitrary"),
        name="moe_experts",
    )(bexp, nval, xs, w1, w3, w2)


def _combine_body(offs_ref, cnts_ref, info_ref, x_ref, gp_ref, gf_ref, ys_ref, o_ref, ybuf, acc_ref, sem):
    i = pl.program_id(0)
    nt = pl.num_programs(0)
    tm = x_ref.shape[0]
    chunks = tm // COMBINE_ROWS
    buf = lax.rem(i, 2)

    def copy(b, e, c, src):
        dst = ybuf.at[b, pl.ds((c * N_EXPERTS + e) * COMBINE_ROWS, COMBINE_ROWS)]
        return pltpu.make_async_copy(ys_ref.at[pl.ds(src, COMBINE_ROWS)], dst, sem.at[b, e, c])

    def fetch(step, b):
        for e in range(N_EXPERTS):
            off = offs_ref[step * N_EXPERTS + e]
            copy(b, e, 0, pl.multiple_of(off, BF16_ROWS)).start()
            for c in range(1, chunks):
                @pl.when(cnts_ref[step * N_EXPERTS + e] > c * COMBINE_ROWS)
                def _():
                    copy(b, e, c, pl.multiple_of(off + c * COMBINE_ROWS, BF16_ROWS)).start()

    @pl.when(i == 0)
    def _():
        fetch(0, 0)

    @pl.when(i + 1 < nt)
    def _():
        fetch(i + 1, 1 - buf)

    (e1, g1, r1), (e2, g2, r2) = _routing(info_ref[...])
    for e in range(N_EXPERTS):
        copy(buf, e, 0, 0).wait()
    stack = N_EXPERTS * COMBINE_ROWS
    slot = lax.broadcasted_iota(jnp.int32, (tm, stack), 1).astype(F32)
    y_first = ybuf[buf, 0:stack]
    acc = None
    for ek, gk, rk in ((e1, g1, r1), (e2, g2, r2)):
        pos = jnp.where(rk < float(COMBINE_ROWS), ek * float(COMBINE_ROWS) + rk, -1.0)
        term = gk * _dot(jnp.where(pos == slot, 1.0, 0.0).astype(BF16), y_first)
        acc = term if acc is None else acc + term
    acc_ref[...] = acc

    late = lax.broadcasted_iota(jnp.int32, (tm, COMBINE_ROWS), 1).astype(F32)
    for e in range(N_EXPERTS):
        for c in range(1, chunks):
            @pl.when(cnts_ref[i * N_EXPERTS + e] > c * COMBINE_ROWS)
            def _():
                first = e1 == float(e)
                second = e2 == float(e)
                rank = jnp.where(first, r1, r2)
                gate = jnp.where(first, g1, jnp.where(second, g2, 0.0))
                copy(buf, e, c, 0).wait()
                hit = (rank == late + float(c * COMBINE_ROWS)) & (first | second)
                rows = ybuf[buf, pl.ds((c * N_EXPERTS + e) * COMBINE_ROWS, COMBINE_ROWS)]
                acc_ref[...] += gate * _dot(jnp.where(hit, 1.0, 0.0).astype(BF16), rows)

    o_ref[...] = x_ref[...] + gf_ref[...] * _rms(acc_ref[...], gp_ref[...])


def _combine(offs, cnts, info, x, gp, gf, ys):
    s, d = x.shape
    tm = ROW_TILE
    chunks = tm // COMBINE_ROWS
    grid_spec = pltpu.PrefetchScalarGridSpec(
        num_scalar_prefetch=2,
        grid=(s // tm,),
        in_specs=[pl.BlockSpec((tm, LANES), lambda i, *_: (i, 0)),
                  pl.BlockSpec((tm, d), lambda i, *_: (i, 0)),
                  pl.BlockSpec((1, d), lambda i, *_: (0, 0)),
                  pl.BlockSpec((1, d), lambda i, *_: (0, 0)),
                  pl.BlockSpec(memory_space=pl.ANY)],
        out_specs=pl.BlockSpec((tm, d), lambda i, *_: (i, 0)),
        scratch_shapes=[pltpu.VMEM((2, chunks * N_EXPERTS * COMBINE_ROWS, d), BF16),
                        pltpu.VMEM((tm, d), F32),
                        pltpu.SemaphoreType.DMA((2, N_EXPERTS, chunks))],
    )
    return pl.pallas_call(
        _combine_body,
        grid_spec=grid_spec,
        out_shape=jax.ShapeDtypeStruct((s, d), F32),
        compiler_params=_params("moe_combine", "arbitrary"),
        name="moe_combine",
    )(offs, cnts, info, x, gp, gf, ys)


def _w_in_body(w_ref, o_ref):
    w = w_ref[0]
    rows = w.shape[0]
    head = 2 * GMLP_WIDTH + Q_RANK + KV_RANK
    kr = w[:, head:head + ROPE]
    before = jnp.zeros((rows, NOPE), F32)
    after = jnp.zeros((rows, HEAD_PAD - NOPE - ROPE), F32)
    wide = jnp.concatenate([w[:, :head], before, kr, after, w[:, head + ROPE:]], axis=1)
    o_ref[0] = wide.astype(o_ref.dtype)


def _w_in_layout(w_in):
    n_layers, d, n = w_in.shape
    tk = d // 4
    wide = n - ROPE + HEAD_PAD
    return pl.pallas_call(
        _w_in_body,
        grid=(n_layers, d // tk),
        in_specs=[pl.BlockSpec((1, tk, n), lambda l, k: (l, k, 0))],
        out_specs=pl.BlockSpec((1, tk, wide), lambda l, k: (l, k, 0)),
        out_shape=jax.ShapeDtypeStruct((n_layers, d, wide), BF16),
        compiler_params=_params("w_in_layout", "arbitrary", "arbitrary"),
        name="w_in_layout",
    )(w_in)


def _mixer_weights(w_uq, w_ukv, w_branch_b):
    wq = w_uq.reshape(Q_RANK, HEADS, NOPE + ROPE)
    zq = jnp.zeros((Q_RANK, HEADS, HEAD_PAD - NOPE - ROPE), w_uq.dtype)
    wq_main = jnp.concatenate([wq, zq], axis=-1).reshape(Q_RANK, HEADS * HEAD_PAD).astype(BF16)

    wkv = w_ukv.reshape(KV_RANK, HEADS, NOPE + VDIM)
    zk = jnp.zeros((KV_RANK, HEADS, HEAD_PAD - NOPE), w_ukv.dtype)
    wk = jnp.concatenate([wkv[..., :NOPE], zk], axis=-1).reshape(KV_RANK, HEADS * HEAD_PAD).astype(BF16)
    zv = jnp.zeros((KV_RANK, HEADS, VT_ROWS - VDIM), w_ukv.dtype)
    wv = jnp.concatenate([wkv[..., NOPE:], zv], axis=-1).reshape(KV_RANK, HEADS * VT_ROWS).T.astype(BF16)

    return wq_main, wk, wv, w_branch_b.astype(BF16)


def _expert_layout(cnts, nt, nblk):
    tb = EXPERT_ROWS
    c = cnts.reshape(nt, N_EXPERTS)
    total = jnp.sum(c, axis=0)
    per = jnp.where(total > 0, (total + REGION_SLACK + tb - 1) // tb, 0)
    stop = jnp.cumsum(per)
    start = stop - per
    offs = start[None, :] * tb + (jnp.cumsum(c, axis=0) - c)
    nval = stop[-1]
    blk = jnp.minimum(jnp.arange(nblk, dtype=jnp.int32), nval - 1)
    bexp = jnp.minimum(jnp.sum(blk[:, None] >= stop[None, :], axis=1), N_EXPERTS - 1)
    fill = jnp.concatenate([jnp.where(per > 0, stop - 1, -1), jnp.where(per > 1, stop - 2, -1)])
    i32 = lambda v: v.astype(jnp.int32)
    return i32(offs.reshape(-1)), i32(bexp), i32(nval.reshape(1)), i32(fill)


def kernel(x, c, positions, ada_w, ada_b, norm_mix_pre, norm_mix_post, norm_ffn_pre, norm_ffn_post, w_in, gmlp_ln_g, gmlp_ln_b, gmlp_ws, gmlp_bs, mla_q_norm, mla_w_uq, mla_kv_norm, mla_w_ukv, w_branch_a, w_branch_b, w_out, ffn_w_gate, ffn_w_up, ffn_w_down, moe_router, moe_router_bias, moe_w1, moe_w3, moe_w2):
    batch, s, d = x.shape
    assert batch == 1 and s % ROW_TILE == 0 and s % ATTN_TILE == 0
    depth = ada_w.shape[0]
    xs = x.reshape(s, d)
    mod = _ada_mod(c, ada_w, ada_b)
    w_in_wide = _w_in_layout(w_in)
    cos_t, sin_t = _rope_tables(positions)
    vone = np.zeros((HEADS, VT_ROWS, LANES), np.float32)
    vone[:, VDIM] = 1.0
    vone = jnp.asarray(vone.reshape(HEADS * VT_ROWS, LANES))
    row = lambda v: v.reshape(1, -1)

    for l in range(depth):
        sh_m, sc_m, g_m, sh_f, sc_f, g_f = [mod[l, :, k * d:(k + 1) * d] for k in range(6)]
        wq_main, wk, wv, wb = _mixer_weights(mla_w_uq[l], mla_w_ukv[l], w_branch_b[l])
        bias = jnp.broadcast_to(gmlp_bs[l].T[:, :, None], (CHUNK, GMLP_GROUPS, GMLP_GROUP_DIM))
        bias = bias.reshape(CHUNK, GMLP_WIDTH)
        a, gb, q, k, vt = _mixer_front(xs, row(norm_mix_pre[l]), sc_m, sh_m, w_in_wide[l],
                                       row(gmlp_ln_g[l]), row(gmlp_ln_b[l]), gmlp_ws[l], bias,
                                       w_branch_a[l].astype(BF16), cos_t, sin_t,
                                       row(mla_q_norm[l]), row(mla_kv_norm[l]), wq_main, wk, wv, vone)
        o = _attention(q, k, vt)
        merge_args = (a, gb, o, xs, wb, w_out[l].astype(BF16), row(norm_mix_post[l]), g_m)

        j = l // 2
        if l % 2 == 0:
            wg, wu, wd = ffn_w_gate[j].astype(BF16), ffn_w_up[j].astype(BF16), ffn_w_down[j].astype(BF16)
            xs = _dense_ffn(merge_args, row(norm_ffn_pre[l]), sc_f, sh_f, wg, wu, wd, row(norm_ffn_post[l]), g_f)
        else:
            nt = s // ROW_TILE
            tb = EXPERT_ROWS
            max_rows = 2 * s + nt * N_EXPERTS * (BF16_ROWS - 1) + N_EXPERTS * (REGION_SLACK + tb - 1)
            nblk = -(-max_rows // tb) + 1
            wr = jnp.pad(moe_router[j], ((0, 0), (0, LANES - N_EXPERTS)))
            br = jnp.pad(moe_router_bias[j], (0, LANES - N_EXPERTS), constant_values=NEG).reshape(1, LANES)
            xs, hb, info, infot, cnts = _route(merge_args, row(norm_ffn_pre[l]), sc_f, sh_f, wr, br)
            offs, bexp, nval, fill = _expert_layout(cnts, nt, nblk)
            xsort = _dispatch(offs, cnts, fill, nval, hb, infot, nblk)
            ysort = _experts(xsort, moe_w1[j].astype(BF16), moe_w3[j].astype(BF16), moe_w2[j],
                             bexp, nval, tf=moe_w1.shape[3] // 2)
            xs = _combine(offs, cnts, info, xs, row(norm_ffn_post[l]), g_f, ysort)
    return xs.reshape(batch, s, d)
```
